```python
import numpy as np
import jax
import jax.numpy as jnp
from jax import lax

D_MODEL = 2048
BATCH = 16
SEQ = 256
DEPTH = 4
DEC_BATCH = 4
DEC_SEQ = 2048
PAST_LEN = 256

GRID_W = 64
N_AB = (DEPTH + 1) // 2
N_C = DEPTH // 2
A_HEADS = 8
A_NOPE = 128
A_ROPE = 64
A_VDIM = 128
A_QK = A_NOPE + A_ROPE
Q_LORA = 512
KV_LORA = 256
A_WIDTH = A_HEADS * A_VDIM
B_HEADS = 8
B_DK = 128
B_DV = 128
B_KDIM = B_HEADS * B_DK
B_WIDTH = B_HEADS * B_DV
B_CHUNK = 32
C_HEADS = 16
C_KV_HEADS = 4
C_GROUP = C_HEADS // C_KV_HEADS
C_HEAD_DIM = 128
C_WIDTH = C_HEADS * C_HEAD_DIM
C_KV_WIDTH = C_KV_HEADS * C_HEAD_DIM
WINDOW = 128
C_BLOCK = 128
Q_BLOCK = 128
ROPE_BASE = 10000.0
EPS = 1e-6
NEG_BIG = -1e30
AB_SIZES = (Q_LORA, KV_LORA, A_ROPE, A_WIDTH, B_KDIM, B_KDIM, B_KDIM, B_WIDTH, B_WIDTH)
AB_IN = sum(AB_SIZES)
AB_MIX = A_WIDTH + B_WIDTH
C_SIZES = (C_WIDTH, C_KV_WIDTH, C_KV_WIDTH, C_WIDTH)
C_IN = sum(C_SIZES)

kernel_name = 'hybrid_mla_hgrn2_swa_prefix_dit_step'


def rmsnorm(x, g):
    xf = x.astype(jnp.float32)
    y = xf * lax.rsqrt(jnp.mean(xf * xf, axis=-1, keepdims=True) + EPS)
    return (y * g.astype(jnp.float32)).astype(x.dtype)


def split_cols(a, sizes):
    return jnp.split(a, np.cumsum(sizes)[:-1].tolist(), axis=-1)


def modulate_and_norm(x, cond, w_mod, b_mod, g):
    m = (jax.nn.silu(cond) @ w_mod + b_mod)[:, None, :]
    shift, scale, gate = jnp.split(m, 3, axis=-1)
    return rmsnorm(x, g) * (1 + scale) + shift, gate


def axial_rope_tables(n_tokens, rot_dim):
    rows = n_tokens // GRID_W
    row = jnp.repeat(jnp.arange(rows, dtype=jnp.float32), GRID_W)
    col = jnp.tile(jnp.arange(GRID_W, dtype=jnp.float32), rows)
    n_freq = rot_dim // 4
    inv = ROPE_BASE ** (-jnp.arange(n_freq, dtype=jnp.float32) / n_freq)
    ang = jnp.concatenate([row[:, None] * inv, col[:, None] * inv], axis=-1)
    return jnp.cos(ang), jnp.sin(ang)


def apply_rope(x, cos, sin):
    x1, x2 = jnp.split(x.astype(jnp.float32), 2, axis=-1)
    c, s = cos[:, None, :], sin[:, None, :]
    return jnp.concatenate([x1 * c - x2 * s, x1 * s + x2 * c], axis=-1).astype(x.dtype)


def rope_tail(x, cos, sin):
    return jnp.concatenate([x[..., :A_NOPE], apply_rope(x[..., A_NOPE:], cos, sin)], axis=-1)


def blocked_attention(q, k, v, scale):
    Bn, T, H, dq = q.shape
    nb = T // Q_BLOCK
    qb = jnp.moveaxis(q.reshape(Bn, nb, Q_BLOCK, H, dq), 1, 0)

    def one(qi):
        s = jnp.einsum('bqhd,bkhd->bhqk', qi, k).astype(jnp.float32) * scale
        p = jax.nn.softmax(s, axis=-1).astype(v.dtype)
        return jnp.einsum('bhqk,bkhd->bqhd', p, v)

    o = lax.map(one, qb)
    return jnp.moveaxis(o, 0, 1).reshape(Bn, T, H, v.shape[-1])


def mla_keys(c_kv, k_pe, w_kv_up, k_norm_g):
    Bn, L, _ = c_kv.shape
    kv = (c_kv @ w_kv_up).reshape(Bn, L, A_HEADS, A_NOPE + A_VDIM)
    k_pe_h = jnp.broadcast_to(k_pe[:, :, None, :], (Bn, L, A_HEADS, A_ROPE))
    k = jnp.concatenate([kv[..., :A_NOPE], k_pe_h], axis=-1)
    return rmsnorm(k, k_norm_g), kv[..., A_NOPE:]


def hgrn_lower_bounds(lb_logits):
    p = jax.nn.softmax(lb_logits.astype(jnp.float32), axis=0)
    return jnp.cumsum(p, axis=0) - p[0:1]


def hgrn_gates(f_raw, lb):
    x = f_raw.astype(jnp.float32)
    one_minus_f = (1.0 - lb) * jax.nn.sigmoid(-x)
    return jnp.log1p(-one_minus_f), one_minus_f


def hgrn_chunk_scan(q, log_f, k, v, s0):
    Bn, T, H, _ = q.shape
    DV = v.shape[-1]
    n = T // B_CHUNK

    def chunks(a):
        a = a.astype(jnp.float32).reshape(Bn, n, B_CHUNK, H, a.shape[-1])
        return jnp.moveaxis(a, 1, 0).swapaxes(2, 3)

    incl = jnp.tril(jnp.ones((B_CHUNK, B_CHUNK), dtype=bool))[:, :, None]

    def step(S, xs):
        qc, lfc, kc, vc = xs
        b = jnp.cumsum(lfc, axis=2)
        rel = jnp.where(incl, b[:, :, :, None, :] - b[:, :, None, :, :], NEG_BIG)
        scores = jnp.einsum('bhtk,bhtsk,bhsk->bhts', qc, jnp.exp(rel), kc)
        o = scores @ vc + jnp.einsum('bhtk,bhkv->bhtv', qc * jnp.exp(b), S)
        b_end = b[:, :, -1:, :]
        S = jnp.exp(b_end[:, :, 0, :, None]) * S + jnp.einsum('bhsk,bhsv->bhkv', kc * jnp.exp(b_end - b), vc)
        return S, o

    S, o = lax.scan(step, s0.astype(jnp.float32), (chunks(q), chunks(log_f), chunks(k), chunks(v)))
    o = jnp.moveaxis(o.swapaxes(2, 3), 0, 1).reshape(Bn, T, H, DV)
    return o.astype(v.dtype), S.astype(v.dtype)


def hgrn_bidir(q, v, f_fwd, f_bwd, lb, s0_fwd, s0_bwd):
    Bn, T = q.shape[:2]

    def heads(a):
        return a.reshape(Bn, T, B_HEADS, B_DK)

    def flip(a):
        return jnp.flip(a, axis=1)

    lf_f, k_f = hgrn_gates(f_fwd, lb[0])
    lf_b, k_b = hgrn_gates(f_bwd, lb[1])
    o_f, s_f = hgrn_chunk_scan(q, heads(lf_f), heads(k_f), v, s0_fwd)
    o_b, s_b = hgrn_chunk_scan(flip(q), flip(heads(lf_b)), flip(heads(k_b)), flip(v), s0_bwd)
    return o_f + flip(o_b), s_f, s_b


def ab_mixer(h, w_in, q_lora_g, kv_lora_g, w_q_up, w_kv_up, q_norm_g, k_norm_g, lb, hgrn_g, w_out,
             rope=None, ctx=None):
    Bn, T, _ = h.shape
    q_lat, kv_lat, k_pe, a_gate, b_q, b_ff, b_fb, b_i, b_gate = split_cols(h @ w_in, AB_SIZES)
    q = (rmsnorm(q_lat, q_lora_g) @ w_q_up).reshape(Bn, T, A_HEADS, A_QK)
    q = rmsnorm(q, q_norm_g)
    c_kv = rmsnorm(kv_lat, kv_lora_g)
    k, v = mla_keys(c_kv, k_pe, w_kv_up, k_norm_g)
    if ctx is None:
        keys, vals = k, v
        zero = jnp.zeros((Bn, B_HEADS, B_DK, B_DV), jnp.float32)
        s0f, s0b = zero, zero
    else:
        q, k = rope_tail(q, *rope), rope_tail(k, *rope)
        k_ctx, v_ctx = mla_keys(ctx[0], ctx[1], w_kv_up, k_norm_g)
        keys = jnp.concatenate([k_ctx, k], axis=1)
        vals = jnp.concatenate([v_ctx, v], axis=1)
        s0f, s0b = ctx[2], ctx[3]
    o_a = blocked_attention(q, keys, vals, A_QK ** -0.5).reshape(Bn, T, A_WIDTH) * jax.nn.silu(a_gate)
    qb = jax.nn.silu(b_q).reshape(Bn, T, B_HEADS, B_DK)
    vb = b_i.reshape(Bn, T, B_HEADS, B_DV)
    o_b, s_f, s_b = hgrn_bidir(qb, vb, b_ff, b_fb, lb, s0f, s0b)
    o_b = rmsnorm(o_b, hgrn_g).reshape(Bn, T, B_WIDTH) * jax.nn.silu(b_gate)
    out = jnp.concatenate([o_a, o_b], axis=-1) @ w_out
    return out, (c_kv, k_pe, s_f, s_b)


def gqa_ctx_attention(q, k, v, sink):
    Bn, T, _, d = q.shape
    nb = T // Q_BLOCK
    qb = jnp.moveaxis(q.reshape(Bn, nb, Q_BLOCK, C_KV_HEADS, C_GROUP, d), 1, 0)
    sink_l = sink.astype(jnp.float32).reshape(C_KV_HEADS, C_GROUP, 1, 1)

    def one(qi):
        s = jnp.einsum('bqkgd,bskd->bkgqs', qi, k).astype(jnp.float32) * (d ** -0.5)
        s = jnp.concatenate([jnp.broadcast_to(sink_l, s.shape[:-1] + (1,)), s], axis=-1)
        p = jax.nn.softmax(s, axis=-1)[..., 1:].astype(v.dtype)
        return jnp.einsum('bkgqs,bskd->bqkgd', p, v)

    o = lax.map(one, qb)
    return jnp.moveaxis(o, 0, 1).reshape(Bn, T, C_HEADS, d)


def gqa_band_attention(q, k, v, k_ctx, v_ctx, sink):
    Bn, T, _, d = q.shape
    L = k_ctx.shape[1]
    nb = T // C_BLOCK

    def band(a):
        ab = jnp.pad(a, ((0, 0), (C_BLOCK, C_BLOCK), (0, 0), (0, 0))).reshape(Bn, nb + 2, C_BLOCK, C_KV_HEADS, d)
        return jnp.moveaxis(jnp.concatenate([ab[:, :-2], ab[:, 1:-1], ab[:, 2:]], axis=2), 1, 0)

    qb = jnp.moveaxis(q.reshape(Bn, nb, C_BLOCK, C_KV_HEADS, C_GROUP, d), 1, 0)
    blk = jnp.arange(nb)[:, None, None]
    qpos = blk * C_BLOCK + jnp.arange(C_BLOCK)[None, :, None]
    kpos = (blk - 1) * C_BLOCK + jnp.arange(3 * C_BLOCK)[None, None, :]
    mask = (jnp.abs(kpos - qpos) <= WINDOW) & (kpos >= 0) & (kpos < T)
    sink_l = sink.astype(jnp.float32).reshape(C_KV_HEADS, C_GROUP, 1, 1)
    scale = d ** -0.5

    def one(args):
        qi, ki, vi, mi = args
        s_c = jnp.einsum('bqkgd,bskd->bkgqs', qi, k_ctx).astype(jnp.float32) * scale
        s_l = jnp.einsum('bqkgd,bskd->bkgqs', qi, ki).astype(jnp.float32) * scale
        s_l = jnp.where(mi, s_l, NEG_BIG)
        s = jnp.concatenate([jnp.broadcast_to(sink_l, s_c.shape[:-1] + (1,)), s_c, s_l], axis=-1)
        p = jax.nn.softmax(s, axis=-1).astype(vi.dtype)
        return (jnp.einsum('bkgqs,bskd->bqkgd', p[..., 1:1 + L], v_ctx)
                + jnp.einsum('bkgqs,bskd->bqkgd', p[..., 1 + L:], vi))

    o = lax.map(one, (qb, band(k), band(v), mask))
    return jnp.moveaxis(o, 0, 1).reshape(Bn, T, C_HEADS, d)


def c_mixer(h, w_in, q_norm_g, k_norm_g, sink, w_out, rope=None, ctx=None):
    Bn, T, _ = h.shape
    q, k, v, gate = split_cols(h @ w_in, C_SIZES)
    q = rmsnorm(q.reshape(Bn, T, C_HEADS, C_HEAD_DIM), q_norm_g)
    k = rmsnorm(k.reshape(Bn, T, C_KV_HEADS, C_HEAD_DIM), k_norm_g)
    v = v.reshape(Bn, T, C_KV_HEADS, C_HEAD_DIM)
    if ctx is None:
        o = gqa_ctx_attention(q, k, v, sink)
    else:
        o = gqa_band_attention(apply_rope(q, *rope), apply_rope(k, *rope), v, ctx[0], ctx[1], sink)
    out = (o.reshape(Bn, T, C_WIDTH) * jax.nn.silu(gate)) @ w_out
    return out, (k, v)


def setup_inputs(seed: int = 0) -> dict:
    key = jax.random.key(seed)
    keys = iter(jax.random.split(key, 64))

    def nrm(shape, scale):
        return jax.random.normal(next(keys), shape, jnp.float32) * scale

    def gain(shape):
        return 1.0 + 0.01 * jax.random.normal(next(keys), shape, jnp.float32)

    D = D_MODEL
    return {
        'x_prompt': nrm((BATCH, SEQ, D), 1.0),
        'x_sample': nrm((DEC_BATCH, DEC_SEQ, D), 1.0),
        'cache_ckv': nrm((DEC_BATCH, N_AB, PAST_LEN, KV_LORA), 1.0),
        'cache_kpe': nrm((DEC_BATCH, N_AB, PAST_LEN, A_ROPE), 1.0),
        'state_hgrn_fwd': nrm((DEC_BATCH, N_AB, B_HEADS, B_DK, B_DV), 0.5),
        'state_hgrn_bwd': nrm((DEC_BATCH, N_AB, B_HEADS, B_DK, B_DV), 0.5),
        'cache_k_c': nrm((DEC_BATCH, N_C, PAST_LEN, C_KV_HEADS, C_HEAD_DIM), 1.0),
        'cache_v_c': nrm((DEC_BATCH, N_C, PAST_LEN, C_KV_HEADS, C_HEAD_DIM), 1.0),
        'c': nrm((DEC_BATCH, D), 1.0),
        'c_ctx': nrm((D,), 1.0),
        'mod_w_ab': nrm((N_AB, D, 3 * D), 0.5 * D ** -0.5),
        'mod_b_ab': nrm((N_AB, 3 * D), 0.01),
        'norm_ab': gain((N_AB, D)),
        'w_in_ab': nrm((N_AB, D, AB_IN), D ** -0.5),
        'q_lora_norm': gain((N_AB, Q_LORA)),
        'kv_lora_norm': gain((N_AB, KV_LORA)),
        'w_q_up': nrm((N_AB, Q_LORA, A_HEADS * A_QK), Q_LORA ** -0.5),
        'w_kv_up': nrm((N_AB, KV_LORA, A_HEADS * (A_NOPE + A_VDIM)), KV_LORA ** -0.5),
        'q_norm_ab': gain((N_AB, A_QK)),
        'k_norm_ab': gain((N_AB, A_QK)),
        'hgrn_lb_logits': nrm((N_AB, 2, B_KDIM), 0.5),
        'hgrn_out_norm': gain((N_AB, B_DV)),
        'w_out_ab': nrm((N_AB, AB_MIX, D), AB_MIX ** -0.5),
        'mod_w_c': nrm((N_C, D, 3 * D), 0.5 * D ** -0.5),
        'mod_b_c': nrm((N_C, 3 * D), 0.01),
        'norm_c': gain((N_C, D)),
        'w_in_c': nrm((N_C, D, C_IN), D ** -0.5),
        'q_norm_c': gain((N_C, C_HEAD_DIM)),
        'k_norm_c': gain((N_C, C_HEAD_DIM)),
        'sink_c': nrm((N_C, C_HEADS), 0.5),
        'w_out_c': nrm((N_C, C_WIDTH, D), C_WIDTH ** -0.5),
    }


def reference(x_prompt, x_sample, cache_ckv, cache_kpe, state_hgrn_fwd, state_hgrn_bwd, cache_k_c, cache_v_c,
              c, c_ctx, mod_w_ab, mod_b_ab, norm_ab, w_in_ab, q_lora_norm, kv_lora_norm, w_q_up, w_kv_up,
              q_norm_ab, k_norm_ab, hgrn_lb_logits, hgrn_out_norm, w_out_ab, mod_w_c, mod_b_c, norm_c, w_in_c,
              q_norm_c, k_norm_c, sink_c, w_out_c):
    lat_len = x_sample.shape[1]
    rope_a = axial_rope_tables(lat_len, A_ROPE)
    rope_c = axial_rope_tables(lat_len, C_HEAD_DIM)
    lower = hgrn_lower_bounds(hgrn_lb_logits)
    cond_ctx = c_ctx[None, :]
    xp, xs = x_prompt, x_sample
    new_ckv, new_kpe, new_sf, new_sb, new_kc, new_vc = [], [], [], [], [], []
    for layer in range(DEPTH):
        j = layer // 2
        if layer % 2 == 0:
            w = (w_in_ab[j], q_lora_norm[j], kv_lora_norm[j], w_q_up[j], w_kv_up[j], q_norm_ab[j], k_norm_ab[j],
                 lower[j], hgrn_out_norm[j], w_out_ab[j])
            h, g = modulate_and_norm(xp, cond_ctx, mod_w_ab[j], mod_b_ab[j], norm_ab[j])
            out, (ckv, kpe, sf, sb) = ab_mixer(h, *w)
            xp = xp + g * out
            new_ckv.append(ckv)
            new_kpe.append(kpe)
            new_sf.append(sf)
            new_sb.append(sb)
            h, g = modulate_and_norm(xs, c, mod_w_ab[j], mod_b_ab[j], norm_ab[j])
            ctx = (cache_ckv[:, j], cache_kpe[:, j], state_hgrn_fwd[:, j], state_hgrn_bwd[:, j])
            out, _ = ab_mixer(h, *w, rope=rope_a, ctx=ctx)
            xs = xs + g * out
        else:
            w = (w_in_c[j], q_norm_c[j], k_norm_c[j], sink_c[j], w_out_c[j])
            h, g = modulate_and_norm(xp, cond_ctx, mod_w_c[j], mod_b_c[j], norm_c[j])
            out, (kc, vc) = c_mixer(h, *w)
            xp = xp + g * out
            new_kc.append(kc)
            new_vc.append(vc)
            h, g = modulate_and_norm(xs, c, mod_w_c[j], mod_b_c[j], norm_c[j])
            out, _ = c_mixer(h, *w, rope=rope_c, ctx=(cache_k_c[:, j], cache_v_c[:, j]))
            xs = xs + g * out
    return (xp, xs, jnp.stack(new_ckv, axis=1), jnp.stack(new_kpe, axis=1), jnp.stack(new_sf, axis=1),
            jnp.stack(new_sb, axis=1), jnp.stack(new_kc, axis=1), jnp.stack(new_vc, axis=1))
```

```python
import functools

import numpy as np
import jax
import jax.numpy as jnp
from jax import lax
from jax.experimental import pallas as pl
from jax.experimental.pallas import tpu as pltpu

F32 = jnp.float32
BF16 = jnp.bfloat16

D_MODEL = 2048
BATCH = 16
SEQ = 256
DEPTH = 4
DEC_BATCH = 4
DEC_SEQ = 2048
PAST_LEN = 256
GRID_W = 64
A_HEADS = 8
A_NOPE = 128
A_ROPE = 64
A_VDIM = 128
A_QK = A_NOPE + A_ROPE
A_QK_PAD = 256
Q_LORA = 512
KV_LORA = 256
B_HEADS = 8
B_DK = 128
B_DV = 128
C_HEADS = 16
C_KV_HEADS = 4
C_GROUP = C_HEADS // C_KV_HEADS
C_HEAD_DIM = 128
C_WIDTH = C_HEADS * C_HEAD_DIM
C_KV_WIDTH = C_KV_HEADS * C_HEAD_DIM
WINDOW = 128
ROPE_BASE = 10000.0
EPS = 1e-6
NEG_BIG = -1e30

LANES = 128
CTX_ROWS = BATCH * SEQ
LAT_ROWS = DEC_BATCH * DEC_SEQ
ROWS = CTX_ROWS + LAT_ROWS
N_COND = 8

AB_QLAT = 0
AB_KVLAT = 512
AB_AGATE = 768
AB_BQ = 1792
AB_BFF = 2816
AB_BFB = 3840
AB_BI = 4864
AB_BGATE = 5888
AB_KPE = 6912
AB_N = 7040
C_N = 2 * C_WIDTH + 2 * C_KV_WIDTH

TM_PROJ = 1024
TN_IN_AB = 640
TN_IN_C = 640
TN_OUT = 512
TM_MID = 512
TQ_A = 512
TQ_C = 256
HG_CHUNK = 128
HG_LEVELS = 7
VMEM_LIMIT = 56 * 1024 * 1024


def _cparams(sem):
    return pltpu.CompilerParams(dimension_semantics=sem, vmem_limit_bytes=VMEM_LIMIT)


def _silu(x):
    return x * jax.nn.sigmoid(x)


def _dot(a, b):
    return jnp.dot(a, b, preferred_element_type=F32)


def _dot_nt(a, b):
    return lax.dot_general(a, b, (((1,), (1,)), ((), ())), preferred_element_type=F32)


def _cond_of_tile(i, tm):
    n_ctx = CTX_ROWS // tm
    per_batch = DEC_SEQ // tm
    return jnp.where(i < n_ctx, 0, 1 + (i - n_ctx) // per_batch)


def _rope_block_of_tile(i, tm):
    n_ctx = CTX_ROWS // tm
    per_batch = DEC_SEQ // tm
    return jnp.where(i < n_ctx, 0, 1 + (i - n_ctx) % per_batch)


def _mod_kernel(c_ref, w_ref, b_ref, o_ref):
    a = _silu(c_ref[...]).astype(BF16)
    o_ref[...] = _dot(a, w_ref[...].astype(BF16)) + b_ref[...]


def _modulation(cond8, w_mod, b_mod):
    n = w_mod.shape[0]
    tn = 1024
    return pl.pallas_call(
        _mod_kernel,
        grid=(n, 3 * D_MODEL // tn),
        in_specs=[pl.BlockSpec((N_COND, D_MODEL), lambda l, j: (0, 0)),
                  pl.BlockSpec((None, D_MODEL, tn), lambda l, j: (l, 0, j)),
                  pl.BlockSpec((None, 1, tn), lambda l, j: (l, 0, j))],
        out_specs=pl.BlockSpec((None, N_COND, tn), lambda l, j: (l, 0, j)),
        out_shape=jax.ShapeDtypeStruct((n, N_COND, 3 * D_MODEL), F32),
        compiler_params=_cparams(("parallel", "parallel")),
        name="adaln_mod",
    )(cond8, w_mod, b_mod.reshape(n, 1, 3 * D_MODEL))


def _in_kernel(x_ref, g_ref, sh_ref, sc_ref, w_ref, o_ref, h_ref):
    @pl.when(pl.program_id(1) == 0)
    def _():
        x = x_ref[...]
        r = lax.rsqrt(jnp.mean(x * x, axis=-1, keepdims=True) + EPS)
        h = (x * r * g_ref[...]) * (1.0 + sc_ref[...]) + sh_ref[...]
        h_ref[...] = h.astype(BF16)

    o_ref[...] = _dot(h_ref[...], w_ref[...])


def _in_proj(x, g, mod, w, tn):
    n = w.shape[1]
    tm = TM_PROJ
    return pl.pallas_call(
        _in_kernel,
        grid=(ROWS // tm, n // tn),
        in_specs=[pl.BlockSpec((tm, D_MODEL), lambda i, j: (i, 0)),
                  pl.BlockSpec((1, D_MODEL), lambda i, j: (0, 0)),
                  pl.BlockSpec((None, 1, D_MODEL), lambda i, j: (_cond_of_tile(i, tm) * 3, 0, 0)),
                  pl.BlockSpec((None, 1, D_MODEL), lambda i, j: (_cond_of_tile(i, tm) * 3 + 1, 0, 0)),
                  pl.BlockSpec((D_MODEL, tn), lambda i, j: (0, j))],
        out_specs=pl.BlockSpec((tm, tn), lambda i, j: (i, j)),
        out_shape=jax.ShapeDtypeStruct((ROWS, n), F32),
        scratch_shapes=[pltpu.VMEM((tm, D_MODEL), BF16)],
        compiler_params=_cparams(("parallel", "arbitrary")),
        name="norm_mod_in_proj",
    )(x, g, mod, mod, w)


def _out_kernel(m_ref, w_ref, x_ref, gt_ref, o_ref):
    o_ref[...] = x_ref[...] + gt_ref[...] * _dot(m_ref[...], w_ref[...])


def _out_proj(mix, w, x, mod):
    tm, tn = TM_PROJ, TN_OUT
    return pl.pallas_call(
        _out_kernel,
        grid=(ROWS // tm, D_MODEL // tn),
        in_specs=[pl.BlockSpec((tm, D_MODEL), lambda i, j: (i, 0)),
                  pl.BlockSpec((D_MODEL, tn), lambda i, j: (0, j)),
                  pl.BlockSpec((tm, tn), lambda i, j: (i, j)),
                  pl.BlockSpec((None, 1, tn), lambda i, j: (_cond_of_tile(i, tm) * 3 + 2, 0, j))],
        out_specs=pl.BlockSpec((tm, tn), lambda i, j: (i, j)),
        out_shape=jax.ShapeDtypeStruct((ROWS, D_MODEL), F32),
        compiler_params=_cparams(("parallel", "parallel")),
        name="out_proj_residual",
    )(mix, w, x, mod)


def _rope_a(x, cos_t, sin_t):
    return x * cos_t + (pltpu.roll(x, 32, 1) + pltpu.roll(x, 96, 1)) * sin_t


def _amid_kernel(*refs, do_q, norm_kv):
    if do_q:
        (ql_ref, kvl_ref, kpe_ref, wq_ref, wk_ref, wv_ref, qlg_ref, kvlg_ref, qng_ref, kng_ref, cos_ref, sin_ref,
         q_out, k_out, v_out, ckv_out) = refs
    else:
        (kvl_ref, kpe_ref, wk_ref, wv_ref, kvlg_ref, kng_ref, cos_ref, sin_ref, k_out, v_out, ckv_out) = refs
    cos_t = cos_ref[...]
    sin_t = sin_ref[...]
    inv_qk = 1.0 / A_QK

    if do_q:
        ql = ql_ref[...]
        qn = ql * lax.rsqrt(jnp.mean(ql * ql, axis=-1, keepdims=True) + EPS) * qlg_ref[...]
        qu = _dot(qn.astype(BF16), wq_ref[...])
        g_nope = qng_ref[:, :A_NOPE]
        g_rope = qng_ref[:, A_NOPE:]
        for h in range(A_HEADS):
            a = qu[:, h * A_QK_PAD:h * A_QK_PAD + A_NOPE]
            b = qu[:, h * A_QK_PAD + A_NOPE:(h + 1) * A_QK_PAD]
            ss = jnp.sum(a * a, axis=-1, keepdims=True) + jnp.sum(b * b, axis=-1, keepdims=True)
            r = lax.rsqrt(ss * inv_qk + EPS)
            q_out[:, h * A_QK_PAD:h * A_QK_PAD + A_NOPE] = (a * r * g_nope).astype(BF16)
            q_out[:, h * A_QK_PAD + A_NOPE:(h + 1) * A_QK_PAD] = _rope_a(b * r * g_rope, cos_t, sin_t).astype(BF16)

    kvl = kvl_ref[...]
    if norm_kv:
        ckv = kvl * lax.rsqrt(jnp.mean(kvl * kvl, axis=-1, keepdims=True) + EPS) * kvlg_ref[...]
    else:
        ckv = kvl
    ckv_out[...] = ckv
    ckv_b = ckv.astype(BF16)
    kn = _dot(ckv_b, wk_ref[...])
    v_out[...] = _dot(ckv_b, wv_ref[...]).astype(BF16)
    kpe = kpe_ref[...]
    sp = jnp.sum(kpe * kpe, axis=-1, keepdims=True)
    g_nope = kng_ref[:, :A_NOPE]
    g_rope = kng_ref[:, A_NOPE:]
    for h in range(A_HEADS):
        a = kn[:, h * A_NOPE:(h + 1) * A_NOPE]
        r = lax.rsqrt((jnp.sum(a * a, axis=-1, keepdims=True) + sp) * inv_qk + EPS)
        k_out[:, h * A_QK_PAD:h * A_QK_PAD + A_NOPE] = (a * r * g_nope).astype(BF16)
        k_out[:, h * A_QK_PAD + A_NOPE:(h + 1) * A_QK_PAD] = _rope_a(kpe * r * g_rope, cos_t, sin_t).astype(BF16)


def _amid_tokens(y, wq, wk, wv, qlg, kvlg, qng, kng, cos_t, sin_t):
    tm = TM_MID
    const = lambda i: (0, 0)
    rope_map = lambda i: (_rope_block_of_tile(i, tm), 0)
    return pl.pallas_call(
        functools.partial(_amid_kernel, do_q=True, norm_kv=True),
        grid=(ROWS // tm,),
        in_specs=[pl.BlockSpec((tm, Q_LORA), lambda i: (i, AB_QLAT // Q_LORA)),
                  pl.BlockSpec((tm, KV_LORA), lambda i: (i, AB_KVLAT // KV_LORA)),
                  pl.BlockSpec((tm, LANES), lambda i: (i, AB_KPE // LANES)),
                  pl.BlockSpec(wq.shape, const), pl.BlockSpec(wk.shape, const), pl.BlockSpec(wv.shape, const),
                  pl.BlockSpec(qlg.shape, const), pl.BlockSpec(kvlg.shape, const),
                  pl.BlockSpec(qng.shape, const), pl.BlockSpec(kng.shape, const),
                  pl.BlockSpec((tm, LANES), rope_map), pl.BlockSpec((tm, LANES), rope_map)],
        out_specs=[pl.BlockSpec((tm, A_HEADS * A_QK_PAD), lambda i: (i, 0)),
                   pl.BlockSpec((tm, A_HEADS * A_QK_PAD), lambda i: (i, 0)),
                   pl.BlockSpec((tm, A_HEADS * A_VDIM), lambda i: (i, 0)),
                   pl.BlockSpec((tm, KV_LORA), lambda i: (i, 0))],
        out_shape=[jax.ShapeDtypeStruct((ROWS, A_HEADS * A_QK_PAD), BF16),
                   jax.ShapeDtypeStruct((ROWS, A_HEADS * A_QK_PAD), BF16),
                   jax.ShapeDtypeStruct((ROWS, A_HEADS * A_VDIM), BF16),
                   jax.ShapeDtypeStruct((ROWS, KV_LORA), F32)],
        compiler_params=_cparams(("parallel",)),
        name="mla_qkv_prep",
    )(y, y, y, wq, wk, wv, qlg, kvlg, qng, kng, cos_t, sin_t)


def _amid_cache(ckv, kpe, wk, wv, kvlg, kng, cos_t, sin_t):
    rows = ckv.shape[0]
    tm = TM_MID
    const = lambda i: (0, 0)
    outs = pl.pallas_call(
        functools.partial(_amid_kernel, do_q=False, norm_kv=False),
        grid=(rows // tm,),
        in_specs=[pl.BlockSpec((tm, KV_LORA), lambda i: (i, 0)),
                  pl.BlockSpec((tm, LANES), lambda i: (i, 0)),
                  pl.BlockSpec(wk.shape, const), pl.BlockSpec(wv.shape, const),
                  pl.BlockSpec(kvlg.shape, const), pl.BlockSpec(kng.shape, const),
                  pl.BlockSpec((tm, LANES), const), pl.BlockSpec((tm, LANES), const)],
        out_specs=[pl.BlockSpec((tm, A_HEADS * A_QK_PAD), lambda i: (i, 0)),
                   pl.BlockSpec((tm, A_HEADS * A_VDIM), lambda i: (i, 0)),
                   pl.BlockSpec((tm, KV_LORA), lambda i: (i, 0))],
        out_shape=[jax.ShapeDtypeStruct((rows, A_HEADS * A_QK_PAD), BF16),
                   jax.ShapeDtypeStruct((rows, A_HEADS * A_VDIM), BF16),
                   jax.ShapeDtypeStruct((rows, KV_LORA), F32)],
        compiler_params=_cparams(("parallel",)),
        name="mla_cache_kv_prep",
    )(ckv, kpe, wk, wv, kvlg, kng, cos_t, sin_t)
    return outs[0], outs[1]


def _attn_a_kernel(*refs, n_src):
    q_ref = refs[0]
    k_refs = refs[1:1 + n_src]
    v_refs = refs[1 + n_src:1 + 2 * n_src]
    gate_ref, o_ref = refs[1 + 2 * n_src:]
    scale = A_QK ** -0.5
    q = q_ref[...]
    ss = [_dot_nt(q, k_ref[...]) * scale for k_ref in k_refs]
    m = ss[0].max(axis=-1, keepdims=True)
    for s in ss[1:]:
        m = jnp.maximum(m, s.max(axis=-1, keepdims=True))
    acc = None
    den = None
    for s, v_ref in zip(ss, v_refs):
        p = jnp.exp(s - m)
        d = jnp.sum(p, axis=-1, keepdims=True)
        o = _dot(p.astype(BF16), v_ref[...])
        acc = o if acc is None else acc + o
        den = d if den is None else den + d
    o_ref[...] = (acc / den * _silu(gate_ref[...])).astype(o_ref.dtype)


def _attn_a(q, k, v, y, mix, kc=None, vc=None):
    latent = kc is not None
    if latent:
        nb, t, tq = DEC_BATCH, DEC_SEQ, TQ_A
        row0 = CTX_ROWS
    else:
        nb, t, tq = BATCH, SEQ, SEQ
        row0 = 0
    nq = t // tq
    qrow = lambda b, h, i: row0 // tq + b * nq + i
    in_specs = [pl.BlockSpec((tq, A_QK_PAD), lambda b, h, i: (qrow(b, h, i), h))]
    args = [q]
    if latent:
        in_specs.append(pl.BlockSpec((PAST_LEN, A_QK_PAD), lambda b, h, i: (b, h)))
        args.append(kc)
    in_specs.append(pl.BlockSpec((t, A_QK_PAD), lambda b, h, i: (row0 // t + b, h)))
    args.append(k)
    if latent:
        in_specs.append(pl.BlockSpec((PAST_LEN, A_VDIM), lambda b, h, i: (b, h)))
        args.append(vc)
    in_specs.append(pl.BlockSpec((t, A_VDIM), lambda b, h, i: (row0 // t + b, h)))
    args.append(v)
    in_specs.append(pl.BlockSpec((tq, A_VDIM), lambda b, h, i: (qrow(b, h, i), AB_AGATE // A_VDIM + h)))
    args.append(y)
    in_specs.append(pl.BlockSpec(memory_space=pl.ANY))
    args.append(mix)
    n_in = len(args)

    def body(*refs):
        _attn_a_kernel(*refs[:n_in - 1], refs[n_in], n_src=2 if latent else 1)

    return pl.pallas_call(
        body,
        grid=(nb, A_HEADS, nq),
        in_specs=in_specs,
        out_specs=pl.BlockSpec((tq, A_VDIM), lambda b, h, i: (qrow(b, h, i), h)),
        out_shape=jax.ShapeDtypeStruct(mix.shape, mix.dtype),
        input_output_aliases={n_in - 1: 0},
        compiler_params=_cparams(("parallel", "parallel", "parallel")),
        name="mla_attention_latent" if latent else "mla_attention_context",
    )(*args)


def _hgrn_constants():
    c, nl = HG_CHUNK, HG_LEVELS
    t = np.arange(c)[:, None]
    u = np.arange(c)[None, :]
    m_f = np.zeros((nl + 2, c, c), np.float32)
    mask_f = np.zeros((nl, c, c), np.float32)
    for l in range(nl):
        half = c >> (l + 1)
        seg = 2 * half
        g0 = (t // seg) * seg
        later = (t - g0) >= half
        anchor = g0 + half - 1
        m_f[l] = np.where(later, (u > anchor) & (u <= t), (u > t) & (u <= anchor))
        mask_f[l] = ((u // seg) == (t // seg)) & later & ((u % seg) < half)
    m_f[nl] = u <= t
    m_f[nl + 1] = u > t
    m_b = m_f[:, ::-1, ::-1]
    mask_b = mask_f[:, ::-1, ::-1]
    to_m = lambda a: jnp.asarray(np.ascontiguousarray(a).reshape((nl + 2) * c, c), BF16)
    to_mask = lambda a: jnp.asarray(np.ascontiguousarray(a), F32)
    return to_m(m_f), to_m(m_b), to_mask(mask_f), to_mask(mask_b)


def _hgrn_kernel(*refs, t_len, zero_init, emit_state):
    c, nl = HG_CHUNK, HG_LEVELS
    n_chunks = t_len // c
    it = iter(refs)
    bq_ref, ff_ref, fb_ref, vi_ref, bg_ref, lb_ref, hg_ref = (next(it) for _ in range(7))
    if not zero_init:
        s0f_ref, s0b_ref = next(it), next(it)
    mf_ref, mb_ref, maskf_ref, maskb_ref = (next(it) for _ in range(4))
    o_ref = next(it)
    if emit_state:
        sf_ref, sb_ref = next(it), next(it)
    of_ref, st_ref = next(it), next(it)

    def chunk(r0, f_ref, lb, m_ref, mask_ref, last_row):
        x = f_ref[pl.ds(r0, c), :]
        kk = (1.0 - lb) * jax.nn.sigmoid(-x)
        lf = jnp.log1p(-kk)
        hi = lf.astype(BF16)
        r1 = lf - hi.astype(F32)
        mid = r1.astype(BF16)
        lo = (r1 - mid.astype(F32)).astype(BF16)
        parts = _dot(m_ref[...], jnp.concatenate([hi, mid, lo], axis=1))
        e = jnp.exp(parts[:, :LANES] + parts[:, LANES:2 * LANES] + parts[:, 2 * LANES:])
        q = _silu(bq_ref[pl.ds(r0, c), :])
        v = vi_ref[pl.ds(r0, c), :]
        v_b = v.astype(BF16)
        sc = jnp.zeros((c, c), F32)
        for l in range(nl):
            el = e[l * c:(l + 1) * c]
            sc = sc + mask_ref[l] * _dot_nt((q * el).astype(BF16), (kk * el).astype(BF16))
        e_b = e[nl * c:(nl + 1) * c]
        e_e = e[(nl + 1) * c:(nl + 2) * c]
        st = st_ref[...]
        o = (_dot(sc.astype(BF16), v_b) + jnp.sum(q * kk, axis=-1, keepdims=True) * v
             + _dot_nt((q * e_b).astype(BF16), st.astype(BF16)))
        tot = e_b[last_row:last_row + 1, :]
        st_ref[...] = tot * st + _dot(v.T.astype(BF16), (kk * e_e).astype(BF16))
        return o

    lb = lb_ref[...]
    if zero_init:
        st_ref[...] = jnp.zeros((B_DV, B_DK), F32)
    else:
        st_ref[...] = s0f_ref[...].T

    def fwd_body(i, carry):
        r0 = pl.multiple_of(i * c, c)
        of_ref[pl.ds(r0, c), :] = chunk(r0, ff_ref, lb[0:1, :], mf_ref, maskf_ref, c - 1)
        return carry

    lax.fori_loop(0, n_chunks, fwd_body, 0)
    if emit_state:
        sf_ref[...] = st_ref[...].T

    if zero_init:
        st_ref[...] = jnp.zeros((B_DV, B_DK), F32)
    else:
        st_ref[...] = s0b_ref[...].T
    hg = hg_ref[...]

    def bwd_body(i, carry):
        r0 = pl.multiple_of((n_chunks - 1 - i) * c, c)
        o = of_ref[pl.ds(r0, c), :] + chunk(r0, fb_ref, lb[1:2, :], mb_ref, maskb_ref, 0)
        o = o * lax.rsqrt(jnp.mean(o * o, axis=-1, keepdims=True) + EPS) * hg
        o_ref[pl.ds(r0, c), :] = (o * _silu(bg_ref[pl.ds(r0, c), :])).astype(o_ref.dtype)
        return carry

    lax.fori_loop(0, n_chunks, bwd_body, 0)
    if emit_state:
        sb_ref[...] = st_ref[...].T


def _hgrn(y, lb, hg, mix, consts, s0f=None, s0b=None):
    latent = s0f is not None
    if latent:
        nb, t, row0 = DEC_BATCH, DEC_SEQ, CTX_ROWS
    else:
        nb, t, row0 = BATCH, SEQ, 0
    rb = lambda b: row0 // t + b
    col = lambda off: (lambda b, h: (rb(b), off // LANES + h))
    const2 = lambda b, h: (0, 0)
    const3 = lambda b, h: (0, 0, 0)
    in_specs = [pl.BlockSpec((t, LANES), col(AB_BQ)), pl.BlockSpec((t, LANES), col(AB_BFF)),
                pl.BlockSpec((t, LANES), col(AB_BFB)), pl.BlockSpec((t, LANES), col(AB_BI)),
                pl.BlockSpec((t, LANES), col(AB_BGATE)),
                pl.BlockSpec((2, LANES), lambda b, h: (0, h)),
                pl.BlockSpec((1, LANES), const2)]
    args = [y, y, y, y, y, lb, hg]
    if latent:
        st_spec = pl.BlockSpec((None, None, B_DK, B_DV), lambda b, h: (b, h, 0, 0))
        in_specs += [st_spec, st_spec]
        args += [s0f, s0b]
    m_f, m_b, mask_f, mask_b = consts
    in_specs += [pl.BlockSpec(m_f.shape, const2), pl.BlockSpec(m_b.shape, const2),
                 pl.BlockSpec(mask_f.shape, const3), pl.BlockSpec(mask_b.shape, const3)]
    args += [m_f, m_b, mask_f, mask_b]
    in_specs.append(pl.BlockSpec(memory_space=pl.ANY))
    args.append(mix)
    n_in = len(args)
    out_specs = [pl.BlockSpec((t, LANES), lambda b, h: (rb(b), B_HEADS + h))]
    out_shape = [jax.ShapeDtypeStruct(mix.shape, mix.dtype)]
    if not latent:
        st_out = pl.BlockSpec((None, None, B_DK, B_DV), lambda b, h: (b, h, 0, 0))
        out_specs += [st_out, st_out]
        out_shape += [jax.ShapeDtypeStruct((nb, B_HEADS, B_DK, B_DV), F32)] * 2

    kern = functools.partial(_hgrn_kernel, t_len=t, zero_init=not latent, emit_state=not latent)

    def body(*refs):
        kern(*refs[:n_in - 1], *refs[n_in:])

    outs = pl.pallas_call(
        body,
        grid=(nb, B_HEADS),
        in_specs=in_specs,
        out_specs=out_specs,
        out_shape=out_shape,
        scratch_shapes=[pltpu.VMEM((t, B_DV), F32), pltpu.VMEM((B_DV, B_DK), F32)],
        input_output_aliases={n_in - 1: 0},
        compiler_params=_cparams(("parallel", "parallel")),
        name="hgrn2_latent" if latent else "hgrn2_context",
    )(*args)
    return outs


def _cmid_kernel(q_ref, k_ref, v_ref, qg_ref, kg_ref, cos_ref, sin_ref, q_out, kc_out, k_out, v_out):
    cos_t = cos_ref[...]
    sin_t = sin_ref[...]
    qg = qg_ref[...]
    kg = kg_ref[...]

    def norm(x, g):
        return x * lax.rsqrt(jnp.mean(x * x, axis=-1, keepdims=True) + EPS) * g

    def rope(x):
        return x * cos_t + pltpu.roll(x, C_HEAD_DIM // 2, 1) * sin_t

    for h in range(C_HEADS):
        sl = slice(h * C_HEAD_DIM, (h + 1) * C_HEAD_DIM)
        q_out[:, sl] = rope(norm(q_ref[:, sl], qg)).astype(BF16)
    for h in range(C_KV_HEADS):
        sl = slice(h * C_HEAD_DIM, (h + 1) * C_HEAD_DIM)
        kn = norm(k_ref[:, sl], kg)
        kc_out[:, sl] = kn
        k_out[:, sl] = rope(kn).astype(BF16)
    v_out[...] = v_ref[...].astype(BF16)


def _cmid(y, qg, kg, cos_t, sin_t):
    tm = TM_MID
    const = lambda i: (0, 0)
    rope_map = lambda i: (_rope_block_of_tile(i, tm), 0)
    return pl.pallas_call(
        _cmid_kernel,
        grid=(ROWS // tm,),
        in_specs=[pl.BlockSpec((tm, C_WIDTH), lambda i: (i, 0)),
                  pl.BlockSpec((tm, C_KV_WIDTH), lambda i: (i, C_WIDTH // C_KV_WIDTH)),
                  pl.BlockSpec((tm, C_KV_WIDTH), lambda i: (i, C_WIDTH // C_KV_WIDTH + 1)),
                  pl.BlockSpec((1, C_HEAD_DIM), const), pl.BlockSpec((1, C_HEAD_DIM), const),
                  pl.BlockSpec((tm, LANES), rope_map), pl.BlockSpec((tm, LANES), rope_map)],
        out_specs=[pl.BlockSpec((tm, C_WIDTH), lambda i: (i, 0)),
                   pl.BlockSpec((tm, C_KV_WIDTH), lambda i: (i, 0)),
                   pl.BlockSpec((tm, C_KV_WIDTH), lambda i: (i, 0)),
                   pl.BlockSpec((tm, C_KV_WIDTH), lambda i: (i, 0))],
        out_shape=[jax.ShapeDtypeStruct((ROWS, C_WIDTH), BF16),
                   jax.ShapeDtypeStruct((ROWS, C_KV_WIDTH), F32),
                   jax.ShapeDtypeStruct((ROWS, C_KV_WIDTH), BF16),
                   jax.ShapeDtypeStruct((ROWS, C_KV_WIDTH), BF16)],
        compiler_params=_cparams(("parallel",)),
        name="gqa_qkv_prep",
    )(y, y, y, qg, kg, cos_t, sin_t)


def _attn_c_kernel(*refs, band, tq, t_len):
    if band:
        q_ref, kc_ref, vc_ref, kl_ref, vl_ref, sink_ref, gate_ref, o_ref = refs
    else:
        q_ref, kc_ref, vc_ref, sink_ref, gate_ref, o_ref = refs
    scale = C_HEAD_DIM ** -0.5
    kc = kc_ref[...]
    vc = vc_ref[...]
    if band:
        i = pl.program_id(2)
        width = tq + 2 * WINDOW
        start = pl.multiple_of(jnp.clip(i * tq - WINDOW, 0, t_len - width), WINDOW)
        kb = kl_ref[pl.ds(start, width), :]
        vb = vl_ref[pl.ds(start, width), :]
        qpos = i * tq + lax.broadcasted_iota(jnp.int32, (tq, width), 0)
        kpos = start + lax.broadcasted_iota(jnp.int32, (tq, width), 1)
        in_band = jnp.abs(kpos - qpos) <= WINDOW
    for r in range(C_GROUP):
        sl = slice(r * C_HEAD_DIM, (r + 1) * C_HEAD_DIM)
        q = q_ref[:, sl]
        sink = sink_ref[r][:, :1]
        s_c = _dot_nt(q, kc) * scale
        m = jnp.maximum(s_c.max(axis=-1, keepdims=True), sink)
        if band:
            s_l = jnp.where(in_band, _dot_nt(q, kb) * scale, NEG_BIG)
            m = jnp.maximum(m, s_l.max(axis=-1, keepdims=True))
        p_c = jnp.exp(s_c - m)
        den = jnp.sum(p_c, axis=-1, keepdims=True) + jnp.exp(sink - m)
        acc = _dot(p_c.astype(BF16), vc)
        if band:
            p_l = jnp.exp(s_l - m)
            den = den + jnp.sum(p_l, axis=-1, keepdims=True)
            acc = acc + _dot(p_l.astype(BF16), vb)
        o_ref[:, sl] = (acc / den * _silu(gate_ref[:, sl])).astype(o_ref.dtype)


def _attn_c(q, k, v, y, sink, mix, kc=None, vc=None):
    latent = kc is not None
    gw = C_GROUP * C_HEAD_DIM
    if latent:
        nb, t, tq, row0 = DEC_BATCH, DEC_SEQ, TQ_C, CTX_ROWS
    else:
        nb, t, tq, row0 = BATCH, SEQ, SEQ, 0
    nq = t // tq
    qrow = lambda b, g, i: row0 // tq + b * nq + i
    own_kv = pl.BlockSpec((t, C_HEAD_DIM), lambda b, g, i: (row0 // t + b, g))
    in_specs = [pl.BlockSpec((tq, gw), lambda b, g, i: (qrow(b, g, i), g))]
    args = [q]
    if latent:
        ctx_kv = pl.BlockSpec((PAST_LEN, C_HEAD_DIM), lambda b, g, i: (b, g))
        in_specs += [ctx_kv, ctx_kv, own_kv, own_kv]
        args += [kc, vc, k, v]
    else:
        in_specs += [own_kv, own_kv]
        args += [k, v]
    in_specs.append(pl.BlockSpec((C_GROUP, 1, LANES), lambda b, g, i: (g, 0, 0)))
    args.append(sink)
    in_specs.append(pl.BlockSpec((tq, gw), lambda b, g, i: (qrow(b, g, i), (C_WIDTH + 2 * C_KV_WIDTH) // gw + g)))
    args.append(y)
    in_specs.append(pl.BlockSpec(memory_space=pl.ANY))
    args.append(mix)
    n_in = len(args)
    kern = functools.partial(_attn_c_kernel, band=latent, tq=tq, t_len=t)

    def body(*refs):
        kern(*refs[:n_in - 1], refs[n_in])

    return pl.pallas_call(
        body,
        grid=(nb, C_KV_HEADS, nq),
        in_specs=in_specs,
        out_specs=pl.BlockSpec((tq, gw), lambda b, g, i: (qrow(b, g, i), g)),
        out_shape=jax.ShapeDtypeStruct(mix.shape, mix.dtype),
        input_output_aliases={n_in - 1: 0},
        compiler_params=_cparams(("parallel", "parallel", "parallel")),
        name="gqa_attention_latent" if latent else "gqa_attention_context",
    )(*args)


def _axial_angles(n_tokens, rot_dim):
    rows = n_tokens // GRID_W
    row = jnp.repeat(jnp.arange(rows, dtype=F32), GRID_W)
    col = jnp.tile(jnp.arange(GRID_W, dtype=F32), rows)
    n_freq = rot_dim // 4
    inv = ROPE_BASE ** (-jnp.arange(n_freq, dtype=F32) / n_freq)
    return jnp.concatenate([row[:, None] * inv, col[:, None] * inv], axis=-1)


def _rope_tables(rot_dim, tm):
    ang = _axial_angles(DEC_SEQ, rot_dim)
    cos, sin = jnp.cos(ang), jnp.sin(ang)
    pad = LANES - rot_dim
    cos_t = jnp.concatenate([cos, cos, jnp.ones((DEC_SEQ, pad), F32)], axis=-1)
    sin_t = jnp.concatenate([-sin, sin, jnp.zeros((DEC_SEQ, pad), F32)], axis=-1)
    cos_t = jnp.concatenate([jnp.ones((tm, LANES), F32), cos_t], axis=0)
    sin_t = jnp.concatenate([jnp.zeros((tm, LANES), F32), sin_t], axis=0)
    return cos_t, sin_t


def _lower_bounds(lb_logits):
    p = jax.nn.softmax(lb_logits.astype(F32), axis=0)
    return jnp.cumsum(p, axis=0) - p[0:1]


def _pad_head_gain(g):
    return jnp.concatenate([g, jnp.zeros((A_QK_PAD - A_QK,), F32)])[None, :]


def kernel(x_prompt, x_sample, cache_ckv, cache_kpe, state_hgrn_fwd, state_hgrn_bwd, cache_k_c, cache_v_c, c, c_ctx,
           mod_w_ab, mod_b_ab, norm_ab, w_in_ab, q_lora_norm, kv_lora_norm, w_q_up, w_kv_up, q_norm_ab, k_norm_ab,
           hgrn_lb_logits, hgrn_out_norm, w_out_ab, mod_w_c, mod_b_c, norm_c, w_in_c, q_norm_c, k_norm_c, sink_c,
           w_out_c):
    x = jnp.concatenate([x_prompt.reshape(CTX_ROWS, D_MODEL), x_sample.reshape(LAT_ROWS, D_MODEL)], axis=0)
    cond8 = jnp.concatenate([c_ctx[None, :], c, jnp.zeros((N_COND - 1 - DEC_BATCH, D_MODEL), F32)], axis=0)
    mods_ab = _modulation(cond8, mod_w_ab, mod_b_ab)
    mods_c = _modulation(cond8, mod_w_c, mod_b_c)
    lower = _lower_bounds(hgrn_lb_logits)
    cos_a, sin_a = _rope_tables(A_ROPE, TM_MID)
    cos_c, sin_c = _rope_tables(C_HEAD_DIM, TM_MID)
    hg_consts = _hgrn_constants()

    new_ckv, new_kpe, new_sf, new_sb, new_kc, new_vc = [], [], [], [], [], []
    for layer in range(DEPTH):
        j = layer // 2
        if layer % 2 == 0:
            mod = mods_ab[j].reshape(3 * N_COND, 1, D_MODEL)
            w = w_in_ab[j]
            w_in = jnp.concatenate([w[:, :768], w[:, 832:], w[:, 768:832],
                                    jnp.zeros((D_MODEL, AB_N - 6976), F32)], axis=1).astype(BF16)
            wq = jnp.pad(w_q_up[j].reshape(Q_LORA, A_HEADS, A_QK),
                         ((0, 0), (0, 0), (0, A_QK_PAD - A_QK))).reshape(Q_LORA, A_HEADS * A_QK_PAD).astype(BF16)
            wkv = w_kv_up[j].reshape(KV_LORA, A_HEADS, A_NOPE + A_VDIM)
            wk = wkv[:, :, :A_NOPE].reshape(KV_LORA, A_HEADS * A_NOPE).astype(BF16)
            wv = wkv[:, :, A_NOPE:].reshape(KV_LORA, A_HEADS * A_VDIM).astype(BF16)
            qlg, kvlg = q_lora_norm[j][None, :], kv_lora_norm[j][None, :]
            qng, kng = _pad_head_gain(q_norm_ab[j]), _pad_head_gain(k_norm_ab[j])

            y = _in_proj(x, norm_ab[j][None, :], mod, w_in, TN_IN_AB)
            q, k, v, ckv = _amid_tokens(y, wq, wk, wv, qlg, kvlg, qng, kng, cos_a, sin_a)
            kpe_cache = jnp.pad(cache_kpe[:, j].reshape(DEC_BATCH * PAST_LEN, A_ROPE), ((0, 0), (0, LANES - A_ROPE)))
            kc, vc = _amid_cache(cache_ckv[:, j].reshape(DEC_BATCH * PAST_LEN, KV_LORA), kpe_cache, wk, wv, kvlg, kng,
                                 cos_a, sin_a)
            mix = jnp.zeros((ROWS, D_MODEL), BF16)
            mix = _attn_a(q, k, v, y, mix)
            mix = _attn_a(q, k, v, y, mix, kc=kc, vc=vc)
            hg = hgrn_out_norm[j][None, :]
            mix, sf, sb = _hgrn(y, lower[j], hg, mix, hg_consts)
            (mix,) = _hgrn(y, lower[j], hg, mix, hg_consts, s0f=state_hgrn_fwd[:, j], s0b=state_hgrn_bwd[:, j])
            x = _out_proj(mix, w_out_ab[j].astype(BF16), x, mod)

            new_ckv.append(ckv[:CTX_ROWS].reshape(BATCH, SEQ, KV_LORA))
            new_kpe.append(y[:CTX_ROWS, AB_KPE:AB_KPE + A_ROPE].reshape(BATCH, SEQ, A_ROPE))
            new_sf.append(sf)
            new_sb.append(sb)
        else:
            mod = mods_c[j].reshape(3 * N_COND, 1, D_MODEL)
            y = _in_proj(x, norm_c[j][None, :], mod, w_in_c[j].astype(BF16), TN_IN_C)
            q, kn, k, v = _cmid(y, q_norm_c[j][None, :], k_norm_c[j][None, :], cos_c, sin_c)
            kc = cache_k_c[:, j].reshape(DEC_BATCH * PAST_LEN, C_KV_WIDTH).astype(BF16)
            vc = cache_v_c[:, j].reshape(DEC_BATCH * PAST_LEN, C_KV_WIDTH).astype(BF16)
            sink = jnp.broadcast_to(sink_c[j][:, None, None], (C_HEADS, 1, LANES))
            mix = jnp.zeros((ROWS, D_MODEL), BF16)
            mix = _attn_c(q, k, v, y, sink, mix)
            mix = _attn_c(q, k, v, y, sink, mix, kc=kc, vc=vc)
            x = _out_proj(mix, w_out_c[j].astype(BF16), x, mod)

            new_kc.append(kn[:CTX_ROWS].reshape(BATCH, SEQ, C_KV_HEADS, C_HEAD_DIM))
            new_vc.append(y[:CTX_ROWS, C_WIDTH + C_KV_WIDTH:C_WIDTH + 2 * C_KV_WIDTH]
                          .reshape(BATCH, SEQ, C_KV_HEADS, C_HEAD_DIM))

    return (x[:CTX_ROWS].reshape(BATCH, SEQ, D_MODEL), x[CTX_ROWS:].reshape(DEC_BATCH, DEC_SEQ, D_MODEL),
            jnp.stack(new_ckv, axis=1), jnp.stack(new_kpe, axis=1), jnp.stack(new_sf, axis=1),
            jnp.stack(new_sb, axis=1), jnp.stack(new_kc, axis=1), jnp.stack(new_vc, axis=1))
```

```python
import functools

import numpy as np
import jax
import jax.numpy as jnp
from jax import lax
from jax.experimental import pallas as pl
from jax.experimental.pallas import tpu as pltpu

F32 = jnp.float32
BF16 = jnp.bfloat16

D_MODEL = 2048
BATCH = 16
SEQ = 256
DEPTH = 4
DEC_BATCH = 4
DEC_SEQ = 2048
PAST_LEN = 256
GRID_W = 64
A_HEADS = 8
A_NOPE = 128
A_ROPE = 64
A_VDIM = 128
A_QK = A_NOPE + A_ROPE
A_QK_PAD = 256
Q_LORA = 512
KV_LORA = 256
B_HEADS = 8
B_DK = 128
B_DV = 128
C_HEADS = 16
C_KV_HEADS = 4
C_GROUP = C_HEADS // C_KV_HEADS
C_HEAD_DIM = 128
C_WIDTH = C_HEADS * C_HEAD_DIM
C_KV_WIDTH = C_KV_HEADS * C_HEAD_DIM
WINDOW = 128
ROPE_BASE = 10000.0
EPS = 1e-6
NEG_BIG = -1e30
LOG2_E = 1.4426950408889634

LANES = 128
CTX_ROWS = BATCH * SEQ
LAT_ROWS = DEC_BATCH * DEC_SEQ
ROWS = CTX_ROWS + LAT_ROWS
N_COND = 8

AB_QLAT = 0
AB_KVLAT = 512
AB_AGATE = 768
AB_BQ = 1792
AB_BFF = 2816
AB_BFB = 3840
AB_BI = 4864
AB_BGATE = 5888
AB_KPE = 6912
AB_N = 7040
C_N = 2 * C_WIDTH + 2 * C_KV_WIDTH

TM_PROJ = 1024
TN_IN_AB = 640
TN_IN_C = 640
TN_OUT = 512
TM_MID = 512
TQ_A = 512
TQ_C = 256
HG_CHUNK = 128
HG_LEVELS = 7
HG_UNROLL = 2
VMEM_LIMIT = 56 * 1024 * 1024


def _cparams(sem):
    return pltpu.CompilerParams(dimension_semantics=sem, vmem_limit_bytes=VMEM_LIMIT)


def _silu(x):
    return x * jax.nn.sigmoid(x)


def _dot(a, b):
    return jnp.dot(a, b, preferred_element_type=F32)


def _dot_nt(a, b):
    return lax.dot_general(a, b, (((1,), (1,)), ((), ())), preferred_element_type=F32)


def _cond_of_tile(i, tm):
    n_ctx = CTX_ROWS // tm
    per_batch = DEC_SEQ // tm
    return jnp.where(i < n_ctx, 0, 1 + (i - n_ctx) // per_batch)


def _rope_block_of_tile(i, tm):
    n_ctx = CTX_ROWS // tm
    per_batch = DEC_SEQ // tm
    return jnp.where(i < n_ctx, 0, 1 + (i - n_ctx) % per_batch)


def _mod_kernel(c_ref, w_ref, b_ref, o_ref):
    a = _silu(c_ref[...]).astype(BF16)
    o_ref[...] = _dot(a, w_ref[...].astype(BF16)) + b_ref[...]


def _modulation(cond8, w_mod, b_mod):
    n = w_mod.shape[0]
    tn = 1024
    return pl.pallas_call(
        _mod_kernel,
        grid=(n, 3 * D_MODEL // tn),
        in_specs=[pl.BlockSpec((N_COND, D_MODEL), lambda l, j: (0, 0)),
                  pl.BlockSpec((None, D_MODEL, tn), lambda l, j: (l, 0, j)),
                  pl.BlockSpec((None, 1, tn), lambda l, j: (l, 0, j))],
        out_specs=pl.BlockSpec((None, N_COND, tn), lambda l, j: (l, 0, j)),
        out_shape=jax.ShapeDtypeStruct((n, N_COND, 3 * D_MODEL), F32),
        compiler_params=_cparams(("parallel", "parallel")),
        name="adaln_mod",
    )(cond8, w_mod, b_mod.reshape(n, 1, 3 * D_MODEL))


def _in_kernel(x_ref, g_ref, sh_ref, sc_ref, w_ref, o_ref, h_ref):
    @pl.when(pl.program_id(1) == 0)
    def _():
        x = x_ref[...]
        r = lax.rsqrt(jnp.mean(x * x, axis=-1, keepdims=True) + EPS)
        h = (x * r * g_ref[...]) * (1.0 + sc_ref[...]) + sh_ref[...]
        h_ref[...] = h.astype(BF16)

    o_ref[...] = _dot(h_ref[...], w_ref[...])


def _in_proj(x, g, mod, w, tn):
    n = w.shape[1]
    tm = TM_PROJ
    return pl.pallas_call(
        _in_kernel,
        grid=(ROWS // tm, n // tn),
        in_specs=[pl.BlockSpec((tm, D_MODEL), lambda i, j: (i, 0)),
                  pl.BlockSpec((1, D_MODEL), lambda i, j: (0, 0)),
                  pl.BlockSpec((None, 1, D_MODEL), lambda i, j: (_cond_of_tile(i, tm) * 3, 0, 0)),
                  pl.BlockSpec((None, 1, D_MODEL), lambda i, j: (_cond_of_tile(i, tm) * 3 + 1, 0, 0)),
                  pl.BlockSpec((D_MODEL, tn), lambda i, j: (0, j))],
        out_specs=pl.BlockSpec((tm, tn), lambda i, j: (i, j)),
        out_shape=jax.ShapeDtypeStruct((ROWS, n), F32),
        scratch_shapes=[pltpu.VMEM((tm, D_MODEL), BF16)],
        compiler_params=_cparams(("parallel", "arbitrary")),
        name="norm_mod_in_proj",
    )(x, g, mod, mod, w)


def _out_kernel(m_ref, w_ref, x_ref, gt_ref, o_ref):
    o_ref[...] = x_ref[...] + gt_ref[...] * _dot(m_ref[...], w_ref[...])


def _out_proj(mix, w, x, mod):
    tm, tn = TM_PROJ, TN_OUT
    return pl.pallas_call(
        _out_kernel,
        grid=(ROWS // tm, D_MODEL // tn),
        in_specs=[pl.BlockSpec((tm, D_MODEL), lambda i, j: (i, 0)),
                  pl.BlockSpec((D_MODEL, tn), lambda i, j: (0, j)),
                  pl.BlockSpec((tm, tn), lambda i, j: (i, j)),
                  pl.BlockSpec((None, 1, tn), lambda i, j: (_cond_of_tile(i, tm) * 3 + 2, 0, j))],
        out_specs=pl.BlockSpec((tm, tn), lambda i, j: (i, j)),
        out_shape=jax.ShapeDtypeStruct((ROWS, D_MODEL), F32),
        compiler_params=_cparams(("parallel", "parallel")),
        name="out_proj_residual",
    )(mix, w, x, mod)


def _rope_a(x, cos_t, sin_t):
    return x * cos_t + (pltpu.roll(x, 32, 1) + pltpu.roll(x, 96, 1)) * sin_t


def _amid_kernel(*refs, do_q, norm_kv):
    if do_q:
        (ql_ref, kvl_ref, kpe_ref, wq_ref, wk_ref, wv_ref, qlg_ref, kvlg_ref, qng_ref, kng_ref, cos_ref, sin_ref,
         q_out, k_out, v_out, ckv_out) = refs
    else:
        (kvl_ref, kpe_ref, wk_ref, wv_ref, kvlg_ref, kng_ref, cos_ref, sin_ref, k_out, v_out, ckv_out) = refs
    cos_t = cos_ref[...]
    sin_t = sin_ref[...]
    inv_qk = 1.0 / A_QK

    if do_q:
        ql = ql_ref[...]
        qn = ql * lax.rsqrt(jnp.mean(ql * ql, axis=-1, keepdims=True) + EPS) * qlg_ref[...]
        qu = _dot(qn.astype(BF16), wq_ref[...])
        g_nope = qng_ref[:, :A_NOPE]
        g_rope = qng_ref[:, A_NOPE:]
        for h in range(A_HEADS):
            a = qu[:, h * A_QK_PAD:h * A_QK_PAD + A_NOPE]
            b = qu[:, h * A_QK_PAD + A_NOPE:(h + 1) * A_QK_PAD]
            ss = jnp.sum(a * a, axis=-1, keepdims=True) + jnp.sum(b * b, axis=-1, keepdims=True)
            r = lax.rsqrt(ss * inv_qk + EPS)
            q_out[:, h * A_QK_PAD:h * A_QK_PAD + A_NOPE] = (a * r * g_nope).astype(BF16)
            q_out[:, h * A_QK_PAD + A_NOPE:(h + 1) * A_QK_PAD] = _rope_a(b * r * g_rope, cos_t, sin_t).astype(BF16)

    kvl = kvl_ref[...]
    if norm_kv:
        ckv = kvl * lax.rsqrt(jnp.mean(kvl * kvl, axis=-1, keepdims=True) + EPS) * kvlg_ref[...]
    else:
        ckv = kvl
    ckv_out[...] = ckv
    ckv_b = ckv.astype(BF16)
    kn = _dot(ckv_b, wk_ref[...])
    v_out[...] = _dot(ckv_b, wv_ref[...]).astype(BF16)
    kpe = kpe_ref[...]
    sp = jnp.sum(kpe * kpe, axis=-1, keepdims=True)
    g_nope = kng_ref[:, :A_NOPE]
    g_rope = kng_ref[:, A_NOPE:]
    for h in range(A_HEADS):
        a = kn[:, h * A_NOPE:(h + 1) * A_NOPE]
        r = lax.rsqrt((jnp.sum(a * a, axis=-1, keepdims=True) + sp) * inv_qk + EPS)
        k_out[:, h * A_QK_PAD:h * A_QK_PAD + A_NOPE] = (a * r * g_nope).astype(BF16)
        k_out[:, h * A_QK_PAD + A_NOPE:(h + 1) * A_QK_PAD] = _rope_a(kpe * r * g_rope, cos_t, sin_t).astype(BF16)


def _amid_tokens(y, wq, wk, wv, qlg, kvlg, qng, kng, cos_t, sin_t):
    tm = TM_MID
    const = lambda i: (0, 0)
    rope_map = lambda i: (_rope_block_of_tile(i, tm), 0)
    return pl.pallas_call(
        functools.partial(_amid_kernel, do_q=True, norm_kv=True),
        grid=(ROWS // tm,),
        in_specs=[pl.BlockSpec((tm, Q_LORA), lambda i: (i, AB_QLAT // Q_LORA)),
                  pl.BlockSpec((tm, KV_LORA), lambda i: (i, AB_KVLAT // KV_LORA)),
                  pl.BlockSpec((tm, LANES), lambda i: (i, AB_KPE // LANES)),
                  pl.BlockSpec(wq.shape, const), pl.BlockSpec(wk.shape, const), pl.BlockSpec(wv.shape, const),
                  pl.BlockSpec(qlg.shape, const), pl.BlockSpec(kvlg.shape, const),
                  pl.BlockSpec(qng.shape, const), pl.BlockSpec(kng.shape, const),
                  pl.BlockSpec((tm, LANES), rope_map), pl.BlockSpec((tm, LANES), rope_map)],
        out_specs=[pl.BlockSpec((tm, A_HEADS * A_QK_PAD), lambda i: (i, 0)),
                   pl.BlockSpec((tm, A_HEADS * A_QK_PAD), lambda i: (i, 0)),
                   pl.BlockSpec((tm, A_HEADS * A_VDIM), lambda i: (i, 0)),
                   pl.BlockSpec((tm, KV_LORA), lambda i: (i, 0))],
        out_shape=[jax.ShapeDtypeStruct((ROWS, A_HEADS * A_QK_PAD), BF16),
                   jax.ShapeDtypeStruct((ROWS, A_HEADS * A_QK_PAD), BF16),
                   jax.ShapeDtypeStruct((ROWS, A_HEADS * A_VDIM), BF16),
                   jax.ShapeDtypeStruct((ROWS, KV_LORA), F32)],
        compiler_params=_cparams(("parallel",)),
        name="mla_qkv_prep",
    )(y, y, y, wq, wk, wv, qlg, kvlg, qng, kng, cos_t, sin_t)


def _amid_cache(ckv, kpe, wk, wv, kvlg, kng, cos_t, sin_t):
    rows = ckv.shape[0]
    tm = TM_MID
    const = lambda i: (0, 0)
    outs = pl.pallas_call(
        functools.partial(_amid_kernel, do_q=False, norm_kv=False),
        grid=(rows // tm,),
        in_specs=[pl.BlockSpec((tm, KV_LORA), lambda i: (i, 0)),
                  pl.BlockSpec((tm, LANES), lambda i: (i, 0)),
                  pl.BlockSpec(wk.shape, const), pl.BlockSpec(wv.shape, const),
                  pl.BlockSpec(kvlg.shape, const), pl.BlockSpec(kng.shape, const),
                  pl.BlockSpec((tm, LANES), const), pl.BlockSpec((tm, LANES), const)],
        out_specs=[pl.BlockSpec((tm, A_HEADS * A_QK_PAD), lambda i: (i, 0)),
                   pl.BlockSpec((tm, A_HEADS * A_VDIM), lambda i: (i, 0)),
                   pl.BlockSpec((tm, KV_LORA), lambda i: (i, 0))],
        out_shape=[jax.ShapeDtypeStruct((rows, A_HEADS * A_QK_PAD), BF16),
                   jax.ShapeDtypeStruct((rows, A_HEADS * A_VDIM), BF16),
                   jax.ShapeDtypeStruct((rows, KV_LORA), F32)],
        compiler_params=_cparams(("parallel",)),
        name="mla_cache_kv_prep",
    )(ckv, kpe, wk, wv, kvlg, kng, cos_t, sin_t)
    return outs[0], outs[1]


def _attn_a_kernel(*refs, n_src):
    q_ref = refs[0]
    k_refs = refs[1:1 + n_src]
    v_refs = refs[1 + n_src:1 + 2 * n_src]
    gate_ref, o_ref = refs[1 + 2 * n_src:]
    scale = A_QK ** -0.5
    q = q_ref[...]
    ss = [_dot_nt(q, k_ref[...]) * scale for k_ref in k_refs]
    m = ss[0].max(axis=-1, keepdims=True)
    for s in ss[1:]:
        m = jnp.maximum(m, s.max(axis=-1, keepdims=True))
    acc = None
    den = None
    for s, v_ref in zip(ss, v_refs):
        p = jnp.exp(s - m)
        d = jnp.sum(p, axis=-1, keepdims=True)
        o = _dot(p.astype(BF16), v_ref[...])
        acc = o if acc is None else acc + o
        den = d if den is None else den + d
    o_ref[...] = (acc / den * _silu(gate_ref[...])).astype(o_ref.dtype)


def _attn_a(q, k, v, y, mix, kc=None, vc=None):
    latent = kc is not None
    if latent:
        nb, t, tq = DEC_BATCH, DEC_SEQ, TQ_A
        row0 = CTX_ROWS
    else:
        nb, t, tq = BATCH, SEQ, SEQ
        row0 = 0
    nq = t // tq
    qrow = lambda b, h, i: row0 // tq + b * nq + i
    in_specs = [pl.BlockSpec((tq, A_QK_PAD), lambda b, h, i: (qrow(b, h, i), h))]
    args = [q]
    if latent:
        in_specs.append(pl.BlockSpec((PAST_LEN, A_QK_PAD), lambda b, h, i: (b, h)))
        args.append(kc)
    in_specs.append(pl.BlockSpec((t, A_QK_PAD), lambda b, h, i: (row0 // t + b, h)))
    args.append(k)
    if latent:
        in_specs.append(pl.BlockSpec((PAST_LEN, A_VDIM), lambda b, h, i: (b, h)))
        args.append(vc)
    in_specs.append(pl.BlockSpec((t, A_VDIM), lambda b, h, i: (row0 // t + b, h)))
    args.append(v)
    in_specs.append(pl.BlockSpec((tq, A_VDIM), lambda b, h, i: (qrow(b, h, i), AB_AGATE // A_VDIM + h)))
    args.append(y)
    in_specs.append(pl.BlockSpec(memory_space=pl.ANY))
    args.append(mix)
    n_in = len(args)

    def body(*refs):
        _attn_a_kernel(*refs[:n_in - 1], refs[n_in], n_src=2 if latent else 1)

    return pl.pallas_call(
        body,
        grid=(nb, A_HEADS, nq),
        in_specs=in_specs,
        out_specs=pl.BlockSpec((tq, A_VDIM), lambda b, h, i: (qrow(b, h, i), h)),
        out_shape=jax.ShapeDtypeStruct(mix.shape, mix.dtype),
        input_output_aliases={n_in - 1: 0},
        compiler_params=_cparams(("parallel", "parallel", "parallel")),
        name="mla_attention_latent" if latent else "mla_attention_context",
    )(*args)


def _hgrn_constants():
    c, nl = HG_CHUNK, HG_LEVELS
    t = np.arange(c)[:, None]
    u = np.arange(c)[None, :]
    tri_f = (u <= t).astype(np.float32)
    mask_f = np.zeros((nl, c, c), np.float32)
    for l in range(nl):
        half = c >> (l + 1)
        seg = 2 * half
        mask_f[l] = ((u // seg) == (t // seg)) & ((t % seg) >= half) & ((u % seg) < half)
    tri_b = tri_f[::-1, ::-1]
    mask_b = mask_f[:, ::-1, ::-1]
    to_tri = lambda a: jnp.asarray(np.concatenate([a, a, a], axis=1), BF16)
    to_mask = lambda a: jnp.asarray(np.ascontiguousarray(a), F32)
    return to_tri(tri_f), to_tri(tri_b), to_mask(mask_f), to_mask(mask_b)


def _hgrn_decays(x, lb, tri3):
    kk = (1.0 - lb) * jax.nn.sigmoid(-x)
    lf = jnp.log1p(-kk) * LOG2_E
    hi = lf.astype(BF16)
    r1 = lf - hi.astype(F32)
    mid = r1.astype(BF16)
    lo = (r1 - mid.astype(F32)).astype(BF16)
    return kk, lf, _dot(tri3, jnp.concatenate([hi, mid, lo], axis=0))


def _hgrn_level_arg(l, cs, lf, forward):
    c = HG_CHUNK
    nv = c // 8
    half = c >> (l + 1)
    cs3 = cs.reshape(nv, 8, LANES)
    sub = lax.broadcasted_iota(jnp.int32, (nv, 8, LANES), 1)

    def in_vreg_row(r):
        return jnp.broadcast_to(cs3[:, r:r + 1, :], (nv, 8, LANES))

    if half >= 8:
        m = half // 8
        nseg = nv // (2 * m)
        edge = cs3[:, 7:8, :] if forward else cs3[:, 0:1, :]
        e4 = edge.reshape(nseg, 2 * m, 1, LANES)
        a = e4[:, m - 1:m] if forward else e4[:, m:m + 1]
        anchor = jnp.broadcast_to(a, (nseg, 2 * m, 8, LANES)).reshape(c, LANES)
    elif half == 4:
        anchor = in_vreg_row(3 if forward else 4).reshape(c, LANES)
    elif half == 2:
        r0, r1 = (1, 5) if forward else (2, 6)
        anchor = jnp.where(sub < 4, in_vreg_row(r0), in_vreg_row(r1)).reshape(c, LANES)
    else:
        is_query = (sub % 2 == 1) if forward else (sub % 2 == 0)
        return jnp.where(is_query.reshape(c, LANES), lf, 0.0)
    return -jnp.abs(cs - anchor)


def _hgrn_kernel(*refs, t_len, zero_init, emit_state):
    c = HG_CHUNK
    n_chunks = t_len // c
    it = iter(refs)
    bq_ref, ff_ref, fb_ref, vi_ref, bg_ref, lb_ref, hg_ref = (next(it) for _ in range(7))
    if not zero_init:
        s0f_ref, s0b_ref = next(it), next(it)
    trif_ref, trib_ref, maskf_ref, maskb_ref = (next(it) for _ in range(4))
    o_ref = next(it)
    if emit_state:
        sf_ref, sb_ref = next(it), next(it)
    of_ref, ob_ref, stf_ref, stb_ref = (next(it) for _ in range(4))

    if zero_init:
        stf_ref[...] = jnp.zeros((B_DV, B_DK), F32)
        stb_ref[...] = jnp.zeros((B_DV, B_DK), F32)
    else:
        stf_ref[...] = s0f_ref[...].T
        stb_ref[...] = s0b_ref[...].T
    lb = lb_ref[...]

    nl = HG_LEVELS
    dirs = ((ff_ref, lb[0:1, :], trif_ref, maskf_ref, stf_ref, of_ref, True),
            (fb_ref, lb[1:2, :], trib_ref, maskb_ref, stb_ref, ob_ref, False))

    def body(i, carry):
        chains = []
        for f_ref, lb_row, tri_ref, mask_ref, st_ref, out_ref, forward in dirs:
            for u in range(HG_UNROLL):
                k = i * HG_UNROLL + u
                r0 = pl.multiple_of((k if forward else n_chunks - 1 - k) * c, c)
                chains.append(dict(rows=pl.ds(r0, c), f_ref=f_ref, lb=lb_row, tri_ref=tri_ref, mask_ref=mask_ref,
                                   st_ref=st_ref, out_ref=out_ref, forward=forward))
        for ch in chains:
            ch["kk"], ch["lf"], ch["cs"] = _hgrn_decays(ch["f_ref"][ch["rows"], :], ch["lb"], ch["tri_ref"][...])
            ch["q"] = _silu(bq_ref[ch["rows"], :])
            ch["v"] = vi_ref[ch["rows"], :]
            ch["sc"] = jnp.zeros((c, c), F32)
        for l in range(nl):
            for ch in chains:
                el = jnp.exp2(_hgrn_level_arg(l, ch["cs"], ch["lf"], ch["forward"]))
                ch["sc"] = ch["sc"] + ch["mask_ref"][l] * _dot_nt((ch["q"] * el).astype(BF16),
                                                                  (ch["kk"] * el).astype(BF16))
        for ch in chains:
            cs, q, kk, v = ch["cs"], ch["q"], ch["kk"], ch["v"]
            end = cs[c - 1:c, :] if ch["forward"] else cs[0:1, :]
            ch["end"] = end
            ch["q_in"] = (q * jnp.exp2(cs)).astype(BF16)
            ch["upd"] = _dot(v.T.astype(BF16), (kk * jnp.exp2(-jnp.abs(cs - end))).astype(BF16))
            ch["o"] = _dot(ch["sc"].astype(BF16), v.astype(BF16)) + jnp.sum(q * kk, axis=-1, keepdims=True) * v
        for ch in chains:
            st = ch["st_ref"][...]
            ch["out_ref"][ch["rows"], :] = ch["o"] + _dot_nt(ch["q_in"], st.astype(BF16))
            ch["st_ref"][...] = jnp.exp2(ch["end"]) * st + ch["upd"]
        return carry

    lax.fori_loop(0, n_chunks // HG_UNROLL, body, 0)
    if emit_state:
        sf_ref[...] = stf_ref[...].T
        sb_ref[...] = stb_ref[...].T

    hg = hg_ref[...]

    def finish(i, carry):
        rows = pl.ds(pl.multiple_of(i * c, c), c)
        o = of_ref[rows, :] + ob_ref[rows, :]
        o = o * lax.rsqrt(jnp.mean(o * o, axis=-1, keepdims=True) + EPS) * hg
        o_ref[rows, :] = (o * _silu(bg_ref[rows, :])).astype(o_ref.dtype)
        return carry

    lax.fori_loop(0, n_chunks, finish, 0)


def _hgrn(y, lb, hg, mix, consts, s0f=None, s0b=None):
    latent = s0f is not None
    if latent:
        nb, t, row0 = DEC_BATCH, DEC_SEQ, CTX_ROWS
    else:
        nb, t, row0 = BATCH, SEQ, 0
    rb = lambda b: row0 // t + b
    col = lambda off: (lambda b, h: (rb(b), off // LANES + h))
    const2 = lambda b, h: (0, 0)
    const3 = lambda b, h: (0, 0, 0)
    in_specs = [pl.BlockSpec((t, LANES), col(AB_BQ)), pl.BlockSpec((t, LANES), col(AB_BFF)),
                pl.BlockSpec((t, LANES), col(AB_BFB)), pl.BlockSpec((t, LANES), col(AB_BI)),
                pl.BlockSpec((t, LANES), col(AB_BGATE)),
                pl.BlockSpec((2, LANES), lambda b, h: (0, h)),
                pl.BlockSpec((1, LANES), const2)]
    args = [y, y, y, y, y, lb, hg]
    if latent:
        st_spec = pl.BlockSpec((None, None, B_DK, B_DV), lambda b, h: (b, h, 0, 0))
        in_specs += [st_spec, st_spec]
        args += [s0f, s0b]
    tri_f, tri_b, mask_f, mask_b = consts
    in_specs += [pl.BlockSpec(tri_f.shape, const2), pl.BlockSpec(tri_b.shape, const2),
                 pl.BlockSpec(mask_f.shape, const3), pl.BlockSpec(mask_b.shape, const3)]
    args += [tri_f, tri_b, mask_f, mask_b]
    in_specs.append(pl.BlockSpec(memory_space=pl.ANY))
    args.append(mix)
    n_in = len(args)
    out_specs = [pl.BlockSpec((t, LANES), lambda b, h: (rb(b), B_HEADS + h))]
    out_shape = [jax.ShapeDtypeStruct(mix.shape, mix.dtype)]
    if not latent:
        st_out = pl.BlockSpec((None, None, B_DK, B_DV), lambda b, h: (b, h, 0, 0))
        out_specs += [st_out, st_out]
        out_shape += [jax.ShapeDtypeStruct((nb, B_HEADS, B_DK, B_DV), F32)] * 2

    kern = functools.partial(_hgrn_kernel, t_len=t, zero_init=not latent, emit_state=not latent)

    def body(*refs):
        kern(*refs[:n_in - 1], *refs[n_in:])

    outs = pl.pallas_call(
        body,
        grid=(nb, B_HEADS),
        in_specs=in_specs,
        out_specs=out_specs,
        out_shape=out_shape,
        scratch_shapes=[pltpu.VMEM((t, B_DV), F32), pltpu.VMEM((t, B_DV), F32),
                        pltpu.VMEM((B_DV, B_DK), F32), pltpu.VMEM((B_DV, B_DK), F32)],
        input_output_aliases={n_in - 1: 0},
        compiler_params=_cparams(("parallel", "parallel")),
        name="hgrn2_latent" if latent else "hgrn2_context",
    )(*args)
    return outs


def _cmid_kernel(q_ref, k_ref, v_ref, qg_ref, kg_ref, cos_ref, sin_ref, q_out, kc_out, k_out, v_out):
    cos_t = cos_ref[...]
    sin_t = sin_ref[...]
    qg = qg_ref[...]
    kg = kg_ref[...]

    def norm(x, g):
        return x * lax.rsqrt(jnp.mean(x * x, axis=-1, keepdims=True) + EPS) * g

    def rope(x):
        return x * cos_t + pltpu.roll(x, C_HEAD_DIM // 2, 1) * sin_t

    for h in range(C_HEADS):
        sl = slice(h * C_HEAD_DIM, (h + 1) * C_HEAD_DIM)
        q_out[:, sl] = rope(norm(q_ref[:, sl], qg)).astype(BF16)
    for h in range(C_KV_HEADS):
        sl = slice(h * C_HEAD_DIM, (h + 1) * C_HEAD_DIM)
        kn = norm(k_ref[:, sl], kg)
        kc_out[:, sl] = kn
        k_out[:, sl] = rope(kn).astype(BF16)
    v_out[...] = v_ref[...].astype(BF16)


def _cmid(y, qg, kg, cos_t, sin_t):
    tm = TM_MID
    const = lambda i: (0, 0)
    rope_map = lambda i: (_rope_block_of_tile(i, tm), 0)
    return pl.pallas_call(
        _cmid_kernel,
        grid=(ROWS // tm,),
        in_specs=[pl.BlockSpec((tm, C_WIDTH), lambda i: (i, 0)),
                  pl.BlockSpec((tm, C_KV_WIDTH), lambda i: (i, C_WIDTH // C_KV_WIDTH)),
                  pl.BlockSpec((tm, C_KV_WIDTH), lambda i: (i, C_WIDTH // C_KV_WIDTH + 1)),
                  pl.BlockSpec((1, C_HEAD_DIM), const), pl.BlockSpec((1, C_HEAD_DIM), const),
                  pl.BlockSpec((tm, LANES), rope_map), pl.BlockSpec((tm, LANES), rope_map)],
        out_specs=[pl.BlockSpec((tm, C_WIDTH), lambda i: (i, 0)),
                   pl.BlockSpec((tm, C_KV_WIDTH), lambda i: (i, 0)),
                   pl.BlockSpec((tm, C_KV_WIDTH), lambda i: (i, 0)),
                   pl.BlockSpec((tm, C_KV_WIDTH), lambda i: (i, 0))],
        out_shape=[jax.ShapeDtypeStruct((ROWS, C_WIDTH), BF16),
                   jax.ShapeDtypeStruct((ROWS, C_KV_WIDTH), F32),
                   jax.ShapeDtypeStruct((ROWS, C_KV_WIDTH), BF16),
                   jax.ShapeDtypeStruct((ROWS, C_KV_WIDTH), BF16)],
        compiler_params=_cparams(("parallel",)),
        name="gqa_qkv_prep",
    )(y, y, y, qg, kg, cos_t, sin_t)


def _attn_c_kernel(*refs, band, tq, t_len):
    if band:
        q_ref, kc_ref, vc_ref, kl_ref, vl_ref, sink_ref, gate_ref, o_ref = refs
    else:
        q_ref, kc_ref, vc_ref, sink_ref, gate_ref, o_ref = refs
    scale = C_HEAD_DIM ** -0.5
    kc = kc_ref[...]
    vc = vc_ref[...]
    if band:
        i = pl.program_id(2)
        width = tq + 2 * WINDOW
        start = pl.multiple_of(jnp.clip(i * tq - WINDOW, 0, t_len - width), WINDOW)
        kb = kl_ref[pl.ds(start, width), :]
        vb = vl_ref[pl.ds(start, width), :]
        qpos = i * tq + lax.broadcasted_iota(jnp.int32, (tq, width), 0)
        kpos = start + lax.broadcasted_iota(jnp.int32, (tq, width), 1)
        in_band = jnp.abs(kpos - qpos) <= WINDOW
    for r in range(C_GROUP):
        sl = slice(r * C_HEAD_DIM, (r + 1) * C_HEAD_DIM)
        q = q_ref[:, sl]
        sink = sink_ref[r][:, :1]
        s_c = _dot_nt(q, kc) * scale
        m = jnp.maximum(s_c.max(axis=-1, keepdims=True), sink)
        if band:
            s_l = jnp.where(in_band, _dot_nt(q, kb) * scale, NEG_BIG)
            m = jnp.maximum(m, s_l.max(axis=-1, keepdims=True))
        p_c = jnp.exp(s_c - m)
        den = jnp.sum(p_c, axis=-1, keepdims=True) + jnp.exp(sink - m)
        acc = _dot(p_c.astype(BF16), vc)
        if band:
            p_l = jnp.exp(s_l - m)
            den = den + jnp.sum(p_l, axis=-1, keepdims=True)
            acc = acc + _dot(p_l.astype(BF16), vb)
        o_ref[:, sl] = (acc / den * _silu(gate_ref[:, sl])).astype(o_ref.dtype)


def _attn_c(q, k, v, y, sink, mix, kc=None, vc=None):
    latent = kc is not None
    gw = C_GROUP * C_HEAD_DIM
    if latent:
        nb, t, tq, row0 = DEC_BATCH, DEC_SEQ, TQ_C, CTX_ROWS
    else:
        nb, t, tq, row0 = BATCH, SEQ, SEQ, 0
    nq = t // tq
    qrow = lambda b, g, i: row0 // tq + b * nq + i
    own_kv = pl.BlockSpec((t, C_HEAD_DIM), lambda b, g, i: (row0 // t + b, g))
    in_specs = [pl.BlockSpec((tq, gw), lambda b, g, i: (qrow(b, g, i), g))]
    args = [q]
    if latent:
        ctx_kv = pl.BlockSpec((PAST_LEN, C_HEAD_DIM), lambda b, g, i: (b, g))
        in_specs += [ctx_kv, ctx_kv, own_kv, own_kv]
        args += [kc, vc, k, v]
    else:
        in_specs += [own_kv, own_kv]
        args += [k, v]
    in_specs.append(pl.BlockSpec((C_GROUP, 1, LANES), lambda b, g, i: (g, 0, 0)))
    args.append(sink)
    in_specs.append(pl.BlockSpec((tq, gw), lambda b, g, i: (qrow(b, g, i), (C_WIDTH + 2 * C_KV_WIDTH) // gw + g)))
    args.append(y)
    in_specs.append(pl.BlockSpec(memory_space=pl.ANY))
    args.append(mix)
    n_in = len(args)
    kern = functools.partial(_attn_c_kernel, band=latent, tq=tq, t_len=t)

    def body(*refs):
        kern(*refs[:n_in - 1], refs[n_in])

    return pl.pallas_call(
        body,
        grid=(nb, C_KV_HEADS, nq),
        in_specs=in_specs,
        out_specs=pl.BlockSpec((tq, gw), lambda b, g, i: (qrow(b, g, i), g)),
        out_shape=jax.ShapeDtypeStruct(mix.shape, mix.dtype),
        input_output_aliases={n_in - 1: 0},
        compiler_params=_cparams(("parallel", "parallel", "parallel")),
        name="gqa_attention_latent" if latent else "gqa_attention_context",
    )(*args)


def _axial_angles(n_tokens, rot_dim):
    rows = n_tokens // GRID_W
    row = jnp.repeat(jnp.arange(rows, dtype=F32), GRID_W)
    col = jnp.tile(jnp.arange(GRID_W, dtype=F32), rows)
    n_freq = rot_dim // 4
    inv = ROPE_BASE ** (-jnp.arange(n_freq, dtype=F32) / n_freq)
    return jnp.concatenate([row[:, None] * inv, col[:, None] * inv], axis=-1)


def _rope_tables(rot_dim, tm):
    ang = _axial_angles(DEC_SEQ, rot_dim)
    cos, sin = jnp.cos(ang), jnp.sin(ang)
    pad = LANES - rot_dim
    cos_t = jnp.concatenate([cos, cos, jnp.ones((DEC_SEQ, pad), F32)], axis=-1)
    sin_t = jnp.concatenate([-sin, sin, jnp.zeros((DEC_SEQ, pad), F32)], axis=-1)
    cos_t = jnp.concatenate([jnp.ones((tm, LANES), F32), cos_t], axis=0)
    sin_t = jnp.concatenate([jnp.zeros((tm, LANES), F32), sin_t], axis=0)
    return cos_t, sin_t


def _lower_bounds(lb_logits):
    p = jax.nn.softmax(lb_logits.astype(F32), axis=0)
    return jnp.cumsum(p, axis=0) - p[0:1]


def _pad_head_gain(g):
    return jnp.concatenate([g, jnp.zeros((A_QK_PAD - A_QK,), F32)])[None, :]


def kernel(x_prompt, x_sample, cache_ckv, cache_kpe, state_hgrn_fwd, state_hgrn_bwd, cache_k_c, cache_v_c, c, c_ctx,
           mod_w_ab, mod_b_ab, norm_ab, w_in_ab, q_lora_norm, kv_lora_norm, w_q_up, w_kv_up, q_norm_ab, k_norm_ab,
           hgrn_lb_logits, hgrn_out_norm, w_out_ab, mod_w_c, mod_b_c, norm_c, w_in_c, q_norm_c, k_norm_c, sink_c,
           w_out_c):
    x = jnp.concatenate([x_prompt.reshape(CTX_ROWS, D_MODEL), x_sample.reshape(LAT_ROWS, D_MODEL)], axis=0)
    cond8 = jnp.concatenate([c_ctx[None, :], c, jnp.zeros((N_COND - 1 - DEC_BATCH, D_MODEL), F32)], axis=0)
    mods_ab = _modulation(cond8, mod_w_ab, mod_b_ab)
    mods_c = _modulation(cond8, mod_w_c, mod_b_c)
    lower = _lower_bounds(hgrn_lb_logits)
    cos_a, sin_a = _rope_tables(A_ROPE, TM_MID)
    cos_c, sin_c = _rope_tables(C_HEAD_DIM, TM_MID)
    hg_consts = _hgrn_constants()

    new_ckv, new_kpe, new_sf, new_sb, new_kc, new_vc = [], [], [], [], [], []
    for layer in range(DEPTH):
        j = layer // 2
        if layer % 2 == 0:
            mod = mods_ab[j].reshape(3 * N_COND, 1, D_MODEL)
            w = w_in_ab[j]
            w_in = jnp.concatenate([w[:, :768], w[:, 832:], w[:, 768:832],
                                    jnp.zeros((D_MODEL, AB_N - 6976), F32)], axis=1).astype(BF16)
            wq = jnp.pad(w_q_up[j].reshape(Q_LORA, A_HEADS, A_QK),
                         ((0, 0), (0, 0), (0, A_QK_PAD - A_QK))).reshape(Q_LORA, A_HEADS * A_QK_PAD).astype(BF16)
            wkv = w_kv_up[j].reshape(KV_LORA, A_HEADS, A_NOPE + A_VDIM)
            wk = wkv[:, :, :A_NOPE].reshape(KV_LORA, A_HEADS * A_NOPE).astype(BF16)
            wv = wkv[:, :, A_NOPE:].reshape(KV_LORA, A_HEADS * A_VDIM).astype(BF16)
            qlg, kvlg = q_lora_norm[j][None, :], kv_lora_norm[j][None, :]
            qng, kng = _pad_head_gain(q_norm_ab[j]), _pad_head_gain(k_norm_ab[j])

            y = _in_proj(x, norm_ab[j][None, :], mod, w_in, TN_IN_AB)
            q, k, v, ckv = _amid_tokens(y, wq, wk, wv, qlg, kvlg, qng, kng, cos_a, sin_a)
            kpe_cache = jnp.pad(cache_kpe[:, j].reshape(DEC_BATCH * PAST_LEN, A_ROPE), ((0, 0), (0, LANES - A_ROPE)))
            kc, vc = _amid_cache(cache_ckv[:, j].reshape(DEC_BATCH * PAST_LEN, KV_LORA), kpe_cache, wk, wv, kvlg, kng,
                                 cos_a, sin_a)
            mix = jnp.zeros((ROWS, D_MODEL), BF16)
            mix = _attn_a(q, k, v, y, mix)
            mix = _attn_a(q, k, v, y, mix, kc=kc, vc=vc)
            hg = hgrn_out_norm[j][None, :]
            mix, sf, sb = _hgrn(y, lower[j], hg, mix, hg_consts)
            (mix,) = _hgrn(y, lower[j], hg, mix, hg_consts, s0f=state_hgrn_fwd[:, j], s0b=state_hgrn_bwd[:, j])
            x = _out_proj(mix, w_out_ab[j].astype(BF16), x, mod)

            new_ckv.append(ckv[:CTX_ROWS].reshape(BATCH, SEQ, KV_LORA))
            new_kpe.append(y[:CTX_ROWS, AB_KPE:AB_KPE + A_ROPE].reshape(BATCH, SEQ, A_ROPE))
            new_sf.append(sf)
            new_sb.append(sb)
        else:
            mod = mods_c[j].reshape(3 * N_COND, 1, D_MODEL)
            y = _in_proj(x, norm_c[j][None, :], mod, w_in_c[j].astype(BF16), TN_IN_C)
            q, kn, k, v = _cmid(y, q_norm_c[j][None, :], k_norm_c[j][None, :], cos_c, sin_c)
            kc = cache_k_c[:, j].reshape(DEC_BATCH * PAST_LEN, C_KV_WIDTH).astype(BF16)
            vc = cache_v_c[:, j].reshape(DEC_BATCH * PAST_LEN, C_KV_WIDTH).astype(BF16)
            sink = jnp.broadcast_to(sink_c[j][:, None, None], (C_HEADS, 1, LANES))
            mix = jnp.zeros((ROWS, D_MODEL), BF16)
            mix = _attn_c(q, k, v, y, sink, mix)
            mix = _attn_c(q, k, v, y, sink, mix, kc=kc, vc=vc)
            x = _out_proj(mix, w_out_c[j].astype(BF16), x, mod)

            new_kc.append(kn[:CTX_ROWS].reshape(BATCH, SEQ, C_KV_HEADS, C_HEAD_DIM))
            new_vc.append(y[:CTX_ROWS, C_WIDTH + C_KV_WIDTH:C_WIDTH + 2 * C_KV_WIDTH]
                          .reshape(BATCH, SEQ, C_KV_HEADS, C_HEAD_DIM))

    return (x[:CTX_ROWS].reshape(BATCH, SEQ, D_MODEL), x[CTX_ROWS:].reshape(DEC_BATCH, DEC_SEQ, D_MODEL),
            jnp.stack(new_ckv, axis=1), jnp.stack(new_kpe, axis=1), jnp.stack(new_sf, axis=1),
            jnp.stack(new_sb, axis=1), jnp.stack(new_kc, axis=1), jnp.stack(new_vc, axis=1))
```

```python
import functools

import numpy as np
import jax
import jax.numpy as jnp
from jax import lax
from jax.experimental import pallas as pl
from jax.experimental.pallas import tpu as pltpu

F32 = jnp.float32
BF16 = jnp.bfloat16

D_MODEL = 2048
BATCH = 16
SEQ = 256
DEPTH = 4
DEC_BATCH = 4
DEC_SEQ = 2048
PAST_LEN = 256
GRID_W = 64
A_HEADS = 8
A_NOPE = 128
A_ROPE = 64
A_VDIM = 128
A_QK = A_NOPE + A_ROPE
A_QK_PAD = 256
Q_LORA = 512
KV_LORA = 256
B_HEADS = 8
B_DK = 128
B_DV = 128
C_HEADS = 16
C_KV_HEADS = 4
C_GROUP = C_HEADS // C_KV_HEADS
C_HEAD_DIM = 128
C_WIDTH = C_HEADS * C_HEAD_DIM
C_KV_WIDTH = C_KV_HEADS * C_HEAD_DIM
WINDOW = 128
ROPE_BASE = 10000.0
EPS = 1e-6
NEG_BIG = -1e30
LOG2_E = 1.4426950408889634

LANES = 128
CTX_ROWS = BATCH * SEQ
LAT_ROWS = DEC_BATCH * DEC_SEQ
ROWS = CTX_ROWS + LAT_ROWS
N_COND = 8

AB_AGATE = 0
AB_BQ = 1024
AB_BFF = 2048
AB_BFB = 3072
AB_BI = 4096
AB_BGATE = 5120
AB_QLAT = 6144
AB_KVLAT = 6656
AB_KPE = 6912
AB_N = 7168
C_Q = 0
C_GATE = 2048
C_K = 4096
C_V = 4608
C_N = 5120

TM_PROJ = 1024
TN_IN = 1024
TM_OUT = 512
TM_MID = 512
TQ_A = 1024
TQ_C = 256
ATTN_SUB = 256
HG_CHUNK = 128
HG_LEVELS = 7
HG_UNROLL = 2
VMEM_LIMIT = 56 * 1024 * 1024


def _cparams(sem):
    return pltpu.CompilerParams(dimension_semantics=sem, vmem_limit_bytes=VMEM_LIMIT)


def _silu(x):
    return x * jax.nn.sigmoid(x)


def _dot(a, b):
    return jnp.dot(a, b, preferred_element_type=F32)


def _dot_nt(a, b):
    return lax.dot_general(a, b, (((1,), (1,)), ((), ())), preferred_element_type=F32)


def _cond_of_tile(i, tm):
    n_ctx = CTX_ROWS // tm
    per_batch = DEC_SEQ // tm
    return jnp.where(i < n_ctx, 0, 1 + (i - n_ctx) // per_batch)


def _rope_block_of_tile(i, tm):
    n_ctx = CTX_ROWS // tm
    per_batch = DEC_SEQ // tm
    return jnp.where(i < n_ctx, 0, 1 + (i - n_ctx) % per_batch)


def _mod_kernel(c_ref, w_ref, b_ref, o_ref):
    a = _silu(c_ref[...]).astype(BF16)
    o_ref[...] = _dot(a, w_ref[...].astype(BF16)) + b_ref[...]


def _modulation(cond8, w_mod, b_mod):
    n = w_mod.shape[0]
    tn = 1024
    return pl.pallas_call(
        _mod_kernel,
        grid=(n, 3 * D_MODEL // tn),
        in_specs=[pl.BlockSpec((N_COND, D_MODEL), lambda l, j: (0, 0)),
                  pl.BlockSpec((None, D_MODEL, tn), lambda l, j: (l, 0, j)),
                  pl.BlockSpec((None, 1, tn), lambda l, j: (l, 0, j))],
        out_specs=pl.BlockSpec((None, N_COND, tn), lambda l, j: (l, 0, j)),
        out_shape=jax.ShapeDtypeStruct((n, N_COND, 3 * D_MODEL), F32),
        compiler_params=_cparams(("parallel", "parallel")),
        name="adaln_mod",
    )(cond8, w_mod, b_mod.reshape(n, 1, 3 * D_MODEL))


def _in_kernel(x_ref, g_ref, sh_ref, sc_ref, w_ref, o_ref, h_ref):
    @pl.when(pl.program_id(1) == 0)
    def _():
        x = x_ref[...]
        r = lax.rsqrt(jnp.mean(x * x, axis=-1, keepdims=True) + EPS)
        h = (x * r * g_ref[...]) * (1.0 + sc_ref[...]) + sh_ref[...]
        h_ref[...] = h.astype(BF16)

    o_ref[...] = _dot(h_ref[...], w_ref[...])


def _in_proj(x, g, mod, w):
    n = w.shape[1]
    tm, tn = TM_PROJ, TN_IN
    return pl.pallas_call(
        _in_kernel,
        grid=(ROWS // tm, n // tn),
        in_specs=[pl.BlockSpec((tm, D_MODEL), lambda i, j: (i, 0)),
                  pl.BlockSpec((1, D_MODEL), lambda i, j: (0, 0)),
                  pl.BlockSpec((None, 1, D_MODEL), lambda i, j: (_cond_of_tile(i, tm) * 3, 0, 0)),
                  pl.BlockSpec((None, 1, D_MODEL), lambda i, j: (_cond_of_tile(i, tm) * 3 + 1, 0, 0)),
                  pl.BlockSpec((D_MODEL, tn), lambda i, j: (0, j))],
        out_specs=pl.BlockSpec((tm, tn), lambda i, j: (i, j)),
        out_shape=jax.ShapeDtypeStruct((ROWS, n), F32),
        scratch_shapes=[pltpu.VMEM((tm, D_MODEL), BF16)],
        compiler_params=_cparams(("parallel", "arbitrary")),
        name="norm_mod_in_proj",
    )(x, g, mod, mod, w)


def _out_kernel(m_ref, w_ref, x_ref, gt_ref, o_ref):
    o_ref[...] = x_ref[...] + gt_ref[...] * _dot(m_ref[...], w_ref[...])


def _out_proj(mix, w, x, mod):
    tm = TM_OUT
    return pl.pallas_call(
        _out_kernel,
        grid=(ROWS // tm,),
        in_specs=[pl.BlockSpec((tm, D_MODEL), lambda i: (i, 0)),
                  pl.BlockSpec((D_MODEL, D_MODEL), lambda i: (0, 0)),
                  pl.BlockSpec((tm, D_MODEL), lambda i: (i, 0)),
                  pl.BlockSpec((None, 1, D_MODEL), lambda i: (_cond_of_tile(i, tm) * 3 + 2, 0, 0))],
        out_specs=pl.BlockSpec((tm, D_MODEL), lambda i: (i, 0)),
        out_shape=jax.ShapeDtypeStruct((ROWS, D_MODEL), F32),
        compiler_params=_cparams(("parallel",)),
        name="out_proj_residual",
    )(mix, w, x, mod)


def _rope_a(x, cos_t, sin_t):
    return x * cos_t + (pltpu.roll(x, 32, 1) + pltpu.roll(x, 96, 1)) * sin_t


def _amid_kernel(*refs, do_q, norm_kv):
    if do_q:
        (ql_ref, kvl_ref, kpe_ref, wq_ref, wk_ref, wv_ref, qlg_ref, kvlg_ref, qng_ref, kng_ref, cos_ref, sin_ref,
         q_out, k_out, v_out, ckv_out) = refs
    else:
        (kvl_ref, kpe_ref, wk_ref, wv_ref, kvlg_ref, kng_ref, cos_ref, sin_ref, k_out, v_out, ckv_out) = refs
    cos_t = cos_ref[...]
    sin_t = sin_ref[...]
    inv_qk = 1.0 / A_QK

    if do_q:
        ql = ql_ref[...]
        qn = ql * lax.rsqrt(jnp.mean(ql * ql, axis=-1, keepdims=True) + EPS) * qlg_ref[...]
        qu = _dot(qn.astype(BF16), wq_ref[...])
        g_nope = qng_ref[:, :A_NOPE]
        g_rope = qng_ref[:, A_NOPE:]
        for h in range(A_HEADS):
            a = qu[:, h * A_QK_PAD:h * A_QK_PAD + A_NOPE]
            b = qu[:, h * A_QK_PAD + A_NOPE:(h + 1) * A_QK_PAD]
            ss = jnp.sum(a * a, axis=-1, keepdims=True) + jnp.sum(b * b, axis=-1, keepdims=True)
            r = lax.rsqrt(ss * inv_qk + EPS)
            q_out[:, h * A_QK_PAD:h * A_QK_PAD + A_NOPE] = (a * r * g_nope).astype(BF16)
            q_out[:, h * A_QK_PAD + A_NOPE:(h + 1) * A_QK_PAD] = _rope_a(b * r * g_rope, cos_t, sin_t).astype(BF16)

    kvl = kvl_ref[...]
    if norm_kv:
        ckv = kvl * lax.rsqrt(jnp.mean(kvl * kvl, axis=-1, keepdims=True) + EPS) * kvlg_ref[...]
    else:
        ckv = kvl
    ckv_out[...] = ckv
    ckv_b = ckv.astype(BF16)
    kn = _dot(ckv_b, wk_ref[...])
    v_out[...] = _dot(ckv_b, wv_ref[...]).astype(BF16)
    kpe = kpe_ref[...]
    sp = jnp.sum(kpe * kpe, axis=-1, keepdims=True)
    g_nope = kng_ref[:, :A_NOPE]
    g_rope = kng_ref[:, A_NOPE:]
    for h in range(A_HEADS):
        a = kn[:, h * A_NOPE:(h + 1) * A_NOPE]
        r = lax.rsqrt((jnp.sum(a * a, axis=-1, keepdims=True) + sp) * inv_qk + EPS)
        k_out[:, h * A_QK_PAD:h * A_QK_PAD + A_NOPE] = (a * r * g_nope).astype(BF16)
        k_out[:, h * A_QK_PAD + A_NOPE:(h + 1) * A_QK_PAD] = _rope_a(kpe * r * g_rope, cos_t, sin_t).astype(BF16)


def _amid_tokens(y, wq, wk, wv, qlg, kvlg, qng, kng, cos_t, sin_t):
    tm = TM_MID
    const = lambda i: (0, 0)
    rope_map = lambda i: (_rope_block_of_tile(i, tm), 0)
    return pl.pallas_call(
        functools.partial(_amid_kernel, do_q=True, norm_kv=True),
        grid=(ROWS // tm,),
        in_specs=[pl.BlockSpec((tm, Q_LORA), lambda i: (i, AB_QLAT // Q_LORA)),
                  pl.BlockSpec((tm, KV_LORA), lambda i: (i, AB_KVLAT // KV_LORA)),
                  pl.BlockSpec((tm, LANES), lambda i: (i, AB_KPE // LANES)),
                  pl.BlockSpec(wq.shape, const), pl.BlockSpec(wk.shape, const), pl.BlockSpec(wv.shape, const),
                  pl.BlockSpec(qlg.shape, const), pl.BlockSpec(kvlg.shape, const),
                  pl.BlockSpec(qng.shape, const), pl.BlockSpec(kng.shape, const),
                  pl.BlockSpec((tm, LANES), rope_map), pl.BlockSpec((tm, LANES), rope_map)],
        out_specs=[pl.BlockSpec((tm, A_HEADS * A_QK_PAD), lambda i: (i, 0)),
                   pl.BlockSpec((tm, A_HEADS * A_QK_PAD), lambda i: (i, 0)),
                   pl.BlockSpec((tm, A_HEADS * A_VDIM), lambda i: (i, 0)),
                   pl.BlockSpec((tm, KV_LORA), lambda i: (i, 0))],
        out_shape=[jax.ShapeDtypeStruct((ROWS, A_HEADS * A_QK_PAD), BF16),
                   jax.ShapeDtypeStruct((ROWS, A_HEADS * A_QK_PAD), BF16),
                   jax.ShapeDtypeStruct((ROWS, A_HEADS * A_VDIM), BF16),
                   jax.ShapeDtypeStruct((ROWS, KV_LORA), F32)],
        compiler_params=_cparams(("parallel",)),
        name="mla_qkv_prep",
    )(y, y, y, wq, wk, wv, qlg, kvlg, qng, kng, cos_t, sin_t)


def _amid_cache(ckv, kpe, wk, wv, kvlg, kng, cos_t, sin_t):
    rows = ckv.shape[0]
    tm = TM_MID
    const = lambda i: (0, 0)
    outs = pl.pallas_call(
        functools.partial(_amid_kernel, do_q=False, norm_kv=False),
        grid=(rows // tm,),
        in_specs=[pl.BlockSpec((tm, KV_LORA), lambda i: (i, 0)),
                  pl.BlockSpec((tm, LANES), lambda i: (i, 0)),
                  pl.BlockSpec(wk.shape, const), pl.BlockSpec(wv.shape, const),
                  pl.BlockSpec(kvlg.shape, const), pl.BlockSpec(kng.shape, const),
                  pl.BlockSpec((tm, LANES), const), pl.BlockSpec((tm, LANES), const)],
        out_specs=[pl.BlockSpec((tm, A_HEADS * A_QK_PAD), lambda i: (i, 0)),
                   pl.BlockSpec((tm, A_HEADS * A_VDIM), lambda i: (i, 0)),
                   pl.BlockSpec((tm, KV_LORA), lambda i: (i, 0))],
        out_shape=[jax.ShapeDtypeStruct((rows, A_HEADS * A_QK_PAD), BF16),
                   jax.ShapeDtypeStruct((rows, A_HEADS * A_VDIM), BF16),
                   jax.ShapeDtypeStruct((rows, KV_LORA), F32)],
        compiler_params=_cparams(("parallel",)),
        name="mla_cache_kv_prep",
    )(ckv, kpe, wk, wv, kvlg, kng, cos_t, sin_t)
    return outs[0], outs[1]


def _scores(q, srcs, c):
    zs = []
    for k, _, keep in srcs:
        z = _dot_nt(q, k) * c
        zs.append(z if keep is None else jnp.where(keep, z, NEG_BIG))
    return zs


def _softmax_pv(zs, srcs, sink_z=None):
    tile_max = None
    for z in zs:
        for j in range(z.shape[1] // LANES):
            blk = z[:, j * LANES:(j + 1) * LANES]
            tile_max = blk if tile_max is None else jnp.maximum(tile_max, blk)
    m = tile_max.max(axis=-1, keepdims=True)
    if sink_z is not None:
        m = jnp.maximum(m, sink_z)
    acc = None
    for z, (_, v, _) in zip(zs, srcs):
        o = _dot(jnp.exp2(z - m).astype(BF16), jnp.concatenate([v, jnp.ones_like(v)], axis=1))
        acc = o if acc is None else acc + o
    dv = acc.shape[1] // 2
    den = acc[:, dv:]
    if sink_z is not None:
        den = den + jnp.exp2(sink_z - m)
    return acc[:, :dv] / den


def _attend_streams(n, q_of, srcs_of, c, sink_of=None):
    outs = []
    zs = _scores(q_of(0), srcs_of(0), c)
    for t in range(n):
        nxt = _scores(q_of(t + 1), srcs_of(t + 1), c) if t + 1 < n else None
        outs.append(_softmax_pv(zs, srcs_of(t), None if sink_of is None else sink_of(t)))
        zs = nxt
    return outs


def _attn_a_kernel(*refs, n_src, heads):
    q_ref = refs[0]
    k_refs = refs[1:1 + n_src]
    v_refs = refs[1 + n_src:1 + 2 * n_src]
    gate_ref, o_ref = refs[1 + 2 * n_src:]
    sub = min(ATTN_SUB, q_ref.shape[0])
    streams = [(h, r) for h in range(heads) for r in range(q_ref.shape[0] // sub)]

    def rows(t):
        return slice(streams[t][1] * sub, (streams[t][1] + 1) * sub)

    def q_of(t):
        h = streams[t][0]
        return q_ref[rows(t), h * A_QK_PAD:(h + 1) * A_QK_PAD]

    def srcs_of(t):
        h = streams[t][0]
        return [(k_ref[:, h * A_QK_PAD:(h + 1) * A_QK_PAD], v_ref[:, h * A_VDIM:(h + 1) * A_VDIM], None)
                for k_ref, v_ref in zip(k_refs, v_refs)]

    outs = _attend_streams(len(streams), q_of, srcs_of, A_QK ** -0.5 * LOG2_E)
    for t, o in enumerate(outs):
        cols = slice(streams[t][0] * A_VDIM, (streams[t][0] + 1) * A_VDIM)
        o_ref[rows(t), cols] = (o * _silu(gate_ref[rows(t), cols])).astype(o_ref.dtype)


def _attn_a(q, k, v, y, mix, kc=None, vc=None):
    latent = kc is not None
    if latent:
        nb, t, tq, hp, row0 = DEC_BATCH, DEC_SEQ, TQ_A, 1, CTX_ROWS
    else:
        nb, t, tq, hp, row0 = BATCH, SEQ, SEQ, A_HEADS, 0
    nq = t // tq
    qrow = lambda b, h, i: row0 // tq + b * nq + i
    in_specs = [pl.BlockSpec((tq, hp * A_QK_PAD), lambda b, h, i: (qrow(b, h, i), h))]
    args = [q]
    if latent:
        in_specs.append(pl.BlockSpec((PAST_LEN, hp * A_QK_PAD), lambda b, h, i: (b, h)))
        args.append(kc)
    in_specs.append(pl.BlockSpec((t, hp * A_QK_PAD), lambda b, h, i: (row0 // t + b, h)))
    args.append(k)
    if latent:
        in_specs.append(pl.BlockSpec((PAST_LEN, hp * A_VDIM), lambda b, h, i: (b, h)))
        args.append(vc)
    in_specs.append(pl.BlockSpec((t, hp * A_VDIM), lambda b, h, i: (row0 // t + b, h)))
    args.append(v)
    in_specs.append(pl.BlockSpec((tq, hp * A_VDIM), lambda b, h, i: (qrow(b, h, i), AB_AGATE // (hp * A_VDIM) + h)))
    args.append(y)
    in_specs.append(pl.BlockSpec(memory_space=pl.ANY))
    args.append(mix)
    n_in = len(args)

    def body(*refs):
        _attn_a_kernel(*refs[:n_in - 1], refs[n_in], n_src=2 if latent else 1, heads=hp)

    return pl.pallas_call(
        body,
        grid=(nb, A_HEADS // hp, nq),
        in_specs=in_specs,
        out_specs=pl.BlockSpec((tq, hp * A_VDIM), lambda b, h, i: (qrow(b, h, i), h)),
        out_shape=jax.ShapeDtypeStruct(mix.shape, mix.dtype),
        input_output_aliases={n_in - 1: 0},
        compiler_params=_cparams(("parallel", "parallel", "parallel")),
        name="mla_attention_latent" if latent else "mla_attention_context",
    )(*args)


def _hgrn_constants():
    c, nl = HG_CHUNK, HG_LEVELS
    t = np.arange(c)[:, None]
    u = np.arange(c)[None, :]
    tri_f = (u <= t).astype(np.float32)
    mask_f = np.zeros((nl, c, c), np.float32)
    for l in range(nl):
        half = c >> (l + 1)
        seg = 2 * half
        mask_f[l] = ((u // seg) == (t // seg)) & ((t % seg) >= half) & ((u % seg) < half)
    tri_b = tri_f[::-1, ::-1]
    mask_b = mask_f[:, ::-1, ::-1]
    to_tri = lambda a: jnp.asarray(np.concatenate([a, a, a], axis=1), BF16)
    to_mask = lambda a: jnp.asarray(np.ascontiguousarray(a), F32)
    return to_tri(tri_f), to_tri(tri_b), to_mask(mask_f), to_mask(mask_b)


def _hgrn_decays(x, lb, tri3):
    kk = (1.0 - lb) * jax.nn.sigmoid(-x)
    lf = jnp.log1p(-kk) * LOG2_E
    hi = lf.astype(BF16)
    r1 = lf - hi.astype(F32)
    mid = r1.astype(BF16)
    lo = (r1 - mid.astype(F32)).astype(BF16)
    return kk, lf, _dot(tri3, jnp.concatenate([hi, mid, lo], axis=0))


def _hgrn_level_arg(l, cs, lf, forward):
    c = HG_CHUNK
    nv = c // 8
    half = c >> (l + 1)
    cs3 = cs.reshape(nv, 8, LANES)
    sub = lax.broadcasted_iota(jnp.int32, (nv, 8, LANES), 1)

    def in_vreg_row(r):
        return jnp.broadcast_to(cs3[:, r:r + 1, :], (nv, 8, LANES))

    if half >= 8:
        m = half // 8
        nseg = nv // (2 * m)
        edge = cs3[:, 7:8, :] if forward else cs3[:, 0:1, :]
        e4 = edge.reshape(nseg, 2 * m, 1, LANES)
        a = e4[:, m - 1:m] if forward else e4[:, m:m + 1]
        anchor = jnp.broadcast_to(a, (nseg, 2 * m, 8, LANES)).reshape(c, LANES)
    elif half == 4:
        anchor = in_vreg_row(3 if forward else 4).reshape(c, LANES)
    elif half == 2:
        r0, r1 = (1, 5) if forward else (2, 6)
        anchor = jnp.where(sub < 4, in_vreg_row(r0), in_vreg_row(r1)).reshape(c, LANES)
    else:
        is_query = (sub % 2 == 1) if forward else (sub % 2 == 0)
        return jnp.where(is_query.reshape(c, LANES), lf, 0.0)
    return -jnp.abs(cs - anchor)


def _hgrn_kernel(*refs, t_len, zero_init, emit_state):
    c = HG_CHUNK
    n_chunks = t_len // c
    it = iter(refs)
    bq_ref, ff_ref, fb_ref, vi_ref, bg_ref, lb_ref, hg_ref = (next(it) for _ in range(7))
    if not zero_init:
        s0f_ref, s0b_ref = next(it), next(it)
    trif_ref, trib_ref, maskf_ref, maskb_ref = (next(it) for _ in range(4))
    o_ref = next(it)
    if emit_state:
        sf_ref, sb_ref = next(it), next(it)
    of_ref, ob_ref, stf_ref, stb_ref = (next(it) for _ in range(4))

    if zero_init:
        stf_ref[...] = jnp.zeros((B_DV, B_DK), F32)
        stb_ref[...] = jnp.zeros((B_DV, B_DK), F32)
    else:
        stf_ref[...] = s0f_ref[...].T
        stb_ref[...] = s0b_ref[...].T
    lb = lb_ref[...]

    nl = HG_LEVELS
    dirs = ((ff_ref, lb[0:1, :], trif_ref, maskf_ref, stf_ref, of_ref, True),
            (fb_ref, lb[1:2, :], trib_ref, maskb_ref, stb_ref, ob_ref, False))

    def body(i, carry):
        chains = []
        for f_ref, lb_row, tri_ref, mask_ref, st_ref, out_ref, forward in dirs:
            for u in range(HG_UNROLL):
                k = i * HG_UNROLL + u
                r0 = pl.multiple_of((k if forward else n_chunks - 1 - k) * c, c)
                chains.append(dict(rows=pl.ds(r0, c), f_ref=f_ref, lb=lb_row, tri_ref=tri_ref, mask_ref=mask_ref,
                                   st_ref=st_ref, out_ref=out_ref, forward=forward))
        for ch in chains:
            ch["kk"], ch["lf"], ch["cs"] = _hgrn_decays(ch["f_ref"][ch["rows"], :], ch["lb"], ch["tri_ref"][...])
            ch["q"] = _silu(bq_ref[ch["rows"], :])
            ch["v"] = vi_ref[ch["rows"], :]
            ch["sc"] = jnp.zeros((c, c), F32)
        for l in range(nl):
            for ch in chains:
                el = jnp.exp2(_hgrn_level_arg(l, ch["cs"], ch["lf"], ch["forward"]))
                ch["sc"] = ch["sc"] + ch["mask_ref"][l] * _dot_nt((ch["q"] * el).astype(BF16),
                                                                  (ch["kk"] * el).astype(BF16))
        for ch in chains:
            cs, q, kk, v = ch["cs"], ch["q"], ch["kk"], ch["v"]
            end = cs[c - 1:c, :] if ch["forward"] else cs[0:1, :]
            ch["end"] = end
            ch["q_in"] = (q * jnp.exp2(cs)).astype(BF16)
            ch["upd"] = _dot(v.T.astype(BF16), (kk * jnp.exp2(-jnp.abs(cs - end))).astype(BF16))
            ch["o"] = _dot(ch["sc"].astype(BF16), v.astype(BF16)) + jnp.sum(q * kk, axis=-1, keepdims=True) * v
        for ch in chains:
            st = ch["st_ref"][...]
            ch["out_ref"][ch["rows"], :] = ch["o"] + _dot_nt(ch["q_in"], st.astype(BF16))
            ch["st_ref"][...] = jnp.exp2(ch["end"]) * st + ch["upd"]
        return carry

    lax.fori_loop(0, n_chunks // HG_UNROLL, body, 0)
    if emit_state:
        sf_ref[...] = stf_ref[...].T
        sb_ref[...] = stb_ref[...].T

    hg = hg_ref[...]

    def finish(i, carry):
        rows = pl.ds(pl.multiple_of(i * c, c), c)
        o = of_ref[rows, :] + ob_ref[rows, :]
        o = o * lax.rsqrt(jnp.mean(o * o, axis=-1, keepdims=True) + EPS) * hg
        o_ref[rows, :] = (o * _silu(bg_ref[rows, :])).astype(o_ref.dtype)
        return carry

    lax.fori_loop(0, n_chunks, finish, 0)


def _hgrn(y, lb, hg, mix, consts, s0f=None, s0b=None):
    latent = s0f is not None
    if latent:
        nb, t, row0 = DEC_BATCH, DEC_SEQ, CTX_ROWS
    else:
        nb, t, row0 = BATCH, SEQ, 0
    rb = lambda b: row0 // t + b
    col = lambda off: (lambda b, h: (rb(b), off // LANES + h))
    const2 = lambda b, h: (0, 0)
    const3 = lambda b, h: (0, 0, 0)
    in_specs = [pl.BlockSpec((t, LANES), col(AB_BQ)), pl.BlockSpec((t, LANES), col(AB_BFF)),
                pl.BlockSpec((t, LANES), col(AB_BFB)), pl.BlockSpec((t, LANES), col(AB_BI)),
                pl.BlockSpec((t, LANES), col(AB_BGATE)),
                pl.BlockSpec((2, LANES), lambda b, h: (0, h)),
                pl.BlockSpec((1, LANES), const2)]
    args = [y, y, y, y, y, lb, hg]
    if latent:
        st_spec = pl.BlockSpec((None, None, B_DK, B_DV), lambda b, h: (b, h, 0, 0))
        in_specs += [st_spec, st_spec]
        args += [s0f, s0b]
    tri_f, tri_b, mask_f, mask_b = consts
    in_specs += [pl.BlockSpec(tri_f.shape, const2), pl.BlockSpec(tri_b.shape, const2),
                 pl.BlockSpec(mask_f.shape, const3), pl.BlockSpec(mask_b.shape, const3)]
    args += [tri_f, tri_b, mask_f, mask_b]
    in_specs.append(pl.BlockSpec(memory_space=pl.ANY))
    args.append(mix)
    n_in = len(args)
    out_specs = [pl.BlockSpec((t, LANES), lambda b, h: (rb(b), B_HEADS + h))]
    out_shape = [jax.ShapeDtypeStruct(mix.shape, mix.dtype)]
    if not latent:
        st_out = pl.BlockSpec((None, None, B_DK, B_DV), lambda b, h: (b, h, 0, 0))
        out_specs += [st_out, st_out]
        out_shape += [jax.ShapeDtypeStruct((nb, B_HEADS, B_DK, B_DV), F32)] * 2

    kern = functools.partial(_hgrn_kernel, t_len=t, zero_init=not latent, emit_state=not latent)

    def body(*refs):
        kern(*refs[:n_in - 1], *refs[n_in:])

    outs = pl.pallas_call(
        body,
        grid=(nb, B_HEADS),
        in_specs=in_specs,
        out_specs=out_specs,
        out_shape=out_shape,
        scratch_shapes=[pltpu.VMEM((t, B_DV), F32), pltpu.VMEM((t, B_DV), F32),
                        pltpu.VMEM((B_DV, B_DK), F32), pltpu.VMEM((B_DV, B_DK), F32)],
        input_output_aliases={n_in - 1: 0},
        compiler_params=_cparams(("parallel", "parallel")),
        name="hgrn2_latent" if latent else "hgrn2_context",
    )(*args)
    return outs


def _cmid_kernel(q_ref, k_ref, v_ref, qg_ref, kg_ref, cos_ref, sin_ref, q_out, kc_out, k_out, v_out):
    cos_t = cos_ref[...]
    sin_t = sin_ref[...]
    qg = qg_ref[...]
    kg = kg_ref[...]

    def norm(x, g):
        return x * lax.rsqrt(jnp.mean(x * x, axis=-1, keepdims=True) + EPS) * g

    def rope(x):
        return x * cos_t + pltpu.roll(x, C_HEAD_DIM // 2, 1) * sin_t

    for h in range(C_HEADS):
        sl = slice(h * C_HEAD_DIM, (h + 1) * C_HEAD_DIM)
        q_out[:, sl] = rope(norm(q_ref[:, sl], qg)).astype(BF16)
    for h in range(C_KV_HEADS):
        sl = slice(h * C_HEAD_DIM, (h + 1) * C_HEAD_DIM)
        kn = norm(k_ref[:, sl], kg)
        kc_out[:, sl] = kn
        k_out[:, sl] = rope(kn).astype(BF16)
    v_out[...] = v_ref[...].astype(BF16)


def _cmid(y, qg, kg, cos_t, sin_t):
    tm = TM_MID
    const = lambda i: (0, 0)
    rope_map = lambda i: (_rope_block_of_tile(i, tm), 0)
    return pl.pallas_call(
        _cmid_kernel,
        grid=(ROWS // tm,),
        in_specs=[pl.BlockSpec((tm, C_WIDTH), lambda i: (i, C_Q // C_WIDTH)),
                  pl.BlockSpec((tm, C_KV_WIDTH), lambda i: (i, C_K // C_KV_WIDTH)),
                  pl.BlockSpec((tm, C_KV_WIDTH), lambda i: (i, C_V // C_KV_WIDTH)),
                  pl.BlockSpec((1, C_HEAD_DIM), const), pl.BlockSpec((1, C_HEAD_DIM), const),
                  pl.BlockSpec((tm, LANES), rope_map), pl.BlockSpec((tm, LANES), rope_map)],
        out_specs=[pl.BlockSpec((tm, C_WIDTH), lambda i: (i, 0)),
                   pl.BlockSpec((tm, C_KV_WIDTH), lambda i: (i, 0)),
                   pl.BlockSpec((tm, C_KV_WIDTH), lambda i: (i, 0)),
                   pl.BlockSpec((tm, C_KV_WIDTH), lambda i: (i, 0))],
        out_shape=[jax.ShapeDtypeStruct((ROWS, C_WIDTH), BF16),
                   jax.ShapeDtypeStruct((ROWS, C_KV_WIDTH), F32),
                   jax.ShapeDtypeStruct((ROWS, C_KV_WIDTH), BF16),
                   jax.ShapeDtypeStruct((ROWS, C_KV_WIDTH), BF16)],
        compiler_params=_cparams(("parallel",)),
        name="gqa_qkv_prep",
    )(y, y, y, qg, kg, cos_t, sin_t)


def _attn_c_kernel(*refs, band, tq, t_len, groups, stack):
    if band:
        q_ref, kc_ref, vc_ref, kl_ref, vl_ref, sink_ref, gate_ref, o_ref = refs
    else:
        q_ref, kc_ref, vc_ref, sink_ref, gate_ref, o_ref = refs
    hd = C_HEAD_DIM
    rows = stack * tq
    streams = [(g, s) for g in range(groups) for s in range(C_GROUP // stack)]
    if band:
        i = pl.program_id(2)
        width = tq + 2 * WINDOW
        start = pl.multiple_of(jnp.clip(i * tq - WINDOW, 0, t_len - width), WINDOW)
        qpos = i * tq + (lax.broadcasted_iota(jnp.int32, (rows, width), 0) & (tq - 1))
        kpos = start + lax.broadcasted_iota(jnp.int32, (rows, width), 1)
        in_band = jnp.abs(kpos - qpos) <= WINDOW

    def heads_of(t):
        g, s = streams[t]
        return [g * C_GROUP + s * stack + r for r in range(stack)]

    def q_of(t):
        return jnp.concatenate([q_ref[:, h * hd:(h + 1) * hd] for h in heads_of(t)], axis=0)

    def srcs_of(t):
        g = streams[t][0]
        cols = slice(g * hd, (g + 1) * hd)
        srcs = [(kc_ref[:, cols], vc_ref[:, cols], None)]
        if band:
            srcs.append((kl_ref[pl.ds(start, width), cols], vl_ref[pl.ds(start, width), cols], in_band))
        return srcs

    def sink_of(t):
        return jnp.concatenate([jnp.broadcast_to(sink_ref[h][:, :1] * LOG2_E, (tq, 1)) for h in heads_of(t)], axis=0)

    outs = _attend_streams(len(streams), q_of, srcs_of, hd ** -0.5 * LOG2_E, sink_of)
    for t, o in enumerate(outs):
        for r, h in enumerate(heads_of(t)):
            cols = slice(h * hd, (h + 1) * hd)
            o_ref[:, cols] = (o[r * tq:(r + 1) * tq] * _silu(gate_ref[:, cols])).astype(o_ref.dtype)


def _attn_c(q, k, v, y, sink, mix, kc=None, vc=None):
    latent = kc is not None
    if latent:
        nb, t, tq, gp, stack, row0 = DEC_BATCH, DEC_SEQ, TQ_C, 1, 2, CTX_ROWS
    else:
        nb, t, tq, gp, stack, row0 = BATCH, SEQ, SEQ, C_KV_HEADS, C_GROUP, 0
    gw = gp * C_GROUP * C_HEAD_DIM
    kvw = gp * C_HEAD_DIM
    nq = t // tq
    qrow = lambda b, g, i: row0 // tq + b * nq + i
    own_kv = pl.BlockSpec((t, kvw), lambda b, g, i: (row0 // t + b, g))
    in_specs = [pl.BlockSpec((tq, gw), lambda b, g, i: (qrow(b, g, i), g))]
    args = [q]
    if latent:
        ctx_kv = pl.BlockSpec((PAST_LEN, kvw), lambda b, g, i: (b, g))
        in_specs += [ctx_kv, ctx_kv, own_kv, own_kv]
        args += [kc, vc, k, v]
    else:
        in_specs += [own_kv, own_kv]
        args += [k, v]
    in_specs.append(pl.BlockSpec((gp * C_GROUP, 1, LANES), lambda b, g, i: (g, 0, 0)))
    args.append(sink)
    in_specs.append(pl.BlockSpec((tq, gw), lambda b, g, i: (qrow(b, g, i), C_GATE // gw + g)))
    args.append(y)
    in_specs.append(pl.BlockSpec(memory_space=pl.ANY))
    args.append(mix)
    n_in = len(args)
    kern = functools.partial(_attn_c_kernel, band=latent, tq=tq, t_len=t, groups=gp, stack=stack)

    def body(*refs):
        kern(*refs[:n_in - 1], refs[n_in])

    return pl.pallas_call(
        body,
        grid=(nb, C_KV_HEADS // gp, nq),
        in_specs=in_specs,
        out_specs=pl.BlockSpec((tq, gw), lambda b, g, i: (qrow(b, g, i), g)),
        out_shape=jax.ShapeDtypeStruct(mix.shape, mix.dtype),
        input_output_aliases={n_in - 1: 0},
        compiler_params=_cparams(("parallel", "parallel", "parallel")),
        name="gqa_attention_latent" if latent else "gqa_attention_context",
    )(*args)


def _axial_angles(n_tokens, rot_dim):
    rows = n_tokens // GRID_W
    row = jnp.repeat(jnp.arange(rows, dtype=F32), GRID_W)
    col = jnp.tile(jnp.arange(GRID_W, dtype=F32), rows)
    n_freq = rot_dim // 4
    inv = ROPE_BASE ** (-jnp.arange(n_freq, dtype=F32) / n_freq)
    return jnp.concatenate([row[:, None] * inv, col[:, None] * inv], axis=-1)


def _rope_tables(rot_dim, tm):
    ang = _axial_angles(DEC_SEQ, rot_dim)
    cos, sin = jnp.cos(ang), jnp.sin(ang)
    pad = LANES - rot_dim
    cos_t = jnp.concatenate([cos, cos, jnp.ones((DEC_SEQ, pad), F32)], axis=-1)
    sin_t = jnp.concatenate([-sin, sin, jnp.zeros((DEC_SEQ, pad), F32)], axis=-1)
    cos_t = jnp.concatenate([jnp.ones((tm, LANES), F32), cos_t], axis=0)
    sin_t = jnp.concatenate([jnp.zeros((tm, LANES), F32), sin_t], axis=0)
    return cos_t, sin_t


def _lower_bounds(lb_logits):
    p = jax.nn.softmax(lb_logits.astype(F32), axis=0)
    return jnp.cumsum(p, axis=0) - p[0:1]


def _pad_head_gain(g):
    return jnp.concatenate([g, jnp.zeros((A_QK_PAD - A_QK,), F32)])[None, :]


def kernel(x_prompt, x_sample, cache_ckv, cache_kpe, state_hgrn_fwd, state_hgrn_bwd, cache_k_c, cache_v_c, c, c_ctx,
           mod_w_ab, mod_b_ab, norm_ab, w_in_ab, q_lora_norm, kv_lora_norm, w_q_up, w_kv_up, q_norm_ab, k_norm_ab,
           hgrn_lb_logits, hgrn_out_norm, w_out_ab, mod_w_c, mod_b_c, norm_c, w_in_c, q_norm_c, k_norm_c, sink_c,
           w_out_c):
    x = jnp.concatenate([x_prompt.reshape(CTX_ROWS, D_MODEL), x_sample.reshape(LAT_ROWS, D_MODEL)], axis=0)
    cond8 = jnp.concatenate([c_ctx[None, :], c, jnp.zeros((N_COND - 1 - DEC_BATCH, D_MODEL), F32)], axis=0)
    mods_ab = _modulation(cond8, mod_w_ab, mod_b_ab)
    mods_c = _modulation(cond8, mod_w_c, mod_b_c)
    lower = _lower_bounds(hgrn_lb_logits)
    cos_a, sin_a = _rope_tables(A_ROPE, TM_MID)
    cos_c, sin_c = _rope_tables(C_HEAD_DIM, TM_MID)
    hg_consts = _hgrn_constants()

    new_ckv, new_kpe, new_sf, new_sb, new_kc, new_vc = [], [], [], [], [], []
    for layer in range(DEPTH):
        j = layer // 2
        if layer % 2 == 0:
            mod = mods_ab[j].reshape(3 * N_COND, 1, D_MODEL)
            w = w_in_ab[j]
            w_in = jnp.concatenate([w[:, 832:], w[:, :832], jnp.zeros((D_MODEL, AB_N - w.shape[1]), F32)],
                                   axis=1).astype(BF16)
            wq = jnp.pad(w_q_up[j].reshape(Q_LORA, A_HEADS, A_QK),
                         ((0, 0), (0, 0), (0, A_QK_PAD - A_QK))).reshape(Q_LORA, A_HEADS * A_QK_PAD).astype(BF16)
            wkv = w_kv_up[j].reshape(KV_LORA, A_HEADS, A_NOPE + A_VDIM)
            wk = wkv[:, :, :A_NOPE].reshape(KV_LORA, A_HEADS * A_NOPE).astype(BF16)
            wv = wkv[:, :, A_NOPE:].reshape(KV_LORA, A_HEADS * A_VDIM).astype(BF16)
            qlg, kvlg = q_lora_norm[j][None, :], kv_lora_norm[j][None, :]
            qng, kng = _pad_head_gain(q_norm_ab[j]), _pad_head_gain(k_norm_ab[j])

            y = _in_proj(x, norm_ab[j][None, :], mod, w_in)
            q, k, v, ckv = _amid_tokens(y, wq, wk, wv, qlg, kvlg, qng, kng, cos_a, sin_a)
            kpe_cache = jnp.pad(cache_kpe[:, j].reshape(DEC_BATCH * PAST_LEN, A_ROPE), ((0, 0), (0, LANES - A_ROPE)))
            kc, vc = _amid_cache(cache_ckv[:, j].reshape(DEC_BATCH * PAST_LEN, KV_LORA), kpe_cache, wk, wv, kvlg, kng,
                                 cos_a, sin_a)
            mix = jnp.zeros((ROWS, D_MODEL), BF16)
            mix = _attn_a(q, k, v, y, mix)
            mix = _attn_a(q, k, v, y, mix, kc=kc, vc=vc)
            hg = hgrn_out_norm[j][None, :]
            mix, sf, sb = _hgrn(y, lower[j], hg, mix, hg_consts)
            (mix,) = _hgrn(y, lower[j], hg, mix, hg_consts, s0f=state_hgrn_fwd[:, j], s0b=state_hgrn_bwd[:, j])
            x = _out_proj(mix, w_out_ab[j].astype(BF16), x, mod)

            new_ckv.append(ckv[:CTX_ROWS].reshape(BATCH, SEQ, KV_LORA))
            new_kpe.append(y[:CTX_ROWS, AB_KPE:AB_KPE + A_ROPE].reshape(BATCH, SEQ, A_ROPE))
            new_sf.append(sf)
            new_sb.append(sb)
        else:
            mod = mods_c[j].reshape(3 * N_COND, 1, D_MODEL)
            w = w_in_c[j]
            w_in = jnp.concatenate([w[:, :C_WIDTH], w[:, C_WIDTH + 2 * C_KV_WIDTH:],
                                    w[:, C_WIDTH:C_WIDTH + 2 * C_KV_WIDTH]], axis=1).astype(BF16)
            y = _in_proj(x, norm_c[j][None, :], mod, w_in)
            q, kn, k, v = _cmid(y, q_norm_c[j][None, :], k_norm_c[j][None, :], cos_c, sin_c)
            kc = cache_k_c[:, j].reshape(DEC_BATCH * PAST_LEN, C_KV_WIDTH).astype(BF16)
            vc = cache_v_c[:, j].reshape(DEC_BATCH * PAST_LEN, C_KV_WIDTH).astype(BF16)
            sink = jnp.broadcast_to(sink_c[j][:, None, None], (C_HEADS, 1, LANES))
            mix = jnp.zeros((ROWS, D_MODEL), BF16)
            mix = _attn_c(q, k, v, y, sink, mix)
            mix = _attn_c(q, k, v, y, sink, mix, kc=kc, vc=vc)
            x = _out_proj(mix, w_out_c[j].astype(BF16), x, mod)

            new_kc.append(kn[:CTX_ROWS].reshape(BATCH, SEQ, C_KV_HEADS, C_HEAD_DIM))
            new_vc.append(y[:CTX_ROWS, C_V:C_V + C_KV_WIDTH].reshape(BATCH, SEQ, C_KV_HEADS, C_HEAD_DIM))

    return (x[:CTX_ROWS].reshape(BATCH, SEQ, D_MODEL), x[CTX_ROWS:].reshape(DEC_BATCH, DEC_SEQ, D_MODEL),
            jnp.stack(new_ckv, axis=1), jnp.stack(new_kpe, axis=1), jnp.stack(new_sf, axis=1),
            jnp.stack(new_sb, axis=1), jnp.stack(new_kc, axis=1), jnp.stack(new_vc, axis=1))
```

```python
import functools

import numpy as np
import jax
import jax.numpy as jnp
from jax import lax
from jax.experimental import pallas as pl
from jax.experimental.pallas import tpu as pltpu

F32 = jnp.float32
BF16 = jnp.bfloat16

D_MODEL = 2048
BATCH = 16
SEQ = 256
DEPTH = 4
DEC_BATCH = 4
DEC_SEQ = 2048
PAST_LEN = 256
GRID_W = 64
A_HEADS = 8
A_NOPE = 128
A_ROPE = 64
A_VDIM = 128
A_QK = A_NOPE + A_ROPE
A_QK_PAD = 256
Q_LORA = 512
KV_LORA = 256
B_HEADS = 8
B_DK = 128
B_DV = 128
C_HEADS = 16
C_KV_HEADS = 4
C_GROUP = C_HEADS // C_KV_HEADS
C_HEAD_DIM = 128
C_WIDTH = C_HEADS * C_HEAD_DIM
C_KV_WIDTH = C_KV_HEADS * C_HEAD_DIM
WINDOW = 128
ROPE_BASE = 10000.0
EPS = 1e-6
NEG_BIG = -1e30
LOG2_E = 1.4426950408889634

LANES = 128
CTX_ROWS = BATCH * SEQ
LAT_ROWS = DEC_BATCH * DEC_SEQ
ROWS = CTX_ROWS + LAT_ROWS
N_COND = 8

AB_AGATE = 0
AB_BQ = 1024
AB_BFF = 2048
AB_BFB = 3072
AB_BI = 4096
AB_BGATE = 5120
AB_QLAT = 6144
AB_KVLAT = 6656
AB_KPE = 6912
AB_N = 7168
C_Q = 0
C_GATE = 2048
C_K = 4096
C_V = 4608
C_N = 5120

TM_PROJ = 1024
TN_IN = 1024
TM_OUT = 512
TM_MID = 512
TQ_A = 1024
TQ_C = 256
ATTN_SUB = 256
HG_CHUNK = 128
HG_LEVELS = 7
HG_UNROLL = 2
VMEM_LIMIT = 56 * 1024 * 1024
_MIX_SHAPE = jax.ShapeDtypeStruct((ROWS, D_MODEL), BF16)


def _cparams(sem):
    return pltpu.CompilerParams(dimension_semantics=sem, vmem_limit_bytes=VMEM_LIMIT)


def _silu(x):
    return x * jax.nn.sigmoid(x)


def _dot(a, b):
    return jnp.dot(a, b, preferred_element_type=F32)


def _dot_nt(a, b):
    return lax.dot_general(a, b, (((1,), (1,)), ((), ())), preferred_element_type=F32)


def _cond_of_tile(i, tm):
    n_ctx = CTX_ROWS // tm
    per_batch = DEC_SEQ // tm
    return jnp.where(i < n_ctx, 0, 1 + (i - n_ctx) // per_batch)


def _rope_block_of_tile(i, tm):
    n_ctx = CTX_ROWS // tm
    per_batch = DEC_SEQ // tm
    return jnp.where(i < n_ctx, 0, 1 + (i - n_ctx) % per_batch)


def _mod_kernel(c_ref, w_ref, b_ref, o_ref):
    a = _silu(c_ref[...]).astype(BF16)
    o_ref[...] = _dot(a, w_ref[...].astype(BF16)) + b_ref[...]


def _modulation(cond8, w_mod, b_mod):
    n = w_mod.shape[0]
    tn = 1024
    return pl.pallas_call(
        _mod_kernel,
        grid=(n, 3 * D_MODEL // tn),
        in_specs=[pl.BlockSpec((N_COND, D_MODEL), lambda l, j: (0, 0)),
                  pl.BlockSpec((None, D_MODEL, tn), lambda l, j: (l, 0, j)),
                  pl.BlockSpec((None, 1, tn), lambda l, j: (l, 0, j))],
        out_specs=pl.BlockSpec((None, N_COND, tn), lambda l, j: (l, 0, j)),
        out_shape=jax.ShapeDtypeStruct((n, N_COND, 3 * D_MODEL), F32),
        compiler_params=_cparams(("parallel", "parallel")),
        name="adaln_mod",
    )(cond8, w_mod, b_mod.reshape(n, 1, 3 * D_MODEL))


def _row_specs(parts, tm, single_buffer=False):
    if len(parts) == 1:
        return [pl.BlockSpec((tm, D_MODEL), lambda i, *_: (i, 0))]
    n_first = CTX_ROWS // tm
    mode = dict(pipeline_mode=pl.Buffered(1)) if single_buffer else {}
    return [pl.BlockSpec((tm, D_MODEL), lambda i, *_: (jnp.minimum(i, n_first - 1), 0), **mode),
            pl.BlockSpec((tm, D_MODEL), lambda i, *_: (jnp.maximum(i - n_first, 0), 0), **mode)]


def _in_kernel(*refs, n_x, w_tiles):
    x_refs = refs[:n_x]
    g_ref, sh_ref, sc_ref = refs[n_x:n_x + 3]
    w_refs = refs[n_x + 3:n_x + 3 + len(w_tiles)]
    o_ref, h_ref = refs[n_x + 3 + len(w_tiles):]
    i, j = pl.program_id(0), pl.program_id(1)

    def prologue(x_ref):
        x = x_ref[...]
        r = lax.rsqrt(jnp.mean(x * x, axis=-1, keepdims=True) + EPS)
        h = (x * r * g_ref[...]) * (1.0 + sc_ref[...]) + sh_ref[...]
        h_ref[...] = h.astype(BF16)

    if n_x == 1:
        pl.when(j == 0)(lambda: prologue(x_refs[0]))
    else:
        n_first = CTX_ROWS // TM_PROJ
        pl.when((j == 0) & (i < n_first))(lambda: prologue(x_refs[0]))
        pl.when((j == 0) & (i >= n_first))(lambda: prologue(x_refs[1]))

    def project(w_ref):
        o_ref[...] = _dot(h_ref[...], w_ref[...])

    lo = 0
    for w_ref, n in zip(w_refs, w_tiles):
        if len(w_tiles) == 1:
            project(w_ref)
        else:
            pl.when((j >= lo) & (j < lo + n))(functools.partial(project, w_ref))
        lo += n


def _in_proj(x_parts, g, mod, w_parts):
    tm, tn = TM_PROJ, TN_IN
    w_tiles = [n for _, n, _ in w_parts]
    w_specs = []
    lo = 0
    for _, n, col_map in w_parts:
        col_map = col_map or (lambda t: t)
        w_specs.append(pl.BlockSpec((D_MODEL, tn), lambda i, j, lo=lo, n=n, f=col_map: (0, f(jnp.clip(j - lo, 0, n - 1)))))
        lo += n
    return pl.pallas_call(
        functools.partial(_in_kernel, n_x=len(x_parts), w_tiles=w_tiles),
        grid=(ROWS // tm, sum(w_tiles)),
        in_specs=_row_specs(x_parts, tm, single_buffer=True) + [
            pl.BlockSpec((1, D_MODEL), lambda i, j: (0, 0)),
            pl.BlockSpec((None, 1, D_MODEL), lambda i, j: (_cond_of_tile(i, tm) * 3, 0, 0)),
            pl.BlockSpec((None, 1, D_MODEL), lambda i, j: (_cond_of_tile(i, tm) * 3 + 1, 0, 0))] + w_specs,
        out_specs=pl.BlockSpec((tm, tn), lambda i, j: (i, j)),
        out_shape=jax.ShapeDtypeStruct((ROWS, sum(w_tiles) * tn), F32),
        scratch_shapes=[pltpu.VMEM((tm, D_MODEL), BF16)],
        compiler_params=_cparams(("arbitrary", "arbitrary")),
        name="norm_mod_in_proj",
    )(*x_parts, g, mod, mod, *[w for w, _, _ in w_parts])


def _out_kernel(*refs, n_x, n_o):
    m_ref, w_ref = refs[:2]
    x_refs = refs[2:2 + n_x]
    gt_ref = refs[2 + n_x]
    o_refs = refs[3 + n_x:]
    y = gt_ref[...] * _dot(m_ref[...], w_ref[...])

    def emit(x_ref, o_ref):
        o_ref[...] = x_ref[...] + y

    if n_x == 1 and n_o == 1:
        emit(x_refs[0], o_refs[0])
    else:
        in_first = pl.program_id(0) < CTX_ROWS // TM_OUT
        pl.when(in_first)(lambda: emit(x_refs[0], o_refs[0]))
        pl.when(jnp.logical_not(in_first))(lambda: emit(x_refs[-1], o_refs[-1]))


def _out_proj(mix, w, x_parts, mod, split_out):
    tm = TM_OUT
    if split_out:
        out_parts = [jax.ShapeDtypeStruct((CTX_ROWS, D_MODEL), F32), jax.ShapeDtypeStruct((LAT_ROWS, D_MODEL), F32)]
    else:
        out_parts = [jax.ShapeDtypeStruct((ROWS, D_MODEL), F32)]
    return pl.pallas_call(
        functools.partial(_out_kernel, n_x=len(x_parts), n_o=len(out_parts)),
        grid=(ROWS // tm,),
        in_specs=[pl.BlockSpec((tm, D_MODEL), lambda i: (i, 0)),
                  pl.BlockSpec((D_MODEL, D_MODEL), lambda i: (0, 0))] + _row_specs(x_parts, tm) + [
                  pl.BlockSpec((None, 1, D_MODEL), lambda i: (_cond_of_tile(i, tm) * 3 + 2, 0, 0))],
        out_specs=_row_specs(out_parts, tm),
        out_shape=out_parts,
        compiler_params=_cparams(("arbitrary",)),
        name="out_proj_residual",
    )(mix, w, *x_parts, mod)


def _rope_a(x, cos_t, sin_t):
    return x * cos_t + (pltpu.roll(x, 32, 1) + pltpu.roll(x, 96, 1)) * sin_t


def _amid_kernel(*refs, do_q, norm_kv):
    if do_q:
        (ql_ref, kvl_ref, kpe_ref, wq_ref, wk_ref, wv_ref, qlg_ref, kvlg_ref, qng_ref, kng_ref, cos_ref, sin_ref,
         q_out, k_out, v_out, ckv_out, kpe_out) = refs
    else:
        (kvl_ref, kpe_ref, wk_ref, wv_ref, kvlg_ref, kng_ref, cos_ref, sin_ref, k_out, v_out) = refs
    cos_t = cos_ref[...]
    sin_t = sin_ref[...]
    inv_qk = 1.0 / A_QK

    if do_q:
        ql = ql_ref[...]
        qn = ql * lax.rsqrt(jnp.mean(ql * ql, axis=-1, keepdims=True) + EPS) * qlg_ref[...]
        qu = _dot(qn.astype(BF16), wq_ref[...])
        g_nope = qng_ref[:, :A_NOPE]
        g_rope = qng_ref[:, A_NOPE:]
        for h in range(A_HEADS):
            a = qu[:, h * A_QK_PAD:h * A_QK_PAD + A_NOPE]
            b = qu[:, h * A_QK_PAD + A_NOPE:(h + 1) * A_QK_PAD]
            ss = jnp.sum(a * a, axis=-1, keepdims=True) + jnp.sum(b * b, axis=-1, keepdims=True)
            r = lax.rsqrt(ss * inv_qk + EPS)
            q_out[:, h * A_QK_PAD:h * A_QK_PAD + A_NOPE] = (a * r * g_nope).astype(BF16)
            q_out[:, h * A_QK_PAD + A_NOPE:(h + 1) * A_QK_PAD] = _rope_a(b * r * g_rope, cos_t, sin_t).astype(BF16)

    kvl = kvl_ref[...]
    if norm_kv:
        ckv = kvl * lax.rsqrt(jnp.mean(kvl * kvl, axis=-1, keepdims=True) + EPS) * kvlg_ref[...]
    else:
        ckv = kvl
    ckv_b = ckv.astype(BF16)
    kn = _dot(ckv_b, wk_ref[...])
    v_out[...] = _dot(ckv_b, wv_ref[...]).astype(BF16)
    kpe = kpe_ref[...]
    if do_q:
        @pl.when(pl.program_id(0) < CTX_ROWS // TM_MID)
        def _():
            ckv_out[...] = ckv.reshape(ckv_out.shape)
            kpe_out[...] = kpe[:, :A_ROPE].reshape(kpe_out.shape)
    sp = jnp.sum(kpe * kpe, axis=-1, keepdims=True)
    g_nope = kng_ref[:, :A_NOPE]
    g_rope = kng_ref[:, A_NOPE:]
    for h in range(A_HEADS):
        a = kn[:, h * A_NOPE:(h + 1) * A_NOPE]
        r = lax.rsqrt((jnp.sum(a * a, axis=-1, keepdims=True) + sp) * inv_qk + EPS)
        k_out[:, h * A_QK_PAD:h * A_QK_PAD + A_NOPE] = (a * r * g_nope).astype(BF16)
        k_out[:, h * A_QK_PAD + A_NOPE:(h + 1) * A_QK_PAD] = _rope_a(kpe * r * g_rope, cos_t, sin_t).astype(BF16)


def _layer_slab_spec(tm, layer, width):
    n_ctx = CTX_ROWS // tm
    return pl.BlockSpec((tm // SEQ, None, SEQ, width), lambda i: (jnp.minimum(i, n_ctx - 1), layer, 0, 0))


def _call_with_carried(kernel, n_plain_out, carried, **kw):
    def run(*args, in_specs):
        args = list(args)
        in_specs = list(in_specs)
        n_real = len(args)
        aliases = {}
        for idx, arr in enumerate(carried):
            if arr is not None:
                aliases[len(args)] = n_plain_out + idx
                args.append(arr)
                in_specs.append(pl.BlockSpec(memory_space=pl.ANY))
        n_in = len(args)

        def body(*refs):
            kernel(*refs[:n_real], *refs[n_in:])

        return pl.pallas_call(body, in_specs=in_specs, input_output_aliases=aliases, **kw)(*args)
    return run


def _amid_tokens(y, wq, wk, wv, qlg, kvlg, qng, kng, cos_t, sin_t, layer, ckv_new, kpe_new):
    tm = TM_MID
    const = lambda i: (0, 0)
    rope_map = lambda i: (_rope_block_of_tile(i, tm), 0)
    n_ab = (DEPTH + 1) // 2
    run = _call_with_carried(
        functools.partial(_amid_kernel, do_q=True, norm_kv=True), 3, [ckv_new, kpe_new],
        grid=(ROWS // tm,),
        out_specs=[pl.BlockSpec((tm, A_HEADS * A_QK_PAD), lambda i: (i, 0)),
                   pl.BlockSpec((tm, A_HEADS * A_QK_PAD), lambda i: (i, 0)),
                   pl.BlockSpec((tm, A_HEADS * A_VDIM), lambda i: (i, 0)),
                   _layer_slab_spec(tm, layer, KV_LORA), _layer_slab_spec(tm, layer, A_ROPE)],
        out_shape=[jax.ShapeDtypeStruct((ROWS, A_HEADS * A_QK_PAD), BF16),
                   jax.ShapeDtypeStruct((ROWS, A_HEADS * A_QK_PAD), BF16),
                   jax.ShapeDtypeStruct((ROWS, A_HEADS * A_VDIM), BF16),
                   jax.ShapeDtypeStruct((BATCH, n_ab, SEQ, KV_LORA), F32),
                   jax.ShapeDtypeStruct((BATCH, n_ab, SEQ, A_ROPE), F32)],
        compiler_params=_cparams(("arbitrary",)),
        name="mla_qkv_prep")
    return run(y, y, y, wq, wk, wv, qlg, kvlg, qng, kng, cos_t, sin_t,
               in_specs=[pl.BlockSpec((tm, Q_LORA), lambda i: (i, AB_QLAT // Q_LORA)),
                         pl.BlockSpec((tm, KV_LORA), lambda i: (i, AB_KVLAT // KV_LORA)),
                         pl.BlockSpec((tm, LANES), lambda i: (i, AB_KPE // LANES)),
                         pl.BlockSpec(wq.shape, const), pl.BlockSpec(wk.shape, const), pl.BlockSpec(wv.shape, const),
                         pl.BlockSpec(qlg.shape, const), pl.BlockSpec(kvlg.shape, const),
                         pl.BlockSpec(qng.shape, const), pl.BlockSpec(kng.shape, const),
                         pl.BlockSpec((tm, LANES), rope_map), pl.BlockSpec((tm, LANES), rope_map)])


def _amid_cache(ckv, kpe, wk, wv, kvlg, kng, cos_t, sin_t):
    rows = ckv.shape[0]
    tm = TM_MID
    const = lambda i: (0, 0)
    outs = pl.pallas_call(
        functools.partial(_amid_kernel, do_q=False, norm_kv=False),
        grid=(rows // tm,),
        in_specs=[pl.BlockSpec((tm, KV_LORA), lambda i: (i, 0)),
                  pl.BlockSpec((tm, LANES), lambda i: (i, 0)),
                  pl.BlockSpec(wk.shape, const), pl.BlockSpec(wv.shape, const),
                  pl.BlockSpec(kvlg.shape, const), pl.BlockSpec(kng.shape, const),
                  pl.BlockSpec((tm, LANES), const), pl.BlockSpec((tm, LANES), const)],
        out_specs=[pl.BlockSpec((tm, A_HEADS * A_QK_PAD), lambda i: (i, 0)),
                   pl.BlockSpec((tm, A_HEADS * A_VDIM), lambda i: (i, 0))],
        out_shape=[jax.ShapeDtypeStruct((rows, A_HEADS * A_QK_PAD), BF16),
                   jax.ShapeDtypeStruct((rows, A_HEADS * A_VDIM), BF16)],
        compiler_params=_cparams(("parallel",)),
        name="mla_cache_kv_prep",
    )(ckv, kpe, wk, wv, kvlg, kng, cos_t, sin_t)
    return outs[0], outs[1]


def _scores(q, srcs, c):
    zs = []
    for k, _, keep in srcs:
        z = _dot_nt(q, k) * c
        zs.append(z if keep is None else jnp.where(keep, z, NEG_BIG))
    return zs


def _softmax_pv(zs, srcs, sink_z=None):
    tile_max = None
    for z in zs:
        for j in range(z.shape[1] // LANES):
            blk = z[:, j * LANES:(j + 1) * LANES]
            tile_max = blk if tile_max is None else jnp.maximum(tile_max, blk)
    m = tile_max.max(axis=-1, keepdims=True)
    if sink_z is not None:
        m = jnp.maximum(m, sink_z)
    acc = None
    for z, (_, v, _) in zip(zs, srcs):
        o = _dot(jnp.exp2(z - m).astype(BF16), jnp.concatenate([v, jnp.ones_like(v)], axis=1))
        acc = o if acc is None else acc + o
    dv = acc.shape[1] // 2
    den = acc[:, dv:]
    if sink_z is not None:
        den = den + jnp.exp2(sink_z - m)
    return acc[:, :dv] / den


def _attend_streams(n, q_of, srcs_of, c, sink_of=None):
    outs = []
    zs = _scores(q_of(0), srcs_of(0), c)
    for t in range(n):
        nxt = _scores(q_of(t + 1), srcs_of(t + 1), c) if t + 1 < n else None
        outs.append(_softmax_pv(zs, srcs_of(t), None if sink_of is None else sink_of(t)))
        zs = nxt
    return outs


def _attn_a_kernel(*refs, n_src, heads):
    q_ref = refs[0]
    k_refs = refs[1:1 + n_src]
    v_refs = refs[1 + n_src:1 + 2 * n_src]
    gate_ref, o_ref = refs[1 + 2 * n_src:]
    sub = min(ATTN_SUB, q_ref.shape[0])
    streams = [(h, r) for h in range(heads) for r in range(q_ref.shape[0] // sub)]

    def rows(t):
        return slice(streams[t][1] * sub, (streams[t][1] + 1) * sub)

    def q_of(t):
        h = streams[t][0]
        return q_ref[rows(t), h * A_QK_PAD:(h + 1) * A_QK_PAD]

    def srcs_of(t):
        h = streams[t][0]
        return [(k_ref[:, h * A_QK_PAD:(h + 1) * A_QK_PAD], v_ref[:, h * A_VDIM:(h + 1) * A_VDIM], None)
                for k_ref, v_ref in zip(k_refs, v_refs)]

    outs = _attend_streams(len(streams), q_of, srcs_of, A_QK ** -0.5 * LOG2_E)
    for t, o in enumerate(outs):
        cols = slice(streams[t][0] * A_VDIM, (streams[t][0] + 1) * A_VDIM)
        o_ref[rows(t), cols] = (o * _silu(gate_ref[rows(t), cols])).astype(o_ref.dtype)


def _attn_a(q, k, v, y, mix, kc=None, vc=None):
    latent = kc is not None
    if latent:
        nb, t, tq, hp, row0 = DEC_BATCH, DEC_SEQ, TQ_A, 1, CTX_ROWS
    else:
        nb, t, tq, hp, row0 = BATCH, SEQ, SEQ, A_HEADS, 0
    nq = t // tq
    qrow = lambda b, h, i: row0 // tq + b * nq + i
    in_specs = [pl.BlockSpec((tq, hp * A_QK_PAD), lambda b, h, i: (qrow(b, h, i), h))]
    args = [q]
    if latent:
        in_specs.append(pl.BlockSpec((PAST_LEN, hp * A_QK_PAD), lambda b, h, i: (b, h)))
        args.append(kc)
    in_specs.append(pl.BlockSpec((t, hp * A_QK_PAD), lambda b, h, i: (row0 // t + b, h)))
    args.append(k)
    if latent:
        in_specs.append(pl.BlockSpec((PAST_LEN, hp * A_VDIM), lambda b, h, i: (b, h)))
        args.append(vc)
    in_specs.append(pl.BlockSpec((t, hp * A_VDIM), lambda b, h, i: (row0 // t + b, h)))
    args.append(v)
    in_specs.append(pl.BlockSpec((tq, hp * A_VDIM), lambda b, h, i: (qrow(b, h, i), AB_AGATE // (hp * A_VDIM) + h)))
    args.append(y)
    run = _call_with_carried(
        functools.partial(_attn_a_kernel, n_src=2 if latent else 1, heads=hp), 0, [mix],
        grid=(nb, A_HEADS // hp, nq),
        out_specs=[pl.BlockSpec((tq, hp * A_VDIM), lambda b, h, i: (qrow(b, h, i), h))],
        out_shape=[_MIX_SHAPE],
        compiler_params=_cparams(("parallel", "parallel", "parallel")),
        name="mla_attention_latent" if latent else "mla_attention_context")
    return run(*args, in_specs=in_specs)[0]


def _hgrn_constants():
    c, nl = HG_CHUNK, HG_LEVELS
    t = np.arange(c)[:, None]
    u = np.arange(c)[None, :]
    tri_f = (u <= t).astype(np.float32)
    mask_f = np.zeros((nl, c, c), np.float32)
    for l in range(nl):
        half = c >> (l + 1)
        seg = 2 * half
        mask_f[l] = ((u // seg) == (t // seg)) & ((t % seg) >= half) & ((u % seg) < half)
    tri_b = tri_f[::-1, ::-1]
    mask_b = mask_f[:, ::-1, ::-1]
    to_tri = lambda a: jnp.asarray(np.concatenate([a, a, a], axis=1), BF16)
    to_mask = lambda a: jnp.asarray(np.ascontiguousarray(a), F32)
    return to_tri(tri_f), to_tri(tri_b), to_mask(mask_f), to_mask(mask_b)


def _hgrn_decays(x, lb, tri3):
    kk = (1.0 - lb) * jax.nn.sigmoid(-x)
    lf = jnp.log1p(-kk) * LOG2_E
    hi = lf.astype(BF16)
    r1 = lf - hi.astype(F32)
    mid = r1.astype(BF16)
    lo = (r1 - mid.astype(F32)).astype(BF16)
    return kk, lf, _dot(tri3, jnp.concatenate([hi, mid, lo], axis=0))


def _hgrn_level_arg(l, cs, lf, forward):
    c = HG_CHUNK
    nv = c // 8
    half = c >> (l + 1)
    cs3 = cs.reshape(nv, 8, LANES)
    sub = lax.broadcasted_iota(jnp.int32, (nv, 8, LANES), 1)

    def in_vreg_row(r):
        return jnp.broadcast_to(cs3[:, r:r + 1, :], (nv, 8, LANES))

    if half >= 8:
        m = half // 8
        nseg = nv // (2 * m)
        edge = cs3[:, 7:8, :] if forward else cs3[:, 0:1, :]
        e4 = edge.reshape(nseg, 2 * m, 1, LANES)
        a = e4[:, m - 1:m] if forward else e4[:, m:m + 1]
        anchor = jnp.broadcast_to(a, (nseg, 2 * m, 8, LANES)).reshape(c, LANES)
    elif half == 4:
        anchor = in_vreg_row(3 if forward else 4).reshape(c, LANES)
    elif half == 2:
        r0, r1 = (1, 5) if forward else (2, 6)
        anchor = jnp.where(sub < 4, in_vreg_row(r0), in_vreg_row(r1)).reshape(c, LANES)
    else:
        is_query = (sub % 2 == 1) if forward else (sub % 2 == 0)
        return jnp.where(is_query.reshape(c, LANES), lf, 0.0)
    return -jnp.abs(cs - anchor)


def _hgrn_kernel(*refs, t_len, zero_init, emit_state):
    c = HG_CHUNK
    n_chunks = t_len // c
    it = iter(refs)
    bq_ref, ff_ref, fb_ref, vi_ref, bg_ref, lb_ref, hg_ref = (next(it) for _ in range(7))
    if not zero_init:
        s0f_ref, s0b_ref = next(it), next(it)
    trif_ref, trib_ref, maskf_ref, maskb_ref = (next(it) for _ in range(4))
    o_ref = next(it)
    if emit_state:
        sf_ref, sb_ref = next(it), next(it)
    of_ref, ob_ref, stf_ref, stb_ref = (next(it) for _ in range(4))

    if zero_init:
        stf_ref[...] = jnp.zeros((B_DV, B_DK), F32)
        stb_ref[...] = jnp.zeros((B_DV, B_DK), F32)
    else:
        stf_ref[...] = s0f_ref[...].T
        stb_ref[...] = s0b_ref[...].T
    lb = lb_ref[...]

    nl = HG_LEVELS
    dirs = ((ff_ref, lb[0:1, :], trif_ref, maskf_ref, stf_ref, of_ref, True),
            (fb_ref, lb[1:2, :], trib_ref, maskb_ref, stb_ref, ob_ref, False))

    def body(i, carry):
        chains = []
        for f_ref, lb_row, tri_ref, mask_ref, st_ref, out_ref, forward in dirs:
            for u in range(HG_UNROLL):
                k = i * HG_UNROLL + u
                r0 = pl.multiple_of((k if forward else n_chunks - 1 - k) * c, c)
                chains.append(dict(rows=pl.ds(r0, c), f_ref=f_ref, lb=lb_row, tri_ref=tri_ref, mask_ref=mask_ref,
                                   st_ref=st_ref, out_ref=out_ref, forward=forward))
        for ch in chains:
            ch["kk"], ch["lf"], ch["cs"] = _hgrn_decays(ch["f_ref"][ch["rows"], :], ch["lb"], ch["tri_ref"][...])
            ch["q"] = _silu(bq_ref[ch["rows"], :])
            ch["v"] = vi_ref[ch["rows"], :]
            ch["sc"] = jnp.zeros((c, c), F32)
        for l in range(nl):
            for ch in chains:
                el = jnp.exp2(_hgrn_level_arg(l, ch["cs"], ch["lf"], ch["forward"]))
                ch["sc"] = ch["sc"] + ch["mask_ref"][l] * _dot_nt((ch["q"] * el).astype(BF16),
                                                                  (ch["kk"] * el).astype(BF16))
        for ch in chains:
            cs, q, kk, v = ch["cs"], ch["q"], ch["kk"], ch["v"]
            end = cs[c - 1:c, :] if ch["forward"] else cs[0:1, :]
            ch["end"] = end
            ch["q_in"] = (q * jnp.exp2(cs)).astype(BF16)
            ch["upd"] = _dot(v.T.astype(BF16), (kk * jnp.exp2(-jnp.abs(cs - end))).astype(BF16))
            ch["o"] = _dot(ch["sc"].astype(BF16), v.astype(BF16)) + jnp.sum(q * kk, axis=-1, keepdims=True) * v
        for ch in chains:
            st = ch["st_ref"][...]
            ch["out_ref"][ch["rows"], :] = ch["o"] + _dot_nt(ch["q_in"], st.astype(BF16))
            ch["st_ref"][...] = jnp.exp2(ch["end"]) * st + ch["upd"]
        return carry

    lax.fori_loop(0, n_chunks // HG_UNROLL, body, 0)
    if emit_state:
        sf_ref[...] = stf_ref[...].T
        sb_ref[...] = stb_ref[...].T

    hg = hg_ref[...]

    def finish(i, carry):
        rows = pl.ds(pl.multiple_of(i * c, c), c)
        o = of_ref[rows, :] + ob_ref[rows, :]
        o = o * lax.rsqrt(jnp.mean(o * o, axis=-1, keepdims=True) + EPS) * hg
        o_ref[rows, :] = (o * _silu(bg_ref[rows, :])).astype(o_ref.dtype)
        return carry

    lax.fori_loop(0, n_chunks, finish, 0)


def _hgrn(y, lb, hg, mix, consts, s0f=None, s0b=None, layer=None, sf_new=None, sb_new=None):
    latent = s0f is not None
    if latent:
        nb, t, row0 = DEC_BATCH, DEC_SEQ, CTX_ROWS
    else:
        nb, t, row0 = BATCH, SEQ, 0
    rb = lambda b: row0 // t + b
    col = lambda off: (lambda b, h: (rb(b), off // LANES + h))
    const2 = lambda b, h: (0, 0)
    const3 = lambda b, h: (0, 0, 0)
    in_specs = [pl.BlockSpec((t, LANES), col(AB_BQ)), pl.BlockSpec((t, LANES), col(AB_BFF)),
                pl.BlockSpec((t, LANES), col(AB_BFB)), pl.BlockSpec((t, LANES), col(AB_BI)),
                pl.BlockSpec((t, LANES), col(AB_BGATE)),
                pl.BlockSpec((2, LANES), lambda b, h: (0, h)),
                pl.BlockSpec((1, LANES), const2)]
    args = [y, y, y, y, y, lb, hg]
    if latent:
        st_spec = pl.BlockSpec((None, None, B_DK, B_DV), lambda b, h: (b, h, 0, 0))
        in_specs += [st_spec, st_spec]
        args += [s0f, s0b]
    tri_f, tri_b, mask_f, mask_b = consts
    in_specs += [pl.BlockSpec(tri_f.shape, const2), pl.BlockSpec(tri_b.shape, const2),
                 pl.BlockSpec(mask_f.shape, const3), pl.BlockSpec(mask_b.shape, const3)]
    args += [tri_f, tri_b, mask_f, mask_b]
    out_specs = [pl.BlockSpec((t, LANES), lambda b, h: (rb(b), B_HEADS + h))]
    out_shape = [_MIX_SHAPE]
    carried = [mix]
    if not latent:
        st_out = pl.BlockSpec((None, None, None, B_DK, B_DV), lambda b, h: (b, layer, h, 0, 0))
        out_specs += [st_out, st_out]
        out_shape += [jax.ShapeDtypeStruct((nb, (DEPTH + 1) // 2, B_HEADS, B_DK, B_DV), F32)] * 2
        carried += [sf_new, sb_new]
    run = _call_with_carried(
        functools.partial(_hgrn_kernel, t_len=t, zero_init=not latent, emit_state=not latent), 0, carried,
        grid=(nb, B_HEADS),
        out_specs=out_specs,
        out_shape=out_shape,
        scratch_shapes=[pltpu.VMEM((t, B_DV), F32), pltpu.VMEM((t, B_DV), F32),
                        pltpu.VMEM((B_DV, B_DK), F32), pltpu.VMEM((B_DV, B_DK), F32)],
        compiler_params=_cparams(("parallel", "parallel")),
        name="hgrn2_latent" if latent else "hgrn2_context")
    return run(*args, in_specs=in_specs)


def _cmid_kernel(q_ref, k_ref, v_ref, qg_ref, kg_ref, cos_ref, sin_ref, q_out, k_out, v_out, kc_out, vc_out):
    cos_t = cos_ref[...]
    sin_t = sin_ref[...]
    qg = qg_ref[...]
    kg = kg_ref[...]
    in_ctx = pl.program_id(0) < CTX_ROWS // TM_MID

    def norm(x, g):
        return x * lax.rsqrt(jnp.mean(x * x, axis=-1, keepdims=True) + EPS) * g

    def rope(x):
        return x * cos_t + pltpu.roll(x, C_HEAD_DIM // 2, 1) * sin_t

    for h in range(C_HEADS):
        sl = slice(h * C_HEAD_DIM, (h + 1) * C_HEAD_DIM)
        q_out[:, sl] = rope(norm(q_ref[:, sl], qg)).astype(BF16)
    for h in range(C_KV_HEADS):
        sl = slice(h * C_HEAD_DIM, (h + 1) * C_HEAD_DIM)
        kn = norm(k_ref[:, sl], kg)
        k_out[:, sl] = rope(kn).astype(BF16)

        @pl.when(in_ctx)
        def _():
            kc_out[:, :, sl] = kn.reshape(kc_out.shape[0], SEQ, C_HEAD_DIM)
    v = v_ref[...]
    v_out[...] = v.astype(BF16)

    @pl.when(in_ctx)
    def _():
        vc_out[...] = v.reshape(vc_out.shape)


def _cmid(y, qg, kg, cos_t, sin_t, layer, kc_new, vc_new):
    tm = TM_MID
    const = lambda i: (0, 0)
    rope_map = lambda i: (_rope_block_of_tile(i, tm), 0)
    cache = jax.ShapeDtypeStruct((BATCH, DEPTH // 2, SEQ, C_KV_WIDTH), F32)
    run = _call_with_carried(
        _cmid_kernel, 3, [kc_new, vc_new],
        grid=(ROWS // tm,),
        out_specs=[pl.BlockSpec((tm, C_WIDTH), lambda i: (i, 0)),
                   pl.BlockSpec((tm, C_KV_WIDTH), lambda i: (i, 0)),
                   pl.BlockSpec((tm, C_KV_WIDTH), lambda i: (i, 0)),
                   _layer_slab_spec(tm, layer, C_KV_WIDTH), _layer_slab_spec(tm, layer, C_KV_WIDTH)],
        out_shape=[jax.ShapeDtypeStruct((ROWS, C_WIDTH), BF16),
                   jax.ShapeDtypeStruct((ROWS, C_KV_WIDTH), BF16),
                   jax.ShapeDtypeStruct((ROWS, C_KV_WIDTH), BF16), cache, cache],
        compiler_params=_cparams(("arbitrary",)),
        name="gqa_qkv_prep")
    return run(y, y, y, qg, kg, cos_t, sin_t,
               in_specs=[pl.BlockSpec((tm, C_WIDTH), lambda i: (i, C_Q // C_WIDTH)),
                         pl.BlockSpec((tm, C_KV_WIDTH), lambda i: (i, C_K // C_KV_WIDTH)),
                         pl.BlockSpec((tm, C_KV_WIDTH), lambda i: (i, C_V // C_KV_WIDTH)),
                         pl.BlockSpec((1, C_HEAD_DIM), const), pl.BlockSpec((1, C_HEAD_DIM), const),
                         pl.BlockSpec((tm, LANES), rope_map), pl.BlockSpec((tm, LANES), rope_map)])


def _attn_c_kernel(*refs, band, tq, t_len, groups, stack):
    if band:
        q_ref, kc_ref, vc_ref, kl_ref, vl_ref, sink_ref, gate_ref, o_ref = refs
    else:
        q_ref, kc_ref, vc_ref, sink_ref, gate_ref, o_ref = refs
    hd = C_HEAD_DIM
    rows = stack * tq
    streams = [(g, s) for g in range(groups) for s in range(C_GROUP // stack)]
    if band:
        i = pl.program_id(2)
        width = tq + 2 * WINDOW
        start = pl.multiple_of(jnp.clip(i * tq - WINDOW, 0, t_len - width), WINDOW)
        qpos = i * tq + (lax.broadcasted_iota(jnp.int32, (rows, width), 0) & (tq - 1))
        kpos = start + lax.broadcasted_iota(jnp.int32, (rows, width), 1)
        in_band = jnp.abs(kpos - qpos) <= WINDOW

    def heads_of(t):
        g, s = streams[t]
        return [g * C_GROUP + s * stack + r for r in range(stack)]

    def q_of(t):
        return jnp.concatenate([q_ref[:, h * hd:(h + 1) * hd] for h in heads_of(t)], axis=0)

    def srcs_of(t):
        g = streams[t][0]
        cols = slice(g * hd, (g + 1) * hd)
        srcs = [(kc_ref[:, cols], vc_ref[:, cols], None)]
        if band:
            srcs.append((kl_ref[pl.ds(start, width), cols], vl_ref[pl.ds(start, width), cols], in_band))
        return srcs

    def sink_of(t):
        return jnp.concatenate([jnp.broadcast_to(sink_ref[h][:, :1] * LOG2_E, (tq, 1)) for h in heads_of(t)], axis=0)

    outs = _attend_streams(len(streams), q_of, srcs_of, hd ** -0.5 * LOG2_E, sink_of)
    for t, o in enumerate(outs):
        for r, h in enumerate(heads_of(t)):
            cols = slice(h * hd, (h + 1) * hd)
            o_ref[:, cols] = (o[r * tq:(r + 1) * tq] * _silu(gate_ref[:, cols])).astype(o_ref.dtype)


def _attn_c(q, k, v, y, sink, mix, kc=None, vc=None):
    latent = kc is not None
    if latent:
        nb, t, tq, gp, stack, row0 = DEC_BATCH, DEC_SEQ, TQ_C, 1, 2, CTX_ROWS
    else:
        nb, t, tq, gp, stack, row0 = BATCH, SEQ, SEQ, C_KV_HEADS, C_GROUP, 0
    gw = gp * C_GROUP * C_HEAD_DIM
    kvw = gp * C_HEAD_DIM
    nq = t // tq
    qrow = lambda b, g, i: row0 // tq + b * nq + i
    own_kv = pl.BlockSpec((t, kvw), lambda b, g, i: (row0 // t + b, g))
    in_specs = [pl.BlockSpec((tq, gw), lambda b, g, i: (qrow(b, g, i), g))]
    args = [q]
    if latent:
        ctx_kv = pl.BlockSpec((PAST_LEN, kvw), lambda b, g, i: (b, g))
        in_specs += [ctx_kv, ctx_kv, own_kv, own_kv]
        args += [kc, vc, k, v]
    else:
        in_specs += [own_kv, own_kv]
        args += [k, v]
    in_specs.append(pl.BlockSpec((gp * C_GROUP, 1, LANES), lambda b, g, i: (g, 0, 0)))
    args.append(sink)
    in_specs.append(pl.BlockSpec((tq, gw), lambda b, g, i: (qrow(b, g, i), C_GATE // gw + g)))
    args.append(y)
    run = _call_with_carried(
        functools.partial(_attn_c_kernel, band=latent, tq=tq, t_len=t, groups=gp, stack=stack), 0, [mix],
        grid=(nb, C_KV_HEADS // gp, nq),
        out_specs=[pl.BlockSpec((tq, gw), lambda b, g, i: (qrow(b, g, i), g))],
        out_shape=[_MIX_SHAPE],
        compiler_params=_cparams(("parallel", "parallel", "parallel")),
        name="gqa_attention_latent" if latent else "gqa_attention_context")
    return run(*args, in_specs=in_specs)[0]


def _axial_angles(n_tokens, rot_dim):
    rows = n_tokens // GRID_W
    row = jnp.repeat(jnp.arange(rows, dtype=F32), GRID_W)
    col = jnp.tile(jnp.arange(GRID_W, dtype=F32), rows)
    n_freq = rot_dim // 4
    inv = ROPE_BASE ** (-jnp.arange(n_freq, dtype=F32) / n_freq)
    return jnp.concatenate([row[:, None] * inv, col[:, None] * inv], axis=-1)


def _rope_tables(rot_dim, tm):
    ang = _axial_angles(DEC_SEQ, rot_dim)
    cos, sin = jnp.cos(ang), jnp.sin(ang)
    pad = LANES - rot_dim
    cos_t = jnp.concatenate([cos, cos, jnp.ones((DEC_SEQ, pad), F32)], axis=-1)
    sin_t = jnp.concatenate([-sin, sin, jnp.zeros((DEC_SEQ, pad), F32)], axis=-1)
    cos_t = jnp.concatenate([jnp.ones((tm, LANES), F32), cos_t], axis=0)
    sin_t = jnp.concatenate([jnp.zeros((tm, LANES), F32), sin_t], axis=0)
    return cos_t, sin_t


def _lower_bounds(lb_logits):
    p = jax.nn.softmax(lb_logits.astype(F32), axis=0)
    return jnp.cumsum(p, axis=0) - p[0:1]


def _pad_head_gain(g):
    return jnp.concatenate([g, jnp.zeros((A_QK_PAD - A_QK,), F32)])[None, :]


def kernel(x_prompt, x_sample, cache_ckv, cache_kpe, state_hgrn_fwd, state_hgrn_bwd, cache_k_c, cache_v_c, c, c_ctx,
           mod_w_ab, mod_b_ab, norm_ab, w_in_ab, q_lora_norm, kv_lora_norm, w_q_up, w_kv_up, q_norm_ab, k_norm_ab,
           hgrn_lb_logits, hgrn_out_norm, w_out_ab, mod_w_c, mod_b_c, norm_c, w_in_c, q_norm_c, k_norm_c, sink_c,
           w_out_c):
    x_parts = [x_prompt.reshape(CTX_ROWS, D_MODEL), x_sample.reshape(LAT_ROWS, D_MODEL)]
    cond8 = jnp.concatenate([c_ctx[None, :], c, jnp.zeros((N_COND - 1 - DEC_BATCH, D_MODEL), F32)], axis=0)
    mods_ab = _modulation(cond8, mod_w_ab, mod_b_ab)
    mods_c = _modulation(cond8, mod_w_c, mod_b_c)
    lower = _lower_bounds(hgrn_lb_logits)
    cos_a, sin_a = _rope_tables(A_ROPE, TM_MID)
    cos_c, sin_c = _rope_tables(C_HEAD_DIM, TM_MID)
    hg_consts = _hgrn_constants()

    ckv_new = kpe_new = sf_new = sb_new = kc_new = vc_new = None
    for layer in range(DEPTH):
        j = layer // 2
        last = layer == DEPTH - 1
        if layer % 2 == 0:
            mod = mods_ab[j].reshape(3 * N_COND, 1, D_MODEL)
            w = w_in_ab[j]
            n_head = Q_LORA + KV_LORA + A_ROPE
            w_tail = jnp.concatenate([w[:, :n_head], jnp.zeros((D_MODEL, TN_IN - n_head), F32)], axis=1).astype(BF16)
            w_parts = [(w[:, n_head:].astype(BF16), (AB_N - TN_IN) // TN_IN, None), (w_tail, 1, None)]
            wq = jnp.pad(w_q_up[j].reshape(Q_LORA, A_HEADS, A_QK),
                         ((0, 0), (0, 0), (0, A_QK_PAD - A_QK))).reshape(Q_LORA, A_HEADS * A_QK_PAD).astype(BF16)
            wkv = w_kv_up[j].reshape(KV_LORA, A_HEADS, A_NOPE + A_VDIM)
            wk = wkv[:, :, :A_NOPE].reshape(KV_LORA, A_HEADS * A_NOPE).astype(BF16)
            wv = wkv[:, :, A_NOPE:].reshape(KV_LORA, A_HEADS * A_VDIM).astype(BF16)
            qlg, kvlg = q_lora_norm[j][None, :], kv_lora_norm[j][None, :]
            qng, kng = _pad_head_gain(q_norm_ab[j]), _pad_head_gain(k_norm_ab[j])

            y = _in_proj(x_parts, norm_ab[j][None, :], mod, w_parts)
            q, k, v, ckv_new, kpe_new = _amid_tokens(y, wq, wk, wv, qlg, kvlg, qng, kng, cos_a, sin_a,
                                                     j, ckv_new, kpe_new)
            kpe_cache = jnp.pad(cache_kpe[:, j].reshape(DEC_BATCH * PAST_LEN, A_ROPE), ((0, 0), (0, LANES - A_ROPE)))
            kc, vc = _amid_cache(cache_ckv[:, j].reshape(DEC_BATCH * PAST_LEN, KV_LORA), kpe_cache, wk, wv, kvlg, kng,
                                 cos_a, sin_a)
            mix = _attn_a(q, k, v, y, None)
            mix = _attn_a(q, k, v, y, mix, kc=kc, vc=vc)
            hg = hgrn_out_norm[j][None, :]
            mix, sf_new, sb_new = _hgrn(y, lower[j], hg, mix, hg_consts, layer=j, sf_new=sf_new, sb_new=sb_new)
            (mix,) = _hgrn(y, lower[j], hg, mix, hg_consts, s0f=state_hgrn_fwd[:, j], s0b=state_hgrn_bwd[:, j])
            x_parts = _out_proj(mix, w_out_ab[j].astype(BF16), x_parts, mod, split_out=last)
        else:
            mod = mods_c[j].reshape(3 * N_COND, 1, D_MODEL)
            w_parts = [(w_in_c[j].astype(BF16), C_N // TN_IN,
                        lambda t: jnp.where(t < 2, t, jnp.where(t < 4, t + 1, 2)))]
            y = _in_proj(x_parts, norm_c[j][None, :], mod, w_parts)
            q, k, v, kc_new, vc_new = _cmid(y, q_norm_c[j][None, :], k_norm_c[j][None, :], cos_c, sin_c,
                                            j, kc_new, vc_new)
            kc = cache_k_c[:, j].reshape(DEC_BATCH * PAST_LEN, C_KV_WIDTH).astype(BF16)
            vc = cache_v_c[:, j].reshape(DEC_BATCH * PAST_LEN, C_KV_WIDTH).astype(BF16)
            sink = jnp.broadcast_to(sink_c[j][:, None, None], (C_HEADS, 1, LANES))
            mix = _attn_c(q, k, v, y, sink, None)
            mix = _attn_c(q, k, v, y, sink, mix, kc=kc, vc=vc)
            x_parts = _out_proj(mix, w_out_c[j].astype(BF16), x_parts, mod, split_out=last)

    cache_c_shape = (BATCH, DEPTH // 2, SEQ, C_KV_HEADS, C_HEAD_DIM)
    return (x_parts[0].reshape(BATCH, SEQ, D_MODEL), x_parts[1].reshape(DEC_BATCH, DEC_SEQ, D_MODEL),
            ckv_new, kpe_new, sf_new, sb_new, kc_new.reshape(cache_c_shape), vc_new.reshape(cache_c_shape))
```

```python
import functools

import numpy as np
import jax
import jax.numpy as jnp
from jax import lax
from jax.experimental import pallas as pl
from jax.experimental.pallas import tpu as pltpu

F32 = jnp.float32
BF16 = jnp.bfloat16

D_MODEL = 2048
BATCH = 16
SEQ = 256
DEPTH = 4
DEC_BATCH = 4
DEC_SEQ = 2048
PAST_LEN = 256
GRID_W = 64
A_HEADS = 8
A_NOPE = 128
A_ROPE = 64
A_VDIM = 128
A_QK = A_NOPE + A_ROPE
A_QK_PAD = 256
Q_LORA = 512
KV_LORA = 256
B_HEADS = 8
B_DK = 128
B_DV = 128
C_HEADS = 16
C_KV_HEADS = 4
C_GROUP = C_HEADS // C_KV_HEADS
C_HEAD_DIM = 128
C_WIDTH = C_HEADS * C_HEAD_DIM
C_KV_WIDTH = C_KV_HEADS * C_HEAD_DIM
WINDOW = 128
ROPE_BASE = 10000.0
EPS = 1e-6
NEG_BIG = -1e30
LOG2_E = 1.4426950408889634

LANES = 128
CTX_ROWS = BATCH * SEQ
LAT_ROWS = DEC_BATCH * DEC_SEQ
ROWS = CTX_ROWS + LAT_ROWS
N_COND = 8

AB_AGATE = 0
AB_BQ = 1024
AB_BFF = 2048
AB_BFB = 3072
AB_BI = 4096
AB_BGATE = 5120
AB_QLAT = 6144
AB_KVLAT = 6656
AB_KPE = 6912
AB_N = 7168
C_Q = 0
C_GATE = 2048
C_K = 4096
C_V = 4608
C_N = 5120

TM_PROJ = 1024
TN_IN = 1024
TM_OUT = 512
TM_MID = 512
TQ_A = 1024
TQ_C = 256
ATTN_SUB = 256
HG_CHUNK = 128
HG_LEVELS = 7
HG_UNROLL = 4
VMEM_LIMIT = 56 * 1024 * 1024
_MIX_SHAPE = jax.ShapeDtypeStruct((ROWS, D_MODEL), BF16)


def _cparams(sem):
    return pltpu.CompilerParams(dimension_semantics=sem, vmem_limit_bytes=VMEM_LIMIT)


def _sigmoid(x):
    return 0.5 * jnp.tanh(0.5 * x) + 0.5


def _silu(x):
    return x * _sigmoid(x)


def _dot(a, b):
    return jnp.dot(a, b, preferred_element_type=F32)


def _dot_nt(a, b):
    return lax.dot_general(a, b, (((1,), (1,)), ((), ())), preferred_element_type=F32)


def _cond_of_tile(i, tm):
    n_ctx = CTX_ROWS // tm
    per_batch = DEC_SEQ // tm
    return jnp.where(i < n_ctx, 0, 1 + (i - n_ctx) // per_batch)


def _rope_block_of_tile(i, tm):
    n_ctx = CTX_ROWS // tm
    per_batch = DEC_SEQ // tm
    return jnp.where(i < n_ctx, 0, 1 + (i - n_ctx) % per_batch)


def _mod_kernel(c_ref, w_ref, b_ref, o_ref):
    a = _silu(c_ref[...]).astype(BF16)
    o_ref[...] = _dot(a, w_ref[...].astype(BF16)) + b_ref[...]


def _modulation(cond8, w_mod, b_mod):
    n = w_mod.shape[0]
    tn = 1024
    return pl.pallas_call(
        _mod_kernel,
        grid=(n, 3 * D_MODEL // tn),
        in_specs=[pl.BlockSpec((N_COND, D_MODEL), lambda l, j: (0, 0)),
                  pl.BlockSpec((None, D_MODEL, tn), lambda l, j: (l, 0, j)),
                  pl.BlockSpec((None, 1, tn), lambda l, j: (l, 0, j))],
        out_specs=pl.BlockSpec((None, N_COND, tn), lambda l, j: (l, 0, j)),
        out_shape=jax.ShapeDtypeStruct((n, N_COND, 3 * D_MODEL), F32),
        compiler_params=_cparams(("parallel", "parallel")),
        name="adaln_mod",
    )(cond8, w_mod, b_mod.reshape(n, 1, 3 * D_MODEL))


def _prep_w_kernel(w_ref, main_ref, tail_ref):
    n_head = Q_LORA + KV_LORA + A_ROPE
    w = w_ref[...]
    main_ref[...] = w[:, n_head:].astype(BF16)
    col = lax.broadcasted_iota(jnp.int32, (w.shape[0], TN_IN), 1)
    tail_ref[...] = jnp.where(col < n_head, w[:, :TN_IN], 0.0).astype(BF16)


def _prep_w_in_ab(w):
    n_layers, _, n = w.shape
    tr = 256
    n_main = AB_N - TN_IN
    return pl.pallas_call(
        _prep_w_kernel,
        grid=(n_layers, D_MODEL // tr),
        in_specs=[pl.BlockSpec((None, tr, n), lambda l, i: (l, i, 0))],
        out_specs=[pl.BlockSpec((None, tr, n_main), lambda l, i: (l, i, 0)),
                   pl.BlockSpec((None, tr, TN_IN), lambda l, i: (l, i, 0))],
        out_shape=[jax.ShapeDtypeStruct((n_layers, D_MODEL, n_main), BF16),
                   jax.ShapeDtypeStruct((n_layers, D_MODEL, TN_IN), BF16)],
        compiler_params=_cparams(("parallel", "parallel")),
        name="w_in_ab_layout",
    )(w)


def _row_specs(parts, tm, single_buffer=False):
    if len(parts) == 1:
        return [pl.BlockSpec((tm, D_MODEL), lambda i, *_: (i, 0))]
    n_first = CTX_ROWS // tm
    mode = dict(pipeline_mode=pl.Buffered(1)) if single_buffer else {}
    return [pl.BlockSpec((tm, D_MODEL), lambda i, *_: (jnp.minimum(i, n_first - 1), 0), **mode),
            pl.BlockSpec((tm, D_MODEL), lambda i, *_: (jnp.maximum(i - n_first, 0), 0), **mode)]


def _in_kernel(*refs, n_x, w_tiles):
    x_refs = refs[:n_x]
    g_ref, sh_ref, sc_ref = refs[n_x:n_x + 3]
    w_refs = refs[n_x + 3:n_x + 3 + len(w_tiles)]
    o_ref, h_ref = refs[n_x + 3 + len(w_tiles):]
    i, j = pl.program_id(0), pl.program_id(1)

    def prologue(x_ref):
        x = x_ref[...]
        r = lax.rsqrt(jnp.mean(x * x, axis=-1, keepdims=True) + EPS)
        h = (x * r * g_ref[...]) * (1.0 + sc_ref[...]) + sh_ref[...]
        h_ref[...] = h.astype(BF16)

    if n_x == 1:
        pl.when(j == 0)(lambda: prologue(x_refs[0]))
    else:
        n_first = CTX_ROWS // TM_PROJ
        pl.when((j == 0) & (i < n_first))(lambda: prologue(x_refs[0]))
        pl.when((j == 0) & (i >= n_first))(lambda: prologue(x_refs[1]))

    def project(w_ref):
        o_ref[...] = _dot(h_ref[...], w_ref[...])

    lo = 0
    for w_ref, n in zip(w_refs, w_tiles):
        if len(w_tiles) == 1:
            project(w_ref)
        else:
            pl.when((j >= lo) & (j < lo + n))(functools.partial(project, w_ref))
        lo += n


def _in_proj(x_parts, g, mod, w_parts, layer):
    tm, tn = TM_PROJ, TN_IN
    w_tiles = [n for _, n, _ in w_parts]
    w_specs = []
    lo = 0
    for _, n, col_map in w_parts:
        col_map = col_map or (lambda t: t)
        w_specs.append(pl.BlockSpec((None, D_MODEL, tn),
                                    lambda i, j, lo=lo, n=n, f=col_map: (layer, 0, f(jnp.clip(j - lo, 0, n - 1)))))
        lo += n
    return pl.pallas_call(
        functools.partial(_in_kernel, n_x=len(x_parts), w_tiles=w_tiles),
        grid=(ROWS // tm, sum(w_tiles)),
        in_specs=_row_specs(x_parts, tm, single_buffer=True) + [
            pl.BlockSpec((1, D_MODEL), lambda i, j: (0, 0)),
            pl.BlockSpec((None, 1, D_MODEL), lambda i, j: (_cond_of_tile(i, tm) * 3, 0, 0)),
            pl.BlockSpec((None, 1, D_MODEL), lambda i, j: (_cond_of_tile(i, tm) * 3 + 1, 0, 0))] + w_specs,
        out_specs=pl.BlockSpec((tm, tn), lambda i, j: (i, j)),
        out_shape=jax.ShapeDtypeStruct((ROWS, sum(w_tiles) * tn), F32),
        scratch_shapes=[pltpu.VMEM((tm, D_MODEL), BF16)],
        compiler_params=_cparams(("arbitrary", "arbitrary")),
        name="norm_mod_in_proj",
    )(*x_parts, g, mod, mod, *[w for w, _, _ in w_parts])


def _out_kernel(*refs, n_x, n_o):
    m_ref, w_ref = refs[:2]
    x_refs = refs[2:2 + n_x]
    gt_ref = refs[2 + n_x]
    o_refs = refs[3 + n_x:]
    y = gt_ref[...] * _dot(m_ref[...], w_ref[...])

    def emit(x_ref, o_ref):
        o_ref[...] = x_ref[...] + y

    if n_x == 1 and n_o == 1:
        emit(x_refs[0], o_refs[0])
    else:
        in_first = pl.program_id(0) < CTX_ROWS // TM_OUT
        pl.when(in_first)(lambda: emit(x_refs[0], o_refs[0]))
        pl.when(jnp.logical_not(in_first))(lambda: emit(x_refs[-1], o_refs[-1]))


def _out_proj(mix, w, layer, x_parts, mod, split_out):
    tm = TM_OUT
    if split_out:
        out_parts = [jax.ShapeDtypeStruct((CTX_ROWS, D_MODEL), F32), jax.ShapeDtypeStruct((LAT_ROWS, D_MODEL), F32)]
    else:
        out_parts = [jax.ShapeDtypeStruct((ROWS, D_MODEL), F32)]
    return pl.pallas_call(
        functools.partial(_out_kernel, n_x=len(x_parts), n_o=len(out_parts)),
        grid=(ROWS // tm,),
        in_specs=[pl.BlockSpec((tm, D_MODEL), lambda i: (i, 0)),
                  pl.BlockSpec((None, D_MODEL, D_MODEL), lambda i: (layer, 0, 0))] + _row_specs(x_parts, tm) + [
                  pl.BlockSpec((None, 1, D_MODEL), lambda i: (_cond_of_tile(i, tm) * 3 + 2, 0, 0))],
        out_specs=_row_specs(out_parts, tm),
        out_shape=out_parts,
        compiler_params=_cparams(("arbitrary",)),
        name="out_proj_residual",
    )(mix, w, *x_parts, mod)


def _rope_a(x, cos_t, sin_t):
    return x * cos_t + (pltpu.roll(x, 32, 1) + pltpu.roll(x, 96, 1)) * sin_t


def _amid_kernel(*refs, do_q, norm_kv):
    if do_q:
        (ql_ref, kvl_ref, kpe_ref, wq_ref, wk_ref, wv_ref, qlg_ref, kvlg_ref, qng_ref, kng_ref, cos_ref, sin_ref,
         q_out, k_out, v_out, ckv_out, kpe_out) = refs
    else:
        (kvl_ref, kpe_ref, wk_ref, wv_ref, kvlg_ref, kng_ref, cos_ref, sin_ref, k_out, v_out) = refs
    cos_t = cos_ref[...]
    sin_t = sin_ref[...]
    inv_qk = 1.0 / A_QK

    if do_q:
        ql = ql_ref[...]
        qn = ql * lax.rsqrt(jnp.mean(ql * ql, axis=-1, keepdims=True) + EPS) * qlg_ref[...]
        qu = _dot(qn.astype(BF16), wq_ref[...])
        g_nope = qng_ref[:, :A_NOPE]
        g_rope = qng_ref[:, A_NOPE:]
        for h in range(A_HEADS):
            a = qu[:, h * A_QK_PAD:h * A_QK_PAD + A_NOPE]
            b = qu[:, h * A_QK_PAD + A_NOPE:(h + 1) * A_QK_PAD]
            ss = jnp.sum(a * a, axis=-1, keepdims=True) + jnp.sum(b * b, axis=-1, keepdims=True)
            r = lax.rsqrt(ss * inv_qk + EPS)
            q_out[:, h * A_QK_PAD:h * A_QK_PAD + A_NOPE] = (a * r * g_nope).astype(BF16)
            q_out[:, h * A_QK_PAD + A_NOPE:(h + 1) * A_QK_PAD] = _rope_a(b * r * g_rope, cos_t, sin_t).astype(BF16)

    kvl = kvl_ref[...]
    if norm_kv:
        ckv = kvl * lax.rsqrt(jnp.mean(kvl * kvl, axis=-1, keepdims=True) + EPS) * kvlg_ref[...]
    else:
        ckv = kvl
    ckv_b = ckv.astype(BF16)
    kn = _dot(ckv_b, wk_ref[...])
    v_out[...] = _dot(ckv_b, wv_ref[...]).astype(BF16)
    kpe = kpe_ref[...]
    if do_q:
        @pl.when(pl.program_id(0) < CTX_ROWS // TM_MID)
        def _():
            ckv_out[...] = ckv.reshape(ckv_out.shape)
            kpe_out[...] = kpe[:, :A_ROPE].reshape(kpe_out.shape)
    sp = jnp.sum(kpe * kpe, axis=-1, keepdims=True)
    g_nope = kng_ref[:, :A_NOPE]
    g_rope = kng_ref[:, A_NOPE:]
    for h in range(A_HEADS):
        a = kn[:, h * A_NOPE:(h + 1) * A_NOPE]
        r = lax.rsqrt((jnp.sum(a * a, axis=-1, keepdims=True) + sp) * inv_qk + EPS)
        k_out[:, h * A_QK_PAD:h * A_QK_PAD + A_NOPE] = (a * r * g_nope).astype(BF16)
        k_out[:, h * A_QK_PAD + A_NOPE:(h + 1) * A_QK_PAD] = _rope_a(kpe * r * g_rope, cos_t, sin_t).astype(BF16)


def _layer_slab_spec(tm, layer, width):
    n_ctx = CTX_ROWS // tm
    return pl.BlockSpec((tm // SEQ, None, SEQ, width), lambda i: (jnp.minimum(i, n_ctx - 1), layer, 0, 0))


def _call_with_carried(kernel, n_plain_out, carried, **kw):
    def run(*args, in_specs):
        args = list(args)
        in_specs = list(in_specs)
        n_real = len(args)
        aliases = {}
        for idx, arr in enumerate(carried):
            if arr is not None:
                aliases[len(args)] = n_plain_out + idx
                args.append(arr)
                in_specs.append(pl.BlockSpec(memory_space=pl.ANY))
        n_in = len(args)

        def body(*refs):
            kernel(*refs[:n_real], *refs[n_in:])

        return pl.pallas_call(body, in_specs=in_specs, input_output_aliases=aliases, **kw)(*args)
    return run


def _amid_tokens(y, wq, wk, wv, qlg, kvlg, qng, kng, cos_t, sin_t, layer, ckv_new, kpe_new):
    tm = TM_MID
    const = lambda i: (0, 0)
    rope_map = lambda i: (_rope_block_of_tile(i, tm), 0)
    n_ab = (DEPTH + 1) // 2
    run = _call_with_carried(
        functools.partial(_amid_kernel, do_q=True, norm_kv=True), 3, [ckv_new, kpe_new],
        grid=(ROWS // tm,),
        out_specs=[pl.BlockSpec((tm, A_HEADS * A_QK_PAD), lambda i: (i, 0)),
                   pl.BlockSpec((tm, A_HEADS * A_QK_PAD), lambda i: (i, 0)),
                   pl.BlockSpec((tm, A_HEADS * A_VDIM), lambda i: (i, 0)),
                   _layer_slab_spec(tm, layer, KV_LORA), _layer_slab_spec(tm, layer, A_ROPE)],
        out_shape=[jax.ShapeDtypeStruct((ROWS, A_HEADS * A_QK_PAD), BF16),
                   jax.ShapeDtypeStruct((ROWS, A_HEADS * A_QK_PAD), BF16),
                   jax.ShapeDtypeStruct((ROWS, A_HEADS * A_VDIM), BF16),
                   jax.ShapeDtypeStruct((BATCH, n_ab, SEQ, KV_LORA), F32),
                   jax.ShapeDtypeStruct((BATCH, n_ab, SEQ, A_ROPE), F32)],
        compiler_params=_cparams(("arbitrary",)),
        name="mla_qkv_prep")
    return run(y, y, y, wq, wk, wv, qlg, kvlg, qng, kng, cos_t, sin_t,
               in_specs=[pl.BlockSpec((tm, Q_LORA), lambda i: (i, AB_QLAT // Q_LORA)),
                         pl.BlockSpec((tm, KV_LORA), lambda i: (i, AB_KVLAT // KV_LORA)),
                         pl.BlockSpec((tm, LANES), lambda i: (i, AB_KPE // LANES)),
                         pl.BlockSpec(wq.shape, const), pl.BlockSpec(wk.shape, const), pl.BlockSpec(wv.shape, const),
                         pl.BlockSpec(qlg.shape, const), pl.BlockSpec(kvlg.shape, const),
                         pl.BlockSpec(qng.shape, const), pl.BlockSpec(kng.shape, const),
                         pl.BlockSpec((tm, LANES), rope_map), pl.BlockSpec((tm, LANES), rope_map)])


def _amid_cache(ckv, kpe, wk, wv, kvlg, kng, cos_t, sin_t):
    rows = ckv.shape[0]
    tm = TM_MID
    const = lambda i: (0, 0)
    outs = pl.pallas_call(
        functools.partial(_amid_kernel, do_q=False, norm_kv=False),
        grid=(rows // tm,),
        in_specs=[pl.BlockSpec((tm, KV_LORA), lambda i: (i, 0)),
                  pl.BlockSpec((tm, LANES), lambda i: (i, 0)),
                  pl.BlockSpec(wk.shape, const), pl.BlockSpec(wv.shape, const),
                  pl.BlockSpec(kvlg.shape, const), pl.BlockSpec(kng.shape, const),
                  pl.BlockSpec((tm, LANES), const), pl.BlockSpec((tm, LANES), const)],
        out_specs=[pl.BlockSpec((tm, A_HEADS * A_QK_PAD), lambda i: (i, 0)),
                   pl.BlockSpec((tm, A_HEADS * A_VDIM), lambda i: (i, 0))],
        out_shape=[jax.ShapeDtypeStruct((rows, A_HEADS * A_QK_PAD), BF16),
                   jax.ShapeDtypeStruct((rows, A_HEADS * A_VDIM), BF16)],
        compiler_params=_cparams(("parallel",)),
        name="mla_cache_kv_prep",
    )(ckv, kpe, wk, wv, kvlg, kng, cos_t, sin_t)
    return outs[0], outs[1]


def _scores(q, srcs, c):
    zs = []
    for k, _, keep in srcs:
        z = _dot_nt(q, k) * c
        zs.append(z if keep is None else jnp.where(keep, z, NEG_BIG))
    return zs


def _softmax_pv(zs, srcs, sink_z=None):
    tile_max = None
    for z in zs:
        for j in range(z.shape[1] // LANES):
            blk = z[:, j * LANES:(j + 1) * LANES]
            tile_max = blk if tile_max is None else jnp.maximum(tile_max, blk)
    m = tile_max.max(axis=-1, keepdims=True)
    if sink_z is not None:
        m = jnp.maximum(m, sink_z)
    acc = None
    for z, (_, v, _) in zip(zs, srcs):
        o = _dot(jnp.exp2(z - m).astype(BF16), jnp.concatenate([v, jnp.ones_like(v)], axis=1))
        acc = o if acc is None else acc + o
    dv = acc.shape[1] // 2
    den = acc[:, dv:]
    if sink_z is not None:
        den = den + jnp.exp2(sink_z - m)
    return acc[:, :dv] / den


def _attend_streams(n, q_of, srcs_of, c, sink_of=None):
    outs = []
    zs = _scores(q_of(0), srcs_of(0), c)
    for t in range(n):
        nxt = _scores(q_of(t + 1), srcs_of(t + 1), c) if t + 1 < n else None
        outs.append(_softmax_pv(zs, srcs_of(t), None if sink_of is None else sink_of(t)))
        zs = nxt
    return outs


def _attn_a_kernel(*refs, n_src, heads):
    q_ref = refs[0]
    k_refs = refs[1:1 + n_src]
    v_refs = refs[1 + n_src:1 + 2 * n_src]
    gate_ref, o_ref = refs[1 + 2 * n_src:]
    sub = min(ATTN_SUB, q_ref.shape[0])
    streams = [(h, r) for h in range(heads) for r in range(q_ref.shape[0] // sub)]

    def rows(t):
        return slice(streams[t][1] * sub, (streams[t][1] + 1) * sub)

    def q_of(t):
        h = streams[t][0]
        return q_ref[rows(t), h * A_QK_PAD:(h + 1) * A_QK_PAD]

    def srcs_of(t):
        h = streams[t][0]
        return [(k_ref[:, h * A_QK_PAD:(h + 1) * A_QK_PAD], v_ref[:, h * A_VDIM:(h + 1) * A_VDIM], None)
                for k_ref, v_ref in zip(k_refs, v_refs)]

    outs = _attend_streams(len(streams), q_of, srcs_of, A_QK ** -0.5 * LOG2_E)
    for t, o in enumerate(outs):
        cols = slice(streams[t][0] * A_VDIM, (streams[t][0] + 1) * A_VDIM)
        o_ref[rows(t), cols] = (o * _silu(gate_ref[rows(t), cols])).astype(o_ref.dtype)


def _attn_a(q, k, v, y, mix, kc=None, vc=None):
    latent = kc is not None
    if latent:
        nb, t, tq, hp, row0 = DEC_BATCH, DEC_SEQ, TQ_A, 1, CTX_ROWS
    else:
        nb, t, tq, hp, row0 = BATCH, SEQ, SEQ, A_HEADS, 0
    nq = t // tq
    qrow = lambda b, h, i: row0 // tq + b * nq + i
    in_specs = [pl.BlockSpec((tq, hp * A_QK_PAD), lambda b, h, i: (qrow(b, h, i), h))]
    args = [q]
    if latent:
        in_specs.append(pl.BlockSpec((PAST_LEN, hp * A_QK_PAD), lambda b, h, i: (b, h)))
        args.append(kc)
    in_specs.append(pl.BlockSpec((t, hp * A_QK_PAD), lambda b, h, i: (row0 // t + b, h)))
    args.append(k)
    if latent:
        in_specs.append(pl.BlockSpec((PAST_LEN, hp * A_VDIM), lambda b, h, i: (b, h)))
        args.append(vc)
    in_specs.append(pl.BlockSpec((t, hp * A_VDIM), lambda b, h, i: (row0 // t + b, h)))
    args.append(v)
    in_specs.append(pl.BlockSpec((tq, hp * A_VDIM), lambda b, h, i: (qrow(b, h, i), AB_AGATE // (hp * A_VDIM) + h)))
    args.append(y)
    run = _call_with_carried(
        functools.partial(_attn_a_kernel, n_src=2 if latent else 1, heads=hp), 0, [mix],
        grid=(nb, A_HEADS // hp, nq),
        out_specs=[pl.BlockSpec((tq, hp * A_VDIM), lambda b, h, i: (qrow(b, h, i), h))],
        out_shape=[_MIX_SHAPE],
        compiler_params=_cparams(("parallel", "parallel", "parallel")),
        name="mla_attention_latent" if latent else "mla_attention_context")
    return run(*args, in_specs=in_specs)[0]


def _hgrn_constants():
    c, nl = HG_CHUNK, HG_LEVELS
    t = np.arange(c)[:, None]
    u = np.arange(c)[None, :]
    tri_f = (u <= t).astype(np.float32)
    mask_f = np.zeros((nl, c, c), np.float32)
    coef_f = np.zeros((nl, c, LANES), np.float32)
    for l in range(nl):
        half = c >> (l + 1)
        seg = 2 * half
        mask_f[l] = ((u // seg) == (t // seg)) & ((t % seg) >= half) & ((u % seg) < half)
        later = np.broadcast_to((t % seg) >= half, (c, LANES))
        coef_f[l] = np.where(later, 1.0, -1.0 if half > 1 else 0.0)
    tri_b = tri_f[::-1, ::-1]
    mask_b = mask_f[:, ::-1, ::-1]
    coef_b = coef_f[:, ::-1, :]
    to_tri = lambda a: jnp.asarray(np.concatenate([a, a, a], axis=1), BF16)
    to_f32 = lambda a: jnp.asarray(np.ascontiguousarray(a), F32)
    return (to_tri(tri_f), to_tri(tri_b)), (to_f32(mask_f), to_f32(mask_b)), (to_f32(coef_f), to_f32(coef_b))


def _hgrn_decays(x, lb, tri3):
    kk = (1.0 - lb) * _sigmoid(-x)
    lf = jnp.log2(1.0 - kk)
    hi = lf.astype(BF16)
    r1 = lf - hi.astype(F32)
    mid = r1.astype(BF16)
    lo = (r1 - mid.astype(F32)).astype(BF16)
    return kk, lf, _dot(tri3, jnp.concatenate([hi, mid, lo], axis=0))


def _hgrn_level_arg(l, cs, lf, coef, forward):
    c = HG_CHUNK
    nv = c // 8
    half = c >> (l + 1)
    if half == 1:
        return lf * coef
    cs3 = cs.reshape(nv, 8, LANES)

    def in_vreg_row(r):
        return jnp.broadcast_to(cs3[:, r:r + 1, :], (nv, 8, LANES))

    if half >= 8:
        m = half // 8
        nseg = nv // (2 * m)
        edge = cs3[:, 7:8, :] if forward else cs3[:, 0:1, :]
        e4 = edge.reshape(nseg, 2 * m, 1, LANES)
        a = e4[:, m - 1:m] if forward else e4[:, m:m + 1]
        anchor = jnp.broadcast_to(a, (nseg, 2 * m, 8, LANES)).reshape(c, LANES)
    elif half == 4:
        anchor = in_vreg_row(3 if forward else 4).reshape(c, LANES)
    else:
        r0, r1 = (1, 5) if forward else (2, 6)
        sub = lax.broadcasted_iota(jnp.int32, (nv, 8, LANES), 1)
        anchor = jnp.where(sub < 4, in_vreg_row(r0), in_vreg_row(r1)).reshape(c, LANES)
    return (cs - anchor) * coef


def _hgrn_kernel(*refs, t_len, zero_init, emit_state):
    c = HG_CHUNK
    n_chunks = t_len // c
    it = iter(refs)
    bq_ref, ff_ref, fb_ref, vi_ref, bg_ref, lb_ref, hg_ref = (next(it) for _ in range(7))
    if not zero_init:
        s0f_ref, s0b_ref = next(it), next(it)
    trif_ref, trib_ref, maskf_ref, maskb_ref, coeff_ref, coefb_ref = (next(it) for _ in range(6))
    o_ref = next(it)
    if emit_state:
        sf_ref, sb_ref = next(it), next(it)
    of_ref, ob_ref, stf_ref, stb_ref = (next(it) for _ in range(4))

    if zero_init:
        stf_ref[...] = jnp.zeros((B_DV, B_DK), F32)
        stb_ref[...] = jnp.zeros((B_DV, B_DK), F32)
    else:
        stf_ref[...] = s0f_ref[...].T
        stb_ref[...] = s0b_ref[...].T
    lb = lb_ref[...]

    nl = HG_LEVELS
    unroll = min(HG_UNROLL, n_chunks)
    dirs = ((ff_ref, lb[0:1, :], trif_ref, maskf_ref, coeff_ref, stf_ref, of_ref, True),
            (fb_ref, lb[1:2, :], trib_ref, maskb_ref, coefb_ref, stb_ref, ob_ref, False))

    def body(i, carry):
        chains = []
        for f_ref, lb_row, tri_ref, mask_ref, coef_ref, st_ref, out_ref, forward in dirs:
            for u in range(unroll):
                k = i * unroll + u
                r0 = pl.multiple_of((k if forward else n_chunks - 1 - k) * c, c)
                chains.append(dict(rows=pl.ds(r0, c), f_ref=f_ref, lb=lb_row, tri_ref=tri_ref, mask_ref=mask_ref,
                                   coef_ref=coef_ref, st_ref=st_ref, out_ref=out_ref, forward=forward))
        for ch in chains:
            ch["kk"], ch["lf"], ch["cs"] = _hgrn_decays(ch["f_ref"][ch["rows"], :], ch["lb"], ch["tri_ref"][...])
            ch["q"] = _silu(bq_ref[ch["rows"], :])
            ch["v"] = vi_ref[ch["rows"], :]
            ch["q16"] = ch["q"].astype(BF16)
            ch["kk16"] = ch["kk"].astype(BF16)
            ch["sc"] = jnp.zeros((c, c), F32)
        for l in range(nl):
            for ch in chains:
                el = jnp.exp2(_hgrn_level_arg(l, ch["cs"], ch["lf"], ch["coef_ref"][l], ch["forward"])).astype(BF16)
                ch["sc"] = ch["sc"] + ch["mask_ref"][l] * _dot_nt(ch["q16"] * el, ch["kk16"] * el)
        for ch in chains:
            cs, q, kk, v = ch["cs"], ch["q"], ch["kk"], ch["v"]
            end = cs[c - 1:c, :] if ch["forward"] else cs[0:1, :]
            ch["end"] = end
            ch["q_in"] = (q * jnp.exp2(cs)).astype(BF16)
            ch["upd"] = _dot(v.T.astype(BF16), (kk * jnp.exp2(end - cs)).astype(BF16))
            ch["o"] = _dot(ch["sc"].astype(BF16), v.astype(BF16)) + jnp.sum(q * kk, axis=-1, keepdims=True) * v
        for ch in chains:
            st = ch["st_ref"][...]
            ch["out_ref"][ch["rows"], :] = ch["o"] + _dot_nt(ch["q_in"], st.astype(BF16))
            ch["st_ref"][...] = jnp.exp2(ch["end"]) * st + ch["upd"]
        return carry

    lax.fori_loop(0, n_chunks // unroll, body, 0)
    if emit_state:
        sf_ref[...] = stf_ref[...].T
        sb_ref[...] = stb_ref[...].T

    hg = hg_ref[...]

    blk = unroll * c

    def finish(i, carry):
        rows = pl.ds(pl.multiple_of(i * blk, blk), blk)
        o = of_ref[rows, :] + ob_ref[rows, :]
        o = o * lax.rsqrt(jnp.mean(o * o, axis=-1, keepdims=True) + EPS) * hg
        o_ref[rows, :] = (o * _silu(bg_ref[rows, :])).astype(o_ref.dtype)
        return carry

    lax.fori_loop(0, t_len // blk, finish, 0)


def _hgrn(y, lb, hg, mix, consts, s0f=None, s0b=None, layer=None, sf_new=None, sb_new=None):
    latent = s0f is not None
    if latent:
        nb, t, row0 = DEC_BATCH, DEC_SEQ, CTX_ROWS
    else:
        nb, t, row0 = BATCH, SEQ, 0
    rb = lambda b: row0 // t + b
    col = lambda off: (lambda b, h: (rb(b), off // LANES + h))
    const2 = lambda b, h: (0, 0)
    const3 = lambda b, h: (0, 0, 0)
    in_specs = [pl.BlockSpec((t, LANES), col(AB_BQ)), pl.BlockSpec((t, LANES), col(AB_BFF)),
                pl.BlockSpec((t, LANES), col(AB_BFB)), pl.BlockSpec((t, LANES), col(AB_BI)),
                pl.BlockSpec((t, LANES), col(AB_BGATE)),
                pl.BlockSpec((2, LANES), lambda b, h: (0, h)),
                pl.BlockSpec((1, LANES), const2)]
    args = [y, y, y, y, y, lb, hg]
    if latent:
        st_spec = pl.BlockSpec((None, None, B_DK, B_DV), lambda b, h: (b, h, 0, 0))
        in_specs += [st_spec, st_spec]
        args += [s0f, s0b]
    tris, masks, coefs = consts
    in_specs += ([pl.BlockSpec(a.shape, const2) for a in tris] + [pl.BlockSpec(a.shape, const3) for a in masks]
                 + [pl.BlockSpec(a.shape, const3) for a in coefs])
    args += [*tris, *masks, *coefs]
    out_specs = [pl.BlockSpec((t, LANES), lambda b, h: (rb(b), B_HEADS + h))]
    out_shape = [_MIX_SHAPE]
    carried = [mix]
    if not latent:
        st_out = pl.BlockSpec((None, None, None, B_DK, B_DV), lambda b, h: (b, layer, h, 0, 0))
        out_specs += [st_out, st_out]
        out_shape += [jax.ShapeDtypeStruct((nb, (DEPTH + 1) // 2, B_HEADS, B_DK, B_DV), F32)] * 2
        carried += [sf_new, sb_new]
    run = _call_with_carried(
        functools.partial(_hgrn_kernel, t_len=t, zero_init=not latent, emit_state=not latent), 0, carried,
        grid=(nb, B_HEADS),
        out_specs=out_specs,
        out_shape=out_shape,
        scratch_shapes=[pltpu.VMEM((t, B_DV), F32), pltpu.VMEM((t, B_DV), F32),
                        pltpu.VMEM((B_DV, B_DK), F32), pltpu.VMEM((B_DV, B_DK), F32)],
        compiler_params=_cparams(("parallel", "parallel")),
        name="hgrn2_latent" if latent else "hgrn2_context")
    return run(*args, in_specs=in_specs)


def _cmid_kernel(q_ref, k_ref, v_ref, qg_ref, kg_ref, cos_ref, sin_ref, q_out, k_out, v_out, kc_out, vc_out):
    cos_t = cos_ref[...]
    sin_t = sin_ref[...]
    qg = qg_ref[...]
    kg = kg_ref[...]
    in_ctx = pl.program_id(0) < CTX_ROWS // TM_MID

    def norm(x, g):
        return x * lax.rsqrt(jnp.mean(x * x, axis=-1, keepdims=True) + EPS) * g

    def rope(x):
        return x * cos_t + pltpu.roll(x, C_HEAD_DIM // 2, 1) * sin_t

    for h in range(C_HEADS):
        sl = slice(h * C_HEAD_DIM, (h + 1) * C_HEAD_DIM)
        q_out[:, sl] = rope(norm(q_ref[:, sl], qg)).astype(BF16)
    for h in range(C_KV_HEADS):
        sl = slice(h * C_HEAD_DIM, (h + 1) * C_HEAD_DIM)
        kn = norm(k_ref[:, sl], kg)
        k_out[:, sl] = rope(kn).astype(BF16)

        @pl.when(in_ctx)
        def _():
            kc_out[:, :, sl] = kn.reshape(kc_out.shape[0], SEQ, C_HEAD_DIM)
    v = v_ref[...]
    v_out[...] = v.astype(BF16)

    @pl.when(in_ctx)
    def _():
        vc_out[...] = v.reshape(vc_out.shape)


def _cmid(y, qg, kg, cos_t, sin_t, layer, kc_new, vc_new):
    tm = TM_MID
    const = lambda i: (0, 0)
    rope_map = lambda i: (_rope_block_of_tile(i, tm), 0)
    cache = jax.ShapeDtypeStruct((BATCH, DEPTH // 2, SEQ, C_KV_WIDTH), F32)
    run = _call_with_carried(
        _cmid_kernel, 3, [kc_new, vc_new],
        grid=(ROWS // tm,),
        out_specs=[pl.BlockSpec((tm, C_WIDTH), lambda i: (i, 0)),
                   pl.BlockSpec((tm, C_KV_WIDTH), lambda i: (i, 0)),
                   pl.BlockSpec((tm, C_KV_WIDTH), lambda i: (i, 0)),
                   _layer_slab_spec(tm, layer, C_KV_WIDTH), _layer_slab_spec(tm, layer, C_KV_WIDTH)],
        out_shape=[jax.ShapeDtypeStruct((ROWS, C_WIDTH), BF16),
                   jax.ShapeDtypeStruct((ROWS, C_KV_WIDTH), BF16),
                   jax.ShapeDtypeStruct((ROWS, C_KV_WIDTH), BF16), cache, cache],
        compiler_params=_cparams(("arbitrary",)),
        name="gqa_qkv_prep")
    return run(y, y, y, qg, kg, cos_t, sin_t,
               in_specs=[pl.BlockSpec((tm, C_WIDTH), lambda i: (i, C_Q // C_WIDTH)),
                         pl.BlockSpec((tm, C_KV_WIDTH), lambda i: (i, C_K // C_KV_WIDTH)),
                         pl.BlockSpec((tm, C_KV_WIDTH), lambda i: (i, C_V // C_KV_WIDTH)),
                         pl.BlockSpec((1, C_HEAD_DIM), const), pl.BlockSpec((1, C_HEAD_DIM), const),
                         pl.BlockSpec((tm, LANES), rope_map), pl.BlockSpec((tm, LANES), rope_map)])


def _attn_c_kernel(*refs, band, tq, t_len, groups, stack):
    if band:
        q_ref, kc_ref, vc_ref, kl_ref, vl_ref, sink_ref, gate_ref, o_ref = refs
    else:
        q_ref, kc_ref, vc_ref, sink_ref, gate_ref, o_ref = refs
    hd = C_HEAD_DIM
    rows = stack * tq
    streams = [(g, s) for g in range(groups) for s in range(C_GROUP // stack)]
    if band:
        i = pl.program_id(2)
        width = tq + 2 * WINDOW
        start = pl.multiple_of(jnp.clip(i * tq - WINDOW, 0, t_len - width), WINDOW)
        qpos = i * tq + (lax.broadcasted_iota(jnp.int32, (rows, width), 0) & (tq - 1))
        kpos = start + lax.broadcasted_iota(jnp.int32, (rows, width), 1)
        in_band = jnp.abs(kpos - qpos) <= WINDOW

    def heads_of(t):
        g, s = streams[t]
        return [g * C_GROUP + s * stack + r for r in range(stack)]

    def q_of(t):
        return jnp.concatenate([q_ref[:, h * hd:(h + 1) * hd] for h in heads_of(t)], axis=0)

    def srcs_of(t):
        g = streams[t][0]
        cols = slice(g * hd, (g + 1) * hd)
        srcs = [(kc_ref[:, cols], vc_ref[:, cols], None)]
        if band:
            srcs.append((kl_ref[pl.ds(start, width), cols], vl_ref[pl.ds(start, width), cols], in_band))
        return srcs

    def sink_of(t):
        return jnp.concatenate([jnp.broadcast_to(sink_ref[h][:, :1] * LOG2_E, (tq, 1)) for h in heads_of(t)], axis=0)

    outs = _attend_streams(len(streams), q_of, srcs_of, hd ** -0.5 * LOG2_E, sink_of)
    for t, o in enumerate(outs):
        for r, h in enumerate(heads_of(t)):
            cols = slice(h * hd, (h + 1) * hd)
            o_ref[:, cols] = (o[r * tq:(r + 1) * tq] * _silu(gate_ref[:, cols])).astype(o_ref.dtype)


def _attn_c(q, k, v, y, sink, mix, kc=None, vc=None):
    latent = kc is not None
    if latent:
        nb, t, tq, gp, stack, row0 = DEC_BATCH, DEC_SEQ, TQ_C, 1, 2, CTX_ROWS
    else:
        nb, t, tq, gp, stack, row0 = BATCH, SEQ, SEQ, C_KV_HEADS, C_GROUP, 0
    gw = gp * C_GROUP * C_HEAD_DIM
    kvw = gp * C_HEAD_DIM
    nq = t // tq
    qrow = lambda b, g, i: row0 // tq + b * nq + i
    own_kv = pl.BlockSpec((t, kvw), lambda b, g, i: (row0 // t + b, g))
    in_specs = [pl.BlockSpec((tq, gw), lambda b, g, i: (qrow(b, g, i), g))]
    args = [q]
    if latent:
        ctx_kv = pl.BlockSpec((PAST_LEN, kvw), lambda b, g, i: (b, g))
        in_specs += [ctx_kv, ctx_kv, own_kv, own_kv]
        args += [kc, vc, k, v]
    else:
        in_specs += [own_kv, own_kv]
        args += [k, v]
    in_specs.append(pl.BlockSpec((gp * C_GROUP, 1, LANES), lambda b, g, i: (g, 0, 0)))
    args.append(sink)
    in_specs.append(pl.BlockSpec((tq, gw), lambda b, g, i: (qrow(b, g, i), C_GATE // gw + g)))
    args.append(y)
    run = _call_with_carried(
        functools.partial(_attn_c_kernel, band=latent, tq=tq, t_len=t, groups=gp, stack=stack), 0, [mix],
        grid=(nb, C_KV_HEADS // gp, nq),
        out_specs=[pl.BlockSpec((tq, gw), lambda b, g, i: (qrow(b, g, i), g))],
        out_shape=[_MIX_SHAPE],
        compiler_params=_cparams(("parallel", "parallel", "parallel")),
        name="gqa_attention_latent" if latent else "gqa_attention_context")
    return run(*args, in_specs=in_specs)[0]


def _axial_angles(n_tokens, rot_dim):
    rows = n_tokens // GRID_W
    row = jnp.repeat(jnp.arange(rows, dtype=F32), GRID_W)
    col = jnp.tile(jnp.arange(GRID_W, dtype=F32), rows)
    n_freq = rot_dim // 4
    inv = ROPE_BASE ** (-jnp.arange(n_freq, dtype=F32) / n_freq)
    return jnp.concatenate([row[:, None] * inv, col[:, None] * inv], axis=-1)


def _rope_tables(rot_dim, tm):
    ang = _axial_angles(DEC_SEQ, rot_dim)
    cos, sin = jnp.cos(ang), jnp.sin(ang)
    pad = LANES - rot_dim
    cos_t = jnp.concatenate([cos, cos, jnp.ones((DEC_SEQ, pad), F32)], axis=-1)
    sin_t = jnp.concatenate([-sin, sin, jnp.zeros((DEC_SEQ, pad), F32)], axis=-1)
    cos_t = jnp.concatenate([jnp.ones((tm, LANES), F32), cos_t], axis=0)
    sin_t = jnp.concatenate([jnp.zeros((tm, LANES), F32), sin_t], axis=0)
    return cos_t, sin_t


def _lower_bounds(lb_logits):
    p = jax.nn.softmax(lb_logits.astype(F32), axis=0)
    return jnp.cumsum(p, axis=0) - p[0:1]


def _pad_head_gain(g):
    return jnp.concatenate([g, jnp.zeros((A_QK_PAD - A_QK,), F32)])[None, :]


def kernel(x_prompt, x_sample, cache_ckv, cache_kpe, state_hgrn_fwd, state_hgrn_bwd, cache_k_c, cache_v_c, c, c_ctx,
           mod_w_ab, mod_b_ab, norm_ab, w_in_ab, q_lora_norm, kv_lora_norm, w_q_up, w_kv_up, q_norm_ab, k_norm_ab,
           hgrn_lb_logits, hgrn_out_norm, w_out_ab, mod_w_c, mod_b_c, norm_c, w_in_c, q_norm_c, k_norm_c, sink_c,
           w_out_c):
    x_parts = [x_prompt.reshape(CTX_ROWS, D_MODEL), x_sample.reshape(LAT_ROWS, D_MODEL)]
    cond8 = jnp.concatenate([c_ctx[None, :], c, jnp.zeros((N_COND - 1 - DEC_BATCH, D_MODEL), F32)], axis=0)
    mods_ab = _modulation(cond8, mod_w_ab, mod_b_ab)
    mods_c = _modulation(cond8, mod_w_c, mod_b_c)
    lower = _lower_bounds(hgrn_lb_logits)
    cos_a, sin_a = _rope_tables(A_ROPE, TM_MID)
    cos_c, sin_c = _rope_tables(C_HEAD_DIM, TM_MID)
    hg_consts = _hgrn_constants()
    w_ab_main, w_ab_tail = _prep_w_in_ab(w_in_ab)
    w_in_c16, w_out_ab16, w_out_c16 = w_in_c.astype(BF16), w_out_ab.astype(BF16), w_out_c.astype(BF16)

    ckv_new = kpe_new = sf_new = sb_new = kc_new = vc_new = None
    for layer in range(DEPTH):
        j = layer // 2
        last = layer == DEPTH - 1
        if layer % 2 == 0:
            mod = mods_ab[j].reshape(3 * N_COND, 1, D_MODEL)
            w_parts = [(w_ab_main, (AB_N - TN_IN) // TN_IN, None), (w_ab_tail, 1, None)]
            wq = jnp.pad(w_q_up[j].reshape(Q_LORA, A_HEADS, A_QK),
                         ((0, 0), (0, 0), (0, A_QK_PAD - A_QK))).reshape(Q_LORA, A_HEADS * A_QK_PAD).astype(BF16)
            wkv = w_kv_up[j].reshape(KV_LORA, A_HEADS, A_NOPE + A_VDIM)
            wk = wkv[:, :, :A_NOPE].reshape(KV_LORA, A_HEADS * A_NOPE).astype(BF16)
            wv = wkv[:, :, A_NOPE:].reshape(KV_LORA, A_HEADS * A_VDIM).astype(BF16)
            qlg, kvlg = q_lora_norm[j][None, :], kv_lora_norm[j][None, :]
            qng, kng = _pad_head_gain(q_norm_ab[j]), _pad_head_gain(k_norm_ab[j])

            y = _in_proj(x_parts, norm_ab[j][None, :], mod, w_parts, j)
            q, k, v, ckv_new, kpe_new = _amid_tokens(y, wq, wk, wv, qlg, kvlg, qng, kng, cos_a, sin_a,
                                                     j, ckv_new, kpe_new)
            kpe_cache = jnp.pad(cache_kpe[:, j].reshape(DEC_BATCH * PAST_LEN, A_ROPE), ((0, 0), (0, LANES - A_ROPE)))
            kc, vc = _amid_cache(cache_ckv[:, j].reshape(DEC_BATCH * PAST_LEN, KV_LORA), kpe_cache, wk, wv, kvlg, kng,
                                 cos_a, sin_a)
            mix = _attn_a(q, k, v, y, None)
            mix = _attn_a(q, k, v, y, mix, kc=kc, vc=vc)
            hg = hgrn_out_norm[j][None, :]
            mix, sf_new, sb_new = _hgrn(y, lower[j], hg, mix, hg_consts, layer=j, sf_new=sf_new, sb_new=sb_new)
            (mix,) = _hgrn(y, lower[j], hg, mix, hg_consts, s0f=state_hgrn_fwd[:, j], s0b=state_hgrn_bwd[:, j])
            x_parts = _out_proj(mix, w_out_ab16, j, x_parts, mod, split_out=last)
        else:
            mod = mods_c[j].reshape(3 * N_COND, 1, D_MODEL)
            w_parts = [(w_in_c16, C_N // TN_IN, lambda t: jnp.where(t < 2, t, jnp.where(t < 4, t + 1, 2)))]
            y = _in_proj(x_parts, norm_c[j][None, :], mod, w_parts, j)
            q, k, v, kc_new, vc_new = _cmid(y, q_norm_c[j][None, :], k_norm_c[j][None, :], cos_c, sin_c,
                                            j, kc_new, vc_new)
            kc = cache_k_c[:, j].reshape(DEC_BATCH * PAST_LEN, C_KV_WIDTH).astype(BF16)
            vc = cache_v_c[:, j].reshape(DEC_BATCH * PAST_LEN, C_KV_WIDTH).astype(BF16)
            sink = jnp.broadcast_to(sink_c[j][:, None, None], (C_HEADS, 1, LANES))
            mix = _attn_c(q, k, v, y, sink, None)
            mix = _attn_c(q, k, v, y, sink, mix, kc=kc, vc=vc)
            x_parts = _out_proj(mix, w_out_c16, j, x_parts, mod, split_out=last)

    cache_c_shape = (BATCH, DEPTH // 2, SEQ, C_KV_HEADS, C_HEAD_DIM)
    return (x_parts[0].reshape(BATCH, SEQ, D_MODEL), x_parts[1].reshape(DEC_BATCH, DEC_SEQ, D_MODEL),
            ckv_new, kpe_new, sf_new, sb_new, kc_new.reshape(cache_c_shape), vc_new.reshape(cache_c_shape))
```

```python
import functools

import numpy as np
import jax
import jax.numpy as jnp
from jax import lax
from jax.experimental import pallas as pl
from jax.experimental.pallas import tpu as pltpu

F32 = jnp.float32
BF16 = jnp.bfloat16

D_MODEL = 2048
BATCH = 16
SEQ = 256
DEPTH = 4
DEC_BATCH = 4
DEC_SEQ = 2048
PAST_LEN = 256
GRID_W = 64
A_HEADS = 8
A_NOPE = 128
A_ROPE = 64
A_VDIM = 128
A_QK = A_NOPE + A_ROPE
A_QK_PAD = 256
Q_LORA = 512
KV_LORA = 256
B_HEADS = 8
B_DK = 128
B_DV = 128
C_HEADS = 16
C_KV_HEADS = 4
C_GROUP = C_HEADS // C_KV_HEADS
C_HEAD_DIM = 128
C_WIDTH = C_HEADS * C_HEAD_DIM
C_KV_WIDTH = C_KV_HEADS * C_HEAD_DIM
WINDOW = 128
ROPE_BASE = 10000.0
EPS = 1e-6
NEG_BIG = -1e30
LOG2_E = 1.4426950408889634

LANES = 128
CTX_ROWS = BATCH * SEQ
LAT_ROWS = DEC_BATCH * DEC_SEQ
ROWS = CTX_ROWS + LAT_ROWS
N_COND = 8

AB_AGATE = 0
AB_BQ = 1024
AB_BFF = 2048
AB_BFB = 3072
AB_BI = 4096
AB_BGATE = 5120
AB_QLAT = 6144
AB_KVLAT = 6656
AB_KPE = 6912
AB_N = 7168
C_Q = 0
C_GATE = 2048
C_K = 4096
C_V = 4608
C_N = 5120

TM_PROJ = 1024
TN_IN = 1024
TM_OUT = 512
TM_MID = 512
TQ_A = 1024
TQ_C = 256
ATTN_SUB = 256
HG_CHUNK = 128
HG_LEVELS = 7
HG_UNROLL = 4
VMEM_LIMIT = 56 * 1024 * 1024
_MIX_SHAPE = jax.ShapeDtypeStruct((ROWS, D_MODEL), BF16)


def _cparams(sem):
    return pltpu.CompilerParams(dimension_semantics=sem, vmem_limit_bytes=VMEM_LIMIT)


def _sigmoid(x):
    return 0.5 * jnp.tanh(0.5 * x) + 0.5


def _silu(x):
    return x * _sigmoid(x)


def _dot(a, b):
    return jnp.dot(a, b, preferred_element_type=F32)


def _dot_nt(a, b):
    return lax.dot_general(a, b, (((1,), (1,)), ((), ())), preferred_element_type=F32)


def _cond_of_tile(i, tm):
    n_ctx = CTX_ROWS // tm
    per_batch = DEC_SEQ // tm
    return jnp.where(i < n_ctx, 0, 1 + (i - n_ctx) // per_batch)


def _rope_block_of_tile(i, tm):
    n_ctx = CTX_ROWS // tm
    per_batch = DEC_SEQ // tm
    return jnp.where(i < n_ctx, 0, 1 + (i - n_ctx) % per_batch)


def _mod_kernel(c_ref, w_ref, b_ref, o_ref):
    a = _silu(c_ref[...]).astype(BF16)
    o_ref[...] = _dot(a, w_ref[...].astype(BF16)) + b_ref[...]


def _modulation(cond8, w_mod, b_mod):
    n = w_mod.shape[0]
    tn = 1024
    return pl.pallas_call(
        _mod_kernel,
        grid=(n, 3 * D_MODEL // tn),
        in_specs=[pl.BlockSpec((N_COND, D_MODEL), lambda l, j: (0, 0)),
                  pl.BlockSpec((None, D_MODEL, tn), lambda l, j: (l, 0, j)),
                  pl.BlockSpec((None, 1, tn), lambda l, j: (l, 0, j))],
        out_specs=pl.BlockSpec((None, N_COND, tn), lambda l, j: (l, 0, j)),
        out_shape=jax.ShapeDtypeStruct((n, N_COND, 3 * D_MODEL), F32),
        compiler_params=_cparams(("parallel", "parallel")),
        name="adaln_mod",
    )(cond8, w_mod, b_mod.reshape(n, 1, 3 * D_MODEL))


def _prep_w_kernel(w_ref, main_ref, tail_ref):
    n_head = Q_LORA + KV_LORA + A_ROPE
    main_ref[...] = w_ref[n_head:, :].astype(BF16)
    tail_ref[:n_head, :] = w_ref[:n_head, :].astype(BF16)
    tail_ref[n_head:, :] = jnp.zeros((TN_IN - n_head, tail_ref.shape[1]), BF16)


def _prep_w_in_ab(w_t):
    n_layers, n, _ = w_t.shape
    tc = 256
    n_main = AB_N - TN_IN
    return pl.pallas_call(
        _prep_w_kernel,
        grid=(n_layers, D_MODEL // tc),
        in_specs=[pl.BlockSpec((None, n, tc), lambda l, i: (l, 0, i))],
        out_specs=[pl.BlockSpec((None, n_main, tc), lambda l, i: (l, 0, i)),
                   pl.BlockSpec((None, TN_IN, tc), lambda l, i: (l, 0, i))],
        out_shape=[jax.ShapeDtypeStruct((n_layers, n_main, D_MODEL), BF16),
                   jax.ShapeDtypeStruct((n_layers, TN_IN, D_MODEL), BF16)],
        compiler_params=_cparams(("parallel", "parallel")),
        name="w_in_ab_layout",
    )(w_t)


def _row_specs(parts, tm, single_buffer=False):
    if len(parts) == 1:
        return [pl.BlockSpec((tm, D_MODEL), lambda i, *_: (i, 0))]
    n_first = CTX_ROWS // tm
    mode = dict(pipeline_mode=pl.Buffered(1)) if single_buffer else {}
    return [pl.BlockSpec((tm, D_MODEL), lambda i, *_: (jnp.minimum(i, n_first - 1), 0), **mode),
            pl.BlockSpec((tm, D_MODEL), lambda i, *_: (jnp.maximum(i - n_first, 0), 0), **mode)]


def _in_kernel(*refs, n_x, w_tiles, w_transposed):
    x_refs = refs[:n_x]
    g_ref, sh_ref, sc_ref = refs[n_x:n_x + 3]
    w_refs = refs[n_x + 3:n_x + 3 + len(w_tiles)]
    o_ref, h_ref = refs[n_x + 3 + len(w_tiles):]
    i, j = pl.program_id(0), pl.program_id(1)

    def prologue(x_ref):
        x = x_ref[...]
        r = lax.rsqrt(jnp.mean(x * x, axis=-1, keepdims=True) + EPS)
        h = (x * r * g_ref[...]) * (1.0 + sc_ref[...]) + sh_ref[...]
        h_ref[...] = h.astype(BF16)

    if n_x == 1:
        pl.when(j == 0)(lambda: prologue(x_refs[0]))
    else:
        n_first = CTX_ROWS // TM_PROJ
        pl.when((j == 0) & (i < n_first))(lambda: prologue(x_refs[0]))
        pl.when((j == 0) & (i >= n_first))(lambda: prologue(x_refs[1]))

    def project(w_ref):
        o_ref[...] = (_dot_nt if w_transposed else _dot)(h_ref[...], w_ref[...])

    lo = 0
    for w_ref, n in zip(w_refs, w_tiles):
        if len(w_tiles) == 1:
            project(w_ref)
        else:
            pl.when((j >= lo) & (j < lo + n))(functools.partial(project, w_ref))
        lo += n


def _in_proj(x_parts, g, mod, w_parts, layer, w_transposed=False):
    tm, tn = TM_PROJ, TN_IN
    w_tiles = [n for _, n, _ in w_parts]
    w_specs = []
    lo = 0
    for _, n, col_map in w_parts:
        col_map = col_map or (lambda t: t)
        tile = lambda i, j, lo=lo, n=n, f=col_map: f(jnp.clip(j - lo, 0, n - 1))
        if w_transposed:
            w_specs.append(pl.BlockSpec((None, tn, D_MODEL), lambda i, j, tile=tile: (layer, tile(i, j), 0)))
        else:
            w_specs.append(pl.BlockSpec((None, D_MODEL, tn), lambda i, j, tile=tile: (layer, 0, tile(i, j))))
        lo += n
    return pl.pallas_call(
        functools.partial(_in_kernel, n_x=len(x_parts), w_tiles=w_tiles, w_transposed=w_transposed),
        grid=(ROWS // tm, sum(w_tiles)),
        in_specs=_row_specs(x_parts, tm, single_buffer=True) + [
            pl.BlockSpec((1, D_MODEL), lambda i, j: (0, 0)),
            pl.BlockSpec((None, 1, D_MODEL), lambda i, j: (_cond_of_tile(i, tm) * 3, 0, 0)),
            pl.BlockSpec((None, 1, D_MODEL), lambda i, j: (_cond_of_tile(i, tm) * 3 + 1, 0, 0))] + w_specs,
        out_specs=pl.BlockSpec((tm, tn), lambda i, j: (i, j)),
        out_shape=jax.ShapeDtypeStruct((ROWS, sum(w_tiles) * tn), F32),
        scratch_shapes=[pltpu.VMEM((tm, D_MODEL), BF16)],
        compiler_params=_cparams(("arbitrary", "arbitrary")),
        name="norm_mod_in_proj",
    )(*x_parts, g, mod, mod, *[w for w, _, _ in w_parts])


def _out_kernel(*refs, n_x, n_o):
    m_ref, w_ref = refs[:2]
    x_refs = refs[2:2 + n_x]
    gt_ref = refs[2 + n_x]
    o_refs = refs[3 + n_x:]
    y = gt_ref[...] * _dot(m_ref[...], w_ref[...])

    def emit(x_ref, o_ref):
        o_ref[...] = x_ref[...] + y

    if n_x == 1 and n_o == 1:
        emit(x_refs[0], o_refs[0])
    else:
        in_first = pl.program_id(0) < CTX_ROWS // TM_OUT
        pl.when(in_first)(lambda: emit(x_refs[0], o_refs[0]))
        pl.when(jnp.logical_not(in_first))(lambda: emit(x_refs[-1], o_refs[-1]))


def _out_proj(mix, w, layer, x_parts, mod, split_out):
    tm = TM_OUT
    if split_out:
        out_parts = [jax.ShapeDtypeStruct((CTX_ROWS, D_MODEL), F32), jax.ShapeDtypeStruct((LAT_ROWS, D_MODEL), F32)]
    else:
        out_parts = [jax.ShapeDtypeStruct((ROWS, D_MODEL), F32)]
    return pl.pallas_call(
        functools.partial(_out_kernel, n_x=len(x_parts), n_o=len(out_parts)),
        grid=(ROWS // tm,),
        in_specs=[pl.BlockSpec((tm, D_MODEL), lambda i: (i, 0)),
                  pl.BlockSpec((None, D_MODEL, D_MODEL), lambda i: (layer, 0, 0))] + _row_specs(x_parts, tm) + [
                  pl.BlockSpec((None, 1, D_MODEL), lambda i: (_cond_of_tile(i, tm) * 3 + 2, 0, 0))],
        out_specs=_row_specs(out_parts, tm),
        out_shape=out_parts,
        compiler_params=_cparams(("arbitrary",)),
        name="out_proj_residual",
    )(mix, w, *x_parts, mod)


def _rope_a(x, cos_t, sin_t):
    return x * cos_t + (pltpu.roll(x, 32, 1) + pltpu.roll(x, 96, 1)) * sin_t


def _amid_kernel(*refs, do_q, norm_kv):
    if do_q:
        (ql_ref, kvl_ref, kpe_ref, wq_ref, wk_ref, wv_ref, qlg_ref, kvlg_ref, qng_ref, kng_ref, cos_ref, sin_ref,
         q_out, k_out, v_out, ckv_out, kpe_out) = refs
    else:
        (kvl_ref, kpe_ref, wk_ref, wv_ref, kvlg_ref, kng_ref, cos_ref, sin_ref, k_out, v_out) = refs
    cos_t = cos_ref[...]
    sin_t = sin_ref[...]
    inv_qk = 1.0 / A_QK

    if do_q:
        ql = ql_ref[...]
        qn = ql * lax.rsqrt(jnp.mean(ql * ql, axis=-1, keepdims=True) + EPS) * qlg_ref[...]
        qu = _dot(qn.astype(BF16), wq_ref[...])
        g_nope = qng_ref[:, :A_NOPE]
        g_rope = qng_ref[:, A_NOPE:]
        for h in range(A_HEADS):
            a = qu[:, h * A_QK_PAD:h * A_QK_PAD + A_NOPE]
            b = qu[:, h * A_QK_PAD + A_NOPE:(h + 1) * A_QK_PAD]
            ss = jnp.sum(a * a, axis=-1, keepdims=True) + jnp.sum(b * b, axis=-1, keepdims=True)
            r = lax.rsqrt(ss * inv_qk + EPS)
            q_out[:, h * A_QK_PAD:h * A_QK_PAD + A_NOPE] = (a * r * g_nope).astype(BF16)
            q_out[:, h * A_QK_PAD + A_NOPE:(h + 1) * A_QK_PAD] = _rope_a(b * r * g_rope, cos_t, sin_t).astype(BF16)

    kvl = kvl_ref[...]
    if norm_kv:
        ckv = kvl * lax.rsqrt(jnp.mean(kvl * kvl, axis=-1, keepdims=True) + EPS) * kvlg_ref[...]
    else:
        ckv = kvl
    ckv_b = ckv.astype(BF16)
    kn = _dot(ckv_b, wk_ref[...])
    v_out[...] = _dot(ckv_b, wv_ref[...]).astype(BF16)
    kpe = kpe_ref[...]
    if do_q:
        @pl.when(pl.program_id(0) < CTX_ROWS // TM_MID)
        def _():
            ckv_out[...] = ckv.reshape(ckv_out.shape)
            kpe_out[...] = kpe[:, :A_ROPE].reshape(kpe_out.shape)
    sp = jnp.sum(kpe * kpe, axis=-1, keepdims=True)
    g_nope = kng_ref[:, :A_NOPE]
    g_rope = kng_ref[:, A_NOPE:]
    for h in range(A_HEADS):
        a = kn[:, h * A_NOPE:(h + 1) * A_NOPE]
        r = lax.rsqrt((jnp.sum(a * a, axis=-1, keepdims=True) + sp) * inv_qk + EPS)
        k_out[:, h * A_QK_PAD:h * A_QK_PAD + A_NOPE] = (a * r * g_nope).astype(BF16)
        k_out[:, h * A_QK_PAD + A_NOPE:(h + 1) * A_QK_PAD] = _rope_a(kpe * r * g_rope, cos_t, sin_t).astype(BF16)


def _layer_slab_spec(tm, layer, width):
    n_ctx = CTX_ROWS // tm
    return pl.BlockSpec((tm // SEQ, None, SEQ, width), lambda i: (jnp.minimum(i, n_ctx - 1), layer, 0, 0))


def _call_with_carried(kernel, n_plain_out, carried, **kw):
    def run(*args, in_specs):
        args = list(args)
        in_specs = list(in_specs)
        n_real = len(args)
        aliases = {}
        for idx, arr in enumerate(carried):
            if arr is not None:
                aliases[len(args)] = n_plain_out + idx
                args.append(arr)
                in_specs.append(pl.BlockSpec(memory_space=pl.ANY))
        n_in = len(args)

        def body(*refs):
            kernel(*refs[:n_real], *refs[n_in:])

        return pl.pallas_call(body, in_specs=in_specs, input_output_aliases=aliases, **kw)(*args)
    return run


def _amid_tokens(y, wq, wk, wv, qlg, kvlg, qng, kng, cos_t, sin_t, layer, ckv_new, kpe_new):
    tm = TM_MID
    const = lambda i: (0, 0)
    rope_map = lambda i: (_rope_block_of_tile(i, tm), 0)
    n_ab = (DEPTH + 1) // 2
    run = _call_with_carried(
        functools.partial(_amid_kernel, do_q=True, norm_kv=True), 3, [ckv_new, kpe_new],
        grid=(ROWS // tm,),
        out_specs=[pl.BlockSpec((tm, A_HEADS * A_QK_PAD), lambda i: (i, 0)),
                   pl.BlockSpec((tm, A_HEADS * A_QK_PAD), lambda i: (i, 0)),
                   pl.BlockSpec((tm, A_HEADS * A_VDIM), lambda i: (i, 0)),
                   _layer_slab_spec(tm, layer, KV_LORA), _layer_slab_spec(tm, layer, A_ROPE)],
        out_shape=[jax.ShapeDtypeStruct((ROWS, A_HEADS * A_QK_PAD), BF16),
                   jax.ShapeDtypeStruct((ROWS, A_HEADS * A_QK_PAD), BF16),
                   jax.ShapeDtypeStruct((ROWS, A_HEADS * A_VDIM), BF16),
                   jax.ShapeDtypeStruct((BATCH, n_ab, SEQ, KV_LORA), F32),
                   jax.ShapeDtypeStruct((BATCH, n_ab, SEQ, A_ROPE), F32)],
        compiler_params=_cparams(("arbitrary",)),
        name="mla_qkv_prep")
    return run(y, y, y, wq, wk, wv, qlg, kvlg, qng, kng, cos_t, sin_t,
               in_specs=[pl.BlockSpec((tm, Q_LORA), lambda i: (i, AB_QLAT // Q_LORA)),
                         pl.BlockSpec((tm, KV_LORA), lambda i: (i, AB_KVLAT // KV_LORA)),
                         pl.BlockSpec((tm, LANES), lambda i: (i, AB_KPE // LANES)),
                         pl.BlockSpec(wq.shape, const), pl.BlockSpec(wk.shape, const), pl.BlockSpec(wv.shape, const),
                         pl.BlockSpec(qlg.shape, const), pl.BlockSpec(kvlg.shape, const),
                         pl.BlockSpec(qng.shape, const), pl.BlockSpec(kng.shape, const),
                         pl.BlockSpec((tm, LANES), rope_map), pl.BlockSpec((tm, LANES), rope_map)])


def _amid_cache(ckv, kpe, wk, wv, kvlg, kng, cos_t, sin_t):
    rows = ckv.shape[0]
    tm = TM_MID
    const = lambda i: (0, 0)
    outs = pl.pallas_call(
        functools.partial(_amid_kernel, do_q=False, norm_kv=False),
        grid=(rows // tm,),
        in_specs=[pl.BlockSpec((tm, KV_LORA), lambda i: (i, 0)),
                  pl.BlockSpec((tm, LANES), lambda i: (i, 0)),
                  pl.BlockSpec(wk.shape, const), pl.BlockSpec(wv.shape, const),
                  pl.BlockSpec(kvlg.shape, const), pl.BlockSpec(kng.shape, const),
                  pl.BlockSpec((tm, LANES), const), pl.BlockSpec((tm, LANES), const)],
        out_specs=[pl.BlockSpec((tm, A_HEADS * A_QK_PAD), lambda i: (i, 0)),
                   pl.BlockSpec((tm, A_HEADS * A_VDIM), lambda i: (i, 0))],
        out_shape=[jax.ShapeDtypeStruct((rows, A_HEADS * A_QK_PAD), BF16),
                   jax.ShapeDtypeStruct((rows, A_HEADS * A_VDIM), BF16)],
        compiler_params=_cparams(("parallel",)),
        name="mla_cache_kv_prep",
    )(ckv, kpe, wk, wv, kvlg, kng, cos_t, sin_t)
    return outs[0], outs[1]


def _scores(q, srcs, c):
    zs = []
    for k, _, keep in srcs:
        z = _dot_nt(q, k) * c
        zs.append(z if keep is None else jnp.where(keep, z, NEG_BIG))
    return zs


def _softmax_pv(zs, srcs, sink_z=None):
    tile_max = None
    for z in zs:
        for j in range(z.shape[1] // LANES):
            blk = z[:, j * LANES:(j + 1) * LANES]
            tile_max = blk if tile_max is None else jnp.maximum(tile_max, blk)
    m = tile_max.max(axis=-1, keepdims=True)
    if sink_z is not None:
        m = jnp.maximum(m, sink_z)
    acc = None
    for z, (_, v, _) in zip(zs, srcs):
        o = _dot(jnp.exp2(z - m).astype(BF16), jnp.concatenate([v, jnp.ones_like(v)], axis=1))
        acc = o if acc is None else acc + o
    dv = acc.shape[1] // 2
    den = acc[:, dv:]
    if sink_z is not None:
        den = den + jnp.exp2(sink_z - m)
    return acc[:, :dv] / den


def _attend_streams(n, q_of, srcs_of, c, sink_of=None):
    outs = []
    zs = _scores(q_of(0), srcs_of(0), c)
    for t in range(n):
        nxt = _scores(q_of(t + 1), srcs_of(t + 1), c) if t + 1 < n else None
        outs.append(_softmax_pv(zs, srcs_of(t), None if sink_of is None else sink_of(t)))
        zs = nxt
    return outs


def _attn_a_kernel(*refs, n_src, heads):
    q_ref = refs[0]
    k_refs = refs[1:1 + n_src]
    v_refs = refs[1 + n_src:1 + 2 * n_src]
    gate_ref, o_ref = refs[1 + 2 * n_src:]
    sub = min(ATTN_SUB, q_ref.shape[0])
    streams = [(h, r) for h in range(heads) for r in range(q_ref.shape[0] // sub)]

    def rows(t):
        return slice(streams[t][1] * sub, (streams[t][1] + 1) * sub)

    def q_of(t):
        h = streams[t][0]
        return q_ref[rows(t), h * A_QK_PAD:(h + 1) * A_QK_PAD]

    def srcs_of(t):
        h = streams[t][0]
        return [(k_ref[:, h * A_QK_PAD:(h + 1) * A_QK_PAD], v_ref[:, h * A_VDIM:(h + 1) * A_VDIM], None)
                for k_ref, v_ref in zip(k_refs, v_refs)]

    outs = _attend_streams(len(streams), q_of, srcs_of, A_QK ** -0.5 * LOG2_E)
    for t, o in enumerate(outs):
        cols = slice(streams[t][0] * A_VDIM, (streams[t][0] + 1) * A_VDIM)
        o_ref[rows(t), cols] = (o * _silu(gate_ref[rows(t), cols])).astype(o_ref.dtype)


def _attn_a(q, k, v, y, mix, kc=None, vc=None):
    latent = kc is not None
    if latent:
        nb, t, tq, hp, row0 = DEC_BATCH, DEC_SEQ, TQ_A, 1, CTX_ROWS
    else:
        nb, t, tq, hp, row0 = BATCH, SEQ, SEQ, A_HEADS, 0
    nq = t // tq
    qrow = lambda b, h, i: row0 // tq + b * nq + i
    in_specs = [pl.BlockSpec((tq, hp * A_QK_PAD), lambda b, h, i: (qrow(b, h, i), h))]
    args = [q]
    if latent:
        in_specs.append(pl.BlockSpec((PAST_LEN, hp * A_QK_PAD), lambda b, h, i: (b, h)))
        args.append(kc)
    in_specs.append(pl.BlockSpec((t, hp * A_QK_PAD), lambda b, h, i: (row0 // t + b, h)))
    args.append(k)
    if latent:
        in_specs.append(pl.BlockSpec((PAST_LEN, hp * A_VDIM), lambda b, h, i: (b, h)))
        args.append(vc)
    in_specs.append(pl.BlockSpec((t, hp * A_VDIM), lambda b, h, i: (row0 // t + b, h)))
    args.append(v)
    in_specs.append(pl.BlockSpec((tq, hp * A_VDIM), lambda b, h, i: (qrow(b, h, i), AB_AGATE // (hp * A_VDIM) + h)))
    args.append(y)
    run = _call_with_carried(
        functools.partial(_attn_a_kernel, n_src=2 if latent else 1, heads=hp), 0, [mix],
        grid=(nb, A_HEADS // hp, nq),
        out_specs=[pl.BlockSpec((tq, hp * A_VDIM), lambda b, h, i: (qrow(b, h, i), h))],
        out_shape=[_MIX_SHAPE],
        compiler_params=_cparams(("parallel", "parallel", "parallel")),
        name="mla_attention_latent" if latent else "mla_attention_context")
    return run(*args, in_specs=in_specs)[0]


def _hgrn_constants():
    c, nl = HG_CHUNK, HG_LEVELS
    t = np.arange(c)[:, None]
    u = np.arange(c)[None, :]
    tri_f = (u <= t).astype(np.float32)
    mask_f = np.zeros((nl, c, c), np.float32)
    coef_f = np.zeros((nl, c, LANES), np.float32)
    for l in range(nl):
        half = c >> (l + 1)
        seg = 2 * half
        mask_f[l] = ((u // seg) == (t // seg)) & ((t % seg) >= half) & ((u % seg) < half)
        later = np.broadcast_to((t % seg) >= half, (c, LANES))
        coef_f[l] = np.where(later, 1.0, -1.0 if half > 1 else 0.0)
    tri_b = tri_f[::-1, ::-1]
    mask_b = mask_f[:, ::-1, ::-1]
    coef_b = coef_f[:, ::-1, :]
    to_tri = lambda a: jnp.asarray(np.concatenate([a, a, a], axis=1), BF16)
    to_f32 = lambda a: jnp.asarray(np.ascontiguousarray(a), F32)
    return (to_tri(tri_f), to_tri(tri_b)), (to_f32(mask_f), to_f32(mask_b)), (to_f32(coef_f), to_f32(coef_b))


def _hgrn_decays(x, lb, tri3):
    kk = (1.0 - lb) * _sigmoid(-x)
    lf = jnp.log2(1.0 - kk)
    hi = lf.astype(BF16)
    r1 = lf - hi.astype(F32)
    mid = r1.astype(BF16)
    lo = (r1 - mid.astype(F32)).astype(BF16)
    return kk, lf, _dot(tri3, jnp.concatenate([hi, mid, lo], axis=0))


def _hgrn_level_arg(l, cs, lf, coef, forward):
    c = HG_CHUNK
    nv = c // 8
    half = c >> (l + 1)
    if half == 1:
        return lf * coef
    cs3 = cs.reshape(nv, 8, LANES)

    def in_vreg_row(r):
        return jnp.broadcast_to(cs3[:, r:r + 1, :], (nv, 8, LANES))

    if half >= 8:
        m = half // 8
        nseg = nv // (2 * m)
        edge = cs3[:, 7:8, :] if forward else cs3[:, 0:1, :]
        e4 = edge.reshape(nseg, 2 * m, 1, LANES)
        a = e4[:, m - 1:m] if forward else e4[:, m:m + 1]
        anchor = jnp.broadcast_to(a, (nseg, 2 * m, 8, LANES)).reshape(c, LANES)
    elif half == 4:
        anchor = in_vreg_row(3 if forward else 4).reshape(c, LANES)
    else:
        r0, r1 = (1, 5) if forward else (2, 6)
        sub = lax.broadcasted_iota(jnp.int32, (nv, 8, LANES), 1)
        anchor = jnp.where(sub < 4, in_vreg_row(r0), in_vreg_row(r1)).reshape(c, LANES)
    return (cs - anchor) * coef


def _hgrn_kernel(*refs, t_len, zero_init, emit_state):
    c = HG_CHUNK
    n_chunks = t_len // c
    it = iter(refs)
    bq_ref, ff_ref, fb_ref, vi_ref, bg_ref, lb_ref, hg_ref = (next(it) for _ in range(7))
    if not zero_init:
        s0f_ref, s0b_ref = next(it), next(it)
    trif_ref, trib_ref, maskf_ref, maskb_ref, coeff_ref, coefb_ref = (next(it) for _ in range(6))
    o_ref = next(it)
    if emit_state:
        sf_ref, sb_ref = next(it), next(it)
    of_ref, ob_ref, stf_ref, stb_ref = (next(it) for _ in range(4))

    if zero_init:
        stf_ref[...] = jnp.zeros((B_DV, B_DK), F32)
        stb_ref[...] = jnp.zeros((B_DV, B_DK), F32)
    else:
        stf_ref[...] = s0f_ref[...].T
        stb_ref[...] = s0b_ref[...].T
    lb = lb_ref[...]

    nl = HG_LEVELS
    unroll = min(HG_UNROLL, n_chunks)
    dirs = ((ff_ref, lb[0:1, :], trif_ref, maskf_ref, coeff_ref, stf_ref, of_ref, True),
            (fb_ref, lb[1:2, :], trib_ref, maskb_ref, coefb_ref, stb_ref, ob_ref, False))

    def body(i, carry):
        chains = []
        for f_ref, lb_row, tri_ref, mask_ref, coef_ref, st_ref, out_ref, forward in dirs:
            for u in range(unroll):
                k = i * unroll + u
                r0 = pl.multiple_of((k if forward else n_chunks - 1 - k) * c, c)
                chains.append(dict(rows=pl.ds(r0, c), f_ref=f_ref, lb=lb_row, tri_ref=tri_ref, mask_ref=mask_ref,
                                   coef_ref=coef_ref, st_ref=st_ref, out_ref=out_ref, forward=forward))
        for ch in chains:
            ch["kk"], ch["lf"], ch["cs"] = _hgrn_decays(ch["f_ref"][ch["rows"], :], ch["lb"], ch["tri_ref"][...])
            ch["q"] = _silu(bq_ref[ch["rows"], :])
            ch["v"] = vi_ref[ch["rows"], :]
            ch["q16"] = ch["q"].astype(BF16)
            ch["kk16"] = ch["kk"].astype(BF16)
            ch["sc"] = jnp.zeros((c, c), F32)
        for l in range(nl):
            for ch in chains:
                el = jnp.exp2(_hgrn_level_arg(l, ch["cs"], ch["lf"], ch["coef_ref"][l], ch["forward"])).astype(BF16)
                ch["sc"] = ch["sc"] + ch["mask_ref"][l] * _dot_nt(ch["q16"] * el, ch["kk16"] * el)
        for ch in chains:
            cs, q, kk, v = ch["cs"], ch["q"], ch["kk"], ch["v"]
            end = cs[c - 1:c, :] if ch["forward"] else cs[0:1, :]
            ch["end"] = end
            ch["q_in"] = (q * jnp.exp2(cs)).astype(BF16)
            ch["upd"] = _dot(v.T.astype(BF16), (kk * jnp.exp2(end - cs)).astype(BF16))
            ch["o"] = _dot(ch["sc"].astype(BF16), v.astype(BF16)) + jnp.sum(q * kk, axis=-1, keepdims=True) * v
        for ch in chains:
            st = ch["st_ref"][...]
            ch["out_ref"][ch["rows"], :] = ch["o"] + _dot_nt(ch["q_in"], st.astype(BF16))
            ch["st_ref"][...] = jnp.exp2(ch["end"]) * st + ch["upd"]
        return carry

    lax.fori_loop(0, n_chunks // unroll, body, 0)
    if emit_state:
        sf_ref[...] = stf_ref[...].T
        sb_ref[...] = stb_ref[...].T

    hg = hg_ref[...]

    blk = unroll * c

    def finish(i, carry):
        rows = pl.ds(pl.multiple_of(i * blk, blk), blk)
        o = of_ref[rows, :] + ob_ref[rows, :]
        o = o * lax.rsqrt(jnp.mean(o * o, axis=-1, keepdims=True) + EPS) * hg
        o_ref[rows, :] = (o * _silu(bg_ref[rows, :])).astype(o_ref.dtype)
        return carry

    lax.fori_loop(0, t_len // blk, finish, 0)


def _hgrn(y, lb, hg, mix, consts, s0f=None, s0b=None, layer=None, sf_new=None, sb_new=None):
    latent = s0f is not None
    if latent:
        nb, t, row0 = DEC_BATCH, DEC_SEQ, CTX_ROWS
    else:
        nb, t, row0 = BATCH, SEQ, 0
    rb = lambda b: row0 // t + b
    col = lambda off: (lambda b, h: (rb(b), off // LANES + h))
    const2 = lambda b, h: (0, 0)
    const3 = lambda b, h: (0, 0, 0)
    in_specs = [pl.BlockSpec((t, LANES), col(AB_BQ)), pl.BlockSpec((t, LANES), col(AB_BFF)),
                pl.BlockSpec((t, LANES), col(AB_BFB)), pl.BlockSpec((t, LANES), col(AB_BI)),
                pl.BlockSpec((t, LANES), col(AB_BGATE)),
                pl.BlockSpec((2, LANES), lambda b, h: (0, h)),
                pl.BlockSpec((1, LANES), const2)]
    args = [y, y, y, y, y, lb, hg]
    if latent:
        st_spec = pl.BlockSpec((None, None, B_DK, B_DV), lambda b, h: (b, h, 0, 0))
        in_specs += [st_spec, st_spec]
        args += [s0f, s0b]
    tris, masks, coefs = consts
    in_specs += ([pl.BlockSpec(a.shape, const2) for a in tris] + [pl.BlockSpec(a.shape, const3) for a in masks]
                 + [pl.BlockSpec(a.shape, const3) for a in coefs])
    args += [*tris, *masks, *coefs]
    out_specs = [pl.BlockSpec((t, LANES), lambda b, h: (rb(b), B_HEADS + h))]
    out_shape = [_MIX_SHAPE]
    carried = [mix]
    if not latent:
        st_out = pl.BlockSpec((None, None, None, B_DK, B_DV), lambda b, h: (b, layer, h, 0, 0))
        out_specs += [st_out, st_out]
        out_shape += [jax.ShapeDtypeStruct((nb, (DEPTH + 1) // 2, B_HEADS, B_DK, B_DV), F32)] * 2
        carried += [sf_new, sb_new]
    run = _call_with_carried(
        functools.partial(_hgrn_kernel, t_len=t, zero_init=not latent, emit_state=not latent), 0, carried,
        grid=(nb, B_HEADS),
        out_specs=out_specs,
        out_shape=out_shape,
        scratch_shapes=[pltpu.VMEM((t, B_DV), F32), pltpu.VMEM((t, B_DV), F32),
                        pltpu.VMEM((B_DV, B_DK), F32), pltpu.VMEM((B_DV, B_DK), F32)],
        compiler_params=_cparams(("parallel", "parallel")),
        name="hgrn2_latent" if latent else "hgrn2_context")
    return run(*args, in_specs=in_specs)


def _cmid_kernel(q_ref, k_ref, v_ref, qg_ref, kg_ref, cos_ref, sin_ref, q_out, k_out, v_out, kc_out, vc_out):
    cos_t = cos_ref[...]
    sin_t = sin_ref[...]
    qg = qg_ref[...]
    kg = kg_ref[...]
    in_ctx = pl.program_id(0) < CTX_ROWS // TM_MID

    def norm(x, g):
        return x * lax.rsqrt(jnp.mean(x * x, axis=-1, keepdims=True) + EPS) * g

    def rope(x):
        return x * cos_t + pltpu.roll(x, C_HEAD_DIM // 2, 1) * sin_t

    for h in range(C_HEADS):
        sl = slice(h * C_HEAD_DIM, (h + 1) * C_HEAD_DIM)
        q_out[:, sl] = rope(norm(q_ref[:, sl], qg)).astype(BF16)
    for h in range(C_KV_HEADS):
        sl = slice(h * C_HEAD_DIM, (h + 1) * C_HEAD_DIM)
        kn = norm(k_ref[:, sl], kg)
        k_out[:, sl] = rope(kn).astype(BF16)

        @pl.when(in_ctx)
        def _():
            kc_out[:, :, sl] = kn.reshape(kc_out.shape[0], SEQ, C_HEAD_DIM)
    v = v_ref[...]
    v_out[...] = v.astype(BF16)

    @pl.when(in_ctx)
    def _():
        vc_out[...] = v.reshape(vc_out.shape)


def _cmid(y, qg, kg, cos_t, sin_t, layer, kc_new, vc_new):
    tm = TM_MID
    const = lambda i: (0, 0)
    rope_map = lambda i: (_rope_block_of_tile(i, tm), 0)
    cache = jax.ShapeDtypeStruct((BATCH, DEPTH // 2, SEQ, C_KV_WIDTH), F32)
    run = _call_with_carried(
        _cmid_kernel, 3, [kc_new, vc_new],
        grid=(ROWS // tm,),
        out_specs=[pl.BlockSpec((tm, C_WIDTH), lambda i: (i, 0)),
                   pl.BlockSpec((tm, C_KV_WIDTH), lambda i: (i, 0)),
                   pl.BlockSpec((tm, C_KV_WIDTH), lambda i: (i, 0)),
                   _layer_slab_spec(tm, layer, C_KV_WIDTH), _layer_slab_spec(tm, layer, C_KV_WIDTH)],
        out_shape=[jax.ShapeDtypeStruct((ROWS, C_WIDTH), BF16),
                   jax.ShapeDtypeStruct((ROWS, C_KV_WIDTH), BF16),
                   jax.ShapeDtypeStruct((ROWS, C_KV_WIDTH), BF16), cache, cache],
        compiler_params=_cparams(("arbitrary",)),
        name="gqa_qkv_prep")
    return run(y, y, y, qg, kg, cos_t, sin_t,
               in_specs=[pl.BlockSpec((tm, C_WIDTH), lambda i: (i, C_Q // C_WIDTH)),
                         pl.BlockSpec((tm, C_KV_WIDTH), lambda i: (i, C_K // C_KV_WIDTH)),
                         pl.BlockSpec((tm, C_KV_WIDTH), lambda i: (i, C_V // C_KV_WIDTH)),
                         pl.BlockSpec((1, C_HEAD_DIM), const), pl.BlockSpec((1, C_HEAD_DIM), const),
                         pl.BlockSpec((tm, LANES), rope_map), pl.BlockSpec((tm, LANES), rope_map)])


def _attn_c_kernel(*refs, band, tq, t_len, groups, stack):
    if band:
        q_ref, kc_ref, vc_ref, kl_ref, vl_ref, sink_ref, gate_ref, o_ref = refs
    else:
        q_ref, kc_ref, vc_ref, sink_ref, gate_ref, o_ref = refs
    hd = C_HEAD_DIM
    rows = stack * tq
    streams = [(g, s) for g in range(groups) for s in range(C_GROUP // stack)]
    if band:
        i = pl.program_id(2)
        width = tq + 2 * WINDOW
        start = pl.multiple_of(jnp.clip(i * tq - WINDOW, 0, t_len - width), WINDOW)
        qpos = i * tq + (lax.broadcasted_iota(jnp.int32, (rows, width), 0) & (tq - 1))
        kpos = start + lax.broadcasted_iota(jnp.int32, (rows, width), 1)
        in_band = jnp.abs(kpos - qpos) <= WINDOW

    def heads_of(t):
        g, s = streams[t]
        return [g * C_GROUP + s * stack + r for r in range(stack)]

    def q_of(t):
        return jnp.concatenate([q_ref[:, h * hd:(h + 1) * hd] for h in heads_of(t)], axis=0)

    def srcs_of(t):
        g = streams[t][0]
        cols = slice(g * hd, (g + 1) * hd)
        srcs = [(kc_ref[:, cols], vc_ref[:, cols], None)]
        if band:
            srcs.append((kl_ref[pl.ds(start, width), cols], vl_ref[pl.ds(start, width), cols], in_band))
        return srcs

    def sink_of(t):
        return jnp.concatenate([jnp.broadcast_to(sink_ref[h][:, :1] * LOG2_E, (tq, 1)) for h in heads_of(t)], axis=0)

    outs = _attend_streams(len(streams), q_of, srcs_of, hd ** -0.5 * LOG2_E, sink_of)
    for t, o in enumerate(outs):
        for r, h in enumerate(heads_of(t)):
            cols = slice(h * hd, (h + 1) * hd)
            o_ref[:, cols] = (o[r * tq:(r + 1) * tq] * _silu(gate_ref[:, cols])).astype(o_ref.dtype)


def _attn_c(q, k, v, y, sink, mix, kc=None, vc=None):
    latent = kc is not None
    if latent:
        nb, t, tq, gp, stack, row0 = DEC_BATCH, DEC_SEQ, TQ_C, 1, 2, CTX_ROWS
    else:
        nb, t, tq, gp, stack, row0 = BATCH, SEQ, SEQ, C_KV_HEADS, C_GROUP, 0
    gw = gp * C_GROUP * C_HEAD_DIM
    kvw = gp * C_HEAD_DIM
    nq = t // tq
    qrow = lambda b, g, i: row0 // tq + b * nq + i
    own_kv = pl.BlockSpec((t, kvw), lambda b, g, i: (row0 // t + b, g))
    in_specs = [pl.BlockSpec((tq, gw), lambda b, g, i: (qrow(b, g, i), g))]
    args = [q]
    if latent:
        ctx_kv = pl.BlockSpec((PAST_LEN, kvw), lambda b, g, i: (b, g))
        in_specs += [ctx_kv, ctx_kv, own_kv, own_kv]
        args += [kc, vc, k, v]
    else:
        in_specs += [own_kv, own_kv]
        args += [k, v]
    in_specs.append(pl.BlockSpec((gp * C_GROUP, 1, LANES), lambda b, g, i: (g, 0, 0)))
    args.append(sink)
    in_specs.append(pl.BlockSpec((tq, gw), lambda b, g, i: (qrow(b, g, i), C_GATE // gw + g)))
    args.append(y)
    run = _call_with_carried(
        functools.partial(_attn_c_kernel, band=latent, tq=tq, t_len=t, groups=gp, stack=stack), 0, [mix],
        grid=(nb, C_KV_HEADS // gp, nq),
        out_specs=[pl.BlockSpec((tq, gw), lambda b, g, i: (qrow(b, g, i), g))],
        out_shape=[_MIX_SHAPE],
        compiler_params=_cparams(("parallel", "parallel", "parallel")),
        name="gqa_attention_latent" if latent else "gqa_attention_context")
    return run(*args, in_specs=in_specs)[0]


def _axial_angles(n_tokens, rot_dim):
    rows = n_tokens // GRID_W
    row = jnp.repeat(jnp.arange(rows, dtype=F32), GRID_W)
    col = jnp.tile(jnp.arange(GRID_W, dtype=F32), rows)
    n_freq = rot_dim // 4
    inv = ROPE_BASE ** (-jnp.arange(n_freq, dtype=F32) / n_freq)
    return jnp.concatenate([row[:, None] * inv, col[:, None] * inv], axis=-1)


def _rope_tables(rot_dim, tm):
    ang = _axial_angles(DEC_SEQ, rot_dim)
    cos, sin = jnp.cos(ang), jnp.sin(ang)
    pad = LANES - rot_dim
    cos_t = jnp.concatenate([cos, cos, jnp.ones((DEC_SEQ, pad), F32)], axis=-1)
    sin_t = jnp.concatenate([-sin, sin, jnp.zeros((DEC_SEQ, pad), F32)], axis=-1)
    cos_t = jnp.concatenate([jnp.ones((tm, LANES), F32), cos_t], axis=0)
    sin_t = jnp.concatenate([jnp.zeros((tm, LANES), F32), sin_t], axis=0)
    return cos_t, sin_t


def _lower_bounds(lb_logits):
    p = jax.nn.softmax(lb_logits.astype(F32), axis=0)
    return jnp.cumsum(p, axis=0) - p[0:1]


def _pad_head_gain(g):
    return jnp.concatenate([g, jnp.zeros((A_QK_PAD - A_QK,), F32)])[None, :]


def kernel(x_prompt, x_sample, cache_ckv, cache_kpe, state_hgrn_fwd, state_hgrn_bwd, cache_k_c, cache_v_c, c, c_ctx,
           mod_w_ab, mod_b_ab, norm_ab, w_in_ab, q_lora_norm, kv_lora_norm, w_q_up, w_kv_up, q_norm_ab, k_norm_ab,
           hgrn_lb_logits, hgrn_out_norm, w_out_ab, mod_w_c, mod_b_c, norm_c, w_in_c, q_norm_c, k_norm_c, sink_c,
           w_out_c):
    x_parts = [x_prompt.reshape(CTX_ROWS, D_MODEL), x_sample.reshape(LAT_ROWS, D_MODEL)]
    cond8 = jnp.concatenate([c_ctx[None, :], c, jnp.zeros((N_COND - 1 - DEC_BATCH, D_MODEL), F32)], axis=0)
    mods_ab = _modulation(cond8, mod_w_ab, mod_b_ab)
    mods_c = _modulation(cond8, mod_w_c, mod_b_c)
    lower = _lower_bounds(hgrn_lb_logits)
    cos_a, sin_a = _rope_tables(A_ROPE, TM_MID)
    cos_c, sin_c = _rope_tables(C_HEAD_DIM, TM_MID)
    hg_consts = _hgrn_constants()
    w_ab_main, w_ab_tail = _prep_w_in_ab(jnp.swapaxes(w_in_ab, 1, 2))
    w_in_c16, w_out_ab16, w_out_c16 = w_in_c.astype(BF16), w_out_ab.astype(BF16), w_out_c.astype(BF16)

    ckv_new = kpe_new = sf_new = sb_new = kc_new = vc_new = None
    for layer in range(DEPTH):
        j = layer // 2
        last = layer == DEPTH - 1
        if layer % 2 == 0:
            mod = mods_ab[j].reshape(3 * N_COND, 1, D_MODEL)
            w_parts = [(w_ab_main, (AB_N - TN_IN) // TN_IN, None), (w_ab_tail, 1, None)]
            wq = jnp.pad(w_q_up[j].reshape(Q_LORA, A_HEADS, A_QK),
                         ((0, 0), (0, 0), (0, A_QK_PAD - A_QK))).reshape(Q_LORA, A_HEADS * A_QK_PAD).astype(BF16)
            wkv = w_kv_up[j].reshape(KV_LORA, A_HEADS, A_NOPE + A_VDIM)
            wk = wkv[:, :, :A_NOPE].reshape(KV_LORA, A_HEADS * A_NOPE).astype(BF16)
            wv = wkv[:, :, A_NOPE:].reshape(KV_LORA, A_HEADS * A_VDIM).astype(BF16)
            qlg, kvlg = q_lora_norm[j][None, :], kv_lora_norm[j][None, :]
            qng, kng = _pad_head_gain(q_norm_ab[j]), _pad_head_gain(k_norm_ab[j])

            y = _in_proj(x_parts, norm_ab[j][None, :], mod, w_parts, j, w_transposed=True)
            q, k, v, ckv_new, kpe_new = _amid_tokens(y, wq, wk, wv, qlg, kvlg, qng, kng, cos_a, sin_a,
                                                     j, ckv_new, kpe_new)
            kpe_cache = jnp.pad(cache_kpe[:, j].reshape(DEC_BATCH * PAST_LEN, A_ROPE), ((0, 0), (0, LANES - A_ROPE)))
            kc, vc = _amid_cache(cache_ckv[:, j].reshape(DEC_BATCH * PAST_LEN, KV_LORA), kpe_cache, wk, wv, kvlg, kng,
                                 cos_a, sin_a)
            mix = _attn_a(q, k, v, y, None)
            mix = _attn_a(q, k, v, y, mix, kc=kc, vc=vc)
            hg = hgrn_out_norm[j][None, :]
            mix, sf_new, sb_new = _hgrn(y, lower[j], hg, mix, hg_consts, layer=j, sf_new=sf_new, sb_new=sb_new)
            (mix,) = _hgrn(y, lower[j], hg, mix, hg_consts, s0f=state_hgrn_fwd[:, j], s0b=state_hgrn_bwd[:, j])
            x_parts = _out_proj(mix, w_out_ab16, j, x_parts, mod, split_out=last)
        else:
            mod = mods_c[j].reshape(3 * N_COND, 1, D_MODEL)
            w_parts = [(w_in_c16, C_N // TN_IN, lambda t: jnp.where(t < 2, t, jnp.where(t < 4, t + 1, 2)))]
            y = _in_proj(x_parts, norm_c[j][None, :], mod, w_parts, j)
            q, k, v, kc_new, vc_new = _cmid(y, q_norm_c[j][None, :], k_norm_c[j][None, :], cos_c, sin_c,
                                            j, kc_new, vc_new)
            kc = cache_k_c[:, j].reshape(DEC_BATCH * PAST_LEN, C_KV_WIDTH).astype(BF16)
            vc = cache_v_c[:, j].reshape(DEC_BATCH * PAST_LEN, C_KV_WIDTH).astype(BF16)
            sink = jnp.broadcast_to(sink_c[j][:, None, None], (C_HEADS, 1, LANES))
            mix = _attn_c(q, k, v, y, sink, None)
            mix = _attn_c(q, k, v, y, sink, mix, kc=kc, vc=vc)
            x_parts = _out_proj(mix, w_out_c16, j, x_parts, mod, split_out=last)

    cache_c_shape = (BATCH, DEPTH // 2, SEQ, C_KV_HEADS, C_HEAD_DIM)
    return (x_parts[0].reshape(BATCH, SEQ, D_MODEL), x_parts[1].reshape(DEC_BATCH, DEC_SEQ, D_MODEL),
            ckv_new, kpe_new, sf_new, sb_new, kc_new.reshape(cache_c_shape), vc_new.reshape(cache_c_shape))
```

```python
import functools

import numpy as np
import jax
import jax.numpy as jnp
from jax import lax
from jax.experimental import pallas as pl
from jax.experimental.pallas import tpu as pltpu

F32 = jnp.float32
BF16 = jnp.bfloat16

D_MODEL = 2048
BATCH = 16
SEQ = 256
DEPTH = 4
DEC_BATCH = 4
DEC_SEQ = 2048
PAST_LEN = 256
GRID_W = 64
A_HEADS = 8
A_NOPE = 128
A_ROPE = 64
A_VDIM = 128
A_QK = A_NOPE + A_ROPE
A_QK_PAD = 256
Q_LORA = 512
KV_LORA = 256
B_HEADS = 8
B_DK = 128
B_DV = 128
C_HEADS = 16
C_KV_HEADS = 4
C_GROUP = C_HEADS // C_KV_HEADS
C_HEAD_DIM = 128
C_WIDTH = C_HEADS * C_HEAD_DIM
C_KV_WIDTH = C_KV_HEADS * C_HEAD_DIM
WINDOW = 128
ROPE_BASE = 10000.0
EPS = 1e-6
NEG_BIG = -1e30
LOG2_E = 1.4426950408889634

LANES = 128
CTX_ROWS = BATCH * SEQ
LAT_ROWS = DEC_BATCH * DEC_SEQ
ROWS = CTX_ROWS + LAT_ROWS
N_COND = 8

AB_AGATE = 0
AB_BQ = 1024
AB_BFF = 2048
AB_BFB = 3072
AB_BI = 4096
AB_BGATE = 5120
AB_QLAT = 6144
AB_KVLAT = 6656
AB_KPE = 6912
AB_N = 7168
C_Q = 0
C_GATE = 2048
C_K = 4096
C_V = 4608
C_N = 5120

TM_PROJ = 1024
TN_IN = 1024
TM_OUT = 512
TM_MID = 512
TQ_A = 1024
TQ_C = 256
ATTN_SUB = 256
HG_CHUNK = 128
HG_LEVELS = 7
HG_UNROLL = 4
VMEM_LIMIT = 56 * 1024 * 1024
_MIX_SHAPE = jax.ShapeDtypeStruct((ROWS, D_MODEL), BF16)


def _cparams(sem):
    return pltpu.CompilerParams(dimension_semantics=sem, vmem_limit_bytes=VMEM_LIMIT)


def _sigmoid(x):
    return 0.5 * jnp.tanh(0.5 * x) + 0.5


def _silu(x):
    return x * _sigmoid(x)


def _dot(a, b):
    return jnp.dot(a, b, preferred_element_type=F32)


def _dot_nt(a, b):
    return lax.dot_general(a, b, (((1,), (1,)), ((), ())), preferred_element_type=F32)


def _cond_of_tile(i, tm):
    n_ctx = CTX_ROWS // tm
    per_batch = DEC_SEQ // tm
    return jnp.where(i < n_ctx, 0, 1 + (i - n_ctx) // per_batch)


def _rope_block_of_tile(i, tm):
    n_ctx = CTX_ROWS // tm
    per_batch = DEC_SEQ // tm
    return jnp.where(i < n_ctx, 0, 1 + (i - n_ctx) % per_batch)


def _mod_kernel(c_ref, w_ref, b_ref, o_ref):
    a = _silu(c_ref[...]).astype(BF16)
    o_ref[...] = _dot(a, w_ref[...].astype(BF16)) + b_ref[...]


def _modulation(cond8, w_mod, b_mod):
    n = w_mod.shape[0]
    tn = 1024
    return pl.pallas_call(
        _mod_kernel,
        grid=(n, 3 * D_MODEL // tn),
        in_specs=[pl.BlockSpec((N_COND, D_MODEL), lambda l, j: (0, 0)),
                  pl.BlockSpec((None, D_MODEL, tn), lambda l, j: (l, 0, j)),
                  pl.BlockSpec((None, 1, tn), lambda l, j: (l, 0, j))],
        out_specs=pl.BlockSpec((None, N_COND, tn), lambda l, j: (l, 0, j)),
        out_shape=jax.ShapeDtypeStruct((n, N_COND, 3 * D_MODEL), F32),
        compiler_params=_cparams(("parallel", "parallel")),
        name="adaln_mod",
    )(cond8, w_mod, b_mod.reshape(n, 1, 3 * D_MODEL))


def _prep_w_kernel(w_ref, o_ref):
    n_head = Q_LORA + KV_LORA + A_ROPE
    n_main = AB_N - TN_IN
    half = A_ROPE // 2
    x1_end = Q_LORA + KV_LORA + half
    x2_at = Q_LORA + KV_LORA + LANES // 2
    o_ref[:n_main, :] = w_ref[n_head:, :].astype(BF16)
    o_ref[n_main:, :] = jnp.zeros((TN_IN, o_ref.shape[1]), BF16)
    o_ref[n_main:n_main + x1_end, :] = w_ref[:x1_end, :].astype(BF16)
    o_ref[n_main + x2_at:n_main + x2_at + half, :] = w_ref[x1_end:n_head, :].astype(BF16)


def _prep_w_in_ab(w_t):
    n_layers, n, _ = w_t.shape
    tc = 256
    return pl.pallas_call(
        _prep_w_kernel,
        grid=(n_layers, D_MODEL // tc),
        in_specs=[pl.BlockSpec((None, n, tc), lambda l, i: (l, 0, i))],
        out_specs=pl.BlockSpec((None, AB_N, tc), lambda l, i: (l, 0, i)),
        out_shape=jax.ShapeDtypeStruct((n_layers, AB_N, D_MODEL), BF16),
        compiler_params=_cparams(("parallel", "parallel")),
        name="w_in_ab_layout",
    )(w_t)


def _row_specs(parts, tm, single_buffer=False, tile_of=lambda i, *_: i):
    if len(parts) == 1:
        return [pl.BlockSpec((tm, D_MODEL), lambda *g: (tile_of(*g), 0))]
    n_first = CTX_ROWS // tm
    mode = dict(pipeline_mode=pl.Buffered(1)) if single_buffer else {}
    return [pl.BlockSpec((tm, D_MODEL), lambda *g: (jnp.minimum(tile_of(*g), n_first - 1), 0), **mode),
            pl.BlockSpec((tm, D_MODEL), lambda *g: (jnp.maximum(tile_of(*g) - n_first, 0), 0), **mode)]


IN_SUB = 4


def _in_next_tile(i, j):
    return jnp.where(j == 0, i, jnp.minimum(i + 1, ROWS // TM_PROJ - 1))


def _in_kernel(*refs, n_x, w_transposed):
    x_refs = refs[:n_x]
    g_ref, sh_ref, sc_ref, w_ref, o_ref, h_ref, hn_ref = refs[n_x:]
    i, j = pl.program_id(0), pl.program_id(1)
    n_first = CTX_ROWS // TM_PROJ
    sub_rows = TM_PROJ // IN_SUB

    def normalise(dst_ref, r0):
        rows = pl.ds(r0, sub_rows)
        if n_x == 1:
            x = x_refs[0][rows, :]
        else:
            x = jnp.where(_in_next_tile(i, j) < n_first, x_refs[0][rows, :], x_refs[1][rows, :])
        r = lax.rsqrt(jnp.mean(x * x, axis=-1, keepdims=True) + EPS)
        h = (x * r * g_ref[...]) * (1.0 + sc_ref[...]) + sh_ref[...]
        dst_ref[rows, :] = h.astype(BF16)

    @pl.when((i == 0) & (j == 0))
    def _():
        for s in range(IN_SUB):
            normalise(h_ref, s * sub_rows)

    @pl.when((i > 0) & (j == 0))
    def _():
        h_ref[...] = hn_ref[...]

    normalise(hn_ref, pl.multiple_of((jnp.clip(j, 1, IN_SUB) - 1) * sub_rows, sub_rows))
    o_ref[...] = (_dot_nt if w_transposed else _dot)(h_ref[...], w_ref[...])


def _in_proj(x_parts, g, mod, w, layer, col_map=None, w_transposed=False):
    tm, tn = TM_PROJ, TN_IN
    n_tiles = (w.shape[1] if w_transposed else w.shape[2]) // tn
    assert n_tiles > IN_SUB
    col_map = col_map or (lambda t: t)
    if w_transposed:
        w_spec = pl.BlockSpec((None, tn, D_MODEL), lambda i, j: (layer, col_map(j), 0))
    else:
        w_spec = pl.BlockSpec((None, D_MODEL, tn), lambda i, j: (layer, 0, col_map(j)))
    mod_row = lambda part: (lambda i, j: (_cond_of_tile(_in_next_tile(i, j), tm) * 3 + part, 0, 0))
    return pl.pallas_call(
        functools.partial(_in_kernel, n_x=len(x_parts), w_transposed=w_transposed),
        grid=(ROWS // tm, n_tiles),
        in_specs=_row_specs(x_parts, tm, single_buffer=True, tile_of=_in_next_tile) + [
            pl.BlockSpec((1, D_MODEL), lambda i, j: (0, 0)),
            pl.BlockSpec((None, 1, D_MODEL), mod_row(0)),
            pl.BlockSpec((None, 1, D_MODEL), mod_row(1)), w_spec],
        out_specs=pl.BlockSpec((tm, tn), lambda i, j: (i, j)),
        out_shape=jax.ShapeDtypeStruct((ROWS, n_tiles * tn), F32),
        scratch_shapes=[pltpu.VMEM((tm, D_MODEL), BF16), pltpu.VMEM((tm, D_MODEL), BF16)],
        compiler_params=_cparams(("arbitrary", "arbitrary")),
        name="norm_mod_in_proj",
    )(*x_parts, g, mod, mod, w)


def _out_kernel(*refs, n_x, n_o):
    m_ref, w_ref = refs[:2]
    x_refs = refs[2:2 + n_x]
    gt_ref = refs[2 + n_x]
    o_refs = refs[3 + n_x:]
    y = gt_ref[...] * _dot(m_ref[...], w_ref[...])

    def emit(x_ref, o_ref):
        o_ref[...] = x_ref[...] + y

    if n_x == 1 and n_o == 1:
        emit(x_refs[0], o_refs[0])
    else:
        in_first = pl.program_id(0) < CTX_ROWS // TM_OUT
        pl.when(in_first)(lambda: emit(x_refs[0], o_refs[0]))
        pl.when(jnp.logical_not(in_first))(lambda: emit(x_refs[-1], o_refs[-1]))


def _out_proj(mix, w, layer, x_parts, mod, split_out):
    tm = TM_OUT
    if split_out:
        out_parts = [jax.ShapeDtypeStruct((CTX_ROWS, D_MODEL), F32), jax.ShapeDtypeStruct((LAT_ROWS, D_MODEL), F32)]
    else:
        out_parts = [jax.ShapeDtypeStruct((ROWS, D_MODEL), F32)]
    return pl.pallas_call(
        functools.partial(_out_kernel, n_x=len(x_parts), n_o=len(out_parts)),
        grid=(ROWS // tm,),
        in_specs=[pl.BlockSpec((tm, D_MODEL), lambda i: (i, 0)),
                  pl.BlockSpec((None, D_MODEL, D_MODEL), lambda i: (layer, 0, 0))] + _row_specs(x_parts, tm) + [
                  pl.BlockSpec((None, 1, D_MODEL), lambda i: (_cond_of_tile(i, tm) * 3 + 2, 0, 0))],
        out_specs=_row_specs(out_parts, tm),
        out_shape=out_parts,
        compiler_params=_cparams(("arbitrary",)),
        name="out_proj_residual",
    )(mix, w, *x_parts, mod)


def _rope(x, cos_t, sin_t):
    return x * cos_t + pltpu.roll(x, LANES // 2, 1) * sin_t


def _row_sums(x):
    hi = x.astype(BF16)
    lo = (x - hi.astype(F32)).astype(BF16)
    return _dot(jnp.concatenate([hi, lo], axis=1), jnp.ones((2 * x.shape[1], LANES), BF16))


def _amid_kernel(*refs, do_q, norm_kv):
    if do_q:
        (ql_ref, kvl_ref, kpe_ref, wq_ref, wk_ref, wv_ref, qlg_ref, kvlg_ref, qng_ref, kng_ref, cos_ref, sin_ref,
         q_out, k_out, v_out, ckv_out, kpe_out) = refs
    else:
        (kvl_ref, kpe_ref, wk_ref, wv_ref, kvlg_ref, kng_ref, cos_ref, sin_ref, k_out, v_out) = refs
    cos_t = cos_ref[...]
    sin_t = sin_ref[...]
    inv_qk = 1.0 / A_QK

    if do_q:
        ql = ql_ref[...]
        qn = ql * lax.rsqrt(jnp.mean(ql * ql, axis=-1, keepdims=True) + EPS) * qlg_ref[...]
        qu = _dot(qn.astype(BF16), wq_ref[...])
        g_nope = qng_ref[:, :A_NOPE]
        g_rope = qng_ref[:, A_NOPE:]
        for h in range(A_HEADS):
            qh = qu[:, h * A_QK_PAD:(h + 1) * A_QK_PAD]
            r = lax.rsqrt(_row_sums(qh * qh) * inv_qk + EPS)
            q_out[:, h * A_QK_PAD:h * A_QK_PAD + A_NOPE] = (qh[:, :A_NOPE] * r * g_nope).astype(BF16)
            q_out[:, h * A_QK_PAD + A_NOPE:(h + 1) * A_QK_PAD] = _rope(qh[:, A_NOPE:] * r * g_rope,
                                                                       cos_t, sin_t).astype(BF16)

    kvl = kvl_ref[...]
    if norm_kv:
        ckv = kvl * lax.rsqrt(jnp.mean(kvl * kvl, axis=-1, keepdims=True) + EPS) * kvlg_ref[...]
    else:
        ckv = kvl
    ckv_b = ckv.astype(BF16)
    kn = _dot(ckv_b, wk_ref[...])
    v_out[...] = _dot(ckv_b, wv_ref[...]).astype(BF16)
    kpe = kpe_ref[...]
    if do_q:
        @pl.when(pl.program_id(0) < CTX_ROWS // TM_MID)
        def _():
            half = A_ROPE // 2
            ckv_out[...] = ckv.reshape(ckv_out.shape)
            kpe_out[...] = jnp.concatenate([kpe[:, :half], kpe[:, LANES // 2:LANES // 2 + half]],
                                           axis=1).reshape(kpe_out.shape)
    sp = _row_sums(kpe * kpe)
    g_nope = kng_ref[:, :A_NOPE]
    g_rope = kng_ref[:, A_NOPE:]
    for h in range(A_HEADS):
        a = kn[:, h * A_NOPE:(h + 1) * A_NOPE]
        r = lax.rsqrt((_row_sums(a * a) + sp) * inv_qk + EPS)
        k_out[:, h * A_QK_PAD:h * A_QK_PAD + A_NOPE] = (a * r * g_nope).astype(BF16)
        k_out[:, h * A_QK_PAD + A_NOPE:(h + 1) * A_QK_PAD] = _rope(kpe * r * g_rope, cos_t, sin_t).astype(BF16)


def _layer_slab_spec(tm, layer, width):
    n_ctx = CTX_ROWS // tm
    return pl.BlockSpec((tm // SEQ, None, SEQ, width), lambda i: (jnp.minimum(i, n_ctx - 1), layer, 0, 0))


def _call_with_carried(kernel, n_plain_out, carried, **kw):
    def run(*args, in_specs):
        args = list(args)
        in_specs = list(in_specs)
        n_real = len(args)
        aliases = {}
        for idx, arr in enumerate(carried):
            if arr is not None:
                aliases[len(args)] = n_plain_out + idx
                args.append(arr)
                in_specs.append(pl.BlockSpec(memory_space=pl.ANY))
        n_in = len(args)

        def body(*refs):
            kernel(*refs[:n_real], *refs[n_in:])

        return pl.pallas_call(body, in_specs=in_specs, input_output_aliases=aliases, **kw)(*args)
    return run


def _amid_tokens(y, wq, wk, wv, qlg, kvlg, qng, kng, cos_t, sin_t, layer, ckv_new, kpe_new):
    tm = TM_MID
    const = lambda i: (0, 0)
    rope_map = lambda i: (_rope_block_of_tile(i, tm), 0)
    n_ab = (DEPTH + 1) // 2
    run = _call_with_carried(
        functools.partial(_amid_kernel, do_q=True, norm_kv=True), 3, [ckv_new, kpe_new],
        grid=(ROWS // tm,),
        out_specs=[pl.BlockSpec((tm, A_HEADS * A_QK_PAD), lambda i: (i, 0)),
                   pl.BlockSpec((tm, A_HEADS * A_QK_PAD), lambda i: (i, 0)),
                   pl.BlockSpec((tm, A_HEADS * A_VDIM), lambda i: (i, 0)),
                   _layer_slab_spec(tm, layer, KV_LORA), _layer_slab_spec(tm, layer, A_ROPE)],
        out_shape=[jax.ShapeDtypeStruct((ROWS, A_HEADS * A_QK_PAD), BF16),
                   jax.ShapeDtypeStruct((ROWS, A_HEADS * A_QK_PAD), BF16),
                   jax.ShapeDtypeStruct((ROWS, A_HEADS * A_VDIM), BF16),
                   jax.ShapeDtypeStruct((BATCH, n_ab, SEQ, KV_LORA), F32),
                   jax.ShapeDtypeStruct((BATCH, n_ab, SEQ, A_ROPE), F32)],
        compiler_params=_cparams(("arbitrary",)),
        name="mla_qkv_prep")
    return run(y, y, y, wq, wk, wv, qlg, kvlg, qng, kng, cos_t, sin_t,
               in_specs=[pl.BlockSpec((tm, Q_LORA), lambda i: (i, AB_QLAT // Q_LORA)),
                         pl.BlockSpec((tm, KV_LORA), lambda i: (i, AB_KVLAT // KV_LORA)),
                         pl.BlockSpec((tm, LANES), lambda i: (i, AB_KPE // LANES)),
                         pl.BlockSpec(wq.shape, const), pl.BlockSpec(wk.shape, const), pl.BlockSpec(wv.shape, const),
                         pl.BlockSpec(qlg.shape, const), pl.BlockSpec(kvlg.shape, const),
                         pl.BlockSpec(qng.shape, const), pl.BlockSpec(kng.shape, const),
                         pl.BlockSpec((tm, LANES), rope_map), pl.BlockSpec((tm, LANES), rope_map)])


def _amid_cache(ckv, kpe, wk, wv, kvlg, kng, cos_t, sin_t):
    rows = ckv.shape[0]
    tm = TM_MID
    const = lambda i: (0, 0)
    outs = pl.pallas_call(
        functools.partial(_amid_kernel, do_q=False, norm_kv=False),
        grid=(rows // tm,),
        in_specs=[pl.BlockSpec((tm, KV_LORA), lambda i: (i, 0)),
                  pl.BlockSpec((tm, LANES), lambda i: (i, 0)),
                  pl.BlockSpec(wk.shape, const), pl.BlockSpec(wv.shape, const),
                  pl.BlockSpec(kvlg.shape, const), pl.BlockSpec(kng.shape, const),
                  pl.BlockSpec((tm, LANES), const), pl.BlockSpec((tm, LANES), const)],
        out_specs=[pl.BlockSpec((tm, A_HEADS * A_QK_PAD), lambda i: (i, 0)),
                   pl.BlockSpec((tm, A_HEADS * A_VDIM), lambda i: (i, 0))],
        out_shape=[jax.ShapeDtypeStruct((rows, A_HEADS * A_QK_PAD), BF16),
                   jax.ShapeDtypeStruct((rows, A_HEADS * A_VDIM), BF16)],
        compiler_params=_cparams(("parallel",)),
        name="mla_cache_kv_prep",
    )(ckv, kpe, wk, wv, kvlg, kng, cos_t, sin_t)
    return outs[0], outs[1]


def _scores(q, srcs, c):
    zs = []
    for k, _, keep in srcs:
        z = _dot_nt(q, k) * c
        zs.append(z if keep is None else jnp.where(keep, z, NEG_BIG))
    return zs


def _softmax_pv(zs, srcs, sink_z=None):
    tile_max = None
    for z in zs:
        for j in range(z.shape[1] // LANES):
            blk = z[:, j * LANES:(j + 1) * LANES]
            tile_max = blk if tile_max is None else jnp.maximum(tile_max, blk)
    m = tile_max.max(axis=-1, keepdims=True)
    if sink_z is not None:
        m = jnp.maximum(m, sink_z)
    acc = None
    for z, (_, v, _) in zip(zs, srcs):
        o = _dot(jnp.exp2(z - m).astype(BF16), jnp.concatenate([v, jnp.ones_like(v)], axis=1))
        acc = o if acc is None else acc + o
    dv = acc.shape[1] // 2
    den = acc[:, dv:]
    if sink_z is not None:
        den = den + jnp.exp2(sink_z - m)
    return acc[:, :dv] / den


def _attend_streams(n, q_of, srcs_of, c, sink_of=None):
    outs = []
    zs = _scores(q_of(0), srcs_of(0), c)
    for t in range(n):
        nxt = _scores(q_of(t + 1), srcs_of(t + 1), c) if t + 1 < n else None
        outs.append(_softmax_pv(zs, srcs_of(t), None if sink_of is None else sink_of(t)))
        zs = nxt
    return outs


def _attn_a_kernel(*refs, n_src, heads):
    q_ref = refs[0]
    k_refs = refs[1:1 + n_src]
    v_refs = refs[1 + n_src:1 + 2 * n_src]
    gate_ref, o_ref = refs[1 + 2 * n_src:]
    sub = min(ATTN_SUB, q_ref.shape[0])
    streams = [(h, r) for h in range(heads) for r in range(q_ref.shape[0] // sub)]

    def rows(t):
        return slice(streams[t][1] * sub, (streams[t][1] + 1) * sub)

    def q_of(t):
        h = streams[t][0]
        return q_ref[rows(t), h * A_QK_PAD:(h + 1) * A_QK_PAD]

    def srcs_of(t):
        h = streams[t][0]
        return [(k_ref[:, h * A_QK_PAD:(h + 1) * A_QK_PAD], v_ref[:, h * A_VDIM:(h + 1) * A_VDIM], None)
                for k_ref, v_ref in zip(k_refs, v_refs)]

    outs = _attend_streams(len(streams), q_of, srcs_of, A_QK ** -0.5 * LOG2_E)
    for t, o in enumerate(outs):
        cols = slice(streams[t][0] * A_VDIM, (streams[t][0] + 1) * A_VDIM)
        o_ref[rows(t), cols] = (o * _silu(gate_ref[rows(t), cols])).astype(o_ref.dtype)


def _attn_a(q, k, v, y, mix, kc=None, vc=None):
    latent = kc is not None
    if latent:
        nb, t, tq, hp, row0 = DEC_BATCH, DEC_SEQ, TQ_A, 1, CTX_ROWS
    else:
        nb, t, tq, hp, row0 = BATCH, SEQ, SEQ, A_HEADS, 0
    nq = t // tq
    qrow = lambda b, h, i: row0 // tq + b * nq + i
    in_specs = [pl.BlockSpec((tq, hp * A_QK_PAD), lambda b, h, i: (qrow(b, h, i), h))]
    args = [q]
    if latent:
        in_specs.append(pl.BlockSpec((PAST_LEN, hp * A_QK_PAD), lambda b, h, i: (b, h)))
        args.append(kc)
    in_specs.append(pl.BlockSpec((t, hp * A_QK_PAD), lambda b, h, i: (row0 // t + b, h)))
    args.append(k)
    if latent:
        in_specs.append(pl.BlockSpec((PAST_LEN, hp * A_VDIM), lambda b, h, i: (b, h)))
        args.append(vc)
    in_specs.append(pl.BlockSpec((t, hp * A_VDIM), lambda b, h, i: (row0 // t + b, h)))
    args.append(v)
    in_specs.append(pl.BlockSpec((tq, hp * A_VDIM), lambda b, h, i: (qrow(b, h, i), AB_AGATE // (hp * A_VDIM) + h)))
    args.append(y)
    run = _call_with_carried(
        functools.partial(_attn_a_kernel, n_src=2 if latent else 1, heads=hp), 0, [mix],
        grid=(nb, A_HEADS // hp, nq),
        out_specs=[pl.BlockSpec((tq, hp * A_VDIM), lambda b, h, i: (qrow(b, h, i), h))],
        out_shape=[_MIX_SHAPE],
        compiler_params=_cparams(("parallel", "parallel", "parallel")),
        name="mla_attention_latent" if latent else "mla_attention_context")
    return run(*args, in_specs=in_specs)[0]


def _hgrn_constants():
    c, nl = HG_CHUNK, HG_LEVELS
    t = np.arange(c)[:, None]
    u = np.arange(c)[None, :]
    tri_f = (u <= t).astype(np.float32)
    mask_f = np.zeros((nl, c, c), np.float32)
    coef_f = np.zeros((nl, c, LANES), np.float32)
    for l in range(nl):
        half = c >> (l + 1)
        seg = 2 * half
        mask_f[l] = ((u // seg) == (t // seg)) & ((t % seg) >= half) & ((u % seg) < half)
        later = np.broadcast_to((t % seg) >= half, (c, LANES))
        coef_f[l] = np.where(later, 1.0, -1.0 if half > 1 else 0.0)
    tri_b = tri_f[::-1, ::-1]
    mask_b = mask_f[:, ::-1, ::-1]
    coef_b = coef_f[:, ::-1, :]
    to_tri = lambda a: jnp.asarray(np.concatenate([a, a, a], axis=1), BF16)
    to_f32 = lambda a: jnp.asarray(np.ascontiguousarray(a), F32)
    return (to_tri(tri_f), to_tri(tri_b)), (to_f32(mask_f), to_f32(mask_b)), (to_f32(coef_f), to_f32(coef_b))


def _hgrn_decays(x, lb, tri3):
    kk = (1.0 - lb) * _sigmoid(-x)
    lf = jnp.log2(1.0 - kk)
    hi = lf.astype(BF16)
    r1 = lf - hi.astype(F32)
    mid = r1.astype(BF16)
    lo = (r1 - mid.astype(F32)).astype(BF16)
    return kk, lf, _dot(tri3, jnp.concatenate([hi, mid, lo], axis=0))


def _hgrn_level_arg(l, cs, lf, coef, forward):
    c = HG_CHUNK
    nv = c // 8
    half = c >> (l + 1)
    if half == 1:
        return lf * coef
    cs3 = cs.reshape(nv, 8, LANES)

    def in_vreg_row(r):
        return jnp.broadcast_to(cs3[:, r:r + 1, :], (nv, 8, LANES))

    if half >= 8:
        m = half // 8
        nseg = nv // (2 * m)
        edge = cs3[:, 7:8, :] if forward else cs3[:, 0:1, :]
        e4 = edge.reshape(nseg, 2 * m, 1, LANES)
        a = e4[:, m - 1:m] if forward else e4[:, m:m + 1]
        anchor = jnp.broadcast_to(a, (nseg, 2 * m, 8, LANES)).reshape(c, LANES)
    elif half == 4:
        anchor = in_vreg_row(3 if forward else 4).reshape(c, LANES)
    else:
        r0, r1 = (1, 5) if forward else (2, 6)
        sub = lax.broadcasted_iota(jnp.int32, (nv, 8, LANES), 1)
        anchor = jnp.where(sub < 4, in_vreg_row(r0), in_vreg_row(r1)).reshape(c, LANES)
    return (cs - anchor) * coef


def _hgrn_kernel(*refs, t_len, zero_init, emit_state):
    c = HG_CHUNK
    n_chunks = t_len // c
    it = iter(refs)
    bq_ref, ff_ref, fb_ref, vi_ref, bg_ref, lb_ref, hg_ref = (next(it) for _ in range(7))
    if not zero_init:
        s0f_ref, s0b_ref = next(it), next(it)
    trif_ref, trib_ref, maskf_ref, maskb_ref, coeff_ref, coefb_ref = (next(it) for _ in range(6))
    o_ref = next(it)
    if emit_state:
        sf_ref, sb_ref = next(it), next(it)
    of_ref, ob_ref, stf_ref, stb_ref = (next(it) for _ in range(4))

    if zero_init:
        stf_ref[...] = jnp.zeros((B_DV, B_DK), F32)
        stb_ref[...] = jnp.zeros((B_DV, B_DK), F32)
    else:
        stf_ref[...] = s0f_ref[...].T
        stb_ref[...] = s0b_ref[...].T
    lb = lb_ref[...]

    nl = HG_LEVELS
    unroll = min(HG_UNROLL, n_chunks)
    dirs = ((ff_ref, lb[0:1, :], trif_ref, maskf_ref, coeff_ref, stf_ref, of_ref, True),
            (fb_ref, lb[1:2, :], trib_ref, maskb_ref, coefb_ref, stb_ref, ob_ref, False))

    def body(i, carry):
        chains = []
        for f_ref, lb_row, tri_ref, mask_ref, coef_ref, st_ref, out_ref, forward in dirs:
            for u in range(unroll):
                k = i * unroll + u
                r0 = pl.multiple_of((k if forward else n_chunks - 1 - k) * c, c)
                chains.append(dict(rows=pl.ds(r0, c), f_ref=f_ref, lb=lb_row, tri_ref=tri_ref, mask_ref=mask_ref,
                                   coef_ref=coef_ref, st_ref=st_ref, out_ref=out_ref, forward=forward))
        for ch in chains:
            ch["kk"], ch["lf"], ch["cs"] = _hgrn_decays(ch["f_ref"][ch["rows"], :], ch["lb"], ch["tri_ref"][...])
            ch["q"] = _silu(bq_ref[ch["rows"], :])
            ch["v"] = vi_ref[ch["rows"], :]
            ch["q16"] = ch["q"].astype(BF16)
            ch["kk16"] = ch["kk"].astype(BF16)
            ch["sc"] = jnp.zeros((c, c), F32)
        for l in range(nl):
            for ch in chains:
                el = jnp.exp2(_hgrn_level_arg(l, ch["cs"], ch["lf"], ch["coef_ref"][l], ch["forward"])).astype(BF16)
                ch["sc"] = ch["sc"] + ch["mask_ref"][l] * _dot_nt(ch["q16"] * el, ch["kk16"] * el)
        for ch in chains:
            cs, q, kk, v = ch["cs"], ch["q"], ch["kk"], ch["v"]
            end = cs[c - 1:c, :] if ch["forward"] else cs[0:1, :]
            ch["end"] = end
            ch["q_in"] = (q * jnp.exp2(cs)).astype(BF16)
            ch["upd"] = _dot(v.T.astype(BF16), (kk * jnp.exp2(end - cs)).astype(BF16))
            ch["o"] = _dot(ch["sc"].astype(BF16), v.astype(BF16)) + jnp.sum(q * kk, axis=-1, keepdims=True) * v
        for ch in chains:
            st = ch["st_ref"][...]
            ch["out_ref"][ch["rows"], :] = ch["o"] + _dot_nt(ch["q_in"], st.astype(BF16))
            ch["st_ref"][...] = jnp.exp2(ch["end"]) * st + ch["upd"]
        return carry

    lax.fori_loop(0, n_chunks // unroll, body, 0)
    if emit_state:
        sf_ref[...] = stf_ref[...].T
        sb_ref[...] = stb_ref[...].T

    hg = hg_ref[...]

    blk = unroll * c

    def finish(i, carry):
        rows = pl.ds(pl.multiple_of(i * blk, blk), blk)
        o = of_ref[rows, :] + ob_ref[rows, :]
        o = o * lax.rsqrt(jnp.mean(o * o, axis=-1, keepdims=True) + EPS) * hg
        o_ref[rows, :] = (o * _silu(bg_ref[rows, :])).astype(o_ref.dtype)
        return carry

    lax.fori_loop(0, t_len // blk, finish, 0)


def _hgrn(y, lb, hg, mix, consts, s0f=None, s0b=None, layer=None, sf_new=None, sb_new=None):
    latent = s0f is not None
    if latent:
        nb, t, row0 = DEC_BATCH, DEC_SEQ, CTX_ROWS
    else:
        nb, t, row0 = BATCH, SEQ, 0
    rb = lambda b: row0 // t + b
    col = lambda off: (lambda b, h: (rb(b), off // LANES + h))
    const2 = lambda b, h: (0, 0)
    const3 = lambda b, h: (0, 0, 0)
    in_specs = [pl.BlockSpec((t, LANES), col(AB_BQ)), pl.BlockSpec((t, LANES), col(AB_BFF)),
                pl.BlockSpec((t, LANES), col(AB_BFB)), pl.BlockSpec((t, LANES), col(AB_BI)),
                pl.BlockSpec((t, LANES), col(AB_BGATE)),
                pl.BlockSpec((2, LANES), lambda b, h: (0, h)),
                pl.BlockSpec((1, LANES), const2)]
    args = [y, y, y, y, y, lb, hg]
    if latent:
        st_spec = pl.BlockSpec((None, None, B_DK, B_DV), lambda b, h: (b, h, 0, 0))
        in_specs += [st_spec, st_spec]
        args += [s0f, s0b]
    tris, masks, coefs = consts
    in_specs += ([pl.BlockSpec(a.shape, const2) for a in tris] + [pl.BlockSpec(a.shape, const3) for a in masks]
                 + [pl.BlockSpec(a.shape, const3) for a in coefs])
    args += [*tris, *masks, *coefs]
    out_specs = [pl.BlockSpec((t, LANES), lambda b, h: (rb(b), B_HEADS + h))]
    out_shape = [_MIX_SHAPE]
    carried = [mix]
    if not latent:
        st_out = pl.BlockSpec((None, None, None, B_DK, B_DV), lambda b, h: (b, layer, h, 0, 0))
        out_specs += [st_out, st_out]
        out_shape += [jax.ShapeDtypeStruct((nb, (DEPTH + 1) // 2, B_HEADS, B_DK, B_DV), F32)] * 2
        carried += [sf_new, sb_new]
    run = _call_with_carried(
        functools.partial(_hgrn_kernel, t_len=t, zero_init=not latent, emit_state=not latent), 0, carried,
        grid=(nb, B_HEADS),
        out_specs=out_specs,
        out_shape=out_shape,
        scratch_shapes=[pltpu.VMEM((t, B_DV), F32), pltpu.VMEM((t, B_DV), F32),
                        pltpu.VMEM((B_DV, B_DK), F32), pltpu.VMEM((B_DV, B_DK), F32)],
        compiler_params=_cparams(("parallel", "parallel")),
        name="hgrn2_latent" if latent else "hgrn2_context")
    return run(*args, in_specs=in_specs)


def _cmid_kernel(q_ref, k_ref, v_ref, qg_ref, kg_ref, cos_ref, sin_ref, q_out, k_out, v_out, kc_out, vc_out):
    cos_t = cos_ref[...]
    sin_t = sin_ref[...]
    qg = qg_ref[...]
    kg = kg_ref[...]
    in_ctx = pl.program_id(0) < CTX_ROWS // TM_MID

    def norm(x, g):
        return x * lax.rsqrt(_row_sums(x * x) * (1.0 / C_HEAD_DIM) + EPS) * g

    def rope(x):
        return _rope(x, cos_t, sin_t)

    for h in range(C_HEADS):
        sl = slice(h * C_HEAD_DIM, (h + 1) * C_HEAD_DIM)
        q_out[:, sl] = rope(norm(q_ref[:, sl], qg)).astype(BF16)
    for h in range(C_KV_HEADS):
        sl = slice(h * C_HEAD_DIM, (h + 1) * C_HEAD_DIM)
        kn = norm(k_ref[:, sl], kg)
        k_out[:, sl] = rope(kn).astype(BF16)

        @pl.when(in_ctx)
        def _():
            kc_out[:, :, sl] = kn.reshape(kc_out.shape[0], SEQ, C_HEAD_DIM)
    v = v_ref[...]
    v_out[...] = v.astype(BF16)

    @pl.when(in_ctx)
    def _():
        vc_out[...] = v.reshape(vc_out.shape)


def _cmid(y, qg, kg, cos_t, sin_t, layer, kc_new, vc_new):
    tm = TM_MID
    const = lambda i: (0, 0)
    rope_map = lambda i: (_rope_block_of_tile(i, tm), 0)
    cache = jax.ShapeDtypeStruct((BATCH, DEPTH // 2, SEQ, C_KV_WIDTH), F32)
    run = _call_with_carried(
        _cmid_kernel, 3, [kc_new, vc_new],
        grid=(ROWS // tm,),
        out_specs=[pl.BlockSpec((tm, C_WIDTH), lambda i: (i, 0)),
                   pl.BlockSpec((tm, C_KV_WIDTH), lambda i: (i, 0)),
                   pl.BlockSpec((tm, C_KV_WIDTH), lambda i: (i, 0)),
                   _layer_slab_spec(tm, layer, C_KV_WIDTH), _layer_slab_spec(tm, layer, C_KV_WIDTH)],
        out_shape=[jax.ShapeDtypeStruct((ROWS, C_WIDTH), BF16),
                   jax.ShapeDtypeStruct((ROWS, C_KV_WIDTH), BF16),
                   jax.ShapeDtypeStruct((ROWS, C_KV_WIDTH), BF16), cache, cache],
        compiler_params=_cparams(("arbitrary",)),
        name="gqa_qkv_prep")
    return run(y, y, y, qg, kg, cos_t, sin_t,
               in_specs=[pl.BlockSpec((tm, C_WIDTH), lambda i: (i, C_Q // C_WIDTH)),
                         pl.BlockSpec((tm, C_KV_WIDTH), lambda i: (i, C_K // C_KV_WIDTH)),
                         pl.BlockSpec((tm, C_KV_WIDTH), lambda i: (i, C_V // C_KV_WIDTH)),
                         pl.BlockSpec((1, C_HEAD_DIM), const), pl.BlockSpec((1, C_HEAD_DIM), const),
                         pl.BlockSpec((tm, LANES), rope_map), pl.BlockSpec((tm, LANES), rope_map)])


def _attn_c_kernel(*refs, band, tq, t_len, groups, stack):
    if band:
        q_ref, kc_ref, vc_ref, kl_ref, vl_ref, sink_ref, gate_ref, o_ref = refs
    else:
        q_ref, kc_ref, vc_ref, sink_ref, gate_ref, o_ref = refs
    hd = C_HEAD_DIM
    rows = stack * tq
    streams = [(g, s) for g in range(groups) for s in range(C_GROUP // stack)]
    if band:
        i = pl.program_id(2)
        width = tq + 2 * WINDOW
        start = pl.multiple_of(jnp.clip(i * tq - WINDOW, 0, t_len - width), WINDOW)
        qpos = i * tq + (lax.broadcasted_iota(jnp.int32, (rows, width), 0) & (tq - 1))
        kpos = start + lax.broadcasted_iota(jnp.int32, (rows, width), 1)
        in_band = jnp.abs(kpos - qpos) <= WINDOW

    def heads_of(t):
        g, s = streams[t]
        return [g * C_GROUP + s * stack + r for r in range(stack)]

    def q_of(t):
        return jnp.concatenate([q_ref[:, h * hd:(h + 1) * hd] for h in heads_of(t)], axis=0)

    def srcs_of(t):
        g = streams[t][0]
        cols = slice(g * hd, (g + 1) * hd)
        srcs = [(kc_ref[:, cols], vc_ref[:, cols], None)]
        if band:
            srcs.append((kl_ref[pl.ds(start, width), cols], vl_ref[pl.ds(start, width), cols], in_band))
        return srcs

    def sink_of(t):
        return jnp.concatenate([jnp.broadcast_to(sink_ref[h][:, :1] * LOG2_E, (tq, 1)) for h in heads_of(t)], axis=0)

    outs = _attend_streams(len(streams), q_of, srcs_of, hd ** -0.5 * LOG2_E, sink_of)
    for t, o in enumerate(outs):
        for r, h in enumerate(heads_of(t)):
            cols = slice(h * hd, (h + 1) * hd)
            o_ref[:, cols] = (o[r * tq:(r + 1) * tq] * _silu(gate_ref[:, cols])).astype(o_ref.dtype)


def _attn_c(q, k, v, y, sink, mix, kc=None, vc=None):
    latent = kc is not None
    if latent:
        nb, t, tq, gp, stack, row0 = DEC_BATCH, DEC_SEQ, TQ_C, 1, 2, CTX_ROWS
    else:
        nb, t, tq, gp, stack, row0 = BATCH, SEQ, SEQ, C_KV_HEADS, C_GROUP, 0
    gw = gp * C_GROUP * C_HEAD_DIM
    kvw = gp * C_HEAD_DIM
    nq = t // tq
    qrow = lambda b, g, i: row0 // tq + b * nq + i
    own_kv = pl.BlockSpec((t, kvw), lambda b, g, i: (row0 // t + b, g))
    in_specs = [pl.BlockSpec((tq, gw), lambda b, g, i: (qrow(b, g, i), g))]
    args = [q]
    if latent:
        ctx_kv = pl.BlockSpec((PAST_LEN, kvw), lambda b, g, i: (b, g))
        in_specs += [ctx_kv, ctx_kv, own_kv, own_kv]
        args += [kc, vc, k, v]
    else:
        in_specs += [own_kv, own_kv]
        args += [k, v]
    in_specs.append(pl.BlockSpec((gp * C_GROUP, 1, LANES), lambda b, g, i: (g, 0, 0)))
    args.append(sink)
    in_specs.append(pl.BlockSpec((tq, gw), lambda b, g, i: (qrow(b, g, i), C_GATE // gw + g)))
    args.append(y)
    run = _call_with_carried(
        functools.partial(_attn_c_kernel, band=latent, tq=tq, t_len=t, groups=gp, stack=stack), 0, [mix],
        grid=(nb, C_KV_HEADS // gp, nq),
        out_specs=[pl.BlockSpec((tq, gw), lambda b, g, i: (qrow(b, g, i), g))],
        out_shape=[_MIX_SHAPE],
        compiler_params=_cparams(("parallel", "parallel", "parallel")),
        name="gqa_attention_latent" if latent else "gqa_attention_context")
    return run(*args, in_specs=in_specs)[0]


def _axial_angles(n_tokens, rot_dim):
    rows = n_tokens // GRID_W
    row = jnp.repeat(jnp.arange(rows, dtype=F32), GRID_W)
    col = jnp.tile(jnp.arange(GRID_W, dtype=F32), rows)
    n_freq = rot_dim // 4
    inv = ROPE_BASE ** (-jnp.arange(n_freq, dtype=F32) / n_freq)
    return jnp.concatenate([row[:, None] * inv, col[:, None] * inv], axis=-1)


def _spread_halves(a, fill=0.0):
    h = a.shape[-1] // 2
    pad = jnp.full(a.shape[:-1] + (LANES // 2 - h,), fill, a.dtype)
    return jnp.concatenate([a[..., :h], pad, a[..., h:], pad], axis=-1)


def _spread_head(a):
    return jnp.concatenate([a[..., :A_NOPE], _spread_halves(a[..., A_NOPE:])], axis=-1)


def _rope_tables(rot_dim, tm):
    ang = _axial_angles(DEC_SEQ, rot_dim)
    cos, sin = jnp.cos(ang), jnp.sin(ang)
    cos_t = _spread_halves(jnp.concatenate([cos, cos], axis=-1), 1.0)
    sin_t = _spread_halves(jnp.concatenate([-sin, sin], axis=-1))
    cos_t = jnp.concatenate([jnp.ones((tm, LANES), F32), cos_t], axis=0)
    sin_t = jnp.concatenate([jnp.zeros((tm, LANES), F32), sin_t], axis=0)
    return cos_t, sin_t


def _lower_bounds(lb_logits):
    p = jax.nn.softmax(lb_logits.astype(F32), axis=0)
    return jnp.cumsum(p, axis=0) - p[0:1]


def kernel(x_prompt, x_sample, cache_ckv, cache_kpe, state_hgrn_fwd, state_hgrn_bwd, cache_k_c, cache_v_c, c, c_ctx,
           mod_w_ab, mod_b_ab, norm_ab, w_in_ab, q_lora_norm, kv_lora_norm, w_q_up, w_kv_up, q_norm_ab, k_norm_ab,
           hgrn_lb_logits, hgrn_out_norm, w_out_ab, mod_w_c, mod_b_c, norm_c, w_in_c, q_norm_c, k_norm_c, sink_c,
           w_out_c):
    x_parts = [x_prompt.reshape(CTX_ROWS, D_MODEL), x_sample.reshape(LAT_ROWS, D_MODEL)]
    cond8 = jnp.concatenate([c_ctx[None, :], c, jnp.zeros((N_COND - 1 - DEC_BATCH, D_MODEL), F32)], axis=0)
    mods_ab = _modulation(cond8, mod_w_ab, mod_b_ab)
    mods_c = _modulation(cond8, mod_w_c, mod_b_c)
    lower = _lower_bounds(hgrn_lb_logits)
    cos_a, sin_a = _rope_tables(A_ROPE, TM_MID)
    cos_c, sin_c = _rope_tables(C_HEAD_DIM, TM_MID)
    hg_consts = _hgrn_constants()
    w_in_ab16 = _prep_w_in_ab(jnp.swapaxes(w_in_ab, 1, 2))
    w_in_c16, w_out_ab16, w_out_c16 = w_in_c.astype(BF16), w_out_ab.astype(BF16), w_out_c.astype(BF16)

    ckv_new = kpe_new = sf_new = sb_new = kc_new = vc_new = None
    for layer in range(DEPTH):
        j = layer // 2
        last = layer == DEPTH - 1
        if layer % 2 == 0:
            mod = mods_ab[j].reshape(3 * N_COND, 1, D_MODEL)
            wq = _spread_head(w_q_up[j].reshape(Q_LORA, A_HEADS, A_QK)).reshape(Q_LORA, A_HEADS * A_QK_PAD).astype(BF16)
            wkv = w_kv_up[j].reshape(KV_LORA, A_HEADS, A_NOPE + A_VDIM)
            wk = wkv[:, :, :A_NOPE].reshape(KV_LORA, A_HEADS * A_NOPE).astype(BF16)
            wv = wkv[:, :, A_NOPE:].reshape(KV_LORA, A_HEADS * A_VDIM).astype(BF16)
            qlg, kvlg = q_lora_norm[j][None, :], kv_lora_norm[j][None, :]
            qng, kng = _spread_head(q_norm_ab[j])[None, :], _spread_head(k_norm_ab[j])[None, :]

            y = _in_proj(x_parts, norm_ab[j][None, :], mod, w_in_ab16, j, w_transposed=True)
            q, k, v, ckv_new, kpe_new = _amid_tokens(y, wq, wk, wv, qlg, kvlg, qng, kng, cos_a, sin_a,
                                                     j, ckv_new, kpe_new)
            kpe_cache = _spread_halves(cache_kpe[:, j].reshape(DEC_BATCH * PAST_LEN, A_ROPE))
            kc, vc = _amid_cache(cache_ckv[:, j].reshape(DEC_BATCH * PAST_LEN, KV_LORA), kpe_cache, wk, wv, kvlg, kng,
                                 cos_a, sin_a)
            mix = _attn_a(q, k, v, y, None)
            mix = _attn_a(q, k, v, y, mix, kc=kc, vc=vc)
            hg = hgrn_out_norm[j][None, :]
            mix, sf_new, sb_new = _hgrn(y, lower[j], hg, mix, hg_consts, layer=j, sf_new=sf_new, sb_new=sb_new)
            (mix,) = _hgrn(y, lower[j], hg, mix, hg_consts, s0f=state_hgrn_fwd[:, j], s0b=state_hgrn_bwd[:, j])
            x_parts = _out_proj(mix, w_out_ab16, j, x_parts, mod, split_out=last)
        else:
            mod = mods_c[j].reshape(3 * N_COND, 1, D_MODEL)
            y = _in_proj(x_parts, norm_c[j][None, :], mod, w_in_c16, j,
                         col_map=lambda t: jnp.where(t < 2, t, jnp.where(t < 4, t + 1, 2)))
            q, k, v, kc_new, vc_new = _cmid(y, q_norm_c[j][None, :], k_norm_c[j][None, :], cos_c, sin_c,
                                            j, kc_new, vc_new)
            kc = cache_k_c[:, j].reshape(DEC_BATCH * PAST_LEN, C_KV_WIDTH).astype(BF16)
            vc = cache_v_c[:, j].reshape(DEC_BATCH * PAST_LEN, C_KV_WIDTH).astype(BF16)
            sink = jnp.broadcast_to(sink_c[j][:, None, None], (C_HEADS, 1, LANES))
            mix = _attn_c(q, k, v, y, sink, None)
            mix = _attn_c(q, k, v, y, sink, mix, kc=kc, vc=vc)
            x_parts = _out_proj(mix, w_out_c16, j, x_parts, mod, split_out=last)

    cache_c_shape = (BATCH, DEPTH // 2, SEQ, C_KV_HEADS, C_HEAD_DIM)
    return (x_parts[0].reshape(BATCH, SEQ, D_MODEL), x_parts[1].reshape(DEC_BATCH, DEC_SEQ, D_MODEL),
            ckv_new, kpe_new, sf_new, sb_new, kc_new.reshape(cache_c_shape), vc_new.reshape(cache_c_shape))
```

```python
import functools

import numpy as np
import jax
import jax.numpy as jnp
from jax import lax
from jax.experimental import pallas as pl
from jax.experimental.pallas import tpu as pltpu

F32 = jnp.float32
BF16 = jnp.bfloat16

D_MODEL = 2048
BATCH = 16
SEQ = 256
DEPTH = 4
DEC_BATCH = 4
DEC_SEQ = 2048
PAST_LEN = 256
GRID_W = 64
A_HEADS = 8
A_NOPE = 128
A_ROPE = 64
A_VDIM = 128
A_QK = A_NOPE + A_ROPE
A_QK_PAD = 256
Q_LORA = 512
KV_LORA = 256
B_HEADS = 8
B_DK = 128
B_DV = 128
C_HEADS = 16
C_KV_HEADS = 4
C_GROUP = C_HEADS // C_KV_HEADS
C_HEAD_DIM = 128
C_WIDTH = C_HEADS * C_HEAD_DIM
C_KV_WIDTH = C_KV_HEADS * C_HEAD_DIM
WINDOW = 128
ROPE_BASE = 10000.0
EPS = 1e-6
NEG_BIG = -1e30
LOG2_E = 1.4426950408889634

LANES = 128
CTX_ROWS = BATCH * SEQ
LAT_ROWS = DEC_BATCH * DEC_SEQ
ROWS = CTX_ROWS + LAT_ROWS
N_COND = 8

AB_AGATE = 0
AB_BQ = 1024
AB_BFF = 2048
AB_BFB = 3072
AB_BI = 4096
AB_BGATE = 5120
AB_QLAT = 6144
AB_KVLAT = 6656
AB_KPE = 6912
AB_N = 7168
C_Q = 0
C_GATE = 2048
C_K = 4096
C_V = 4608
C_N = 5120

TM_PROJ = 1024
TN_IN = 1024
TM_OUT = 512
TM_MID = 512
TQ_A = 1024
TQ_C = 256
ATTN_SUB = 256
HG_CHUNK = 128
HG_LEVELS = 7
HG_UNROLL = 4
VMEM_LIMIT = 56 * 1024 * 1024
_MIX_SHAPE = jax.ShapeDtypeStruct((ROWS, D_MODEL), BF16)


def _cparams(sem):
    return pltpu.CompilerParams(dimension_semantics=sem, vmem_limit_bytes=VMEM_LIMIT)


def _sigmoid(x):
    return 0.5 * jnp.tanh(0.5 * x) + 0.5


def _silu(x):
    return x * _sigmoid(x)


def _dot(a, b):
    return jnp.dot(a, b, preferred_element_type=F32)


def _dot_nt(a, b):
    return lax.dot_general(a, b, (((1,), (1,)), ((), ())), preferred_element_type=F32)


def _cond_of_tile(i, tm):
    n_ctx = CTX_ROWS // tm
    per_batch = DEC_SEQ // tm
    return jnp.where(i < n_ctx, 0, 1 + (i - n_ctx) // per_batch)


def _rope_block_of_tile(i, tm):
    n_ctx = CTX_ROWS // tm
    per_batch = DEC_SEQ // tm
    return jnp.where(i < n_ctx, 0, 1 + (i - n_ctx) % per_batch)


def _mod_kernel(c_ref, w_ref, b_ref, o_ref):
    a = _silu(c_ref[...]).astype(BF16)
    o_ref[...] = _dot(a, w_ref[...].astype(BF16)) + b_ref[...]


def _modulation(cond8, w_mod, b_mod):
    n = w_mod.shape[0]
    tn = 1024
    return pl.pallas_call(
        _mod_kernel,
        grid=(n, 3 * D_MODEL // tn),
        in_specs=[pl.BlockSpec((N_COND, D_MODEL), lambda l, j: (0, 0)),
                  pl.BlockSpec((None, D_MODEL, tn), lambda l, j: (l, 0, j)),
                  pl.BlockSpec((None, 1, tn), lambda l, j: (l, 0, j))],
        out_specs=pl.BlockSpec((None, N_COND, tn), lambda l, j: (l, 0, j)),
        out_shape=jax.ShapeDtypeStruct((n, N_COND, 3 * D_MODEL), F32),
        compiler_params=_cparams(("parallel", "parallel")),
        name="adaln_mod",
    )(cond8, w_mod, b_mod.reshape(n, 1, 3 * D_MODEL))


def _prep_w_kernel(w_ref, o_ref):
    n_head = Q_LORA + KV_LORA + A_ROPE
    n_main = AB_N - TN_IN
    half = A_ROPE // 2
    x1_end = Q_LORA + KV_LORA + half
    x2_at = Q_LORA + KV_LORA + LANES // 2
    o_ref[:n_main, :] = w_ref[n_head:, :].astype(BF16)
    o_ref[n_main:, :] = jnp.zeros((TN_IN, o_ref.shape[1]), BF16)
    o_ref[n_main:n_main + x1_end, :] = w_ref[:x1_end, :].astype(BF16)
    o_ref[n_main + x2_at:n_main + x2_at + half, :] = w_ref[x1_end:n_head, :].astype(BF16)


def _prep_w_in_ab(w_t):
    n_layers, n, _ = w_t.shape
    tc = 256
    return pl.pallas_call(
        _prep_w_kernel,
        grid=(n_layers, D_MODEL // tc),
        in_specs=[pl.BlockSpec((None, n, tc), lambda l, i: (l, 0, i))],
        out_specs=pl.BlockSpec((None, AB_N, tc), lambda l, i: (l, 0, i)),
        out_shape=jax.ShapeDtypeStruct((n_layers, AB_N, D_MODEL), BF16),
        compiler_params=_cparams(("parallel", "parallel")),
        name="w_in_ab_layout",
    )(w_t)


def _row_specs(parts, tm, single_buffer=False, tile_of=lambda i, *_: i):
    if len(parts) == 1:
        return [pl.BlockSpec((tm, D_MODEL), lambda *g: (tile_of(*g), 0))]
    n_first = CTX_ROWS // tm
    mode = dict(pipeline_mode=pl.Buffered(1)) if single_buffer else {}
    return [pl.BlockSpec((tm, D_MODEL), lambda *g: (jnp.minimum(tile_of(*g), n_first - 1), 0), **mode),
            pl.BlockSpec((tm, D_MODEL), lambda *g: (jnp.maximum(tile_of(*g) - n_first, 0), 0), **mode)]


IN_SUB = 4


def _in_next_tile(i, j):
    return jnp.where(j == 0, i, jnp.minimum(i + 1, ROWS // TM_PROJ - 1))


def _in_kernel(*refs, n_x, w_transposed):
    x_refs = refs[:n_x]
    g_ref, sh_ref, sc_ref, w_ref, o_ref, h_ref, hn_ref = refs[n_x:]
    i, j = pl.program_id(0), pl.program_id(1)
    n_first = CTX_ROWS // TM_PROJ
    sub_rows = TM_PROJ // IN_SUB

    def normalise(dst_ref, r0):
        rows = pl.ds(r0, sub_rows)
        if n_x == 1:
            x = x_refs[0][rows, :]
        else:
            x = jnp.where(_in_next_tile(i, j) < n_first, x_refs[0][rows, :], x_refs[1][rows, :])
        r = lax.rsqrt(jnp.mean(x * x, axis=-1, keepdims=True) + EPS)
        h = (x * r * g_ref[...]) * (1.0 + sc_ref[...]) + sh_ref[...]
        dst_ref[rows, :] = h.astype(BF16)

    @pl.when((i == 0) & (j == 0))
    def _():
        for s in range(IN_SUB):
            normalise(h_ref, s * sub_rows)

    @pl.when((i > 0) & (j == 0))
    def _():
        h_ref[...] = hn_ref[...]

    normalise(hn_ref, pl.multiple_of((jnp.clip(j, 1, IN_SUB) - 1) * sub_rows, sub_rows))
    o_ref[...] = (_dot_nt if w_transposed else _dot)(h_ref[...], w_ref[...])


def _in_proj(x_parts, g, mod, w, layer, col_map=None, w_transposed=False):
    tm, tn = TM_PROJ, TN_IN
    n_tiles = (w.shape[1] if w_transposed else w.shape[2]) // tn
    assert n_tiles > IN_SUB
    col_map = col_map or (lambda t: t)
    if w_transposed:
        w_spec = pl.BlockSpec((None, tn, D_MODEL), lambda i, j: (layer, col_map(j), 0))
    else:
        w_spec = pl.BlockSpec((None, D_MODEL, tn), lambda i, j: (layer, 0, col_map(j)))
    mod_row = lambda part: (lambda i, j: (_cond_of_tile(_in_next_tile(i, j), tm) * 3 + part, 0, 0))
    return pl.pallas_call(
        functools.partial(_in_kernel, n_x=len(x_parts), w_transposed=w_transposed),
        grid=(ROWS // tm, n_tiles),
        in_specs=_row_specs(x_parts, tm, single_buffer=True, tile_of=_in_next_tile) + [
            pl.BlockSpec((1, D_MODEL), lambda i, j: (0, 0)),
            pl.BlockSpec((None, 1, D_MODEL), mod_row(0)),
            pl.BlockSpec((None, 1, D_MODEL), mod_row(1)), w_spec],
        out_specs=pl.BlockSpec((tm, tn), lambda i, j: (i, j)),
        out_shape=jax.ShapeDtypeStruct((ROWS, n_tiles * tn), F32),
        scratch_shapes=[pltpu.VMEM((tm, D_MODEL), BF16), pltpu.VMEM((tm, D_MODEL), BF16)],
        compiler_params=_cparams(("arbitrary", "arbitrary")),
        name="norm_mod_in_proj",
    )(*x_parts, g, mod, mod, w)


def _out_kernel(*refs, n_x, n_o):
    m_ref, w_ref = refs[:2]
    x_refs = refs[2:2 + n_x]
    gt_ref = refs[2 + n_x]
    o_refs = refs[3 + n_x:]
    y = gt_ref[...] * _dot(m_ref[...], w_ref[...])

    def emit(x_ref, o_ref):
        o_ref[...] = x_ref[...] + y

    if n_x == 1 and n_o == 1:
        emit(x_refs[0], o_refs[0])
    else:
        in_first = pl.program_id(0) < CTX_ROWS // TM_OUT
        pl.when(in_first)(lambda: emit(x_refs[0], o_refs[0]))
        pl.when(jnp.logical_not(in_first))(lambda: emit(x_refs[-1], o_refs[-1]))


def _out_proj(mix, w, layer, x_parts, mod, split_out):
    tm = TM_OUT
    if split_out:
        out_parts = [jax.ShapeDtypeStruct((CTX_ROWS, D_MODEL), F32), jax.ShapeDtypeStruct((LAT_ROWS, D_MODEL), F32)]
    else:
        out_parts = [jax.ShapeDtypeStruct((ROWS, D_MODEL), F32)]
    return pl.pallas_call(
        functools.partial(_out_kernel, n_x=len(x_parts), n_o=len(out_parts)),
        grid=(ROWS // tm,),
        in_specs=[pl.BlockSpec((tm, D_MODEL), lambda i: (i, 0)),
                  pl.BlockSpec((None, D_MODEL, D_MODEL), lambda i: (layer, 0, 0))] + _row_specs(x_parts, tm) + [
                  pl.BlockSpec((None, 1, D_MODEL), lambda i: (_cond_of_tile(i, tm) * 3 + 2, 0, 0))],
        out_specs=_row_specs(out_parts, tm),
        out_shape=out_parts,
        compiler_params=_cparams(("arbitrary",)),
        name="out_proj_residual",
    )(mix, w, *x_parts, mod)


def _rope(x, cos_t, sin_t):
    return x * cos_t + pltpu.roll(x, LANES // 2, 1) * sin_t


def _row_sums(x):
    hi = x.astype(BF16)
    lo = (x - hi.astype(F32)).astype(BF16)
    return _dot(jnp.concatenate([hi, lo], axis=1), jnp.ones((2 * x.shape[1], LANES), BF16))


def _amid_kernel(*refs, do_q, norm_kv):
    if do_q:
        (ql_ref, kvl_ref, kpe_ref, wq_ref, wk_ref, wv_ref, qlg_ref, kvlg_ref, qng_ref, kng_ref, cos_ref, sin_ref,
         q_out, k_out, v_out, ckv_out, kpe_out) = refs
    else:
        (kvl_ref, kpe_ref, wk_ref, wv_ref, kvlg_ref, kng_ref, cos_ref, sin_ref, k_out, v_out) = refs
    cos_t = cos_ref[...]
    sin_t = sin_ref[...]
    inv_qk = 1.0 / A_QK

    if do_q:
        ql = ql_ref[...]
        qn = ql * lax.rsqrt(jnp.mean(ql * ql, axis=-1, keepdims=True) + EPS) * qlg_ref[...]
        qu = _dot(qn.astype(BF16), wq_ref[...])
        g_nope = qng_ref[:, :A_NOPE]
        g_rope = qng_ref[:, A_NOPE:]
        for h in range(A_HEADS):
            qh = qu[:, h * A_QK_PAD:(h + 1) * A_QK_PAD]
            r = lax.rsqrt(_row_sums(qh * qh) * inv_qk + EPS)
            q_out[:, h * A_QK_PAD:h * A_QK_PAD + A_NOPE] = (qh[:, :A_NOPE] * r * g_nope).astype(BF16)
            q_out[:, h * A_QK_PAD + A_NOPE:(h + 1) * A_QK_PAD] = _rope(qh[:, A_NOPE:] * r * g_rope,
                                                                       cos_t, sin_t).astype(BF16)

    kvl = kvl_ref[...]
    if norm_kv:
        ckv = kvl * lax.rsqrt(jnp.mean(kvl * kvl, axis=-1, keepdims=True) + EPS) * kvlg_ref[...]
    else:
        ckv = kvl
    ckv_b = ckv.astype(BF16)
    kn = _dot(ckv_b, wk_ref[...])
    v_out[...] = _dot(ckv_b, wv_ref[...]).astype(BF16)
    kpe = kpe_ref[...]
    if do_q:
        @pl.when(pl.program_id(0) < CTX_ROWS // TM_MID)
        def _():
            half = A_ROPE // 2
            ckv_out[...] = ckv.reshape(ckv_out.shape)
            kpe_out[...] = jnp.concatenate([kpe[:, :half], kpe[:, LANES // 2:LANES // 2 + half]],
                                           axis=1).reshape(kpe_out.shape)
    sp = _row_sums(kpe * kpe)
    g_nope = kng_ref[:, :A_NOPE]
    g_rope = kng_ref[:, A_NOPE:]
    for h in range(A_HEADS):
        a = kn[:, h * A_NOPE:(h + 1) * A_NOPE]
        r = lax.rsqrt((_row_sums(a * a) + sp) * inv_qk + EPS)
        k_out[:, h * A_QK_PAD:h * A_QK_PAD + A_NOPE] = (a * r * g_nope).astype(BF16)
        k_out[:, h * A_QK_PAD + A_NOPE:(h + 1) * A_QK_PAD] = _rope(kpe * r * g_rope, cos_t, sin_t).astype(BF16)


def _layer_slab_spec(tm, layer, width):
    n_ctx = CTX_ROWS // tm
    return pl.BlockSpec((tm // SEQ, None, SEQ, width), lambda i: (jnp.minimum(i, n_ctx - 1), layer, 0, 0))


def _call_with_carried(kernel, n_plain_out, carried, **kw):
    def run(*args, in_specs):
        args = list(args)
        in_specs = list(in_specs)
        n_real = len(args)
        aliases = {}
        for idx, arr in enumerate(carried):
            if arr is not None:
                aliases[len(args)] = n_plain_out + idx
                args.append(arr)
                in_specs.append(pl.BlockSpec(memory_space=pl.ANY))
        n_in = len(args)

        def body(*refs):
            kernel(*refs[:n_real], *refs[n_in:])

        return pl.pallas_call(body, in_specs=in_specs, input_output_aliases=aliases, **kw)(*args)
    return run


def _amid_tokens(y, wq, wk, wv, qlg, kvlg, qng, kng, cos_t, sin_t, layer, ckv_new, kpe_new):
    tm = TM_MID
    const = lambda i: (0, 0)
    rope_map = lambda i: (_rope_block_of_tile(i, tm), 0)
    n_ab = (DEPTH + 1) // 2
    run = _call_with_carried(
        functools.partial(_amid_kernel, do_q=True, norm_kv=True), 3, [ckv_new, kpe_new],
        grid=(ROWS // tm,),
        out_specs=[pl.BlockSpec((tm, A_HEADS * A_QK_PAD), lambda i: (i, 0)),
                   pl.BlockSpec((tm, A_HEADS * A_QK_PAD), lambda i: (i, 0)),
                   pl.BlockSpec((tm, A_HEADS * A_VDIM), lambda i: (i, 0)),
                   _layer_slab_spec(tm, layer, KV_LORA), _layer_slab_spec(tm, layer, A_ROPE)],
        out_shape=[jax.ShapeDtypeStruct((ROWS, A_HEADS * A_QK_PAD), BF16),
                   jax.ShapeDtypeStruct((ROWS, A_HEADS * A_QK_PAD), BF16),
                   jax.ShapeDtypeStruct((ROWS, A_HEADS * A_VDIM), BF16),
                   jax.ShapeDtypeStruct((BATCH, n_ab, SEQ, KV_LORA), F32),
                   jax.ShapeDtypeStruct((BATCH, n_ab, SEQ, A_ROPE), F32)],
        compiler_params=_cparams(("arbitrary",)),
        name="mla_qkv_prep")
    return run(y, y, y, wq, wk, wv, qlg, kvlg, qng, kng, cos_t, sin_t,
               in_specs=[pl.BlockSpec((tm, Q_LORA), lambda i: (i, AB_QLAT // Q_LORA)),
                         pl.BlockSpec((tm, KV_LORA), lambda i: (i, AB_KVLAT // KV_LORA)),
                         pl.BlockSpec((tm, LANES), lambda i: (i, AB_KPE // LANES)),
                         pl.BlockSpec(wq.shape, const), pl.BlockSpec(wk.shape, const), pl.BlockSpec(wv.shape, const),
                         pl.BlockSpec(qlg.shape, const), pl.BlockSpec(kvlg.shape, const),
                         pl.BlockSpec(qng.shape, const), pl.BlockSpec(kng.shape, const),
                         pl.BlockSpec((tm, LANES), rope_map), pl.BlockSpec((tm, LANES), rope_map)])


def _amid_cache(ckv, kpe, wk, wv, kvlg, kng, cos_t, sin_t):
    rows = ckv.shape[0]
    tm = TM_MID
    const = lambda i: (0, 0)
    outs = pl.pallas_call(
        functools.partial(_amid_kernel, do_q=False, norm_kv=False),
        grid=(rows // tm,),
        in_specs=[pl.BlockSpec((tm, KV_LORA), lambda i: (i, 0)),
                  pl.BlockSpec((tm, LANES), lambda i: (i, 0)),
                  pl.BlockSpec(wk.shape, const), pl.BlockSpec(wv.shape, const),
                  pl.BlockSpec(kvlg.shape, const), pl.BlockSpec(kng.shape, const),
                  pl.BlockSpec((tm, LANES), const), pl.BlockSpec((tm, LANES), const)],
        out_specs=[pl.BlockSpec((tm, A_HEADS * A_QK_PAD), lambda i: (i, 0)),
                   pl.BlockSpec((tm, A_HEADS * A_VDIM), lambda i: (i, 0))],
        out_shape=[jax.ShapeDtypeStruct((rows, A_HEADS * A_QK_PAD), BF16),
                   jax.ShapeDtypeStruct((rows, A_HEADS * A_VDIM), BF16)],
        compiler_params=_cparams(("parallel",)),
        name="mla_cache_kv_prep",
    )(ckv, kpe, wk, wv, kvlg, kng, cos_t, sin_t)
    return outs[0], outs[1]


def _scores(q, srcs, c):
    zs = []
    for k, _, bias in srcs:
        z = _dot_nt(q, k) * c
        if bias is not None:
            z = (z.reshape(z.shape[0] // bias.shape[0], *bias.shape) + bias).reshape(z.shape)
        zs.append(z)
    return zs


def _softmax_pv(zs, srcs, sink_z=None):
    tile_max = None
    for z in zs:
        for j in range(z.shape[1] // LANES):
            blk = z[:, j * LANES:(j + 1) * LANES]
            tile_max = blk if tile_max is None else jnp.maximum(tile_max, blk)
    m = tile_max.max(axis=-1, keepdims=True)
    if sink_z is not None:
        m = jnp.maximum(m, sink_z)
    acc = None
    for z, (_, v, _) in zip(zs, srcs):
        o = _dot(jnp.exp2(z - m).astype(BF16), jnp.concatenate([v, jnp.ones_like(v)], axis=1))
        acc = o if acc is None else acc + o
    dv = acc.shape[1] // 2
    den = acc[:, dv:]
    if sink_z is not None:
        den = den + jnp.exp2(sink_z - m)
    return acc[:, :dv] / den


def _attend_streams(n, q_of, srcs_of, c, sink_of=None):
    outs = []
    zs = _scores(q_of(0), srcs_of(0), c)
    for t in range(n):
        nxt = _scores(q_of(t + 1), srcs_of(t + 1), c) if t + 1 < n else None
        outs.append(_softmax_pv(zs, srcs_of(t), None if sink_of is None else sink_of(t)))
        zs = nxt
    return outs


def _attn_a_kernel(*refs, n_src, heads):
    q_ref = refs[0]
    k_refs = refs[1:1 + n_src]
    v_refs = refs[1 + n_src:1 + 2 * n_src]
    gate_ref, o_ref = refs[1 + 2 * n_src:]
    sub = min(ATTN_SUB, q_ref.shape[0])
    streams = [(h, r) for h in range(heads) for r in range(q_ref.shape[0] // sub)]

    def rows(t):
        return slice(streams[t][1] * sub, (streams[t][1] + 1) * sub)

    def q_of(t):
        h = streams[t][0]
        return q_ref[rows(t), h * A_QK_PAD:(h + 1) * A_QK_PAD]

    def srcs_of(t):
        h = streams[t][0]
        return [(k_ref[:, h * A_QK_PAD:(h + 1) * A_QK_PAD], v_ref[:, h * A_VDIM:(h + 1) * A_VDIM], None)
                for k_ref, v_ref in zip(k_refs, v_refs)]

    outs = _attend_streams(len(streams), q_of, srcs_of, A_QK ** -0.5 * LOG2_E)
    for t, o in enumerate(outs):
        cols = slice(streams[t][0] * A_VDIM, (streams[t][0] + 1) * A_VDIM)
        o_ref[rows(t), cols] = (o * _silu(gate_ref[rows(t), cols])).astype(o_ref.dtype)


def _attn_a(q, k, v, y, mix, kc=None, vc=None):
    latent = kc is not None
    if latent:
        nb, t, tq, hp, row0 = DEC_BATCH, DEC_SEQ, TQ_A, 1, CTX_ROWS
    else:
        nb, t, tq, hp, row0 = BATCH, SEQ, SEQ, A_HEADS, 0
    nq = t // tq
    qrow = lambda b, h, i: row0 // tq + b * nq + i
    in_specs = [pl.BlockSpec((tq, hp * A_QK_PAD), lambda b, h, i: (qrow(b, h, i), h))]
    args = [q]
    if latent:
        in_specs.append(pl.BlockSpec((PAST_LEN, hp * A_QK_PAD), lambda b, h, i: (b, h)))
        args.append(kc)
    in_specs.append(pl.BlockSpec((t, hp * A_QK_PAD), lambda b, h, i: (row0 // t + b, h)))
    args.append(k)
    if latent:
        in_specs.append(pl.BlockSpec((PAST_LEN, hp * A_VDIM), lambda b, h, i: (b, h)))
        args.append(vc)
    in_specs.append(pl.BlockSpec((t, hp * A_VDIM), lambda b, h, i: (row0 // t + b, h)))
    args.append(v)
    in_specs.append(pl.BlockSpec((tq, hp * A_VDIM), lambda b, h, i: (qrow(b, h, i), AB_AGATE // (hp * A_VDIM) + h)))
    args.append(y)
    run = _call_with_carried(
        functools.partial(_attn_a_kernel, n_src=2 if latent else 1, heads=hp), 0, [mix],
        grid=(nb, A_HEADS // hp, nq),
        out_specs=[pl.BlockSpec((tq, hp * A_VDIM), lambda b, h, i: (qrow(b, h, i), h))],
        out_shape=[_MIX_SHAPE],
        compiler_params=_cparams(("parallel", "parallel", "parallel")),
        name="mla_attention_latent" if latent else "mla_attention_context")
    return run(*args, in_specs=in_specs)[0]


def _hgrn_constants():
    c, nl = HG_CHUNK, HG_LEVELS
    t = np.arange(c)[:, None]
    u = np.arange(c)[None, :]
    tri_f = (u <= t).astype(np.float32)
    mask_f = np.zeros((nl, c, c), np.float32)
    coef_f = np.zeros((nl, c, LANES), np.float32)
    for l in range(nl):
        half = c >> (l + 1)
        seg = 2 * half
        mask_f[l] = ((u // seg) == (t // seg)) & ((t % seg) >= half) & ((u % seg) < half)
        later = np.broadcast_to((t % seg) >= half, (c, LANES))
        coef_f[l] = np.where(later, 1.0, -1.0 if half > 1 else 0.0)
    tri_b = tri_f[::-1, ::-1]
    mask_b = mask_f[:, ::-1, ::-1]
    coef_b = coef_f[:, ::-1, :]
    to_tri = lambda a: jnp.asarray(np.concatenate([a, a, a], axis=1), BF16)
    to_f32 = lambda a: jnp.asarray(np.ascontiguousarray(a), F32)
    return (to_tri(tri_f), to_tri(tri_b)), (to_f32(mask_f), to_f32(mask_b)), (to_f32(coef_f), to_f32(coef_b))


def _hgrn_decays(x, lb, tri3):
    kk = (1.0 - lb) * _sigmoid(-x)
    lf = jnp.log2(1.0 - kk)
    hi = lf.astype(BF16)
    r1 = lf - hi.astype(F32)
    mid = r1.astype(BF16)
    lo = (r1 - mid.astype(F32)).astype(BF16)
    return kk, lf, _dot(tri3, jnp.concatenate([hi, mid, lo], axis=0))


def _hgrn_level_arg(l, cs, lf, coef, forward):
    c = HG_CHUNK
    nv = c // 8
    half = c >> (l + 1)
    if half == 1:
        return lf * coef
    cs3 = cs.reshape(nv, 8, LANES)

    def in_vreg_row(r):
        return jnp.broadcast_to(cs3[:, r:r + 1, :], (nv, 8, LANES))

    if half >= 8:
        m = half // 8
        nseg = nv // (2 * m)
        edge = cs3[:, 7:8, :] if forward else cs3[:, 0:1, :]
        e4 = edge.reshape(nseg, 2 * m, 1, LANES)
        a = e4[:, m - 1:m] if forward else e4[:, m:m + 1]
        anchor = jnp.broadcast_to(a, (nseg, 2 * m, 8, LANES)).reshape(c, LANES)
    elif half == 4:
        anchor = in_vreg_row(3 if forward else 4).reshape(c, LANES)
    else:
        r0, r1 = (1, 5) if forward else (2, 6)
        sub = lax.broadcasted_iota(jnp.int32, (nv, 8, LANES), 1)
        anchor = jnp.where(sub < 4, in_vreg_row(r0), in_vreg_row(r1)).reshape(c, LANES)
    return (cs - anchor) * coef


def _hgrn_kernel(*refs, t_len, heads, zero_init, emit_state):
    c = HG_CHUNK
    n_chunks = t_len // c
    it = iter(refs)
    bq_ref, ff_ref, fb_ref, vi_ref, bg_ref, lb_ref, hg_ref = (next(it) for _ in range(7))
    if not zero_init:
        s0f_ref, s0b_ref = next(it), next(it)
    trif_ref, trib_ref, maskf_ref, maskb_ref, coeff_ref, coefb_ref = (next(it) for _ in range(6))
    o_ref = next(it)
    if emit_state:
        sf_ref, sb_ref = next(it), next(it)
    of_ref, ob_ref, stf_ref, stb_ref = (next(it) for _ in range(4))

    for h in range(heads):
        if zero_init:
            stf_ref[h] = jnp.zeros((B_DV, B_DK), F32)
            stb_ref[h] = jnp.zeros((B_DV, B_DK), F32)
        else:
            stf_ref[h] = s0f_ref[h].T
            stb_ref[h] = s0b_ref[h].T
    lb = lb_ref[...]

    nl = HG_LEVELS
    unroll = min(HG_UNROLL // heads, n_chunks)
    dirs = ((ff_ref, 0, trif_ref, maskf_ref, coeff_ref, stf_ref, of_ref, True),
            (fb_ref, 1, trib_ref, maskb_ref, coefb_ref, stb_ref, ob_ref, False))

    def body(i, carry):
        chains = []
        for h in range(heads):
            cols = slice(h * LANES, (h + 1) * LANES)
            for f_ref, lb_row, tri_ref, mask_ref, coef_ref, st_ref, out_ref, forward in dirs:
                for u in range(unroll):
                    k = i * unroll + u
                    r0 = pl.multiple_of((k if forward else n_chunks - 1 - k) * c, c)
                    chains.append(dict(rows=pl.ds(r0, c), cols=cols, head=h, f_ref=f_ref,
                                       lb=lb[lb_row:lb_row + 1, cols], tri_ref=tri_ref, mask_ref=mask_ref,
                                       coef_ref=coef_ref, st_ref=st_ref, out_ref=out_ref, forward=forward))
        for ch in chains:
            ch["kk"], ch["lf"], ch["cs"] = _hgrn_decays(ch["f_ref"][ch["rows"], ch["cols"]], ch["lb"],
                                                        ch["tri_ref"][...])
            ch["q"] = _silu(bq_ref[ch["rows"], ch["cols"]])
            ch["v"] = vi_ref[ch["rows"], ch["cols"]]
            ch["q16"] = ch["q"].astype(BF16)
            ch["kk16"] = ch["kk"].astype(BF16)
            ch["sc"] = jnp.zeros((c, c), F32)
        for l in range(nl):
            for ch in chains:
                el = jnp.exp2(_hgrn_level_arg(l, ch["cs"], ch["lf"], ch["coef_ref"][l], ch["forward"])).astype(BF16)
                ch["sc"] = ch["sc"] + ch["mask_ref"][l] * _dot_nt(ch["q16"] * el, ch["kk16"] * el)
        for ch in chains:
            cs, q, kk, v = ch["cs"], ch["q"], ch["kk"], ch["v"]
            end = cs[c - 1:c, :] if ch["forward"] else cs[0:1, :]
            ch["end"] = end
            ch["q_in"] = (q * jnp.exp2(cs)).astype(BF16)
            ch["upd"] = _dot(v.T.astype(BF16), (kk * jnp.exp2(end - cs)).astype(BF16))
            ch["o"] = _dot(ch["sc"].astype(BF16), v.astype(BF16)) + jnp.sum(q * kk, axis=-1, keepdims=True) * v
        for ch in chains:
            st = ch["st_ref"][ch["head"]]
            ch["out_ref"][ch["rows"], ch["cols"]] = ch["o"] + _dot_nt(ch["q_in"], st.astype(BF16))
            ch["st_ref"][ch["head"]] = jnp.exp2(ch["end"]) * st + ch["upd"]
        return carry

    lax.fori_loop(0, n_chunks // unroll, body, 0)
    if emit_state:
        for h in range(heads):
            sf_ref[h] = stf_ref[h].T
            sb_ref[h] = stb_ref[h].T

    hg = hg_ref[...]
    blk = min(HG_UNROLL, n_chunks) * c

    def finish(i, carry):
        rows = pl.ds(pl.multiple_of(i * blk, blk), blk)
        for h in range(heads):
            cols = slice(h * LANES, (h + 1) * LANES)
            o = of_ref[rows, cols] + ob_ref[rows, cols]
            o = o * lax.rsqrt(jnp.mean(o * o, axis=-1, keepdims=True) + EPS) * hg
            o_ref[rows, cols] = (o * _silu(bg_ref[rows, cols])).astype(o_ref.dtype)
        return carry

    lax.fori_loop(0, t_len // blk, finish, 0)


def _hgrn(y, lb, hg, mix, consts, s0f=None, s0b=None, layer=None, sf_new=None, sb_new=None):
    latent = s0f is not None
    if latent:
        nb, t, row0, hp = DEC_BATCH, DEC_SEQ, CTX_ROWS, 1
    else:
        nb, t, row0, hp = BATCH, SEQ, 0, 2
    w = hp * LANES
    rb = lambda b: row0 // t + b
    col = lambda off: (lambda b, h: (rb(b), off // w + h))
    const2 = lambda b, h: (0, 0)
    const3 = lambda b, h: (0, 0, 0)
    in_specs = [pl.BlockSpec((t, w), col(AB_BQ)), pl.BlockSpec((t, w), col(AB_BFF)),
                pl.BlockSpec((t, w), col(AB_BFB)), pl.BlockSpec((t, w), col(AB_BI)),
                pl.BlockSpec((t, w), col(AB_BGATE)),
                pl.BlockSpec((2, w), lambda b, h: (0, h)),
                pl.BlockSpec((1, LANES), const2)]
    args = [y, y, y, y, y, lb, hg]
    if latent:
        st_spec = pl.BlockSpec((None, hp, B_DK, B_DV), lambda b, h: (b, h, 0, 0))
        in_specs += [st_spec, st_spec]
        args += [s0f, s0b]
    tris, masks, coefs = consts
    in_specs += ([pl.BlockSpec(a.shape, const2) for a in tris] + [pl.BlockSpec(a.shape, const3) for a in masks]
                 + [pl.BlockSpec(a.shape, const3) for a in coefs])
    args += [*tris, *masks, *coefs]
    out_specs = [pl.BlockSpec((t, w), lambda b, h: (rb(b), A_HEADS * A_VDIM // w + h))]
    out_shape = [_MIX_SHAPE]
    carried = [mix]
    if not latent:
        st_out = pl.BlockSpec((None, None, hp, B_DK, B_DV), lambda b, h: (b, layer, h, 0, 0))
        out_specs += [st_out, st_out]
        out_shape += [jax.ShapeDtypeStruct((nb, (DEPTH + 1) // 2, B_HEADS, B_DK, B_DV), F32)] * 2
        carried += [sf_new, sb_new]
    run = _call_with_carried(
        functools.partial(_hgrn_kernel, t_len=t, heads=hp, zero_init=not latent, emit_state=not latent), 0, carried,
        grid=(nb, B_HEADS // hp),
        out_specs=out_specs,
        out_shape=out_shape,
        scratch_shapes=[pltpu.VMEM((t, w), F32), pltpu.VMEM((t, w), F32),
                        pltpu.VMEM((hp, B_DV, B_DK), F32), pltpu.VMEM((hp, B_DV, B_DK), F32)],
        compiler_params=_cparams(("parallel", "parallel")),
        name="hgrn2_latent" if latent else "hgrn2_context")
    return run(*args, in_specs=in_specs)


def _cmid_kernel(q_ref, k_ref, v_ref, qg_ref, kg_ref, cos_ref, sin_ref, q_out, k_out, v_out, kc_out, vc_out):
    cos_t = cos_ref[...]
    sin_t = sin_ref[...]
    qg = qg_ref[...]
    kg = kg_ref[...]
    in_ctx = pl.program_id(0) < CTX_ROWS // TM_MID

    def norm(x, g):
        return x * lax.rsqrt(_row_sums(x * x) * (1.0 / C_HEAD_DIM) + EPS) * g

    def rope(x):
        return _rope(x, cos_t, sin_t)

    for h in range(C_HEADS):
        sl = slice(h * C_HEAD_DIM, (h + 1) * C_HEAD_DIM)
        q_out[:, sl] = rope(norm(q_ref[:, sl], qg)).astype(BF16)
    for h in range(C_KV_HEADS):
        sl = slice(h * C_HEAD_DIM, (h + 1) * C_HEAD_DIM)
        kn = norm(k_ref[:, sl], kg)
        k_out[:, sl] = rope(kn).astype(BF16)

        @pl.when(in_ctx)
        def _():
            kc_out[:, :, sl] = kn.reshape(kc_out.shape[0], SEQ, C_HEAD_DIM)
    v = v_ref[...]
    v_out[...] = v.astype(BF16)

    @pl.when(in_ctx)
    def _():
        vc_out[...] = v.reshape(vc_out.shape)


def _cmid(y, qg, kg, cos_t, sin_t, layer, kc_new, vc_new):
    tm = TM_MID
    const = lambda i: (0, 0)
    rope_map = lambda i: (_rope_block_of_tile(i, tm), 0)
    cache = jax.ShapeDtypeStruct((BATCH, DEPTH // 2, SEQ, C_KV_WIDTH), F32)
    run = _call_with_carried(
        _cmid_kernel, 3, [kc_new, vc_new],
        grid=(ROWS // tm,),
        out_specs=[pl.BlockSpec((tm, C_WIDTH), lambda i: (i, 0)),
                   pl.BlockSpec((tm, C_KV_WIDTH), lambda i: (i, 0)),
                   pl.BlockSpec((tm, C_KV_WIDTH), lambda i: (i, 0)),
                   _layer_slab_spec(tm, layer, C_KV_WIDTH), _layer_slab_spec(tm, layer, C_KV_WIDTH)],
        out_shape=[jax.ShapeDtypeStruct((ROWS, C_WIDTH), BF16),
                   jax.ShapeDtypeStruct((ROWS, C_KV_WIDTH), BF16),
                   jax.ShapeDtypeStruct((ROWS, C_KV_WIDTH), BF16), cache, cache],
        compiler_params=_cparams(("arbitrary",)),
        name="gqa_qkv_prep")
    return run(y, y, y, qg, kg, cos_t, sin_t,
               in_specs=[pl.BlockSpec((tm, C_WIDTH), lambda i: (i, C_Q // C_WIDTH)),
                         pl.BlockSpec((tm, C_KV_WIDTH), lambda i: (i, C_K // C_KV_WIDTH)),
                         pl.BlockSpec((tm, C_KV_WIDTH), lambda i: (i, C_V // C_KV_WIDTH)),
                         pl.BlockSpec((1, C_HEAD_DIM), const), pl.BlockSpec((1, C_HEAD_DIM), const),
                         pl.BlockSpec((tm, LANES), rope_map), pl.BlockSpec((tm, LANES), rope_map)])


def _band_start(i, tq, t_len):
    return pl.multiple_of(jnp.clip(i * tq - WINDOW, 0, t_len - (tq + 2 * WINDOW)), WINDOW)


def _band_bias(tq, t_len):
    width = tq + 2 * WINDOW
    i = np.arange(t_len // tq)[:, None, None]
    qpos = i * tq + np.arange(tq)[None, :, None]
    kpos = np.clip(i * tq - WINDOW, 0, t_len - width) + np.arange(width)[None, None, :]
    return jnp.asarray(np.where(np.abs(kpos - qpos) <= WINDOW, 0.0, NEG_BIG), F32)


def _attn_c_kernel(*refs, band, tq, t_len, groups, stack):
    if band:
        q_ref, kc_ref, vc_ref, kl_ref, vl_ref, bias_ref, sink_ref, gate_ref, o_ref = refs
    else:
        q_ref, kc_ref, vc_ref, sink_ref, gate_ref, o_ref = refs
    hd = C_HEAD_DIM
    streams = [(g, s) for g in range(groups) for s in range(C_GROUP // stack)]
    if band:
        width = tq + 2 * WINDOW
        start = _band_start(pl.program_id(2), tq, t_len)
        band_bias = bias_ref[...]

    def heads_of(t):
        g, s = streams[t]
        return [g * C_GROUP + s * stack + r for r in range(stack)]

    def q_of(t):
        return jnp.concatenate([q_ref[:, h * hd:(h + 1) * hd] for h in heads_of(t)], axis=0)

    def srcs_of(t):
        g = streams[t][0]
        cols = slice(g * hd, (g + 1) * hd)
        srcs = [(kc_ref[:, cols], vc_ref[:, cols], None)]
        if band:
            srcs.append((kl_ref[pl.ds(start, width), cols], vl_ref[pl.ds(start, width), cols], band_bias))
        return srcs

    def sink_of(t):
        return jnp.concatenate([jnp.broadcast_to(sink_ref[h][:, :1] * LOG2_E, (tq, 1)) for h in heads_of(t)], axis=0)

    outs = _attend_streams(len(streams), q_of, srcs_of, hd ** -0.5 * LOG2_E, sink_of)
    for t, o in enumerate(outs):
        for r, h in enumerate(heads_of(t)):
            cols = slice(h * hd, (h + 1) * hd)
            o_ref[:, cols] = (o[r * tq:(r + 1) * tq] * _silu(gate_ref[:, cols])).astype(o_ref.dtype)


def _attn_c(q, k, v, y, sink, mix, kc=None, vc=None):
    latent = kc is not None
    if latent:
        nb, t, tq, gp, stack, row0 = DEC_BATCH, DEC_SEQ, TQ_C, 1, 2, CTX_ROWS
    else:
        nb, t, tq, gp, stack, row0 = BATCH, SEQ, SEQ, C_KV_HEADS, C_GROUP, 0
    gw = gp * C_GROUP * C_HEAD_DIM
    kvw = gp * C_HEAD_DIM
    nq = t // tq
    qrow = lambda b, g, i: row0 // tq + b * nq + i
    own_kv = pl.BlockSpec((t, kvw), lambda b, g, i: (row0 // t + b, g))
    in_specs = [pl.BlockSpec((tq, gw), lambda b, g, i: (qrow(b, g, i), g))]
    args = [q]
    if latent:
        ctx_kv = pl.BlockSpec((PAST_LEN, kvw), lambda b, g, i: (b, g))
        in_specs += [ctx_kv, ctx_kv, own_kv, own_kv,
                     pl.BlockSpec((None, tq, tq + 2 * WINDOW), lambda b, g, i: (i, 0, 0))]
        args += [kc, vc, k, v, _band_bias(tq, t)]
    else:
        in_specs += [own_kv, own_kv]
        args += [k, v]
    in_specs.append(pl.BlockSpec((gp * C_GROUP, 1, LANES), lambda b, g, i: (g, 0, 0)))
    args.append(sink)
    in_specs.append(pl.BlockSpec((tq, gw), lambda b, g, i: (qrow(b, g, i), C_GATE // gw + g)))
    args.append(y)
    run = _call_with_carried(
        functools.partial(_attn_c_kernel, band=latent, tq=tq, t_len=t, groups=gp, stack=stack), 0, [mix],
        grid=(nb, C_KV_HEADS // gp, nq),
        out_specs=[pl.BlockSpec((tq, gw), lambda b, g, i: (qrow(b, g, i), g))],
        out_shape=[_MIX_SHAPE],
        compiler_params=_cparams(("parallel", "parallel", "parallel")),
        name="gqa_attention_latent" if latent else "gqa_attention_context")
    return run(*args, in_specs=in_specs)[0]


def _axial_angles(n_tokens, rot_dim):
    rows = n_tokens // GRID_W
    row = jnp.repeat(jnp.arange(rows, dtype=F32), GRID_W)
    col = jnp.tile(jnp.arange(GRID_W, dtype=F32), rows)
    n_freq = rot_dim // 4
    inv = ROPE_BASE ** (-jnp.arange(n_freq, dtype=F32) / n_freq)
    return jnp.concatenate([row[:, None] * inv, col[:, None] * inv], axis=-1)


def _spread_halves(a, fill=0.0):
    h = a.shape[-1] // 2
    pad = jnp.full(a.shape[:-1] + (LANES // 2 - h,), fill, a.dtype)
    return jnp.concatenate([a[..., :h], pad, a[..., h:], pad], axis=-1)


def _spread_head(a):
    return jnp.concatenate([a[..., :A_NOPE], _spread_halves(a[..., A_NOPE:])], axis=-1)


def _rope_tables(rot_dim, tm):
    ang = _axial_angles(DEC_SEQ, rot_dim)
    cos, sin = jnp.cos(ang), jnp.sin(ang)
    cos_t = _spread_halves(jnp.concatenate([cos, cos], axis=-1), 1.0)
    sin_t = _spread_halves(jnp.concatenate([-sin, sin], axis=-1))
    cos_t = jnp.concatenate([jnp.ones((tm, LANES), F32), cos_t], axis=0)
    sin_t = jnp.concatenate([jnp.zeros((tm, LANES), F32), sin_t], axis=0)
    return cos_t, sin_t


def _lower_bounds(lb_logits):
    p = jax.nn.softmax(lb_logits.astype(F32), axis=0)
    return jnp.cumsum(p, axis=0) - p[0:1]


def kernel(x_prompt, x_sample, cache_ckv, cache_kpe, state_hgrn_fwd, state_hgrn_bwd, cache_k_c, cache_v_c, c, c_ctx,
           mod_w_ab, mod_b_ab, norm_ab, w_in_ab, q_lora_norm, kv_lora_norm, w_q_up, w_kv_up, q_norm_ab, k_norm_ab,
           hgrn_lb_logits, hgrn_out_norm, w_out_ab, mod_w_c, mod_b_c, norm_c, w_in_c, q_norm_c, k_norm_c, sink_c,
           w_out_c):
    x_parts = [x_prompt.reshape(CTX_ROWS, D_MODEL), x_sample.reshape(LAT_ROWS, D_MODEL)]
    cond8 = jnp.concatenate([c_ctx[None, :], c, jnp.zeros((N_COND - 1 - DEC_BATCH, D_MODEL), F32)], axis=0)
    mods_ab = _modulation(cond8, mod_w_ab, mod_b_ab)
    mods_c = _modulation(cond8, mod_w_c, mod_b_c)
    lower = _lower_bounds(hgrn_lb_logits)
    cos_a, sin_a = _rope_tables(A_ROPE, TM_MID)
    cos_c, sin_c = _rope_tables(C_HEAD_DIM, TM_MID)
    hg_consts = _hgrn_constants()
    w_in_ab16 = _prep_w_in_ab(jnp.swapaxes(w_in_ab, 1, 2))
    w_in_c16, w_out_ab16, w_out_c16 = w_in_c.astype(BF16), w_out_ab.astype(BF16), w_out_c.astype(BF16)

    ckv_new = kpe_new = sf_new = sb_new = kc_new = vc_new = None
    for layer in range(DEPTH):
        j = layer // 2
        last = layer == DEPTH - 1
        if layer % 2 == 0:
            mod = mods_ab[j].reshape(3 * N_COND, 1, D_MODEL)
            wq = _spread_head(w_q_up[j].reshape(Q_LORA, A_HEADS, A_QK)).reshape(Q_LORA, A_HEADS * A_QK_PAD).astype(BF16)
            wkv = w_kv_up[j].reshape(KV_LORA, A_HEADS, A_NOPE + A_VDIM)
            wk = wkv[:, :, :A_NOPE].reshape(KV_LORA, A_HEADS * A_NOPE).astype(BF16)
            wv = wkv[:, :, A_NOPE:].reshape(KV_LORA, A_HEADS * A_VDIM).astype(BF16)
            qlg, kvlg = q_lora_norm[j][None, :], kv_lora_norm[j][None, :]
            qng, kng = _spread_head(q_norm_ab[j])[None, :], _spread_head(k_norm_ab[j])[None, :]

            y = _in_proj(x_parts, norm_ab[j][None, :], mod, w_in_ab16, j, w_transposed=True)
            q, k, v, ckv_new, kpe_new = _amid_tokens(y, wq, wk, wv, qlg, kvlg, qng, kng, cos_a, sin_a,
                                                     j, ckv_new, kpe_new)
            kpe_cache = _spread_halves(cache_kpe[:, j].reshape(DEC_BATCH * PAST_LEN, A_ROPE))
            kc, vc = _amid_cache(cache_ckv[:, j].reshape(DEC_BATCH * PAST_LEN, KV_LORA), kpe_cache, wk, wv, kvlg, kng,
                                 cos_a, sin_a)
            mix = _attn_a(q, k, v, y, None)
            mix = _attn_a(q, k, v, y, mix, kc=kc, vc=vc)
            hg = hgrn_out_norm[j][None, :]
            mix, sf_new, sb_new = _hgrn(y, lower[j], hg, mix, hg_consts, layer=j, sf_new=sf_new, sb_new=sb_new)
            (mix,) = _hgrn(y, lower[j], hg, mix, hg_consts, s0f=state_hgrn_fwd[:, j], s0b=state_hgrn_bwd[:, j])
            x_parts = _out_proj(mix, w_out_ab16, j, x_parts, mod, split_out=last)
        else:
            mod = mods_c[j].reshape(3 * N_COND, 1, D_MODEL)
            y = _in_proj(x_parts, norm_c[j][None, :], mod, w_in_c16, j,
                         col_map=lambda t: jnp.where(t < 2, t, jnp.where(t < 4, t + 1, 2)))
            q, k, v, kc_new, vc_new = _cmid(y, q_norm_c[j][None, :], k_norm_c[j][None, :], cos_c, sin_c,
                                            j, kc_new, vc_new)
            kc = cache_k_c[:, j].reshape(DEC_BATCH * PAST_LEN, C_KV_WIDTH).astype(BF16)
            vc = cache_v_c[:, j].reshape(DEC_BATCH * PAST_LEN, C_KV_WIDTH).astype(BF16)
            sink = jnp.broadcast_to(sink_c[j][:, None, None], (C_HEADS, 1, LANES))
            mix = _attn_c(q, k, v, y, sink, None)
            mix = _attn_c(q, k, v, y, sink, mix, kc=kc, vc=vc)
            x_parts = _out_proj(mix, w_out_c16, j, x_parts, mod, split_out=last)

    cache_c_shape = (BATCH, DEPTH // 2, SEQ, C_KV_HEADS, C_HEAD_DIM)
    return (x_parts[0].reshape(BATCH, SEQ, D_MODEL), x_parts[1].reshape(DEC_BATCH, DEC_SEQ, D_MODEL),
            ckv_new, kpe_new, sf_new, sb_new, kc_new.reshape(cache_c_shape), vc_new.reshape(cache_c_shape))
```

```python
import functools

import numpy as np
import jax
import jax.numpy as jnp
from jax import lax
from jax.experimental import pallas as pl
from jax.experimental.pallas import tpu as pltpu

F32 = jnp.float32
BF16 = jnp.bfloat16

D_MODEL = 2048
BATCH = 16
SEQ = 256
DEPTH = 4
DEC_BATCH = 4
DEC_SEQ = 2048
PAST_LEN = 256
GRID_W = 64
A_HEADS = 8
A_NOPE = 128
A_ROPE = 64
A_VDIM = 128
A_QK = A_NOPE + A_ROPE
A_QK_PAD = 256
Q_LORA = 512
KV_LORA = 256
B_HEADS = 8
B_DK = 128
B_DV = 128
C_HEADS = 16
C_KV_HEADS = 4
C_GROUP = C_HEADS // C_KV_HEADS
C_HEAD_DIM = 128
C_WIDTH = C_HEADS * C_HEAD_DIM
C_KV_WIDTH = C_KV_HEADS * C_HEAD_DIM
WINDOW = 128
ROPE_BASE = 10000.0
EPS = 1e-6
NEG_BIG = -1e30
LOG2_E = 1.4426950408889634

LANES = 128
CTX_ROWS = BATCH * SEQ
LAT_ROWS = DEC_BATCH * DEC_SEQ
ROWS = CTX_ROWS + LAT_ROWS
N_COND = 8

AB_AGATE = 0
AB_BQ = 1024
AB_BFF = 2048
AB_BFB = 3072
AB_BI = 4096
AB_BGATE = 5120
AB_QLAT = 6144
AB_KVLAT = 6656
AB_KPE = 6912
AB_N = 7168
C_Q = 0
C_GATE = 2048
C_K = 4096
C_V = 4608
C_N = 5120

TM_PROJ = 1024
TN_IN = 1024
TM_OUT = 512
TM_MID = 512
TQ_A = 1024
TQ_C = 256
ATTN_SUB = 256
HG_CHUNK = 128
HG_LEVELS = 7
HG_UNROLL = 4
VMEM_LIMIT = 56 * 1024 * 1024
_MIX_SHAPE = jax.ShapeDtypeStruct((ROWS, D_MODEL), BF16)


def _cparams(sem):
    return pltpu.CompilerParams(dimension_semantics=sem, vmem_limit_bytes=VMEM_LIMIT)


def _sigmoid(x):
    return 0.5 * jnp.tanh(0.5 * x) + 0.5


def _silu(x):
    return x * _sigmoid(x)


def _dot(a, b):
    return jnp.dot(a, b, preferred_element_type=F32)


def _dot_nt(a, b):
    return lax.dot_general(a, b, (((1,), (1,)), ((), ())), preferred_element_type=F32)


def _cond_of_tile(i, tm):
    n_ctx = CTX_ROWS // tm
    per_batch = DEC_SEQ // tm
    return jnp.where(i < n_ctx, 0, 1 + (i - n_ctx) // per_batch)


def _rope_block_of_tile(i, tm):
    n_ctx = CTX_ROWS // tm
    per_batch = DEC_SEQ // tm
    return jnp.where(i < n_ctx, 0, 1 + (i - n_ctx) % per_batch)


def _mod_kernel(c_ref, w_ref, b_ref, o_ref):
    a = _silu(c_ref[...]).astype(BF16)
    o_ref[...] = _dot(a, w_ref[...].astype(BF16)) + b_ref[...]


def _modulation(cond8, w_mod, b_mod):
    n = w_mod.shape[0]
    tn = 1024
    return pl.pallas_call(
        _mod_kernel,
        grid=(n, 3 * D_MODEL // tn),
        in_specs=[pl.BlockSpec((N_COND, D_MODEL), lambda l, j: (0, 0)),
                  pl.BlockSpec((None, D_MODEL, tn), lambda l, j: (l, 0, j)),
                  pl.BlockSpec((None, 1, tn), lambda l, j: (l, 0, j))],
        out_specs=pl.BlockSpec((None, N_COND, tn), lambda l, j: (l, 0, j)),
        out_shape=jax.ShapeDtypeStruct((n, N_COND, 3 * D_MODEL), F32),
        compiler_params=_cparams(("parallel", "parallel")),
        name="adaln_mod",
    )(cond8, w_mod, b_mod.reshape(n, 1, 3 * D_MODEL))


def _prep_w_kernel(w_ref, o_ref):
    n_head = Q_LORA + KV_LORA + A_ROPE
    n_main = AB_N - TN_IN
    half = A_ROPE // 2
    x1_end = Q_LORA + KV_LORA + half
    x2_at = Q_LORA + KV_LORA + LANES // 2
    o_ref[:n_main, :] = w_ref[n_head:, :].astype(BF16)
    o_ref[n_main:, :] = jnp.zeros((TN_IN, o_ref.shape[1]), BF16)
    o_ref[n_main:n_main + x1_end, :] = w_ref[:x1_end, :].astype(BF16)
    o_ref[n_main + x2_at:n_main + x2_at + half, :] = w_ref[x1_end:n_head, :].astype(BF16)


def _prep_w_in_ab(w_t):
    n_layers, n, _ = w_t.shape
    tc = 256
    return pl.pallas_call(
        _prep_w_kernel,
        grid=(n_layers, D_MODEL // tc),
        in_specs=[pl.BlockSpec((None, n, tc), lambda l, i: (l, 0, i))],
        out_specs=pl.BlockSpec((None, AB_N, tc), lambda l, i: (l, 0, i)),
        out_shape=jax.ShapeDtypeStruct((n_layers, AB_N, D_MODEL), BF16),
        compiler_params=_cparams(("parallel", "parallel")),
        name="w_in_ab_layout",
    )(w_t)


def _row_specs(parts, tm, single_buffer=False, tile_of=lambda i, *_: i):
    if len(parts) == 1:
        return [pl.BlockSpec((tm, D_MODEL), lambda *g: (tile_of(*g), 0))]
    n_first = CTX_ROWS // tm
    mode = dict(pipeline_mode=pl.Buffered(1)) if single_buffer else {}
    return [pl.BlockSpec((tm, D_MODEL), lambda *g: (jnp.minimum(tile_of(*g), n_first - 1), 0), **mode),
            pl.BlockSpec((tm, D_MODEL), lambda *g: (jnp.maximum(tile_of(*g) - n_first, 0), 0), **mode)]


IN_SUB = 4


def _in_next_tile(i, j):
    return jnp.where(j == 0, i, jnp.minimum(i + 1, ROWS // TM_PROJ - 1))


def _in_kernel(*refs, n_x, w_transposed):
    x_refs = refs[:n_x]
    g_ref, sh_ref, sc_ref, w_ref, o_ref, h_ref, hn_ref = refs[n_x:]
    i, j = pl.program_id(0), pl.program_id(1)
    n_first = CTX_ROWS // TM_PROJ
    sub_rows = TM_PROJ // IN_SUB

    def normalise(dst_ref, r0):
        rows = pl.ds(r0, sub_rows)
        if n_x == 1:
            x = x_refs[0][rows, :]
        else:
            x = jnp.where(_in_next_tile(i, j) < n_first, x_refs[0][rows, :], x_refs[1][rows, :])
        r = lax.rsqrt(jnp.mean(x * x, axis=-1, keepdims=True) + EPS)
        h = (x * r * g_ref[...]) * (1.0 + sc_ref[...]) + sh_ref[...]
        dst_ref[rows, :] = h.astype(BF16)

    @pl.when((i == 0) & (j == 0))
    def _():
        for s in range(IN_SUB):
            normalise(h_ref, s * sub_rows)

    @pl.when((i > 0) & (j == 0))
    def _():
        h_ref[...] = hn_ref[...]

    normalise(hn_ref, pl.multiple_of((jnp.clip(j, 1, IN_SUB) - 1) * sub_rows, sub_rows))
    o_ref[...] = (_dot_nt if w_transposed else _dot)(h_ref[...], w_ref[...])


def _in_proj(x_parts, g, mod, w, layer, col_map=None, w_transposed=False):
    tm, tn = TM_PROJ, TN_IN
    n_tiles = (w.shape[1] if w_transposed else w.shape[2]) // tn
    assert n_tiles > IN_SUB
    col_map = col_map or (lambda t: t)
    if w_transposed:
        w_spec = pl.BlockSpec((None, tn, D_MODEL), lambda i, j: (layer, col_map(j), 0))
    else:
        w_spec = pl.BlockSpec((None, D_MODEL, tn), lambda i, j: (layer, 0, col_map(j)))
    mod_row = lambda part: (lambda i, j: (_cond_of_tile(_in_next_tile(i, j), tm) * 3 + part, 0, 0))
    return pl.pallas_call(
        functools.partial(_in_kernel, n_x=len(x_parts), w_transposed=w_transposed),
        grid=(ROWS // tm, n_tiles),
        in_specs=_row_specs(x_parts, tm, single_buffer=True, tile_of=_in_next_tile) + [
            pl.BlockSpec((1, D_MODEL), lambda i, j: (0, 0)),
            pl.BlockSpec((None, 1, D_MODEL), mod_row(0)),
            pl.BlockSpec((None, 1, D_MODEL), mod_row(1)), w_spec],
        out_specs=pl.BlockSpec((tm, tn), lambda i, j: (i, j)),
        out_shape=jax.ShapeDtypeStruct((ROWS, n_tiles * tn), F32),
        scratch_shapes=[pltpu.VMEM((tm, D_MODEL), BF16), pltpu.VMEM((tm, D_MODEL), BF16)],
        compiler_params=_cparams(("arbitrary", "arbitrary")),
        name="norm_mod_in_proj",
    )(*x_parts, g, mod, mod, w)


def _out_kernel(*refs, n_x, n_o):
    m_ref, w_ref = refs[:2]
    x_refs = refs[2:2 + n_x]
    gt_ref = refs[2 + n_x]
    o_refs = refs[3 + n_x:]
    y = gt_ref[...] * _dot(m_ref[...], w_ref[...])

    def emit(x_ref, o_ref):
        o_ref[...] = x_ref[...] + y

    if n_x == 1 and n_o == 1:
        emit(x_refs[0], o_refs[0])
    else:
        in_first = pl.program_id(0) < CTX_ROWS // TM_OUT
        pl.when(in_first)(lambda: emit(x_refs[0], o_refs[0]))
        pl.when(jnp.logical_not(in_first))(lambda: emit(x_refs[-1], o_refs[-1]))


def _out_proj(mix, w, layer, x_parts, mod, split_out):
    tm = TM_OUT
    if split_out:
        out_parts = [jax.ShapeDtypeStruct((CTX_ROWS, D_MODEL), F32), jax.ShapeDtypeStruct((LAT_ROWS, D_MODEL), F32)]
    else:
        out_parts = [jax.ShapeDtypeStruct((ROWS, D_MODEL), F32)]
    return pl.pallas_call(
        functools.partial(_out_kernel, n_x=len(x_parts), n_o=len(out_parts)),
        grid=(ROWS // tm,),
        in_specs=[pl.BlockSpec((tm, D_MODEL), lambda i: (i, 0)),
                  pl.BlockSpec((None, D_MODEL, D_MODEL), lambda i: (layer, 0, 0))] + _row_specs(x_parts, tm) + [
                  pl.BlockSpec((None, 1, D_MODEL), lambda i: (_cond_of_tile(i, tm) * 3 + 2, 0, 0))],
        out_specs=_row_specs(out_parts, tm),
        out_shape=out_parts,
        compiler_params=_cparams(("arbitrary",)),
        name="out_proj_residual",
    )(mix, w, *x_parts, mod)


def _rope(x, cos_t, sin_t):
    return x * cos_t + pltpu.roll(x, LANES // 2, 1) * sin_t


def _row_sums(x):
    hi = x.astype(BF16)
    lo = (x - hi.astype(F32)).astype(BF16)
    return _dot(jnp.concatenate([hi, lo], axis=1), jnp.ones((2 * x.shape[1], LANES), BF16))


def _amid_kernel(*refs, do_q, norm_kv):
    if do_q:
        (ql_ref, kvl_ref, kpe_ref, wq_ref, wk_ref, wv_ref, qlg_ref, kvlg_ref, qng_ref, kng_ref, cos_ref, sin_ref,
         q_out, k_out, v_out, ckv_out, kpe_out) = refs
    else:
        (kvl_ref, kpe_ref, wk_ref, wv_ref, kvlg_ref, kng_ref, cos_ref, sin_ref, k_out, v_out) = refs
    cos_t = cos_ref[...]
    sin_t = sin_ref[...]
    inv_qk = 1.0 / A_QK

    if do_q:
        ql = ql_ref[...]
        qn = ql * lax.rsqrt(jnp.mean(ql * ql, axis=-1, keepdims=True) + EPS) * qlg_ref[...]
        qu = _dot(qn.astype(BF16), wq_ref[...])
        g_nope = qng_ref[:, :A_NOPE]
        g_rope = qng_ref[:, A_NOPE:]
        for h in range(A_HEADS):
            qh = qu[:, h * A_QK_PAD:(h + 1) * A_QK_PAD]
            r = lax.rsqrt(_row_sums(qh * qh) * inv_qk + EPS)
            q_out[:, h * A_QK_PAD:h * A_QK_PAD + A_NOPE] = (qh[:, :A_NOPE] * r * g_nope).astype(BF16)
            q_out[:, h * A_QK_PAD + A_NOPE:(h + 1) * A_QK_PAD] = _rope(qh[:, A_NOPE:] * r * g_rope,
                                                                       cos_t, sin_t).astype(BF16)

    kvl = kvl_ref[...]
    if norm_kv:
        ckv = kvl * lax.rsqrt(jnp.mean(kvl * kvl, axis=-1, keepdims=True) + EPS) * kvlg_ref[...]
    else:
        ckv = kvl
    ckv_b = ckv.astype(BF16)
    kn = _dot(ckv_b, wk_ref[...])
    v_out[...] = _dot(ckv_b, wv_ref[...]).astype(BF16)
    kpe = kpe_ref[...]
    if do_q:
        @pl.when(pl.program_id(0) < CTX_ROWS // TM_MID)
        def _():
            half = A_ROPE // 2
            ckv_out[...] = ckv.reshape(ckv_out.shape)
            kpe_out[...] = jnp.concatenate([kpe[:, :half], kpe[:, LANES // 2:LANES // 2 + half]],
                                           axis=1).reshape(kpe_out.shape)
    sp = _row_sums(kpe * kpe)
    g_nope = kng_ref[:, :A_NOPE]
    g_rope = kng_ref[:, A_NOPE:]
    for h in range(A_HEADS):
        a = kn[:, h * A_NOPE:(h + 1) * A_NOPE]
        r = lax.rsqrt((_row_sums(a * a) + sp) * inv_qk + EPS)
        k_out[:, h * A_QK_PAD:h * A_QK_PAD + A_NOPE] = (a * r * g_nope).astype(BF16)
        k_out[:, h * A_QK_PAD + A_NOPE:(h + 1) * A_QK_PAD] = _rope(kpe * r * g_rope, cos_t, sin_t).astype(BF16)


def _layer_slab_spec(tm, layer, width):
    n_ctx = CTX_ROWS // tm
    return pl.BlockSpec((tm // SEQ, None, SEQ, width), lambda i: (jnp.minimum(i, n_ctx - 1), layer, 0, 0))


def _call_with_carried(kernel, n_plain_out, carried, **kw):
    def run(*args, in_specs):
        args = list(args)
        in_specs = list(in_specs)
        n_real = len(args)
        aliases = {}
        for idx, arr in enumerate(carried):
            if arr is not None:
                aliases[len(args)] = n_plain_out + idx
                args.append(arr)
                in_specs.append(pl.BlockSpec(memory_space=pl.ANY))
        n_in = len(args)

        def body(*refs):
            kernel(*refs[:n_real], *refs[n_in:])

        return pl.pallas_call(body, in_specs=in_specs, input_output_aliases=aliases, **kw)(*args)
    return run


def _amid_tokens(y, wq, wk, wv, qlg, kvlg, qng, kng, cos_t, sin_t, layer, ckv_new, kpe_new):
    tm = TM_MID
    const = lambda i: (0, 0)
    rope_map = lambda i: (_rope_block_of_tile(i, tm), 0)
    n_ab = (DEPTH + 1) // 2
    run = _call_with_carried(
        functools.partial(_amid_kernel, do_q=True, norm_kv=True), 3, [ckv_new, kpe_new],
        grid=(ROWS // tm,),
        out_specs=[pl.BlockSpec((tm, A_HEADS * A_QK_PAD), lambda i: (i, 0)),
                   pl.BlockSpec((tm, A_HEADS * A_QK_PAD), lambda i: (i, 0)),
                   pl.BlockSpec((tm, A_HEADS * A_VDIM), lambda i: (i, 0)),
                   _layer_slab_spec(tm, layer, KV_LORA), _layer_slab_spec(tm, layer, A_ROPE)],
        out_shape=[jax.ShapeDtypeStruct((ROWS, A_HEADS * A_QK_PAD), BF16),
                   jax.ShapeDtypeStruct((ROWS, A_HEADS * A_QK_PAD), BF16),
                   jax.ShapeDtypeStruct((ROWS, A_HEADS * A_VDIM), BF16),
                   jax.ShapeDtypeStruct((BATCH, n_ab, SEQ, KV_LORA), F32),
                   jax.ShapeDtypeStruct((BATCH, n_ab, SEQ, A_ROPE), F32)],
        compiler_params=_cparams(("arbitrary",)),
        name="mla_qkv_prep")
    return run(y, y, y, wq, wk, wv, qlg, kvlg, qng, kng, cos_t, sin_t,
               in_specs=[pl.BlockSpec((tm, Q_LORA), lambda i: (i, AB_QLAT // Q_LORA)),
                         pl.BlockSpec((tm, KV_LORA), lambda i: (i, AB_KVLAT // KV_LORA)),
                         pl.BlockSpec((tm, LANES), lambda i: (i, AB_KPE // LANES)),
                         pl.BlockSpec(wq.shape, const), pl.BlockSpec(wk.shape, const), pl.BlockSpec(wv.shape, const),
                         pl.BlockSpec(qlg.shape, const), pl.BlockSpec(kvlg.shape, const),
                         pl.BlockSpec(qng.shape, const), pl.BlockSpec(kng.shape, const),
                         pl.BlockSpec((tm, LANES), rope_map), pl.BlockSpec((tm, LANES), rope_map)])


def _amid_cache(ckv, kpe, wk, wv, kvlg, kng, cos_t, sin_t):
    rows = ckv.shape[0]
    tm = TM_MID
    const = lambda i: (0, 0)
    outs = pl.pallas_call(
        functools.partial(_amid_kernel, do_q=False, norm_kv=False),
        grid=(rows // tm,),
        in_specs=[pl.BlockSpec((tm, KV_LORA), lambda i: (i, 0)),
                  pl.BlockSpec((tm, LANES), lambda i: (i, 0)),
                  pl.BlockSpec(wk.shape, const), pl.BlockSpec(wv.shape, const),
                  pl.BlockSpec(kvlg.shape, const), pl.BlockSpec(kng.shape, const),
                  pl.BlockSpec((tm, LANES), const), pl.BlockSpec((tm, LANES), const)],
        out_specs=[pl.BlockSpec((tm, A_HEADS * A_QK_PAD), lambda i: (i, 0)),
                   pl.BlockSpec((tm, A_HEADS * A_VDIM), lambda i: (i, 0))],
        out_shape=[jax.ShapeDtypeStruct((rows, A_HEADS * A_QK_PAD), BF16),
                   jax.ShapeDtypeStruct((rows, A_HEADS * A_VDIM), BF16)],
        compiler_params=_cparams(("parallel",)),
        name="mla_cache_kv_prep",
    )(ckv, kpe, wk, wv, kvlg, kng, cos_t, sin_t)
    return outs[0], outs[1]


def _scores(q, srcs):
    zs = []
    for k, _, bias in srcs:
        z = _dot_nt(q, k)
        if bias is not None:
            z = (z.reshape(z.shape[0] // bias.shape[0], *bias.shape) + bias).reshape(z.shape)
        zs.append(z)
    return zs


def _softmax_pv(zs, srcs, sink_z=None):
    tile_max = None
    for z in zs:
        for j in range(z.shape[1] // LANES):
            blk = z[:, j * LANES:(j + 1) * LANES]
            tile_max = blk if tile_max is None else jnp.maximum(tile_max, blk)
    m = tile_max.max(axis=-1, keepdims=True)
    if sink_z is not None:
        m = jnp.maximum(m, sink_z)
    acc = None
    for z, (_, v, _) in zip(zs, srcs):
        o = _dot(jnp.exp2(z - m).astype(BF16), jnp.concatenate([v, jnp.ones_like(v)], axis=1))
        acc = o if acc is None else acc + o
    dv = acc.shape[1] // 2
    den = acc[:, dv:]
    if sink_z is not None:
        den = den + jnp.exp2(sink_z - m)
    return acc[:, :dv] / den


def _attend_streams(n, q_of, srcs_of, sink_of=None):
    outs = []
    zs = _scores(q_of(0), srcs_of(0))
    for t in range(n):
        nxt = _scores(q_of(t + 1), srcs_of(t + 1)) if t + 1 < n else None
        outs.append(_softmax_pv(zs, srcs_of(t), None if sink_of is None else sink_of(t)))
        zs = nxt
    return outs


def _attn_a_kernel(*refs, n_src, heads):
    q_ref = refs[0]
    k_refs = refs[1:1 + n_src]
    v_refs = refs[1 + n_src:1 + 2 * n_src]
    gate_ref, o_ref = refs[1 + 2 * n_src:]
    sub = min(ATTN_SUB, q_ref.shape[0])
    streams = [(h, r) for h in range(heads) for r in range(q_ref.shape[0] // sub)]

    def rows(t):
        return slice(streams[t][1] * sub, (streams[t][1] + 1) * sub)

    def q_of(t):
        h = streams[t][0]
        return q_ref[rows(t), h * A_QK_PAD:(h + 1) * A_QK_PAD]

    def srcs_of(t):
        h = streams[t][0]
        return [(k_ref[:, h * A_QK_PAD:(h + 1) * A_QK_PAD], v_ref[:, h * A_VDIM:(h + 1) * A_VDIM], None)
                for k_ref, v_ref in zip(k_refs, v_refs)]

    outs = _attend_streams(len(streams), q_of, srcs_of)
    for t, o in enumerate(outs):
        cols = slice(streams[t][0] * A_VDIM, (streams[t][0] + 1) * A_VDIM)
        o_ref[rows(t), cols] = (o * _silu(gate_ref[rows(t), cols])).astype(o_ref.dtype)


def _attn_a(q, k, v, y, mix, kc=None, vc=None):
    latent = kc is not None
    if latent:
        nb, t, tq, hp, row0 = DEC_BATCH, DEC_SEQ, TQ_A, 1, CTX_ROWS
    else:
        nb, t, tq, hp, row0 = BATCH, SEQ, SEQ, A_HEADS, 0
    nq = t // tq
    qrow = lambda b, h, i: row0 // tq + b * nq + i
    in_specs = [pl.BlockSpec((tq, hp * A_QK_PAD), lambda b, h, i: (qrow(b, h, i), h))]
    args = [q]
    if latent:
        in_specs.append(pl.BlockSpec((PAST_LEN, hp * A_QK_PAD), lambda b, h, i: (b, h)))
        args.append(kc)
    in_specs.append(pl.BlockSpec((t, hp * A_QK_PAD), lambda b, h, i: (row0 // t + b, h)))
    args.append(k)
    if latent:
        in_specs.append(pl.BlockSpec((PAST_LEN, hp * A_VDIM), lambda b, h, i: (b, h)))
        args.append(vc)
    in_specs.append(pl.BlockSpec((t, hp * A_VDIM), lambda b, h, i: (row0 // t + b, h)))
    args.append(v)
    in_specs.append(pl.BlockSpec((tq, hp * A_VDIM), lambda b, h, i: (qrow(b, h, i), AB_AGATE // (hp * A_VDIM) + h)))
    args.append(y)
    run = _call_with_carried(
        functools.partial(_attn_a_kernel, n_src=2 if latent else 1, heads=hp), 0, [mix],
        grid=(nb, A_HEADS // hp, nq),
        out_specs=[pl.BlockSpec((tq, hp * A_VDIM), lambda b, h, i: (qrow(b, h, i), h))],
        out_shape=[_MIX_SHAPE],
        compiler_params=_cparams(("parallel", "parallel", "parallel")),
        name="mla_attention_latent" if latent else "mla_attention_context")
    return run(*args, in_specs=in_specs)[0]


def _hgrn_constants():
    c, nl = HG_CHUNK, HG_LEVELS
    t = np.arange(c)[:, None]
    u = np.arange(c)[None, :]
    tri_f = (u <= t).astype(np.float32)
    mask_f = np.zeros((nl, c, c), np.float32)
    coef_f = np.zeros((nl, c, LANES), np.float32)
    for l in range(nl):
        half = c >> (l + 1)
        seg = 2 * half
        mask_f[l] = ((u // seg) == (t // seg)) & ((t % seg) >= half) & ((u % seg) < half)
        later = np.broadcast_to((t % seg) >= half, (c, LANES))
        coef_f[l] = np.where(later, 1.0, -1.0 if half > 1 else 0.0)
    tri_b = tri_f[::-1, ::-1]
    mask_b = mask_f[:, ::-1, ::-1]
    coef_b = coef_f[:, ::-1, :]
    to_tri = lambda a: jnp.asarray(np.concatenate([a, a, a], axis=1), BF16)
    to_f32 = lambda a: jnp.asarray(np.ascontiguousarray(a), F32)
    return (to_tri(tri_f), to_tri(tri_b)), (to_f32(mask_f), to_f32(mask_b)), (to_f32(coef_f), to_f32(coef_b))


def _hgrn_decays(x, lb, tri3):
    kk = (1.0 - lb) * _sigmoid(-x)
    lf = jnp.log2(1.0 - kk)
    hi = lf.astype(BF16)
    r1 = lf - hi.astype(F32)
    mid = r1.astype(BF16)
    lo = (r1 - mid.astype(F32)).astype(BF16)
    return kk, lf, _dot(tri3, jnp.concatenate([hi, mid, lo], axis=0))


def _hgrn_level_arg(l, cs, lf, coef, forward):
    c = HG_CHUNK
    nv = c // 8
    half = c >> (l + 1)
    if half == 1:
        return lf * coef
    cs3 = cs.reshape(nv, 8, LANES)

    def in_vreg_row(r):
        return jnp.broadcast_to(cs3[:, r:r + 1, :], (nv, 8, LANES))

    if half >= 8:
        m = half // 8
        nseg = nv // (2 * m)
        edge = cs3[:, 7:8, :] if forward else cs3[:, 0:1, :]
        e4 = edge.reshape(nseg, 2 * m, 1, LANES)
        a = e4[:, m - 1:m] if forward else e4[:, m:m + 1]
        anchor = jnp.broadcast_to(a, (nseg, 2 * m, 8, LANES)).reshape(c, LANES)
    elif half == 4:
        anchor = in_vreg_row(3 if forward else 4).reshape(c, LANES)
    else:
        r0, r1 = (1, 5) if forward else (2, 6)
        sub = lax.broadcasted_iota(jnp.int32, (nv, 8, LANES), 1)
        anchor = jnp.where(sub < 4, in_vreg_row(r0), in_vreg_row(r1)).reshape(c, LANES)
    return (cs - anchor) * coef


def _hgrn_kernel(*refs, t_len, heads, zero_init, emit_state):
    c = HG_CHUNK
    n_chunks = t_len // c
    it = iter(refs)
    bq_ref, ff_ref, fb_ref, vi_ref, bg_ref, lb_ref, hg_ref = (next(it) for _ in range(7))
    if not zero_init:
        s0f_ref, s0b_ref = next(it), next(it)
    trif_ref, trib_ref, maskf_ref, maskb_ref, coeff_ref, coefb_ref = (next(it) for _ in range(6))
    o_ref = next(it)
    if emit_state:
        sf_ref, sb_ref = next(it), next(it)
    of_ref, ob_ref, stf_ref, stb_ref = (next(it) for _ in range(4))

    for h in range(heads):
        if zero_init:
            stf_ref[h] = jnp.zeros((B_DV, B_DK), F32)
            stb_ref[h] = jnp.zeros((B_DV, B_DK), F32)
        else:
            stf_ref[h] = s0f_ref[h].T
            stb_ref[h] = s0b_ref[h].T
    lb = lb_ref[...]

    nl = HG_LEVELS
    unroll = min(HG_UNROLL // heads, n_chunks)
    dirs = ((ff_ref, 0, trif_ref, maskf_ref, coeff_ref, stf_ref, of_ref, True),
            (fb_ref, 1, trib_ref, maskb_ref, coefb_ref, stb_ref, ob_ref, False))

    def body(i, carry):
        chains = []
        for h in range(heads):
            cols = slice(h * LANES, (h + 1) * LANES)
            for f_ref, lb_row, tri_ref, mask_ref, coef_ref, st_ref, out_ref, forward in dirs:
                for u in range(unroll):
                    k = i * unroll + u
                    r0 = pl.multiple_of((k if forward else n_chunks - 1 - k) * c, c)
                    chains.append(dict(rows=pl.ds(r0, c), cols=cols, head=h, f_ref=f_ref,
                                       lb=lb[lb_row:lb_row + 1, cols], tri_ref=tri_ref, mask_ref=mask_ref,
                                       coef_ref=coef_ref, st_ref=st_ref, out_ref=out_ref, forward=forward))
        for ch in chains:
            ch["kk"], ch["lf"], ch["cs"] = _hgrn_decays(ch["f_ref"][ch["rows"], ch["cols"]], ch["lb"],
                                                        ch["tri_ref"][...])
            ch["q"] = _silu(bq_ref[ch["rows"], ch["cols"]])
            ch["v"] = vi_ref[ch["rows"], ch["cols"]]
            ch["q16"] = ch["q"].astype(BF16)
            ch["kk16"] = ch["kk"].astype(BF16)
            ch["sc"] = jnp.zeros((c, c), F32)
        for l in range(nl):
            for ch in chains:
                el = jnp.exp2(_hgrn_level_arg(l, ch["cs"], ch["lf"], ch["coef_ref"][l], ch["forward"])).astype(BF16)
                ch["sc"] = ch["sc"] + ch["mask_ref"][l] * _dot_nt(ch["q16"] * el, ch["kk16"] * el)
        for ch in chains:
            cs, q, kk, v = ch["cs"], ch["q"], ch["kk"], ch["v"]
            end = cs[c - 1:c, :] if ch["forward"] else cs[0:1, :]
            ch["end"] = end
            ch["q_in"] = (q * jnp.exp2(cs)).astype(BF16)
            ch["upd"] = _dot(v.T.astype(BF16), (kk * jnp.exp2(end - cs)).astype(BF16))
            ch["o"] = _dot(ch["sc"].astype(BF16), v.astype(BF16)) + jnp.sum(q * kk, axis=-1, keepdims=True) * v
        for ch in chains:
            st = ch["st_ref"][ch["head"]]
            ch["out_ref"][ch["rows"], ch["cols"]] = ch["o"] + _dot_nt(ch["q_in"], st.astype(BF16))
            ch["st_ref"][ch["head"]] = jnp.exp2(ch["end"]) * st + ch["upd"]
        return carry

    lax.fori_loop(0, n_chunks // unroll, body, 0)
    if emit_state:
        for h in range(heads):
            sf_ref[h] = stf_ref[h].T
            sb_ref[h] = stb_ref[h].T

    hg = hg_ref[...]
    blk = min(HG_UNROLL, n_chunks) * c

    def finish(i, carry):
        rows = pl.ds(pl.multiple_of(i * blk, blk), blk)
        for h in range(heads):
            cols = slice(h * LANES, (h + 1) * LANES)
            o = of_ref[rows, cols] + ob_ref[rows, cols]
            o = o * lax.rsqrt(jnp.mean(o * o, axis=-1, keepdims=True) + EPS) * hg
            o_ref[rows, cols] = (o * _silu(bg_ref[rows, cols])).astype(o_ref.dtype)
        return carry

    lax.fori_loop(0, t_len // blk, finish, 0)


def _hgrn(y, lb, hg, mix, consts, s0f=None, s0b=None, layer=None, sf_new=None, sb_new=None):
    latent = s0f is not None
    if latent:
        nb, t, row0, hp = DEC_BATCH, DEC_SEQ, CTX_ROWS, 1
    else:
        nb, t, row0, hp = BATCH, SEQ, 0, 2
    w = hp * LANES
    rb = lambda b: row0 // t + b
    col = lambda off: (lambda b, h: (rb(b), off // w + h))
    const2 = lambda b, h: (0, 0)
    const3 = lambda b, h: (0, 0, 0)
    in_specs = [pl.BlockSpec((t, w), col(AB_BQ)), pl.BlockSpec((t, w), col(AB_BFF)),
                pl.BlockSpec((t, w), col(AB_BFB)), pl.BlockSpec((t, w), col(AB_BI)),
                pl.BlockSpec((t, w), col(AB_BGATE)),
                pl.BlockSpec((2, w), lambda b, h: (0, h)),
                pl.BlockSpec((1, LANES), const2)]
    args = [y, y, y, y, y, lb, hg]
    if latent:
        st_spec = pl.BlockSpec((None, hp, B_DK, B_DV), lambda b, h: (b, h, 0, 0))
        in_specs += [st_spec, st_spec]
        args += [s0f, s0b]
    tris, masks, coefs = consts
    in_specs += ([pl.BlockSpec(a.shape, const2) for a in tris] + [pl.BlockSpec(a.shape, const3) for a in masks]
                 + [pl.BlockSpec(a.shape, const3) for a in coefs])
    args += [*tris, *masks, *coefs]
    out_specs = [pl.BlockSpec((t, w), lambda b, h: (rb(b), A_HEADS * A_VDIM // w + h))]
    out_shape = [_MIX_SHAPE]
    carried = [mix]
    if not latent:
        st_out = pl.BlockSpec((None, None, hp, B_DK, B_DV), lambda b, h: (b, layer, h, 0, 0))
        out_specs += [st_out, st_out]
        out_shape += [jax.ShapeDtypeStruct((nb, (DEPTH + 1) // 2, B_HEADS, B_DK, B_DV), F32)] * 2
        carried += [sf_new, sb_new]
    run = _call_with_carried(
        functools.partial(_hgrn_kernel, t_len=t, heads=hp, zero_init=not latent, emit_state=not latent), 0, carried,
        grid=(nb, B_HEADS // hp),
        out_specs=out_specs,
        out_shape=out_shape,
        scratch_shapes=[pltpu.VMEM((t, w), F32), pltpu.VMEM((t, w), F32),
                        pltpu.VMEM((hp, B_DV, B_DK), F32), pltpu.VMEM((hp, B_DV, B_DK), F32)],
        compiler_params=_cparams(("parallel", "parallel")),
        name="hgrn2_latent" if latent else "hgrn2_context")
    return run(*args, in_specs=in_specs)


def _cmid_kernel(q_ref, k_ref, v_ref, qg_ref, kg_ref, cos_ref, sin_ref, q_out, k_out, v_out, kc_out, vc_out):
    cos_t = cos_ref[...]
    sin_t = sin_ref[...]
    qg = qg_ref[...]
    kg = kg_ref[...]
    in_ctx = pl.program_id(0) < CTX_ROWS // TM_MID

    def norm(x, g):
        return x * lax.rsqrt(_row_sums(x * x) * (1.0 / C_HEAD_DIM) + EPS) * g

    def rope(x):
        return _rope(x, cos_t, sin_t)

    for h in range(C_HEADS):
        sl = slice(h * C_HEAD_DIM, (h + 1) * C_HEAD_DIM)
        q_out[:, sl] = rope(norm(q_ref[:, sl], qg)).astype(BF16)
    for h in range(C_KV_HEADS):
        sl = slice(h * C_HEAD_DIM, (h + 1) * C_HEAD_DIM)
        kn = norm(k_ref[:, sl], kg)
        k_out[:, sl] = rope(kn).astype(BF16)

        @pl.when(in_ctx)
        def _():
            kc_out[:, :, sl] = kn.reshape(kc_out.shape[0], SEQ, C_HEAD_DIM)
    v = v_ref[...]
    v_out[...] = v.astype(BF16)

    @pl.when(in_ctx)
    def _():
        vc_out[...] = v.reshape(vc_out.shape)


def _cmid(y, qg, kg, cos_t, sin_t, layer, kc_new, vc_new):
    tm = TM_MID
    const = lambda i: (0, 0)
    rope_map = lambda i: (_rope_block_of_tile(i, tm), 0)
    cache = jax.ShapeDtypeStruct((BATCH, DEPTH // 2, SEQ, C_KV_WIDTH), F32)
    run = _call_with_carried(
        _cmid_kernel, 3, [kc_new, vc_new],
        grid=(ROWS // tm,),
        out_specs=[pl.BlockSpec((tm, C_WIDTH), lambda i: (i, 0)),
                   pl.BlockSpec((tm, C_KV_WIDTH), lambda i: (i, 0)),
                   pl.BlockSpec((tm, C_KV_WIDTH), lambda i: (i, 0)),
                   _layer_slab_spec(tm, layer, C_KV_WIDTH), _layer_slab_spec(tm, layer, C_KV_WIDTH)],
        out_shape=[jax.ShapeDtypeStruct((ROWS, C_WIDTH), BF16),
                   jax.ShapeDtypeStruct((ROWS, C_KV_WIDTH), BF16),
                   jax.ShapeDtypeStruct((ROWS, C_KV_WIDTH), BF16), cache, cache],
        compiler_params=_cparams(("arbitrary",)),
        name="gqa_qkv_prep")
    return run(y, y, y, qg, kg, cos_t, sin_t,
               in_specs=[pl.BlockSpec((tm, C_WIDTH), lambda i: (i, C_Q // C_WIDTH)),
                         pl.BlockSpec((tm, C_KV_WIDTH), lambda i: (i, C_K // C_KV_WIDTH)),
                         pl.BlockSpec((tm, C_KV_WIDTH), lambda i: (i, C_V // C_KV_WIDTH)),
                         pl.BlockSpec((1, C_HEAD_DIM), const), pl.BlockSpec((1, C_HEAD_DIM), const),
                         pl.BlockSpec((tm, LANES), rope_map), pl.BlockSpec((tm, LANES), rope_map)])


def _band_start(i, tq, t_len):
    return pl.multiple_of(jnp.clip(i * tq - WINDOW, 0, t_len - (tq + 2 * WINDOW)), WINDOW)


def _band_bias(tq, t_len):
    width = tq + 2 * WINDOW
    i = np.arange(t_len // tq)[:, None, None]
    qpos = i * tq + np.arange(tq)[None, :, None]
    kpos = np.clip(i * tq - WINDOW, 0, t_len - width) + np.arange(width)[None, None, :]
    return jnp.asarray(np.where(np.abs(kpos - qpos) <= WINDOW, 0.0, NEG_BIG), F32)


def _attn_c_kernel(*refs, band, tq, t_len, groups, stack):
    if band:
        q_ref, kc_ref, vc_ref, kl_ref, vl_ref, bias_ref, sink_ref, gate_ref, o_ref = refs
    else:
        q_ref, kc_ref, vc_ref, sink_ref, gate_ref, o_ref = refs
    hd = C_HEAD_DIM
    streams = [(g, s) for g in range(groups) for s in range(C_GROUP // stack)]
    if band:
        width = tq + 2 * WINDOW
        start = _band_start(pl.program_id(2), tq, t_len)
        band_bias = bias_ref[...]

    def heads_of(t):
        g, s = streams[t]
        return [g * C_GROUP + s * stack + r for r in range(stack)]

    def q_of(t):
        return jnp.concatenate([q_ref[:, h * hd:(h + 1) * hd] for h in heads_of(t)], axis=0)

    def srcs_of(t):
        g = streams[t][0]
        cols = slice(g * hd, (g + 1) * hd)
        srcs = [(kc_ref[:, cols], vc_ref[:, cols], None)]
        if band:
            srcs.append((kl_ref[pl.ds(start, width), cols], vl_ref[pl.ds(start, width), cols], band_bias))
        return srcs

    def sink_of(t):
        return jnp.concatenate([jnp.broadcast_to(sink_ref[h][:, :1] * LOG2_E, (tq, 1)) for h in heads_of(t)], axis=0)

    outs = _attend_streams(len(streams), q_of, srcs_of, sink_of)
    for t, o in enumerate(outs):
        for r, h in enumerate(heads_of(t)):
            cols = slice(h * hd, (h + 1) * hd)
            o_ref[:, cols] = (o[r * tq:(r + 1) * tq] * _silu(gate_ref[:, cols])).astype(o_ref.dtype)


def _attn_c(q, k, v, y, sink, mix, kc=None, vc=None):
    latent = kc is not None
    if latent:
        nb, t, tq, gp, stack, row0 = DEC_BATCH, DEC_SEQ, TQ_C, 1, 2, CTX_ROWS
    else:
        nb, t, tq, gp, stack, row0 = BATCH, SEQ, SEQ, C_KV_HEADS, C_GROUP, 0
    gw = gp * C_GROUP * C_HEAD_DIM
    kvw = gp * C_HEAD_DIM
    nq = t // tq
    qrow = lambda b, g, i: row0 // tq + b * nq + i
    own_kv = pl.BlockSpec((t, kvw), lambda b, g, i: (row0 // t + b, g))
    in_specs = [pl.BlockSpec((tq, gw), lambda b, g, i: (qrow(b, g, i), g))]
    args = [q]
    if latent:
        ctx_kv = pl.BlockSpec((PAST_LEN, kvw), lambda b, g, i: (b, g))
        in_specs += [ctx_kv, ctx_kv, own_kv, own_kv,
                     pl.BlockSpec((None, tq, tq + 2 * WINDOW), lambda b, g, i: (i, 0, 0))]
        args += [kc, vc, k, v, _band_bias(tq, t)]
    else:
        in_specs += [own_kv, own_kv]
        args += [k, v]
    in_specs.append(pl.BlockSpec((gp * C_GROUP, 1, LANES), lambda b, g, i: (g, 0, 0)))
    args.append(sink)
    in_specs.append(pl.BlockSpec((tq, gw), lambda b, g, i: (qrow(b, g, i), C_GATE // gw + g)))
    args.append(y)
    run = _call_with_carried(
        functools.partial(_attn_c_kernel, band=latent, tq=tq, t_len=t, groups=gp, stack=stack), 0, [mix],
        grid=(nb, C_KV_HEADS // gp, nq),
        out_specs=[pl.BlockSpec((tq, gw), lambda b, g, i: (qrow(b, g, i), g))],
        out_shape=[_MIX_SHAPE],
        compiler_params=_cparams(("parallel", "parallel", "parallel")),
        name="gqa_attention_latent" if latent else "gqa_attention_context")
    return run(*args, in_specs=in_specs)[0]


def _axial_angles(n_tokens, rot_dim):
    rows = n_tokens // GRID_W
    row = jnp.repeat(jnp.arange(rows, dtype=F32), GRID_W)
    col = jnp.tile(jnp.arange(GRID_W, dtype=F32), rows)
    n_freq = rot_dim // 4
    inv = ROPE_BASE ** (-jnp.arange(n_freq, dtype=F32) / n_freq)
    return jnp.concatenate([row[:, None] * inv, col[:, None] * inv], axis=-1)


def _spread_halves(a, fill=0.0):
    h = a.shape[-1] // 2
    pad = jnp.full(a.shape[:-1] + (LANES // 2 - h,), fill, a.dtype)
    return jnp.concatenate([a[..., :h], pad, a[..., h:], pad], axis=-1)


def _spread_head(a):
    return jnp.concatenate([a[..., :A_NOPE], _spread_halves(a[..., A_NOPE:])], axis=-1)


def _rope_tables(rot_dim, tm):
    ang = _axial_angles(DEC_SEQ, rot_dim)
    cos, sin = jnp.cos(ang), jnp.sin(ang)
    cos_t = _spread_halves(jnp.concatenate([cos, cos], axis=-1), 1.0)
    sin_t = _spread_halves(jnp.concatenate([-sin, sin], axis=-1))
    cos_t = jnp.concatenate([jnp.ones((tm, LANES), F32), cos_t], axis=0)
    sin_t = jnp.concatenate([jnp.zeros((tm, LANES), F32), sin_t], axis=0)
    return cos_t, sin_t


def _lower_bounds(lb_logits):
    p = jax.nn.softmax(lb_logits.astype(F32), axis=0)
    return jnp.cumsum(p, axis=0) - p[0:1]


def kernel(x_prompt, x_sample, cache_ckv, cache_kpe, state_hgrn_fwd, state_hgrn_bwd, cache_k_c, cache_v_c, c, c_ctx,
           mod_w_ab, mod_b_ab, norm_ab, w_in_ab, q_lora_norm, kv_lora_norm, w_q_up, w_kv_up, q_norm_ab, k_norm_ab,
           hgrn_lb_logits, hgrn_out_norm, w_out_ab, mod_w_c, mod_b_c, norm_c, w_in_c, q_norm_c, k_norm_c, sink_c,
           w_out_c):
    x_parts = [x_prompt.reshape(CTX_ROWS, D_MODEL), x_sample.reshape(LAT_ROWS, D_MODEL)]
    cond8 = jnp.concatenate([c_ctx[None, :], c, jnp.zeros((N_COND - 1 - DEC_BATCH, D_MODEL), F32)], axis=0)
    mods_ab = _modulation(cond8, mod_w_ab, mod_b_ab)
    mods_c = _modulation(cond8, mod_w_c, mod_b_c)
    lower = _lower_bounds(hgrn_lb_logits)
    cos_a, sin_a = _rope_tables(A_ROPE, TM_MID)
    cos_c, sin_c = _rope_tables(C_HEAD_DIM, TM_MID)
    hg_consts = _hgrn_constants()
    w_in_ab16 = _prep_w_in_ab(jnp.swapaxes(w_in_ab, 1, 2))
    w_in_c16, w_out_ab16, w_out_c16 = w_in_c.astype(BF16), w_out_ab.astype(BF16), w_out_c.astype(BF16)

    ckv_new = kpe_new = sf_new = sb_new = kc_new = vc_new = None
    for layer in range(DEPTH):
        j = layer // 2
        last = layer == DEPTH - 1
        if layer % 2 == 0:
            mod = mods_ab[j].reshape(3 * N_COND, 1, D_MODEL)
            wq = _spread_head(w_q_up[j].reshape(Q_LORA, A_HEADS, A_QK)).reshape(Q_LORA, A_HEADS * A_QK_PAD).astype(BF16)
            wkv = w_kv_up[j].reshape(KV_LORA, A_HEADS, A_NOPE + A_VDIM)
            wk = wkv[:, :, :A_NOPE].reshape(KV_LORA, A_HEADS * A_NOPE).astype(BF16)
            wv = wkv[:, :, A_NOPE:].reshape(KV_LORA, A_HEADS * A_VDIM).astype(BF16)
            qlg, kvlg = q_lora_norm[j][None, :], kv_lora_norm[j][None, :]
            qng = _spread_head(q_norm_ab[j] * (A_QK ** -0.5 * LOG2_E))[None, :]
            kng = _spread_head(k_norm_ab[j])[None, :]

            y = _in_proj(x_parts, norm_ab[j][None, :], mod, w_in_ab16, j, w_transposed=True)
            q, k, v, ckv_new, kpe_new = _amid_tokens(y, wq, wk, wv, qlg, kvlg, qng, kng, cos_a, sin_a,
                                                     j, ckv_new, kpe_new)
            kpe_cache = _spread_halves(cache_kpe[:, j].reshape(DEC_BATCH * PAST_LEN, A_ROPE))
            kc, vc = _amid_cache(cache_ckv[:, j].reshape(DEC_BATCH * PAST_LEN, KV_LORA), kpe_cache, wk, wv, kvlg, kng,
                                 cos_a, sin_a)
            mix = _attn_a(q, k, v, y, None)
            mix = _attn_a(q, k, v, y, mix, kc=kc, vc=vc)
            hg = hgrn_out_norm[j][None, :]
            mix, sf_new, sb_new = _hgrn(y, lower[j], hg, mix, hg_consts, layer=j, sf_new=sf_new, sb_new=sb_new)
            (mix,) = _hgrn(y, lower[j], hg, mix, hg_consts, s0f=state_hgrn_fwd[:, j], s0b=state_hgrn_bwd[:, j])
            x_parts = _out_proj(mix, w_out_ab16, j, x_parts, mod, split_out=last)
        else:
            mod = mods_c[j].reshape(3 * N_COND, 1, D_MODEL)
            y = _in_proj(x_parts, norm_c[j][None, :], mod, w_in_c16, j,
                         col_map=lambda t: jnp.where(t < 2, t, jnp.where(t < 4, t + 1, 2)))
            q, k, v, kc_new, vc_new = _cmid(y, q_norm_c[j][None, :] * (C_HEAD_DIM ** -0.5 * LOG2_E),
                                            k_norm_c[j][None, :], cos_c, sin_c,
                                            j, kc_new, vc_new)
            kc = cache_k_c[:, j].reshape(DEC_BATCH * PAST_LEN, C_KV_WIDTH).astype(BF16)
            vc = cache_v_c[:, j].reshape(DEC_BATCH * PAST_LEN, C_KV_WIDTH).astype(BF16)
            sink = jnp.broadcast_to(sink_c[j][:, None, None], (C_HEADS, 1, LANES))
            mix = _attn_c(q, k, v, y, sink, None)
            mix = _attn_c(q, k, v, y, sink, mix, kc=kc, vc=vc)
            x_parts = _out_proj(mix, w_out_c16, j, x_parts, mod, split_out=last)

    cache_c_shape = (BATCH, DEPTH // 2, SEQ, C_KV_HEADS, C_HEAD_DIM)
    return (x_parts[0].reshape(BATCH, SEQ, D_MODEL), x_parts[1].reshape(DEC_BATCH, DEC_SEQ, D_MODEL),
            ckv_new, kpe_new, sf_new, sb_new, kc_new.reshape(cache_c_shape), vc_new.reshape(cache_c_shape))
```

```python
import functools

import numpy as np
import jax
import jax.numpy as jnp
from jax import lax
from jax.experimental import pallas as pl
from jax.experimental.pallas import tpu as pltpu

F32 = jnp.float32
BF16 = jnp.bfloat16

D_MODEL = 2048
BATCH = 16
SEQ = 256
DEPTH = 4
DEC_BATCH = 4
DEC_SEQ = 2048
PAST_LEN = 256
GRID_W = 64
A_HEADS = 8
A_NOPE = 128
A_ROPE = 64
A_VDIM = 128
A_QK = A_NOPE + A_ROPE
A_QK_PAD = 256
Q_LORA = 512
KV_LORA = 256
B_HEADS = 8
B_DK = 128
B_DV = 128
C_HEADS = 16
C_KV_HEADS = 4
C_GROUP = C_HEADS // C_KV_HEADS
C_HEAD_DIM = 128
C_WIDTH = C_HEADS * C_HEAD_DIM
C_KV_WIDTH = C_KV_HEADS * C_HEAD_DIM
WINDOW = 128
ROPE_BASE = 10000.0
EPS = 1e-6
NEG_BIG = -1e30
LOG2_E = 1.4426950408889634

LANES = 128
CTX_ROWS = BATCH * SEQ
LAT_ROWS = DEC_BATCH * DEC_SEQ
ROWS = CTX_ROWS + LAT_ROWS
N_COND = 8

AB_AGATE = 0
AB_BQ = 1024
AB_BFF = 2048
AB_BFB = 3072
AB_BI = 4096
AB_BGATE = 5120
AB_QLAT = 6144
AB_KVLAT = 6656
AB_KPE = 6912
AB_N = 7168
C_Q = 0
C_GATE = 2048
C_K = 4096
C_V = 4608
C_N = 5120

TM_PROJ = 1024
TN_IN = 1024
TM_OUT = 512
TM_MID = 512
TQ_A = 1024
TQ_C = 256
ATTN_SUB = 256
HG_CHUNK = 128
HG_LEVELS = 7
HG_UNROLL = 4
VMEM_LIMIT = 56 * 1024 * 1024
_MIX_SHAPE = jax.ShapeDtypeStruct((ROWS, D_MODEL), BF16)


def _cparams(sem):
    return pltpu.CompilerParams(dimension_semantics=sem, vmem_limit_bytes=VMEM_LIMIT)


def _sigmoid(x):
    return 0.5 * jnp.tanh(0.5 * x) + 0.5


def _silu(x):
    return x * _sigmoid(x)


def _dot(a, b):
    return jnp.dot(a, b, preferred_element_type=F32)


def _dot_nt(a, b):
    return lax.dot_general(a, b, (((1,), (1,)), ((), ())), preferred_element_type=F32)


def _cond_of_tile(i, tm):
    n_ctx = CTX_ROWS // tm
    per_batch = DEC_SEQ // tm
    return jnp.where(i < n_ctx, 0, 1 + (i - n_ctx) // per_batch)


def _rope_block_of_tile(i, tm):
    n_ctx = CTX_ROWS // tm
    per_batch = DEC_SEQ // tm
    return jnp.where(i < n_ctx, 0, 1 + (i - n_ctx) % per_batch)


def _mod_kernel(c_ref, w_ref, b_ref, o_ref):
    a = _silu(c_ref[...]).astype(BF16)
    o_ref[...] = _dot(a, w_ref[...].astype(BF16)) + b_ref[...]


def _modulation(cond8, w_mod, b_mod):
    n = w_mod.shape[0]
    tn = 1024
    return pl.pallas_call(
        _mod_kernel,
        grid=(n, 3 * D_MODEL // tn),
        in_specs=[pl.BlockSpec((N_COND, D_MODEL), lambda l, j: (0, 0)),
                  pl.BlockSpec((None, D_MODEL, tn), lambda l, j: (l, 0, j)),
                  pl.BlockSpec((None, 1, tn), lambda l, j: (l, 0, j))],
        out_specs=pl.BlockSpec((None, N_COND, tn), lambda l, j: (l, 0, j)),
        out_shape=jax.ShapeDtypeStruct((n, N_COND, 3 * D_MODEL), F32),
        compiler_params=_cparams(("parallel", "parallel")),
        name="adaln_mod",
    )(cond8, w_mod, b_mod.reshape(n, 1, 3 * D_MODEL))


def _prep_w_kernel(w_ref, o_ref):
    n_head = Q_LORA + KV_LORA + A_ROPE
    n_main = AB_N - TN_IN
    half = A_ROPE // 2
    x1_end = Q_LORA + KV_LORA + half
    x2_at = Q_LORA + KV_LORA + LANES // 2
    o_ref[:n_main, :] = w_ref[n_head:, :].astype(BF16)
    o_ref[n_main:, :] = jnp.zeros((TN_IN, o_ref.shape[1]), BF16)
    o_ref[n_main:n_main + x1_end, :] = w_ref[:x1_end, :].astype(BF16)
    o_ref[n_main + x2_at:n_main + x2_at + half, :] = w_ref[x1_end:n_head, :].astype(BF16)


def _prep_w_in_ab(w_t):
    n_layers, n, _ = w_t.shape
    tc = 256
    return pl.pallas_call(
        _prep_w_kernel,
        grid=(n_layers, D_MODEL // tc),
        in_specs=[pl.BlockSpec((None, n, tc), lambda l, i: (l, 0, i))],
        out_specs=pl.BlockSpec((None, AB_N, tc), lambda l, i: (l, 0, i)),
        out_shape=jax.ShapeDtypeStruct((n_layers, AB_N, D_MODEL), BF16),
        compiler_params=_cparams(("parallel", "parallel")),
        name="w_in_ab_layout",
    )(w_t)


def _row_specs(parts, tm, single_buffer=False, tile_of=lambda i, *_: i):
    if len(parts) == 1:
        return [pl.BlockSpec((tm, D_MODEL), lambda *g: (tile_of(*g), 0))]
    n_first = CTX_ROWS // tm
    mode = dict(pipeline_mode=pl.Buffered(1)) if single_buffer else {}
    return [pl.BlockSpec((tm, D_MODEL), lambda *g: (jnp.minimum(tile_of(*g), n_first - 1), 0), **mode),
            pl.BlockSpec((tm, D_MODEL), lambda *g: (jnp.maximum(tile_of(*g) - n_first, 0), 0), **mode)]


IN_SUB = 4
IN_SUB_ROWS = TM_PROJ // IN_SUB


def _norm_mod(x, g, shift, scale):
    r = lax.rsqrt(jnp.mean(x * x, axis=-1, keepdims=True) + EPS)
    return ((x * r * g) * (1.0 + scale) + shift).astype(BF16)


def _norm_tile_kernel(x_ref, g_ref, sh_ref, sc_ref, o_ref):
    o_ref[...] = _norm_mod(x_ref[...], g_ref[...], sh_ref[...], sc_ref[...])


def _norm_first_tile(x0, g, mod):
    const = lambda s: (0, 0)
    return pl.pallas_call(
        _norm_tile_kernel,
        grid=(IN_SUB,),
        in_specs=[pl.BlockSpec((IN_SUB_ROWS, D_MODEL), lambda s: (s, 0)), pl.BlockSpec((1, D_MODEL), const),
                  pl.BlockSpec((None, 1, D_MODEL), lambda s: (0, 0, 0)),
                  pl.BlockSpec((None, 1, D_MODEL), lambda s: (1, 0, 0))],
        out_specs=pl.BlockSpec((IN_SUB_ROWS, D_MODEL), lambda s: (s, 0)),
        out_shape=jax.ShapeDtypeStruct((TM_PROJ, D_MODEL), BF16),
        compiler_params=_cparams(("parallel",)),
        name="norm_mod_first_tile",
    )(x0, g, mod, mod)


def _in_next_tile(i):
    return jnp.minimum(i + 1, ROWS // TM_PROJ - 1)


def _in_next_sub_block(i, j):
    return _in_next_tile(i) * IN_SUB + jnp.clip(j - 1, 0, IN_SUB - 1)


def _in_kernel(*refs, n_x, w_transposed):
    x_refs = refs[:n_x]
    h0_ref, g_ref, sh_ref, sc_ref, w_ref, o_ref, h_ref, hn_ref = refs[n_x:]
    i, j = pl.program_id(0), pl.program_id(1)

    @pl.when((i == 0) & (j == 0))
    def _():
        h_ref[...] = h0_ref[...]

    @pl.when((i > 0) & (j == 0))
    def _():
        h_ref[...] = hn_ref[...]

    if n_x == 1:
        x = x_refs[0][...]
    else:
        x = jnp.where(_in_next_sub_block(i, j) < CTX_ROWS // IN_SUB_ROWS, x_refs[0][...], x_refs[1][...])
    r0 = pl.multiple_of(jnp.clip(j - 1, 0, IN_SUB - 1) * IN_SUB_ROWS, IN_SUB_ROWS)
    hn_ref[pl.ds(r0, IN_SUB_ROWS), :] = _norm_mod(x, g_ref[...], sh_ref[...], sc_ref[...])
    o_ref[...] = (_dot_nt if w_transposed else _dot)(h_ref[...], w_ref[...])


def _in_proj(x_parts, g, mod, w, layer, col_map=None, w_transposed=False):
    tm, tn = TM_PROJ, TN_IN
    n_tiles = (w.shape[1] if w_transposed else w.shape[2]) // tn
    assert n_tiles > IN_SUB
    col_map = col_map or (lambda t: t)
    if w_transposed:
        w_spec = pl.BlockSpec((None, tn, D_MODEL), lambda i, j: (layer, col_map(j), 0))
    else:
        w_spec = pl.BlockSpec((None, D_MODEL, tn), lambda i, j: (layer, 0, col_map(j)))
    mod_row = lambda part: (lambda i, j: (_cond_of_tile(_in_next_tile(i), tm) * 3 + part, 0, 0))
    h0 = _norm_first_tile(x_parts[0], g, mod)
    return pl.pallas_call(
        functools.partial(_in_kernel, n_x=len(x_parts), w_transposed=w_transposed),
        grid=(ROWS // tm, n_tiles),
        in_specs=_row_specs(x_parts, IN_SUB_ROWS, tile_of=_in_next_sub_block) + [
            pl.BlockSpec((tm, D_MODEL), lambda i, j: (0, 0), pipeline_mode=pl.Buffered(1)),
            pl.BlockSpec((1, D_MODEL), lambda i, j: (0, 0)),
            pl.BlockSpec((None, 1, D_MODEL), mod_row(0)),
            pl.BlockSpec((None, 1, D_MODEL), mod_row(1)), w_spec],
        out_specs=pl.BlockSpec((tm, tn), lambda i, j: (i, j)),
        out_shape=jax.ShapeDtypeStruct((ROWS, n_tiles * tn), F32),
        scratch_shapes=[pltpu.VMEM((tm, D_MODEL), BF16), pltpu.VMEM((tm, D_MODEL), BF16)],
        compiler_params=_cparams(("arbitrary", "arbitrary")),
        name="norm_mod_in_proj",
    )(*x_parts, h0, g, mod, mod, w)


def _out_kernel(*refs, n_x, n_o):
    m_ref, w_ref = refs[:2]
    x_refs = refs[2:2 + n_x]
    gt_ref = refs[2 + n_x]
    o_refs = refs[3 + n_x:]
    y = gt_ref[...] * _dot(m_ref[...], w_ref[...])

    def emit(x_ref, o_ref):
        o_ref[...] = x_ref[...] + y

    if n_x == 1 and n_o == 1:
        emit(x_refs[0], o_refs[0])
    else:
        in_first = pl.program_id(0) < CTX_ROWS // TM_OUT
        pl.when(in_first)(lambda: emit(x_refs[0], o_refs[0]))
        pl.when(jnp.logical_not(in_first))(lambda: emit(x_refs[-1], o_refs[-1]))


def _out_proj(mix, w, layer, x_parts, mod, split_out):
    tm = TM_OUT
    if split_out:
        out_parts = [jax.ShapeDtypeStruct((CTX_ROWS, D_MODEL), F32), jax.ShapeDtypeStruct((LAT_ROWS, D_MODEL), F32)]
    else:
        out_parts = [jax.ShapeDtypeStruct((ROWS, D_MODEL), F32)]
    return pl.pallas_call(
        functools.partial(_out_kernel, n_x=len(x_parts), n_o=len(out_parts)),
        grid=(ROWS // tm,),
        in_specs=[pl.BlockSpec((tm, D_MODEL), lambda i: (i, 0)),
                  pl.BlockSpec((None, D_MODEL, D_MODEL), lambda i: (layer, 0, 0))] + _row_specs(x_parts, tm) + [
                  pl.BlockSpec((None, 1, D_MODEL), lambda i: (_cond_of_tile(i, tm) * 3 + 2, 0, 0))],
        out_specs=_row_specs(out_parts, tm),
        out_shape=out_parts,
        compiler_params=_cparams(("arbitrary",)),
        name="out_proj_residual",
    )(mix, w, *x_parts, mod)


def _rope(x, cos_t, sin_t):
    return x * cos_t + pltpu.roll(x, LANES // 2, 1) * sin_t


def _row_sums(x):
    hi = x.astype(BF16)
    lo = (x - hi.astype(F32)).astype(BF16)
    return _dot(jnp.concatenate([hi, lo], axis=1), jnp.ones((2 * x.shape[1], LANES), BF16))


def _amid_kernel(*refs, do_q, norm_kv):
    if do_q:
        (ql_ref, kvl_ref, kpe_ref, wq_ref, wk_ref, wv_ref, qlg_ref, kvlg_ref, qng_ref, kng_ref, cos_ref, sin_ref,
         q_out, k_out, v_out, ckv_out, kpe_out) = refs
    else:
        (kvl_ref, kpe_ref, wk_ref, wv_ref, kvlg_ref, kng_ref, cos_ref, sin_ref, k_out, v_out) = refs
    cos_t = cos_ref[...]
    sin_t = sin_ref[...]
    inv_qk = 1.0 / A_QK

    if do_q:
        ql = ql_ref[...]
        qn = ql * lax.rsqrt(jnp.mean(ql * ql, axis=-1, keepdims=True) + EPS) * qlg_ref[...]
        qu = _dot(qn.astype(BF16), wq_ref[...])
        g_nope = qng_ref[:, :A_NOPE]
        g_rope = qng_ref[:, A_NOPE:]
        for h in range(A_HEADS):
            qh = qu[:, h * A_QK_PAD:(h + 1) * A_QK_PAD]
            r = lax.rsqrt(_row_sums(qh * qh) * inv_qk + EPS)
            q_out[:, h * A_QK_PAD:h * A_QK_PAD + A_NOPE] = (qh[:, :A_NOPE] * r * g_nope).astype(BF16)
            q_out[:, h * A_QK_PAD + A_NOPE:(h + 1) * A_QK_PAD] = _rope(qh[:, A_NOPE:] * r * g_rope,
                                                                       cos_t, sin_t).astype(BF16)

    kvl = kvl_ref[...]
    if norm_kv:
        ckv = kvl * lax.rsqrt(jnp.mean(kvl * kvl, axis=-1, keepdims=True) + EPS) * kvlg_ref[...]
    else:
        ckv = kvl
    ckv_b = ckv.astype(BF16)
    kn = _dot(ckv_b, wk_ref[...])
    v_out[...] = _dot(ckv_b, wv_ref[...]).astype(BF16)
    kpe = kpe_ref[...]
    if do_q:
        @pl.when(pl.program_id(0) < CTX_ROWS // TM_MID)
        def _():
            half = A_ROPE // 2
            ckv_out[...] = ckv.reshape(ckv_out.shape)
            kpe_out[...] = jnp.concatenate([kpe[:, :half], kpe[:, LANES // 2:LANES // 2 + half]],
                                           axis=1).reshape(kpe_out.shape)
    sp = _row_sums(kpe * kpe)
    g_nope = kng_ref[:, :A_NOPE]
    g_rope = kng_ref[:, A_NOPE:]
    for h in range(A_HEADS):
        a = kn[:, h * A_NOPE:(h + 1) * A_NOPE]
        r = lax.rsqrt((_row_sums(a * a) + sp) * inv_qk + EPS)
        k_out[:, h * A_QK_PAD:h * A_QK_PAD + A_NOPE] = (a * r * g_nope).astype(BF16)
        k_out[:, h * A_QK_PAD + A_NOPE:(h + 1) * A_QK_PAD] = _rope(kpe * r * g_rope, cos_t, sin_t).astype(BF16)


def _layer_slab_spec(tm, layer, width):
    n_ctx = CTX_ROWS // tm
    return pl.BlockSpec((tm // SEQ, None, SEQ, width), lambda i: (jnp.minimum(i, n_ctx - 1), layer, 0, 0))


def _call_with_carried(kernel, n_plain_out, carried, **kw):
    def run(*args, in_specs):
        args = list(args)
        in_specs = list(in_specs)
        n_real = len(args)
        aliases = {}
        for idx, arr in enumerate(carried):
            if arr is not None:
                aliases[len(args)] = n_plain_out + idx
                args.append(arr)
                in_specs.append(pl.BlockSpec(memory_space=pl.ANY))
        n_in = len(args)

        def body(*refs):
            kernel(*refs[:n_real], *refs[n_in:])

        return pl.pallas_call(body, in_specs=in_specs, input_output_aliases=aliases, **kw)(*args)
    return run


def _amid_tokens(y, wq, wk, wv, qlg, kvlg, qng, kng, cos_t, sin_t, layer, ckv_new, kpe_new):
    tm = TM_MID
    const = lambda i: (0, 0)
    rope_map = lambda i: (_rope_block_of_tile(i, tm), 0)
    n_ab = (DEPTH + 1) // 2
    run = _call_with_carried(
        functools.partial(_amid_kernel, do_q=True, norm_kv=True), 3, [ckv_new, kpe_new],
        grid=(ROWS // tm,),
        out_specs=[pl.BlockSpec((tm, A_HEADS * A_QK_PAD), lambda i: (i, 0)),
                   pl.BlockSpec((tm, A_HEADS * A_QK_PAD), lambda i: (i, 0)),
                   pl.BlockSpec((tm, A_HEADS * A_VDIM), lambda i: (i, 0)),
                   _layer_slab_spec(tm, layer, KV_LORA), _layer_slab_spec(tm, layer, A_ROPE)],
        out_shape=[jax.ShapeDtypeStruct((ROWS, A_HEADS * A_QK_PAD), BF16),
                   jax.ShapeDtypeStruct((ROWS, A_HEADS * A_QK_PAD), BF16),
                   jax.ShapeDtypeStruct((ROWS, A_HEADS * A_VDIM), BF16),
                   jax.ShapeDtypeStruct((BATCH, n_ab, SEQ, KV_LORA), F32),
                   jax.ShapeDtypeStruct((BATCH, n_ab, SEQ, A_ROPE), F32)],
        compiler_params=_cparams(("arbitrary",)),
        name="mla_qkv_prep")
    return run(y, y, y, wq, wk, wv, qlg, kvlg, qng, kng, cos_t, sin_t,
               in_specs=[pl.BlockSpec((tm, Q_LORA), lambda i: (i, AB_QLAT // Q_LORA)),
                         pl.BlockSpec((tm, KV_LORA), lambda i: (i, AB_KVLAT // KV_LORA)),
                         pl.BlockSpec((tm, LANES), lambda i: (i, AB_KPE // LANES)),
                         pl.BlockSpec(wq.shape, const), pl.BlockSpec(wk.shape, const), pl.BlockSpec(wv.shape, const),
                         pl.BlockSpec(qlg.shape, const), pl.BlockSpec(kvlg.shape, const),
                         pl.BlockSpec(qng.shape, const), pl.BlockSpec(kng.shape, const),
                         pl.BlockSpec((tm, LANES), rope_map), pl.BlockSpec((tm, LANES), rope_map)])


def _amid_cache(ckv, kpe, wk, wv, kvlg, kng, cos_t, sin_t):
    rows = ckv.shape[0]
    tm = TM_MID
    const = lambda i: (0, 0)
    outs = pl.pallas_call(
        functools.partial(_amid_kernel, do_q=False, norm_kv=False),
        grid=(rows // tm,),
        in_specs=[pl.BlockSpec((tm, KV_LORA), lambda i: (i, 0)),
                  pl.BlockSpec((tm, LANES), lambda i: (i, 0)),
                  pl.BlockSpec(wk.shape, const), pl.BlockSpec(wv.shape, const),
                  pl.BlockSpec(kvlg.shape, const), pl.BlockSpec(kng.shape, const),
                  pl.BlockSpec((tm, LANES), const), pl.BlockSpec((tm, LANES), const)],
        out_specs=[pl.BlockSpec((tm, A_HEADS * A_QK_PAD), lambda i: (i, 0)),
                   pl.BlockSpec((tm, A_HEADS * A_VDIM), lambda i: (i, 0))],
        out_shape=[jax.ShapeDtypeStruct((rows, A_HEADS * A_QK_PAD), BF16),
                   jax.ShapeDtypeStruct((rows, A_HEADS * A_VDIM), BF16)],
        compiler_params=_cparams(("parallel",)),
        name="mla_cache_kv_prep",
    )(ckv, kpe, wk, wv, kvlg, kng, cos_t, sin_t)
    return outs[0], outs[1]


def _scores(q, srcs):
    zs = []
    for k, _, bias in srcs:
        z = _dot_nt(q, k)
        if bias is not None:
            z = (z.reshape(z.shape[0] // bias.shape[0], *bias.shape) + bias).reshape(z.shape)
        zs.append(z)
    return zs


def _softmax_pv(zs, srcs, sink_z=None):
    tile_max = None
    for z in zs:
        for j in range(z.shape[1] // LANES):
            blk = z[:, j * LANES:(j + 1) * LANES]
            tile_max = blk if tile_max is None else jnp.maximum(tile_max, blk)
    m = tile_max.max(axis=-1, keepdims=True)
    if sink_z is not None:
        m = jnp.maximum(m, sink_z)
    acc = None
    for z, (_, v, _) in zip(zs, srcs):
        o = _dot(jnp.exp2(z - m).astype(BF16), jnp.concatenate([v, jnp.ones_like(v)], axis=1))
        acc = o if acc is None else acc + o
    dv = acc.shape[1] // 2
    den = acc[:, dv:]
    if sink_z is not None:
        den = den + jnp.exp2(sink_z - m)
    return acc[:, :dv] / den


def _attend_streams(n, q_of, srcs_of, sink_of=None):
    outs = []
    zs = _scores(q_of(0), srcs_of(0))
    for t in range(n):
        nxt = _scores(q_of(t + 1), srcs_of(t + 1)) if t + 1 < n else None
        outs.append(_softmax_pv(zs, srcs_of(t), None if sink_of is None else sink_of(t)))
        zs = nxt
    return outs


def _attn_a_kernel(*refs, n_src, heads):
    q_ref = refs[0]
    k_refs = refs[1:1 + n_src]
    v_refs = refs[1 + n_src:1 + 2 * n_src]
    gate_ref, o_ref = refs[1 + 2 * n_src:]
    sub = min(ATTN_SUB, q_ref.shape[0])
    streams = [(h, r) for h in range(heads) for r in range(q_ref.shape[0] // sub)]

    def rows(t):
        return slice(streams[t][1] * sub, (streams[t][1] + 1) * sub)

    def q_of(t):
        h = streams[t][0]
        return q_ref[rows(t), h * A_QK_PAD:(h + 1) * A_QK_PAD]

    def srcs_of(t):
        h = streams[t][0]
        return [(k_ref[:, h * A_QK_PAD:(h + 1) * A_QK_PAD], v_ref[:, h * A_VDIM:(h + 1) * A_VDIM], None)
                for k_ref, v_ref in zip(k_refs, v_refs)]

    outs = _attend_streams(len(streams), q_of, srcs_of)
    for t, o in enumerate(outs):
        cols = slice(streams[t][0] * A_VDIM, (streams[t][0] + 1) * A_VDIM)
        o_ref[rows(t), cols] = (o * _silu(gate_ref[rows(t), cols])).astype(o_ref.dtype)


def _attn_a(q, k, v, y, mix, kc=None, vc=None):
    latent = kc is not None
    if latent:
        nb, t, tq, hp, row0 = DEC_BATCH, DEC_SEQ, TQ_A, 1, CTX_ROWS
    else:
        nb, t, tq, hp, row0 = BATCH, SEQ, SEQ, A_HEADS, 0
    nq = t // tq
    qrow = lambda b, h, i: row0 // tq + b * nq + i
    in_specs = [pl.BlockSpec((tq, hp * A_QK_PAD), lambda b, h, i: (qrow(b, h, i), h))]
    args = [q]
    if latent:
        in_specs.append(pl.BlockSpec((PAST_LEN, hp * A_QK_PAD), lambda b, h, i: (b, h)))
        args.append(kc)
    in_specs.append(pl.BlockSpec((t, hp * A_QK_PAD), lambda b, h, i: (row0 // t + b, h)))
    args.append(k)
    if latent:
        in_specs.append(pl.BlockSpec((PAST_LEN, hp * A_VDIM), lambda b, h, i: (b, h)))
        args.append(vc)
    in_specs.append(pl.BlockSpec((t, hp * A_VDIM), lambda b, h, i: (row0 // t + b, h)))
    args.append(v)
    in_specs.append(pl.BlockSpec((tq, hp * A_VDIM), lambda b, h, i: (qrow(b, h, i), AB_AGATE // (hp * A_VDIM) + h)))
    args.append(y)
    run = _call_with_carried(
        functools.partial(_attn_a_kernel, n_src=2 if latent else 1, heads=hp), 0, [mix],
        grid=(nb, A_HEADS // hp, nq),
        out_specs=[pl.BlockSpec((tq, hp * A_VDIM), lambda b, h, i: (qrow(b, h, i), h))],
        out_shape=[_MIX_SHAPE],
        compiler_params=_cparams(("parallel", "parallel", "parallel")),
        name="mla_attention_latent" if latent else "mla_attention_context")
    return run(*args, in_specs=in_specs)[0]


def _hgrn_constants():
    c, nl = HG_CHUNK, HG_LEVELS
    t = np.arange(c)[:, None]
    u = np.arange(c)[None, :]
    tri_f = (u <= t).astype(np.float32)
    mask_f = np.zeros((nl, c, c), np.float32)
    coef_f = np.zeros((nl, c, LANES), np.float32)
    for l in range(nl):
        half = c >> (l + 1)
        seg = 2 * half
        mask_f[l] = ((u // seg) == (t // seg)) & ((t % seg) >= half) & ((u % seg) < half)
        later = np.broadcast_to((t % seg) >= half, (c, LANES))
        coef_f[l] = np.where(later, 1.0, -1.0 if half > 1 else 0.0)
    tri_b = tri_f[::-1, ::-1]
    mask_b = mask_f[:, ::-1, ::-1]
    coef_b = coef_f[:, ::-1, :]
    to_tri = lambda a: jnp.asarray(np.concatenate([a, a, a], axis=1), BF16)
    to_f32 = lambda a: jnp.asarray(np.ascontiguousarray(a), F32)
    return (to_tri(tri_f), to_tri(tri_b)), (to_f32(mask_f), to_f32(mask_b)), (to_f32(coef_f), to_f32(coef_b))


def _hgrn_decays(x, lb, tri3):
    kk = (1.0 - lb) * _sigmoid(-x)
    lf = jnp.log2(1.0 - kk)
    hi = lf.astype(BF16)
    r1 = lf - hi.astype(F32)
    mid = r1.astype(BF16)
    lo = (r1 - mid.astype(F32)).astype(BF16)
    return kk, lf, _dot(tri3, jnp.concatenate([hi, mid, lo], axis=0))


def _hgrn_level_arg(l, cs, lf, coef, forward):
    c = HG_CHUNK
    nv = c // 8
    half = c >> (l + 1)
    if half == 1:
        return lf * coef
    cs3 = cs.reshape(nv, 8, LANES)

    def in_vreg_row(r):
        return jnp.broadcast_to(cs3[:, r:r + 1, :], (nv, 8, LANES))

    if half >= 8:
        m = half // 8
        nseg = nv // (2 * m)
        edge = cs3[:, 7:8, :] if forward else cs3[:, 0:1, :]
        e4 = edge.reshape(nseg, 2 * m, 1, LANES)
        a = e4[:, m - 1:m] if forward else e4[:, m:m + 1]
        anchor = jnp.broadcast_to(a, (nseg, 2 * m, 8, LANES)).reshape(c, LANES)
    elif half == 4:
        anchor = in_vreg_row(3 if forward else 4).reshape(c, LANES)
    else:
        r0, r1 = (1, 5) if forward else (2, 6)
        sub = lax.broadcasted_iota(jnp.int32, (nv, 8, LANES), 1)
        anchor = jnp.where(sub < 4, in_vreg_row(r0), in_vreg_row(r1)).reshape(c, LANES)
    return (cs - anchor) * coef


def _hgrn_kernel(*refs, t_len, heads, zero_init, emit_state):
    c = HG_CHUNK
    n_chunks = t_len // c
    it = iter(refs)
    bq_ref, ff_ref, fb_ref, vi_ref, bg_ref, lb_ref, hg_ref = (next(it) for _ in range(7))
    if not zero_init:
        s0f_ref, s0b_ref = next(it), next(it)
    trif_ref, trib_ref, maskf_ref, maskb_ref, coeff_ref, coefb_ref = (next(it) for _ in range(6))
    o_ref = next(it)
    if emit_state:
        sf_ref, sb_ref = next(it), next(it)
    of_ref, ob_ref, stf_ref, stb_ref = (next(it) for _ in range(4))

    for h in range(heads):
        if zero_init:
            stf_ref[h] = jnp.zeros((B_DV, B_DK), F32)
            stb_ref[h] = jnp.zeros((B_DV, B_DK), F32)
        else:
            stf_ref[h] = s0f_ref[h].T
            stb_ref[h] = s0b_ref[h].T
    lb = lb_ref[...]

    nl = HG_LEVELS
    unroll = min(HG_UNROLL // heads, n_chunks)
    dirs = ((ff_ref, 0, trif_ref, maskf_ref, coeff_ref, stf_ref, of_ref, True),
            (fb_ref, 1, trib_ref, maskb_ref, coefb_ref, stb_ref, ob_ref, False))

    def body(i, carry):
        chains = []
        for h in range(heads):
            cols = slice(h * LANES, (h + 1) * LANES)
            for f_ref, lb_row, tri_ref, mask_ref, coef_ref, st_ref, out_ref, forward in dirs:
                for u in range(unroll):
                    k = i * unroll + u
                    r0 = pl.multiple_of((k if forward else n_chunks - 1 - k) * c, c)
                    chains.append(dict(rows=pl.ds(r0, c), cols=cols, head=h, f_ref=f_ref,
                                       lb=lb[lb_row:lb_row + 1, cols], tri_ref=tri_ref, mask_ref=mask_ref,
                                       coef_ref=coef_ref, st_ref=st_ref, out_ref=out_ref, forward=forward))
        for ch in chains:
            ch["kk"], ch["lf"], ch["cs"] = _hgrn_decays(ch["f_ref"][ch["rows"], ch["cols"]], ch["lb"],
                                                        ch["tri_ref"][...])
            ch["q"] = _silu(bq_ref[ch["rows"], ch["cols"]])
            ch["v"] = vi_ref[ch["rows"], ch["cols"]]
            ch["q16"] = ch["q"].astype(BF16)
            ch["kk16"] = ch["kk"].astype(BF16)
            ch["sc"] = jnp.zeros((c, c), F32)
        for l in range(nl):
            for ch in chains:
                el = jnp.exp2(_hgrn_level_arg(l, ch["cs"], ch["lf"], ch["coef_ref"][l], ch["forward"])).astype(BF16)
                ch["sc"] = ch["sc"] + ch["mask_ref"][l] * _dot_nt(ch["q16"] * el, ch["kk16"] * el)
        for ch in chains:
            cs, q, kk, v = ch["cs"], ch["q"], ch["kk"], ch["v"]
            end = cs[c - 1:c, :] if ch["forward"] else cs[0:1, :]
            ch["end"] = end
            ch["q_in"] = (q * jnp.exp2(cs)).astype(BF16)
            ch["upd"] = _dot(v.T.astype(BF16), (kk * jnp.exp2(end - cs)).astype(BF16))
            ch["o"] = _dot(ch["sc"].astype(BF16), v.astype(BF16)) + jnp.sum(q * kk, axis=-1, keepdims=True) * v
        for ch in chains:
            st = ch["st_ref"][ch["head"]]
            ch["out_ref"][ch["rows"], ch["cols"]] = ch["o"] + _dot_nt(ch["q_in"], st.astype(BF16))
            ch["st_ref"][ch["head"]] = jnp.exp2(ch["end"]) * st + ch["upd"]
        return carry

    lax.fori_loop(0, n_chunks // unroll, body, 0)
    if emit_state:
        for h in range(heads):
            sf_ref[h] = stf_ref[h].T
            sb_ref[h] = stb_ref[h].T

    hg = hg_ref[...]
    blk = min(HG_UNROLL, n_chunks) * c

    def finish(i, carry):
        rows = pl.ds(pl.multiple_of(i * blk, blk), blk)
        for h in range(heads):
            cols = slice(h * LANES, (h + 1) * LANES)
            o = of_ref[rows, cols] + ob_ref[rows, cols]
            o = o * lax.rsqrt(jnp.mean(o * o, axis=-1, keepdims=True) + EPS) * hg
            o_ref[rows, cols] = (o * _silu(bg_ref[rows, cols])).astype(o_ref.dtype)
        return carry

    lax.fori_loop(0, t_len // blk, finish, 0)


def _hgrn(y, lb, hg, mix, consts, s0f=None, s0b=None, layer=None, sf_new=None, sb_new=None):
    latent = s0f is not None
    if latent:
        nb, t, row0, hp = DEC_BATCH, DEC_SEQ, CTX_ROWS, 1
    else:
        nb, t, row0, hp = BATCH, SEQ, 0, 2
    w = hp * LANES
    rb = lambda b: row0 // t + b
    col = lambda off: (lambda b, h: (rb(b), off // w + h))
    const2 = lambda b, h: (0, 0)
    const3 = lambda b, h: (0, 0, 0)
    in_specs = [pl.BlockSpec((t, w), col(AB_BQ)), pl.BlockSpec((t, w), col(AB_BFF)),
                pl.BlockSpec((t, w), col(AB_BFB)), pl.BlockSpec((t, w), col(AB_BI)),
                pl.BlockSpec((t, w), col(AB_BGATE)),
                pl.BlockSpec((2, w), lambda b, h: (0, h)),
                pl.BlockSpec((1, LANES), const2)]
    args = [y, y, y, y, y, lb, hg]
    if latent:
        st_spec = pl.BlockSpec((None, hp, B_DK, B_DV), lambda b, h: (b, h, 0, 0))
        in_specs += [st_spec, st_spec]
        args += [s0f, s0b]
    tris, masks, coefs = consts
    in_specs += ([pl.BlockSpec(a.shape, const2) for a in tris] + [pl.BlockSpec(a.shape, const3) for a in masks]
                 + [pl.BlockSpec(a.shape, const3) for a in coefs])
    args += [*tris, *masks, *coefs]
    out_specs = [pl.BlockSpec((t, w), lambda b, h: (rb(b), A_HEADS * A_VDIM // w + h))]
    out_shape = [_MIX_SHAPE]
    carried = [mix]
    if not latent:
        st_out = pl.BlockSpec((None, None, hp, B_DK, B_DV), lambda b, h: (b, layer, h, 0, 0))
        out_specs += [st_out, st_out]
        out_shape += [jax.ShapeDtypeStruct((nb, (DEPTH + 1) // 2, B_HEADS, B_DK, B_DV), F32)] * 2
        carried += [sf_new, sb_new]
    run = _call_with_carried(
        functools.partial(_hgrn_kernel, t_len=t, heads=hp, zero_init=not latent, emit_state=not latent), 0, carried,
        grid=(nb, B_HEADS // hp),
        out_specs=out_specs,
        out_shape=out_shape,
        scratch_shapes=[pltpu.VMEM((t, w), F32), pltpu.VMEM((t, w), F32),
                        pltpu.VMEM((hp, B_DV, B_DK), F32), pltpu.VMEM((hp, B_DV, B_DK), F32)],
        compiler_params=_cparams(("parallel", "parallel")),
        name="hgrn2_latent" if latent else "hgrn2_context")
    return run(*args, in_specs=in_specs)


def _cmid_kernel(q_ref, k_ref, v_ref, qg_ref, kg_ref, cos_ref, sin_ref, q_out, k_out, v_out, kc_out, vc_out):
    cos_t = cos_ref[...]
    sin_t = sin_ref[...]
    qg = qg_ref[...]
    kg = kg_ref[...]
    in_ctx = pl.program_id(0) < CTX_ROWS // TM_MID

    def norm(x, g):
        return x * lax.rsqrt(_row_sums(x * x) * (1.0 / C_HEAD_DIM) + EPS) * g

    def rope(x):
        return _rope(x, cos_t, sin_t)

    for h in range(C_HEADS):
        sl = slice(h * C_HEAD_DIM, (h + 1) * C_HEAD_DIM)
        q_out[:, sl] = rope(norm(q_ref[:, sl], qg)).astype(BF16)
    for h in range(C_KV_HEADS):
        sl = slice(h * C_HEAD_DIM, (h + 1) * C_HEAD_DIM)
        kn = norm(k_ref[:, sl], kg)
        k_out[:, sl] = rope(kn).astype(BF16)

        @pl.when(in_ctx)
        def _():
            kc_out[:, :, sl] = kn.reshape(kc_out.shape[0], SEQ, C_HEAD_DIM)
    v = v_ref[...]
    v_out[...] = v.astype(BF16)

    @pl.when(in_ctx)
    def _():
        vc_out[...] = v.reshape(vc_out.shape)


def _cmid(y, qg, kg, cos_t, sin_t, layer, kc_new, vc_new):
    tm = TM_MID
    const = lambda i: (0, 0)
    rope_map = lambda i: (_rope_block_of_tile(i, tm), 0)
    cache = jax.ShapeDtypeStruct((BATCH, DEPTH // 2, SEQ, C_KV_WIDTH), F32)
    run = _call_with_carried(
        _cmid_kernel, 3, [kc_new, vc_new],
        grid=(ROWS // tm,),
        out_specs=[pl.BlockSpec((tm, C_WIDTH), lambda i: (i, 0)),
                   pl.BlockSpec((tm, C_KV_WIDTH), lambda i: (i, 0)),
                   pl.BlockSpec((tm, C_KV_WIDTH), lambda i: (i, 0)),
                   _layer_slab_spec(tm, layer, C_KV_WIDTH), _layer_slab_spec(tm, layer, C_KV_WIDTH)],
        out_shape=[jax.ShapeDtypeStruct((ROWS, C_WIDTH), BF16),
                   jax.ShapeDtypeStruct((ROWS, C_KV_WIDTH), BF16),
                   jax.ShapeDtypeStruct((ROWS, C_KV_WIDTH), BF16), cache, cache],
        compiler_params=_cparams(("arbitrary",)),
        name="gqa_qkv_prep")
    return run(y, y, y, qg, kg, cos_t, sin_t,
               in_specs=[pl.BlockSpec((tm, C_WIDTH), lambda i: (i, C_Q // C_WIDTH)),
                         pl.BlockSpec((tm, C_KV_WIDTH), lambda i: (i, C_K // C_KV_WIDTH)),
                         pl.BlockSpec((tm, C_KV_WIDTH), lambda i: (i, C_V // C_KV_WIDTH)),
                         pl.BlockSpec((1, C_HEAD_DIM), const), pl.BlockSpec((1, C_HEAD_DIM), const),
                         pl.BlockSpec((tm, LANES), rope_map), pl.BlockSpec((tm, LANES), rope_map)])


def _band_start(i, tq, t_len):
    return pl.multiple_of(jnp.clip(i * tq - WINDOW, 0, t_len - (tq + 2 * WINDOW)), WINDOW)


def _band_bias(tq, t_len):
    width = tq + 2 * WINDOW
    i = np.arange(t_len // tq)[:, None, None]
    qpos = i * tq + np.arange(tq)[None, :, None]
    kpos = np.clip(i * tq - WINDOW, 0, t_len - width) + np.arange(width)[None, None, :]
    return jnp.asarray(np.where(np.abs(kpos - qpos) <= WINDOW, 0.0, NEG_BIG), F32)


def _attn_c_kernel(*refs, band, tq, t_len, groups, stack):
    if band:
        q_ref, kc_ref, vc_ref, kl_ref, vl_ref, bias_ref, sink_ref, gate_ref, o_ref = refs
    else:
        q_ref, kc_ref, vc_ref, sink_ref, gate_ref, o_ref = refs
    hd = C_HEAD_DIM
    streams = [(g, s) for g in range(groups) for s in range(C_GROUP // stack)]
    if band:
        width = tq + 2 * WINDOW
        start = _band_start(pl.program_id(2), tq, t_len)
        band_bias = bias_ref[...]

    def heads_of(t):
        g, s = streams[t]
        return [g * C_GROUP + s * stack + r for r in range(stack)]

    def q_of(t):
        return jnp.concatenate([q_ref[:, h * hd:(h + 1) * hd] for h in heads_of(t)], axis=0)

    def srcs_of(t):
        g = streams[t][0]
        cols = slice(g * hd, (g + 1) * hd)
        srcs = [(kc_ref[:, cols], vc_ref[:, cols], None)]
        if band:
            srcs.append((kl_ref[pl.ds(start, width), cols], vl_ref[pl.ds(start, width), cols], band_bias))
        return srcs

    def sink_of(t):
        return jnp.concatenate([jnp.broadcast_to(sink_ref[h][:, :1] * LOG2_E, (tq, 1)) for h in heads_of(t)], axis=0)

    outs = _attend_streams(len(streams), q_of, srcs_of, sink_of)
    for t, o in enumerate(outs):
        for r, h in enumerate(heads_of(t)):
            cols = slice(h * hd, (h + 1) * hd)
            o_ref[:, cols] = (o[r * tq:(r + 1) * tq] * _silu(gate_ref[:, cols])).astype(o_ref.dtype)


def _attn_c(q, k, v, y, sink, mix, kc=None, vc=None):
    latent = kc is not None
    if latent:
        nb, t, tq, gp, stack, row0 = DEC_BATCH, DEC_SEQ, TQ_C, 1, 1, CTX_ROWS
    else:
        nb, t, tq, gp, stack, row0 = BATCH, SEQ, SEQ, C_KV_HEADS, C_GROUP, 0
    gw = gp * C_GROUP * C_HEAD_DIM
    kvw = gp * C_HEAD_DIM
    nq = t // tq
    qrow = lambda b, g, i: row0 // tq + b * nq + i
    own_kv = pl.BlockSpec((t, kvw), lambda b, g, i: (row0 // t + b, g))
    in_specs = [pl.BlockSpec((tq, gw), lambda b, g, i: (qrow(b, g, i), g))]
    args = [q]
    if latent:
        ctx_kv = pl.BlockSpec((PAST_LEN, kvw), lambda b, g, i: (b, g))
        in_specs += [ctx_kv, ctx_kv, own_kv, own_kv,
                     pl.BlockSpec((None, tq, tq + 2 * WINDOW), lambda b, g, i: (i, 0, 0))]
        args += [kc, vc, k, v, _band_bias(tq, t)]
    else:
        in_specs += [own_kv, own_kv]
        args += [k, v]
    in_specs.append(pl.BlockSpec((gp * C_GROUP, 1, LANES), lambda b, g, i: (g, 0, 0)))
    args.append(sink)
    in_specs.append(pl.BlockSpec((tq, gw), lambda b, g, i: (qrow(b, g, i), C_GATE // gw + g)))
    args.append(y)
    run = _call_with_carried(
        functools.partial(_attn_c_kernel, band=latent, tq=tq, t_len=t, groups=gp, stack=stack), 0, [mix],
        grid=(nb, C_KV_HEADS // gp, nq),
        out_specs=[pl.BlockSpec((tq, gw), lambda b, g, i: (qrow(b, g, i), g))],
        out_shape=[_MIX_SHAPE],
        compiler_params=_cparams(("parallel", "parallel", "parallel")),
        name="gqa_attention_latent" if latent else "gqa_attention_context")
    return run(*args, in_specs=in_specs)[0]


def _axial_angles(n_tokens, rot_dim):
    rows = n_tokens // GRID_W
    row = jnp.repeat(jnp.arange(rows, dtype=F32), GRID_W)
    col = jnp.tile(jnp.arange(GRID_W, dtype=F32), rows)
    n_freq = rot_dim // 4
    inv = ROPE_BASE ** (-jnp.arange(n_freq, dtype=F32) / n_freq)
    return jnp.concatenate([row[:, None] * inv, col[:, None] * inv], axis=-1)


def _spread_halves(a, fill=0.0):
    h = a.shape[-1] // 2
    pad = jnp.full(a.shape[:-1] + (LANES // 2 - h,), fill, a.dtype)
    return jnp.concatenate([a[..., :h], pad, a[..., h:], pad], axis=-1)


def _spread_head(a):
    return jnp.concatenate([a[..., :A_NOPE], _spread_halves(a[..., A_NOPE:])], axis=-1)


def _rope_tables(rot_dim, tm):
    ang = _axial_angles(DEC_SEQ, rot_dim)
    cos, sin = jnp.cos(ang), jnp.sin(ang)
    cos_t = _spread_halves(jnp.concatenate([cos, cos], axis=-1), 1.0)
    sin_t = _spread_halves(jnp.concatenate([-sin, sin], axis=-1))
    cos_t = jnp.concatenate([jnp.ones((tm, LANES), F32), cos_t], axis=0)
    sin_t = jnp.concatenate([jnp.zeros((tm, LANES), F32), sin_t], axis=0)
    return cos_t, sin_t


def _lower_bounds(lb_logits):
    p = jax.nn.softmax(lb_logits.astype(F32), axis=0)
    return jnp.cumsum(p, axis=0) - p[0:1]


def kernel(x_prompt, x_sample, cache_ckv, cache_kpe, state_hgrn_fwd, state_hgrn_bwd, cache_k_c, cache_v_c, c, c_ctx,
           mod_w_ab, mod_b_ab, norm_ab, w_in_ab, q_lora_norm, kv_lora_norm, w_q_up, w_kv_up, q_norm_ab, k_norm_ab,
           hgrn_lb_logits, hgrn_out_norm, w_out_ab, mod_w_c, mod_b_c, norm_c, w_in_c, q_norm_c, k_norm_c, sink_c,
           w_out_c):
    x_parts = [x_prompt.reshape(CTX_ROWS, D_MODEL), x_sample.reshape(LAT_ROWS, D_MODEL)]
    cond8 = jnp.concatenate([c_ctx[None, :], c, jnp.zeros((N_COND - 1 - DEC_BATCH, D_MODEL), F32)], axis=0)
    mods_ab = _modulation(cond8, mod_w_ab, mod_b_ab)
    mods_c = _modulation(cond8, mod_w_c, mod_b_c)
    lower = _lower_bounds(hgrn_lb_logits)
    cos_a, sin_a = _rope_tables(A_ROPE, TM_MID)
    cos_c, sin_c = _rope_tables(C_HEAD_DIM, TM_MID)
    hg_consts = _hgrn_constants()
    w_in_ab16 = _prep_w_in_ab(jnp.swapaxes(w_in_ab, 1, 2))
    w_in_c16, w_out_ab16, w_out_c16 = w_in_c.astype(BF16), w_out_ab.astype(BF16), w_out_c.astype(BF16)

    ckv_new = kpe_new = sf_new = sb_new = kc_new = vc_new = None
    for layer in range(DEPTH):
        j = layer // 2
        last = layer == DEPTH - 1
        if layer % 2 == 0:
            mod = mods_ab[j].reshape(3 * N_COND, 1, D_MODEL)
            wq = _spread_head(w_q_up[j].reshape(Q_LORA, A_HEADS, A_QK)).reshape(Q_LORA, A_HEADS * A_QK_PAD).astype(BF16)
            wkv = w_kv_up[j].reshape(KV_LORA, A_HEADS, A_NOPE + A_VDIM)
            wk = wkv[:, :, :A_NOPE].reshape(KV_LORA, A_HEADS * A_NOPE).astype(BF16)
            wv = wkv[:, :, A_NOPE:].reshape(KV_LORA, A_HEADS * A_VDIM).astype(BF16)
            qlg, kvlg = q_lora_norm[j][None, :], kv_lora_norm[j][None, :]
            qng = _spread_head(q_norm_ab[j] * (A_QK ** -0.5 * LOG2_E))[None, :]
            kng = _spread_head(k_norm_ab[j])[None, :]

            y = _in_proj(x_parts, norm_ab[j][None, :], mod, w_in_ab16, j, w_transposed=True)
            q, k, v, ckv_new, kpe_new = _amid_tokens(y, wq, wk, wv, qlg, kvlg, qng, kng, cos_a, sin_a,
                                                     j, ckv_new, kpe_new)
            kpe_cache = _spread_halves(cache_kpe[:, j].reshape(DEC_BATCH * PAST_LEN, A_ROPE))
            kc, vc = _amid_cache(cache_ckv[:, j].reshape(DEC_BATCH * PAST_LEN, KV_LORA), kpe_cache, wk, wv, kvlg, kng,
                                 cos_a, sin_a)
            mix = _attn_a(q, k, v, y, None)
            mix = _attn_a(q, k, v, y, mix, kc=kc, vc=vc)
            hg = hgrn_out_norm[j][None, :]
            mix, sf_new, sb_new = _hgrn(y, lower[j], hg, mix, hg_consts, layer=j, sf_new=sf_new, sb_new=sb_new)
            (mix,) = _hgrn(y, lower[j], hg, mix, hg_consts, s0f=state_hgrn_fwd[:, j], s0b=state_hgrn_bwd[:, j])
            x_parts = _out_proj(mix, w_out_ab16, j, x_parts, mod, split_out=last)
        else:
            mod = mods_c[j].reshape(3 * N_COND, 1, D_MODEL)
            y = _in_proj(x_parts, norm_c[j][None, :], mod, w_in_c16, j,
                         col_map=lambda t: jnp.where(t < 2, t, jnp.where(t < 4, t + 1, 2)))
            q, k, v, kc_new, vc_new = _cmid(y, q_norm_c[j][None, :] * (C_HEAD_DIM ** -0.5 * LOG2_E),
                                            k_norm_c[j][None, :], cos_c, sin_c,
                                            j, kc_new, vc_new)
            kc = cache_k_c[:, j].reshape(DEC_BATCH * PAST_LEN, C_KV_WIDTH).astype(BF16)
            vc = cache_v_c[:, j].reshape(DEC_BATCH * PAST_LEN, C_KV_WIDTH).astype(BF16)
            sink = jnp.broadcast_to(sink_c[j][:, None, None], (C_HEADS, 1, LANES))
            mix = _attn_c(q, k, v, y, sink, None)
            mix = _attn_c(q, k, v, y, sink, mix, kc=kc, vc=vc)
            x_parts = _out_proj(mix, w_out_c16, j, x_parts, mod, split_out=last)

    cache_c_shape = (BATCH, DEPTH // 2, SEQ, C_KV_HEADS, C_HEAD_DIM)
    return (x_parts[0].reshape(BATCH, SEQ, D_MODEL), x_parts[1].reshape(DEC_BATCH, DEC_SEQ, D_MODEL),
            ckv_new, kpe_new, sf_new, sb_new, kc_new.reshape(cache_c_shape), vc_new.reshape(cache_c_shape))
```

```python
import functools

import numpy as np
import jax
import jax.numpy as jnp
from jax import lax
from jax.experimental import pallas as pl
from jax.experimental.pallas import tpu as pltpu

F32 = jnp.float32
BF16 = jnp.bfloat16

D_MODEL = 2048
BATCH = 16
SEQ = 256
DEPTH = 4
DEC_BATCH = 4
DEC_SEQ = 2048
PAST_LEN = 256
GRID_W = 64
A_HEADS = 8
A_NOPE = 128
A_ROPE = 64
A_VDIM = 128
A_QK = A_NOPE + A_ROPE
A_QK_PAD = 256
Q_LORA = 512
KV_LORA = 256
B_HEADS = 8
B_DK = 128
B_DV = 128
C_HEADS = 16
C_KV_HEADS = 4
C_GROUP = C_HEADS // C_KV_HEADS
C_HEAD_DIM = 128
C_WIDTH = C_HEADS * C_HEAD_DIM
C_KV_WIDTH = C_KV_HEADS * C_HEAD_DIM
WINDOW = 128
ROPE_BASE = 10000.0
EPS = 1e-6
NEG_BIG = -1e30
LOG2_E = 1.4426950408889634

LANES = 128
CTX_ROWS = BATCH * SEQ
LAT_ROWS = DEC_BATCH * DEC_SEQ
ROWS = CTX_ROWS + LAT_ROWS
N_COND = 8

AB_AGATE = 0
AB_BQ = 1024
AB_BFF = 2048
AB_BFB = 3072
AB_BI = 4096
AB_BGATE = 5120
AB_QLAT = 6144
AB_KVLAT = 6656
AB_KPE = 6912
AB_N = 7168
C_Q = 0
C_GATE = 2048
C_K = 4096
C_V = 4608
C_N = 5120

TM_PROJ = 1024
TN_IN = 1024
TM_OUT = 512
TM_MID = 512
TQ_A = 1024
TQ_C = 1024
ATTN_SUB = 256
HG_CHUNK = 128
HG_LEVELS = 7
HG_UNROLL = 4
VMEM_LIMIT = 56 * 1024 * 1024
_MIX_SHAPE = jax.ShapeDtypeStruct((ROWS, D_MODEL), BF16)


def _cparams(sem):
    return pltpu.CompilerParams(dimension_semantics=sem, vmem_limit_bytes=VMEM_LIMIT)


def _sigmoid(x):
    return 0.5 * jnp.tanh(0.5 * x) + 0.5


def _silu(x):
    return x * _sigmoid(x)


def _dot(a, b):
    return jnp.dot(a, b, preferred_element_type=F32)


def _dot_nt(a, b):
    return lax.dot_general(a, b, (((1,), (1,)), ((), ())), preferred_element_type=F32)


def _cond_of_tile(i, tm):
    n_ctx = CTX_ROWS // tm
    per_batch = DEC_SEQ // tm
    return jnp.where(i < n_ctx, 0, 1 + (i - n_ctx) // per_batch)


def _rope_block_of_tile(i, tm):
    n_ctx = CTX_ROWS // tm
    per_batch = DEC_SEQ // tm
    return jnp.where(i < n_ctx, 0, 1 + (i - n_ctx) % per_batch)


def _mod_kernel(c_ref, w_ref, b_ref, o_ref):
    a = _silu(c_ref[...]).astype(BF16)
    o_ref[...] = _dot(a, w_ref[...].astype(BF16)) + b_ref[...]


def _modulation(cond8, w_mod, b_mod):
    n = w_mod.shape[0]
    tn = 1024
    return pl.pallas_call(
        _mod_kernel,
        grid=(n, 3 * D_MODEL // tn),
        in_specs=[pl.BlockSpec((N_COND, D_MODEL), lambda l, j: (0, 0)),
                  pl.BlockSpec((None, D_MODEL, tn), lambda l, j: (l, 0, j)),
                  pl.BlockSpec((None, 1, tn), lambda l, j: (l, 0, j))],
        out_specs=pl.BlockSpec((None, N_COND, tn), lambda l, j: (l, 0, j)),
        out_shape=jax.ShapeDtypeStruct((n, N_COND, 3 * D_MODEL), F32),
        compiler_params=_cparams(("parallel", "parallel")),
        name="adaln_mod",
    )(cond8, w_mod, b_mod.reshape(n, 1, 3 * D_MODEL))


def _prep_w_kernel(w_ref, o_ref):
    n_head = Q_LORA + KV_LORA + A_ROPE
    n_main = AB_N - TN_IN
    half = A_ROPE // 2
    x1_end = Q_LORA + KV_LORA + half
    x2_at = Q_LORA + KV_LORA + LANES // 2
    o_ref[:n_main, :] = w_ref[n_head:, :].astype(BF16)
    o_ref[n_main:, :] = jnp.zeros((TN_IN, o_ref.shape[1]), BF16)
    o_ref[n_main:n_main + x1_end, :] = w_ref[:x1_end, :].astype(BF16)
    o_ref[n_main + x2_at:n_main + x2_at + half, :] = w_ref[x1_end:n_head, :].astype(BF16)


def _prep_w_in_ab(w_t):
    n_layers, n, _ = w_t.shape
    tc = 256
    return pl.pallas_call(
        _prep_w_kernel,
        grid=(n_layers, D_MODEL // tc),
        in_specs=[pl.BlockSpec((None, n, tc), lambda l, i: (l, 0, i))],
        out_specs=pl.BlockSpec((None, AB_N, tc), lambda l, i: (l, 0, i)),
        out_shape=jax.ShapeDtypeStruct((n_layers, AB_N, D_MODEL), BF16),
        compiler_params=_cparams(("parallel", "parallel")),
        name="w_in_ab_layout",
    )(w_t)


def _row_specs(parts, tm, single_buffer=False, tile_of=lambda i, *_: i):
    if len(parts) == 1:
        return [pl.BlockSpec((tm, D_MODEL), lambda *g: (tile_of(*g), 0))]
    n_first = CTX_ROWS // tm
    mode = dict(pipeline_mode=pl.Buffered(1)) if single_buffer else {}
    return [pl.BlockSpec((tm, D_MODEL), lambda *g: (jnp.minimum(tile_of(*g), n_first - 1), 0), **mode),
            pl.BlockSpec((tm, D_MODEL), lambda *g: (jnp.maximum(tile_of(*g) - n_first, 0), 0), **mode)]


IN_SUB = 4
IN_SUB_ROWS = TM_PROJ // IN_SUB


def _norm_mod(x, g, shift, scale):
    r = lax.rsqrt(jnp.mean(x * x, axis=-1, keepdims=True) + EPS)
    return ((x * r * g) * (1.0 + scale) + shift).astype(BF16)


def _norm_tile_kernel(x_ref, g_ref, sh_ref, sc_ref, o_ref):
    o_ref[...] = _norm_mod(x_ref[...], g_ref[...], sh_ref[...], sc_ref[...])


def _norm_first_tile(x0, g, mod):
    const = lambda s: (0, 0)
    return pl.pallas_call(
        _norm_tile_kernel,
        grid=(IN_SUB,),
        in_specs=[pl.BlockSpec((IN_SUB_ROWS, D_MODEL), lambda s: (s, 0)), pl.BlockSpec((1, D_MODEL), const),
                  pl.BlockSpec((None, 1, D_MODEL), lambda s: (0, 0, 0)),
                  pl.BlockSpec((None, 1, D_MODEL), lambda s: (1, 0, 0))],
        out_specs=pl.BlockSpec((IN_SUB_ROWS, D_MODEL), lambda s: (s, 0)),
        out_shape=jax.ShapeDtypeStruct((TM_PROJ, D_MODEL), BF16),
        compiler_params=_cparams(("parallel",)),
        name="norm_mod_first_tile",
    )(x0, g, mod, mod)


def _in_next_tile(i):
    return jnp.minimum(i + 1, ROWS // TM_PROJ - 1)


def _in_next_sub_block(i, j):
    return _in_next_tile(i) * IN_SUB + jnp.clip(j - 1, 0, IN_SUB - 1)


def _in_kernel(*refs, n_x, w_transposed):
    x_refs = refs[:n_x]
    h0_ref, g_ref, sh_ref, sc_ref, w_ref, o_ref, h_ref, hn_ref = refs[n_x:]
    i, j = pl.program_id(0), pl.program_id(1)

    @pl.when((i == 0) & (j == 0))
    def _():
        h_ref[...] = h0_ref[...]

    @pl.when((i > 0) & (j == 0))
    def _():
        h_ref[...] = hn_ref[...]

    if n_x == 1:
        x = x_refs[0][...]
    else:
        x = jnp.where(_in_next_sub_block(i, j) < CTX_ROWS // IN_SUB_ROWS, x_refs[0][...], x_refs[1][...])
    r0 = pl.multiple_of(jnp.clip(j - 1, 0, IN_SUB - 1) * IN_SUB_ROWS, IN_SUB_ROWS)
    hn_ref[pl.ds(r0, IN_SUB_ROWS), :] = _norm_mod(x, g_ref[...], sh_ref[...], sc_ref[...])
    o_ref[...] = (_dot_nt if w_transposed else _dot)(h_ref[...], w_ref[...])


def _in_proj(x_parts, g, mod, w, layer, col_map=None, w_transposed=False):
    tm, tn = TM_PROJ, TN_IN
    n_tiles = (w.shape[1] if w_transposed else w.shape[2]) // tn
    assert n_tiles > IN_SUB
    col_map = col_map or (lambda t: t)
    if w_transposed:
        w_spec = pl.BlockSpec((None, tn, D_MODEL), lambda i, j: (layer, col_map(j), 0))
    else:
        w_spec = pl.BlockSpec((None, D_MODEL, tn), lambda i, j: (layer, 0, col_map(j)))
    mod_row = lambda part: (lambda i, j: (_cond_of_tile(_in_next_tile(i), tm) * 3 + part, 0, 0))
    h0 = _norm_first_tile(x_parts[0], g, mod)
    return pl.pallas_call(
        functools.partial(_in_kernel, n_x=len(x_parts), w_transposed=w_transposed),
        grid=(ROWS // tm, n_tiles),
        in_specs=_row_specs(x_parts, IN_SUB_ROWS, tile_of=_in_next_sub_block) + [
            pl.BlockSpec((tm, D_MODEL), lambda i, j: (0, 0), pipeline_mode=pl.Buffered(1)),
            pl.BlockSpec((1, D_MODEL), lambda i, j: (0, 0)),
            pl.BlockSpec((None, 1, D_MODEL), mod_row(0)),
            pl.BlockSpec((None, 1, D_MODEL), mod_row(1)), w_spec],
        out_specs=pl.BlockSpec((tm, tn), lambda i, j: (i, j)),
        out_shape=jax.ShapeDtypeStruct((ROWS, n_tiles * tn), F32),
        scratch_shapes=[pltpu.VMEM((tm, D_MODEL), BF16), pltpu.VMEM((tm, D_MODEL), BF16)],
        compiler_params=_cparams(("arbitrary", "arbitrary")),
        name="norm_mod_in_proj",
    )(*x_parts, h0, g, mod, mod, w)


def _out_kernel(*refs, n_x, n_o):
    m_ref, w_ref = refs[:2]
    x_refs = refs[2:2 + n_x]
    gt_ref = refs[2 + n_x]
    o_refs = refs[3 + n_x:]
    y = gt_ref[...] * _dot(m_ref[...], w_ref[...])

    def emit(x_ref, o_ref):
        o_ref[...] = x_ref[...] + y

    if n_x == 1 and n_o == 1:
        emit(x_refs[0], o_refs[0])
    else:
        in_first = pl.program_id(0) < CTX_ROWS // TM_OUT
        pl.when(in_first)(lambda: emit(x_refs[0], o_refs[0]))
        pl.when(jnp.logical_not(in_first))(lambda: emit(x_refs[-1], o_refs[-1]))


def _out_proj(mix, w, layer, x_parts, mod, split_out):
    tm = TM_OUT
    if split_out:
        out_parts = [jax.ShapeDtypeStruct((CTX_ROWS, D_MODEL), F32), jax.ShapeDtypeStruct((LAT_ROWS, D_MODEL), F32)]
    else:
        out_parts = [jax.ShapeDtypeStruct((ROWS, D_MODEL), F32)]
    return pl.pallas_call(
        functools.partial(_out_kernel, n_x=len(x_parts), n_o=len(out_parts)),
        grid=(ROWS // tm,),
        in_specs=[pl.BlockSpec((tm, D_MODEL), lambda i: (i, 0)),
                  pl.BlockSpec((None, D_MODEL, D_MODEL), lambda i: (layer, 0, 0))] + _row_specs(x_parts, tm) + [
                  pl.BlockSpec((None, 1, D_MODEL), lambda i: (_cond_of_tile(i, tm) * 3 + 2, 0, 0))],
        out_specs=_row_specs(out_parts, tm),
        out_shape=out_parts,
        compiler_params=_cparams(("arbitrary",)),
        name="out_proj_residual",
    )(mix, w, *x_parts, mod)


def _rope(x, cos_t, sin_t):
    return x * cos_t + pltpu.roll(x, LANES // 2, 1) * sin_t


def _row_sums(x):
    hi = x.astype(BF16)
    lo = (x - hi.astype(F32)).astype(BF16)
    return _dot(jnp.concatenate([hi, lo], axis=1), jnp.ones((2 * x.shape[1], LANES), BF16))


def _amid_kernel(*refs, do_q, norm_kv):
    if do_q:
        (ql_ref, kvl_ref, kpe_ref, wq_ref, wk_ref, wv_ref, qlg_ref, kvlg_ref, qng_ref, kng_ref, cos_ref, sin_ref,
         q_out, k_out, v_out, ckv_out, kpe_out) = refs
    else:
        (kvl_ref, kpe_ref, wk_ref, wv_ref, kvlg_ref, kng_ref, cos_ref, sin_ref, k_out, v_out) = refs
    cos_t = cos_ref[...]
    sin_t = sin_ref[...]
    inv_qk = 1.0 / A_QK

    if do_q:
        ql = ql_ref[...]
        qn = ql * lax.rsqrt(jnp.mean(ql * ql, axis=-1, keepdims=True) + EPS) * qlg_ref[...]
        qu = _dot(qn.astype(BF16), wq_ref[...])
        g_nope = qng_ref[:, :A_NOPE]
        g_rope = qng_ref[:, A_NOPE:]
        for h in range(A_HEADS):
            qh = qu[:, h * A_QK_PAD:(h + 1) * A_QK_PAD]
            r = lax.rsqrt(_row_sums(qh * qh) * inv_qk + EPS)
            q_out[:, h * A_QK_PAD:h * A_QK_PAD + A_NOPE] = (qh[:, :A_NOPE] * r * g_nope).astype(BF16)
            q_out[:, h * A_QK_PAD + A_NOPE:(h + 1) * A_QK_PAD] = _rope(qh[:, A_NOPE:] * r * g_rope,
                                                                       cos_t, sin_t).astype(BF16)

    kvl = kvl_ref[...]
    if norm_kv:
        ckv = kvl * lax.rsqrt(jnp.mean(kvl * kvl, axis=-1, keepdims=True) + EPS) * kvlg_ref[...]
    else:
        ckv = kvl
    ckv_b = ckv.astype(BF16)
    kn = _dot(ckv_b, wk_ref[...])
    v_out[...] = _dot(ckv_b, wv_ref[...]).astype(BF16)
    kpe = kpe_ref[...]
    if do_q:
        @pl.when(pl.program_id(0) < CTX_ROWS // TM_MID)
        def _():
            half = A_ROPE // 2
            ckv_out[...] = ckv.reshape(ckv_out.shape)
            kpe_out[...] = jnp.concatenate([kpe[:, :half], kpe[:, LANES // 2:LANES // 2 + half]],
                                           axis=1).reshape(kpe_out.shape)
    sp = _row_sums(kpe * kpe)
    g_nope = kng_ref[:, :A_NOPE]
    g_rope = kng_ref[:, A_NOPE:]
    for h in range(A_HEADS):
        a = kn[:, h * A_NOPE:(h + 1) * A_NOPE]
        r = lax.rsqrt((_row_sums(a * a) + sp) * inv_qk + EPS)
        k_out[:, h * A_QK_PAD:h * A_QK_PAD + A_NOPE] = (a * r * g_nope).astype(BF16)
        k_out[:, h * A_QK_PAD + A_NOPE:(h + 1) * A_QK_PAD] = _rope(kpe * r * g_rope, cos_t, sin_t).astype(BF16)


def _layer_slab_spec(tm, layer, width):
    n_ctx = CTX_ROWS // tm
    return pl.BlockSpec((tm // SEQ, None, SEQ, width), lambda i: (jnp.minimum(i, n_ctx - 1), layer, 0, 0))


def _call_with_carried(kernel, n_plain_out, carried, **kw):
    def run(*args, in_specs):
        args = list(args)
        in_specs = list(in_specs)
        n_real = len(args)
        aliases = {}
        for idx, arr in enumerate(carried):
            if arr is not None:
                aliases[len(args)] = n_plain_out + idx
                args.append(arr)
                in_specs.append(pl.BlockSpec(memory_space=pl.ANY))
        n_in = len(args)

        def body(*refs):
            kernel(*refs[:n_real], *refs[n_in:])

        return pl.pallas_call(body, in_specs=in_specs, input_output_aliases=aliases, **kw)(*args)
    return run


def _amid_tokens(y, wq, wk, wv, qlg, kvlg, qng, kng, cos_t, sin_t, layer, ckv_new, kpe_new):
    tm = TM_MID
    const = lambda i: (0, 0)
    rope_map = lambda i: (_rope_block_of_tile(i, tm), 0)
    n_ab = (DEPTH + 1) // 2
    run = _call_with_carried(
        functools.partial(_amid_kernel, do_q=True, norm_kv=True), 3, [ckv_new, kpe_new],
        grid=(ROWS // tm,),
        out_specs=[pl.BlockSpec((tm, A_HEADS * A_QK_PAD), lambda i: (i, 0)),
                   pl.BlockSpec((tm, A_HEADS * A_QK_PAD), lambda i: (i, 0)),
                   pl.BlockSpec((tm, A_HEADS * A_VDIM), lambda i: (i, 0)),
                   _layer_slab_spec(tm, layer, KV_LORA), _layer_slab_spec(tm, layer, A_ROPE)],
        out_shape=[jax.ShapeDtypeStruct((ROWS, A_HEADS * A_QK_PAD), BF16),
                   jax.ShapeDtypeStruct((ROWS, A_HEADS * A_QK_PAD), BF16),
                   jax.ShapeDtypeStruct((ROWS, A_HEADS * A_VDIM), BF16),
                   jax.ShapeDtypeStruct((BATCH, n_ab, SEQ, KV_LORA), F32),
                   jax.ShapeDtypeStruct((BATCH, n_ab, SEQ, A_ROPE), F32)],
        compiler_params=_cparams(("arbitrary",)),
        name="mla_qkv_prep")
    return run(y, y, y, wq, wk, wv, qlg, kvlg, qng, kng, cos_t, sin_t,
               in_specs=[pl.BlockSpec((tm, Q_LORA), lambda i: (i, AB_QLAT // Q_LORA)),
                         pl.BlockSpec((tm, KV_LORA), lambda i: (i, AB_KVLAT // KV_LORA)),
                         pl.BlockSpec((tm, LANES), lambda i: (i, AB_KPE // LANES)),
                         pl.BlockSpec(wq.shape, const), pl.BlockSpec(wk.shape, const), pl.BlockSpec(wv.shape, const),
                         pl.BlockSpec(qlg.shape, const), pl.BlockSpec(kvlg.shape, const),
                         pl.BlockSpec(qng.shape, const), pl.BlockSpec(kng.shape, const),
                         pl.BlockSpec((tm, LANES), rope_map), pl.BlockSpec((tm, LANES), rope_map)])


def _amid_cache(ckv, kpe, wk, wv, kvlg, kng, cos_t, sin_t):
    rows = ckv.shape[0]
    tm = TM_MID
    const = lambda i: (0, 0)
    outs = pl.pallas_call(
        functools.partial(_amid_kernel, do_q=False, norm_kv=False),
        grid=(rows // tm,),
        in_specs=[pl.BlockSpec((tm, KV_LORA), lambda i: (i, 0)),
                  pl.BlockSpec((tm, LANES), lambda i: (i, 0)),
                  pl.BlockSpec(wk.shape, const), pl.BlockSpec(wv.shape, const),
                  pl.BlockSpec(kvlg.shape, const), pl.BlockSpec(kng.shape, const),
                  pl.BlockSpec((tm, LANES), const), pl.BlockSpec((tm, LANES), const)],
        out_specs=[pl.BlockSpec((tm, A_HEADS * A_QK_PAD), lambda i: (i, 0)),
                   pl.BlockSpec((tm, A_HEADS * A_VDIM), lambda i: (i, 0))],
        out_shape=[jax.ShapeDtypeStruct((rows, A_HEADS * A_QK_PAD), BF16),
                   jax.ShapeDtypeStruct((rows, A_HEADS * A_VDIM), BF16)],
        compiler_params=_cparams(("parallel",)),
        name="mla_cache_kv_prep",
    )(ckv, kpe, wk, wv, kvlg, kng, cos_t, sin_t)
    return outs[0], outs[1]


def _scores(q, srcs):
    zs = []
    for k, _, bias in srcs:
        z = _dot_nt(q, k)
        if bias is not None:
            z = (z.reshape(z.shape[0] // bias.shape[0], *bias.shape) + bias).reshape(z.shape)
        zs.append(z)
    return zs


def _softmax_pv(zs, srcs, sink_z=None):
    tile_max = None
    for z in zs:
        for j in range(z.shape[1] // LANES):
            blk = z[:, j * LANES:(j + 1) * LANES]
            tile_max = blk if tile_max is None else jnp.maximum(tile_max, blk)
    m = tile_max.max(axis=-1, keepdims=True)
    if sink_z is not None:
        m = jnp.maximum(m, sink_z)
    acc = None
    for z, (_, v, _) in zip(zs, srcs):
        o = _dot(jnp.exp2(z - m).astype(BF16), jnp.concatenate([v, jnp.ones_like(v)], axis=1))
        acc = o if acc is None else acc + o
    dv = acc.shape[1] // 2
    den = acc[:, dv:]
    if sink_z is not None:
        den = den + jnp.exp2(sink_z - m)
    return acc[:, :dv] / den


def _attend_streams(n, q_of, srcs_of, sink_of=None):
    outs = []
    zs = _scores(q_of(0), srcs_of(0))
    for t in range(n):
        nxt = _scores(q_of(t + 1), srcs_of(t + 1)) if t + 1 < n else None
        outs.append(_softmax_pv(zs, srcs_of(t), None if sink_of is None else sink_of(t)))
        zs = nxt
    return outs


def _attn_a_kernel(*refs, n_src, heads):
    q_ref = refs[0]
    k_refs = refs[1:1 + n_src]
    v_refs = refs[1 + n_src:1 + 2 * n_src]
    gate_ref, o_ref = refs[1 + 2 * n_src:]
    sub = min(ATTN_SUB, q_ref.shape[0])
    streams = [(h, r) for h in range(heads) for r in range(q_ref.shape[0] // sub)]

    def rows(t):
        return slice(streams[t][1] * sub, (streams[t][1] + 1) * sub)

    def q_of(t):
        h = streams[t][0]
        return q_ref[rows(t), h * A_QK_PAD:(h + 1) * A_QK_PAD]

    def srcs_of(t):
        h = streams[t][0]
        return [(k_ref[:, h * A_QK_PAD:(h + 1) * A_QK_PAD], v_ref[:, h * A_VDIM:(h + 1) * A_VDIM], None)
                for k_ref, v_ref in zip(k_refs, v_refs)]

    outs = _attend_streams(len(streams), q_of, srcs_of)
    for t, o in enumerate(outs):
        cols = slice(streams[t][0] * A_VDIM, (streams[t][0] + 1) * A_VDIM)
        o_ref[rows(t), cols] = (o * _silu(gate_ref[rows(t), cols])).astype(o_ref.dtype)


def _attn_a(q, k, v, y, mix, kc=None, vc=None):
    latent = kc is not None
    if latent:
        nb, t, tq, hp, row0 = DEC_BATCH, DEC_SEQ, TQ_A, 1, CTX_ROWS
    else:
        nb, t, tq, hp, row0 = BATCH, SEQ, SEQ, A_HEADS, 0
    nq = t // tq
    qrow = lambda b, h, i: row0 // tq + b * nq + i
    in_specs = [pl.BlockSpec((tq, hp * A_QK_PAD), lambda b, h, i: (qrow(b, h, i), h))]
    args = [q]
    if latent:
        in_specs.append(pl.BlockSpec((PAST_LEN, hp * A_QK_PAD), lambda b, h, i: (b, h)))
        args.append(kc)
    in_specs.append(pl.BlockSpec((t, hp * A_QK_PAD), lambda b, h, i: (row0 // t + b, h)))
    args.append(k)
    if latent:
        in_specs.append(pl.BlockSpec((PAST_LEN, hp * A_VDIM), lambda b, h, i: (b, h)))
        args.append(vc)
    in_specs.append(pl.BlockSpec((t, hp * A_VDIM), lambda b, h, i: (row0 // t + b, h)))
    args.append(v)
    in_specs.append(pl.BlockSpec((tq, hp * A_VDIM), lambda b, h, i: (qrow(b, h, i), AB_AGATE // (hp * A_VDIM) + h)))
    args.append(y)
    run = _call_with_carried(
        functools.partial(_attn_a_kernel, n_src=2 if latent else 1, heads=hp), 0, [mix],
        grid=(nb, A_HEADS // hp, nq),
        out_specs=[pl.BlockSpec((tq, hp * A_VDIM), lambda b, h, i: (qrow(b, h, i), h))],
        out_shape=[_MIX_SHAPE],
        compiler_params=_cparams(("parallel", "parallel", "parallel")),
        name="mla_attention_latent" if latent else "mla_attention_context")
    return run(*args, in_specs=in_specs)[0]


def _hgrn_constants():
    c, nl = HG_CHUNK, HG_LEVELS
    t = np.arange(c)[:, None]
    u = np.arange(c)[None, :]
    tri_f = (u <= t).astype(np.float32)
    mask_f = np.zeros((nl, c, c), np.float32)
    coef_f = np.zeros((nl, c, LANES), np.float32)
    for l in range(nl):
        half = c >> (l + 1)
        seg = 2 * half
        mask_f[l] = ((u // seg) == (t // seg)) & ((t % seg) >= half) & ((u % seg) < half)
        later = np.broadcast_to((t % seg) >= half, (c, LANES))
        coef_f[l] = np.where(later, 1.0, -1.0 if half > 1 else 0.0)
    tri_b = tri_f[::-1, ::-1]
    mask_b = mask_f[:, ::-1, ::-1]
    coef_b = coef_f[:, ::-1, :]
    to_tri = lambda a: jnp.asarray(np.concatenate([a, a, a], axis=1), BF16)
    to_f32 = lambda a: jnp.asarray(np.ascontiguousarray(a), F32)
    return (to_tri(tri_f), to_tri(tri_b)), (to_f32(mask_f), to_f32(mask_b)), (to_f32(coef_f), to_f32(coef_b))


def _hgrn_decays(x, lb, tri3):
    kk = (1.0 - lb) * _sigmoid(-x)
    lf = jnp.log2(1.0 - kk)
    hi = lf.astype(BF16)
    r1 = lf - hi.astype(F32)
    mid = r1.astype(BF16)
    lo = (r1 - mid.astype(F32)).astype(BF16)
    return kk, lf, _dot(tri3, jnp.concatenate([hi, mid, lo], axis=0))


def _hgrn_level_arg(l, cs, lf, coef, forward):
    c = HG_CHUNK
    nv = c // 8
    half = c >> (l + 1)
    if half == 1:
        return lf * coef
    cs3 = cs.reshape(nv, 8, LANES)

    def in_vreg_row(r):
        return jnp.broadcast_to(cs3[:, r:r + 1, :], (nv, 8, LANES))

    if half >= 8:
        m = half // 8
        nseg = nv // (2 * m)
        edge = cs3[:, 7:8, :] if forward else cs3[:, 0:1, :]
        e4 = edge.reshape(nseg, 2 * m, 1, LANES)
        a = e4[:, m - 1:m] if forward else e4[:, m:m + 1]
        anchor = jnp.broadcast_to(a, (nseg, 2 * m, 8, LANES)).reshape(c, LANES)
    elif half == 4:
        anchor = in_vreg_row(3 if forward else 4).reshape(c, LANES)
    else:
        r0, r1 = (1, 5) if forward else (2, 6)
        sub = lax.broadcasted_iota(jnp.int32, (nv, 8, LANES), 1)
        anchor = jnp.where(sub < 4, in_vreg_row(r0), in_vreg_row(r1)).reshape(c, LANES)
    return (cs - anchor) * coef


def _hgrn_kernel(*refs, t_len, heads, zero_init, emit_state):
    c = HG_CHUNK
    n_chunks = t_len // c
    it = iter(refs)
    bq_ref, ff_ref, fb_ref, vi_ref, bg_ref, lb_ref, hg_ref = (next(it) for _ in range(7))
    if not zero_init:
        s0f_ref, s0b_ref = next(it), next(it)
    trif_ref, trib_ref, maskf_ref, maskb_ref, coeff_ref, coefb_ref = (next(it) for _ in range(6))
    o_ref = next(it)
    if emit_state:
        sf_ref, sb_ref = next(it), next(it)
    of_ref, ob_ref, stf_ref, stb_ref = (next(it) for _ in range(4))

    for h in range(heads):
        if zero_init:
            stf_ref[h] = jnp.zeros((B_DV, B_DK), F32)
            stb_ref[h] = jnp.zeros((B_DV, B_DK), F32)
        else:
            stf_ref[h] = s0f_ref[h].T
            stb_ref[h] = s0b_ref[h].T
    lb = lb_ref[...]

    nl = HG_LEVELS
    unroll = min(HG_UNROLL // heads, n_chunks)
    dirs = ((ff_ref, 0, trif_ref, maskf_ref, coeff_ref, stf_ref, of_ref, True),
            (fb_ref, 1, trib_ref, maskb_ref, coefb_ref, stb_ref, ob_ref, False))

    def body(i, carry):
        chains = []
        for h in range(heads):
            cols = slice(h * LANES, (h + 1) * LANES)
            for f_ref, lb_row, tri_ref, mask_ref, coef_ref, st_ref, out_ref, forward in dirs:
                for u in range(unroll):
                    k = i * unroll + u
                    r0 = pl.multiple_of((k if forward else n_chunks - 1 - k) * c, c)
                    chains.append(dict(rows=pl.ds(r0, c), cols=cols, head=h, f_ref=f_ref,
                                       lb=lb[lb_row:lb_row + 1, cols], tri_ref=tri_ref, mask_ref=mask_ref,
                                       coef_ref=coef_ref, st_ref=st_ref, out_ref=out_ref, forward=forward))
        for ch in chains:
            ch["kk"], ch["lf"], ch["cs"] = _hgrn_decays(ch["f_ref"][ch["rows"], ch["cols"]], ch["lb"],
                                                        ch["tri_ref"][...])
            ch["q"] = _silu(bq_ref[ch["rows"], ch["cols"]])
            ch["v"] = vi_ref[ch["rows"], ch["cols"]]
            ch["q16"] = ch["q"].astype(BF16)
            ch["kk16"] = ch["kk"].astype(BF16)
            ch["sc"] = jnp.zeros((c, c), F32)
        for l in range(nl):
            for ch in chains:
                el = jnp.exp2(_hgrn_level_arg(l, ch["cs"], ch["lf"], ch["coef_ref"][l], ch["forward"])).astype(BF16)
                ch["sc"] = ch["sc"] + ch["mask_ref"][l] * _dot_nt(ch["q16"] * el, ch["kk16"] * el)
        for ch in chains:
            cs, q, kk, v = ch["cs"], ch["q"], ch["kk"], ch["v"]
            end = cs[c - 1:c, :] if ch["forward"] else cs[0:1, :]
            ch["end"] = end
            ch["q_in"] = (q * jnp.exp2(cs)).astype(BF16)
            ch["upd"] = _dot(v.T.astype(BF16), (kk * jnp.exp2(end - cs)).astype(BF16))
            ch["o"] = _dot(ch["sc"].astype(BF16), v.astype(BF16)) + jnp.sum(q * kk, axis=-1, keepdims=True) * v
        for ch in chains:
            st = ch["st_ref"][ch["head"]]
            ch["out_ref"][ch["rows"], ch["cols"]] = ch["o"] + _dot_nt(ch["q_in"], st.astype(BF16))
            ch["st_ref"][ch["head"]] = jnp.exp2(ch["end"]) * st + ch["upd"]
        return carry

    lax.fori_loop(0, n_chunks // unroll, body, 0)
    if emit_state:
        for h in range(heads):
            sf_ref[h] = stf_ref[h].T
            sb_ref[h] = stb_ref[h].T

    hg = hg_ref[...]
    blk = min(HG_UNROLL, n_chunks) * c

    def finish(i, carry):
        rows = pl.ds(pl.multiple_of(i * blk, blk), blk)
        for h in range(heads):
            cols = slice(h * LANES, (h + 1) * LANES)
            o = of_ref[rows, cols] + ob_ref[rows, cols]
            o = o * lax.rsqrt(jnp.mean(o * o, axis=-1, keepdims=True) + EPS) * hg
            o_ref[rows, cols] = (o * _silu(bg_ref[rows, cols])).astype(o_ref.dtype)
        return carry

    lax.fori_loop(0, t_len // blk, finish, 0)


def _hgrn(y, lb, hg, mix, consts, s0f=None, s0b=None, layer=None, sf_new=None, sb_new=None):
    latent = s0f is not None
    if latent:
        nb, t, row0, hp = DEC_BATCH, DEC_SEQ, CTX_ROWS, 1
    else:
        nb, t, row0, hp = BATCH, SEQ, 0, 2
    w = hp * LANES
    rb = lambda b: row0 // t + b
    col = lambda off: (lambda b, h: (rb(b), off // w + h))
    const2 = lambda b, h: (0, 0)
    const3 = lambda b, h: (0, 0, 0)
    in_specs = [pl.BlockSpec((t, w), col(AB_BQ)), pl.BlockSpec((t, w), col(AB_BFF)),
                pl.BlockSpec((t, w), col(AB_BFB)), pl.BlockSpec((t, w), col(AB_BI)),
                pl.BlockSpec((t, w), col(AB_BGATE)),
                pl.BlockSpec((2, w), lambda b, h: (0, h)),
                pl.BlockSpec((1, LANES), const2)]
    args = [y, y, y, y, y, lb, hg]
    if latent:
        st_spec = pl.BlockSpec((None, hp, B_DK, B_DV), lambda b, h: (b, h, 0, 0))
        in_specs += [st_spec, st_spec]
        args += [s0f, s0b]
    tris, masks, coefs = consts
    in_specs += ([pl.BlockSpec(a.shape, const2) for a in tris] + [pl.BlockSpec(a.shape, const3) for a in masks]
                 + [pl.BlockSpec(a.shape, const3) for a in coefs])
    args += [*tris, *masks, *coefs]
    out_specs = [pl.BlockSpec((t, w), lambda b, h: (rb(b), A_HEADS * A_VDIM // w + h))]
    out_shape = [_MIX_SHAPE]
    carried = [mix]
    if not latent:
        st_out = pl.BlockSpec((None, None, hp, B_DK, B_DV), lambda b, h: (b, layer, h, 0, 0))
        out_specs += [st_out, st_out]
        out_shape += [jax.ShapeDtypeStruct((nb, (DEPTH + 1) // 2, B_HEADS, B_DK, B_DV), F32)] * 2
        carried += [sf_new, sb_new]
    run = _call_with_carried(
        functools.partial(_hgrn_kernel, t_len=t, heads=hp, zero_init=not latent, emit_state=not latent), 0, carried,
        grid=(nb, B_HEADS // hp),
        out_specs=out_specs,
        out_shape=out_shape,
        scratch_shapes=[pltpu.VMEM((t, w), F32), pltpu.VMEM((t, w), F32),
                        pltpu.VMEM((hp, B_DV, B_DK), F32), pltpu.VMEM((hp, B_DV, B_DK), F32)],
        compiler_params=_cparams(("parallel", "parallel")),
        name="hgrn2_latent" if latent else "hgrn2_context")
    return run(*args, in_specs=in_specs)


def _cmid_kernel(q_ref, k_ref, v_ref, qg_ref, kg_ref, cos_ref, sin_ref, q_out, k_out, v_out, kc_out, vc_out):
    cos_t = cos_ref[...]
    sin_t = sin_ref[...]
    qg = qg_ref[...]
    kg = kg_ref[...]
    in_ctx = pl.program_id(0) < CTX_ROWS // TM_MID

    def norm(x, g):
        return x * lax.rsqrt(_row_sums(x * x) * (1.0 / C_HEAD_DIM) + EPS) * g

    def rope(x):
        return _rope(x, cos_t, sin_t)

    for h in range(C_HEADS):
        sl = slice(h * C_HEAD_DIM, (h + 1) * C_HEAD_DIM)
        q_out[:, sl] = rope(norm(q_ref[:, sl], qg)).astype(BF16)
    for h in range(C_KV_HEADS):
        sl = slice(h * C_HEAD_DIM, (h + 1) * C_HEAD_DIM)
        kn = norm(k_ref[:, sl], kg)
        k_out[:, sl] = rope(kn).astype(BF16)

        @pl.when(in_ctx)
        def _():
            kc_out[:, :, sl] = kn.reshape(kc_out.shape[0], SEQ, C_HEAD_DIM)
    v = v_ref[...]
    v_out[...] = v.astype(BF16)

    @pl.when(in_ctx)
    def _():
        vc_out[...] = v.reshape(vc_out.shape)


def _cmid(y, qg, kg, cos_t, sin_t, layer, kc_new, vc_new):
    tm = TM_MID
    const = lambda i: (0, 0)
    rope_map = lambda i: (_rope_block_of_tile(i, tm), 0)
    cache = jax.ShapeDtypeStruct((BATCH, DEPTH // 2, SEQ, C_KV_WIDTH), F32)
    run = _call_with_carried(
        _cmid_kernel, 3, [kc_new, vc_new],
        grid=(ROWS // tm,),
        out_specs=[pl.BlockSpec((tm, C_WIDTH), lambda i: (i, 0)),
                   pl.BlockSpec((tm, C_KV_WIDTH), lambda i: (i, 0)),
                   pl.BlockSpec((tm, C_KV_WIDTH), lambda i: (i, 0)),
                   _layer_slab_spec(tm, layer, C_KV_WIDTH), _layer_slab_spec(tm, layer, C_KV_WIDTH)],
        out_shape=[jax.ShapeDtypeStruct((ROWS, C_WIDTH), BF16),
                   jax.ShapeDtypeStruct((ROWS, C_KV_WIDTH), BF16),
                   jax.ShapeDtypeStruct((ROWS, C_KV_WIDTH), BF16), cache, cache],
        compiler_params=_cparams(("arbitrary",)),
        name="gqa_qkv_prep")
    return run(y, y, y, qg, kg, cos_t, sin_t,
               in_specs=[pl.BlockSpec((tm, C_WIDTH), lambda i: (i, C_Q // C_WIDTH)),
                         pl.BlockSpec((tm, C_KV_WIDTH), lambda i: (i, C_K // C_KV_WIDTH)),
                         pl.BlockSpec((tm, C_KV_WIDTH), lambda i: (i, C_V // C_KV_WIDTH)),
                         pl.BlockSpec((1, C_HEAD_DIM), const), pl.BlockSpec((1, C_HEAD_DIM), const),
                         pl.BlockSpec((tm, LANES), rope_map), pl.BlockSpec((tm, LANES), rope_map)])


def _band_start(i, tq, t_len):
    return pl.multiple_of(jnp.clip(i * tq - WINDOW, 0, t_len - (tq + 2 * WINDOW)), WINDOW)


def _band_bias(tq, t_len):
    width = tq + 2 * WINDOW
    i = np.arange(t_len // tq)[:, None, None]
    qpos = i * tq + np.arange(tq)[None, :, None]
    kpos = np.clip(i * tq - WINDOW, 0, t_len - width) + np.arange(width)[None, None, :]
    return jnp.asarray(np.where(np.abs(kpos - qpos) <= WINDOW, 0.0, NEG_BIG), F32)


def _attn_c_kernel(*refs, band, tq, t_len, groups, stack):
    if band:
        q_ref, kc_ref, vc_ref, kl_ref, vl_ref, bias_ref, sink_ref, gate_ref, o_ref = refs
    else:
        q_ref, kc_ref, vc_ref, sink_ref, gate_ref, o_ref = refs
    hd = C_HEAD_DIM
    sub = WINDOW if band else tq
    width = sub + 2 * WINDOW
    streams = [(g, s, u) for g in range(groups) for u in range(tq // sub) for s in range(C_GROUP // stack)]

    def heads_of(t):
        g, s, _ = streams[t]
        return [g * C_GROUP + s * stack + r for r in range(stack)]

    def rows_of(t):
        u = streams[t][2]
        return slice(u * sub, (u + 1) * sub)

    def q_of(t):
        return jnp.concatenate([q_ref[rows_of(t), h * hd:(h + 1) * hd] for h in heads_of(t)], axis=0)

    def srcs_of(t):
        g, _, u = streams[t]
        cols = slice(g * hd, (g + 1) * hd)
        srcs = [(kc_ref[:, cols], vc_ref[:, cols], None)]
        if band:
            start = _band_start(pl.program_id(2) * (tq // sub) + u, sub, t_len)
            srcs.append((kl_ref[pl.ds(start, width), cols], vl_ref[pl.ds(start, width), cols], bias_ref[u]))
        return srcs

    def sink_of(t):
        return jnp.concatenate([jnp.broadcast_to(sink_ref[h][:, :1] * LOG2_E, (sub, 1)) for h in heads_of(t)], axis=0)

    outs = _attend_streams(len(streams), q_of, srcs_of, sink_of)
    for t, o in enumerate(outs):
        for r, h in enumerate(heads_of(t)):
            cols = slice(h * hd, (h + 1) * hd)
            o_ref[rows_of(t), cols] = (o[r * sub:(r + 1) * sub] * _silu(gate_ref[rows_of(t), cols])).astype(o_ref.dtype)


def _attn_c(q, k, v, y, sink, mix, kc=None, vc=None):
    latent = kc is not None
    if latent:
        nb, t, tq, gp, stack, row0 = DEC_BATCH, DEC_SEQ, TQ_C, 1, 4, CTX_ROWS
    else:
        nb, t, tq, gp, stack, row0 = BATCH, SEQ, SEQ, C_KV_HEADS, C_GROUP, 0
    gw = gp * C_GROUP * C_HEAD_DIM
    kvw = gp * C_HEAD_DIM
    nq = t // tq
    qrow = lambda b, g, i: row0 // tq + b * nq + i
    own_kv = pl.BlockSpec((t, kvw), lambda b, g, i: (row0 // t + b, g))
    in_specs = [pl.BlockSpec((tq, gw), lambda b, g, i: (qrow(b, g, i), g))]
    args = [q]
    if latent:
        ctx_kv = pl.BlockSpec((PAST_LEN, kvw), lambda b, g, i: (b, g))
        in_specs += [ctx_kv, ctx_kv, own_kv, own_kv,
                     pl.BlockSpec((None, tq // WINDOW, WINDOW, 3 * WINDOW), lambda b, g, i: (i, 0, 0, 0))]
        args += [kc, vc, k, v, _band_bias(WINDOW, t).reshape(nq, tq // WINDOW, WINDOW, 3 * WINDOW)]
    else:
        in_specs += [own_kv, own_kv]
        args += [k, v]
    in_specs.append(pl.BlockSpec((gp * C_GROUP, 1, LANES), lambda b, g, i: (g, 0, 0)))
    args.append(sink)
    in_specs.append(pl.BlockSpec((tq, gw), lambda b, g, i: (qrow(b, g, i), C_GATE // gw + g)))
    args.append(y)
    run = _call_with_carried(
        functools.partial(_attn_c_kernel, band=latent, tq=tq, t_len=t, groups=gp, stack=stack), 0, [mix],
        grid=(nb, C_KV_HEADS // gp, nq),
        out_specs=[pl.BlockSpec((tq, gw), lambda b, g, i: (qrow(b, g, i), g))],
        out_shape=[_MIX_SHAPE],
        compiler_params=_cparams(("parallel", "parallel", "parallel")),
        name="gqa_attention_latent" if latent else "gqa_attention_context")
    return run(*args, in_specs=in_specs)[0]


def _axial_angles(n_tokens, rot_dim):
    rows = n_tokens // GRID_W
    row = jnp.repeat(jnp.arange(rows, dtype=F32), GRID_W)
    col = jnp.tile(jnp.arange(GRID_W, dtype=F32), rows)
    n_freq = rot_dim // 4
    inv = ROPE_BASE ** (-jnp.arange(n_freq, dtype=F32) / n_freq)
    return jnp.concatenate([row[:, None] * inv, col[:, None] * inv], axis=-1)


def _spread_halves(a, fill=0.0):
    h = a.shape[-1] // 2
    pad = jnp.full(a.shape[:-1] + (LANES // 2 - h,), fill, a.dtype)
    return jnp.concatenate([a[..., :h], pad, a[..., h:], pad], axis=-1)


def _spread_head(a):
    return jnp.concatenate([a[..., :A_NOPE], _spread_halves(a[..., A_NOPE:])], axis=-1)


def _rope_tables(rot_dim, tm):
    ang = _axial_angles(DEC_SEQ, rot_dim)
    cos, sin = jnp.cos(ang), jnp.sin(ang)
    cos_t = _spread_halves(jnp.concatenate([cos, cos], axis=-1), 1.0)
    sin_t = _spread_halves(jnp.concatenate([-sin, sin], axis=-1))
    cos_t = jnp.concatenate([jnp.ones((tm, LANES), F32), cos_t], axis=0)
    sin_t = jnp.concatenate([jnp.zeros((tm, LANES), F32), sin_t], axis=0)
    return cos_t, sin_t


def _lower_bounds(lb_logits):
    p = jax.nn.softmax(lb_logits.astype(F32), axis=0)
    return jnp.cumsum(p, axis=0) - p[0:1]


def kernel(x_prompt, x_sample, cache_ckv, cache_kpe, state_hgrn_fwd, state_hgrn_bwd, cache_k_c, cache_v_c, c, c_ctx,
           mod_w_ab, mod_b_ab, norm_ab, w_in_ab, q_lora_norm, kv_lora_norm, w_q_up, w_kv_up, q_norm_ab, k_norm_ab,
           hgrn_lb_logits, hgrn_out_norm, w_out_ab, mod_w_c, mod_b_c, norm_c, w_in_c, q_norm_c, k_norm_c, sink_c,
           w_out_c):
    x_parts = [x_prompt.reshape(CTX_ROWS, D_MODEL), x_sample.reshape(LAT_ROWS, D_MODEL)]
    cond8 = jnp.concatenate([c_ctx[None, :], c, jnp.zeros((N_COND - 1 - DEC_BATCH, D_MODEL), F32)], axis=0)
    mods_ab = _modulation(cond8, mod_w_ab, mod_b_ab)
    mods_c = _modulation(cond8, mod_w_c, mod_b_c)
    lower = _lower_bounds(hgrn_lb_logits)
    cos_a, sin_a = _rope_tables(A_ROPE, TM_MID)
    cos_c, sin_c = _rope_tables(C_HEAD_DIM, TM_MID)
    hg_consts = _hgrn_constants()
    w_in_ab16 = _prep_w_in_ab(jnp.swapaxes(w_in_ab, 1, 2))
    w_in_c16, w_out_ab16, w_out_c16 = w_in_c.astype(BF16), w_out_ab.astype(BF16), w_out_c.astype(BF16)

    ckv_new = kpe_new = sf_new = sb_new = kc_new = vc_new = None
    for layer in range(DEPTH):
        j = layer // 2
        last = layer == DEPTH - 1
        if layer % 2 == 0:
            mod = mods_ab[j].reshape(3 * N_COND, 1, D_MODEL)
            wq = _spread_head(w_q_up[j].reshape(Q_LORA, A_HEADS, A_QK)).reshape(Q_LORA, A_HEADS * A_QK_PAD).astype(BF16)
            wkv = w_kv_up[j].reshape(KV_LORA, A_HEADS, A_NOPE + A_VDIM)
            wk = wkv[:, :, :A_NOPE].reshape(KV_LORA, A_HEADS * A_NOPE).astype(BF16)
            wv = wkv[:, :, A_NOPE:].reshape(KV_LORA, A_HEADS * A_VDIM).astype(BF16)
            qlg, kvlg = q_lora_norm[j][None, :], kv_lora_norm[j][None, :]
            qng = _spread_head(q_norm_ab[j] * (A_QK ** -0.5 * LOG2_E))[None, :]
            kng = _spread_head(k_norm_ab[j])[None, :]

            y = _in_proj(x_parts, norm_ab[j][None, :], mod, w_in_ab16, j, w_transposed=True)
            q, k, v, ckv_new, kpe_new = _amid_tokens(y, wq, wk, wv, qlg, kvlg, qng, kng, cos_a, sin_a,
                                                     j, ckv_new, kpe_new)
            kpe_cache = _spread_halves(cache_kpe[:, j].reshape(DEC_BATCH * PAST_LEN, A_ROPE))
            kc, vc = _amid_cache(cache_ckv[:, j].reshape(DEC_BATCH * PAST_LEN, KV_LORA), kpe_cache, wk, wv, kvlg, kng,
                                 cos_a, sin_a)
            mix = _attn_a(q, k, v, y, None)
            mix = _attn_a(q, k, v, y, mix, kc=kc, vc=vc)
            hg = hgrn_out_norm[j][None, :]
            mix, sf_new, sb_new = _hgrn(y, lower[j], hg, mix, hg_consts, layer=j, sf_new=sf_new, sb_new=sb_new)
            (mix,) = _hgrn(y, lower[j], hg, mix, hg_consts, s0f=state_hgrn_fwd[:, j], s0b=state_hgrn_bwd[:, j])
            x_parts = _out_proj(mix, w_out_ab16, j, x_parts, mod, split_out=last)
        else:
            mod = mods_c[j].reshape(3 * N_COND, 1, D_MODEL)
            y = _in_proj(x_parts, norm_c[j][None, :], mod, w_in_c16, j,
                         col_map=lambda t: jnp.where(t < 2, t, jnp.where(t < 4, t + 1, 2)))
            q, k, v, kc_new, vc_new = _cmid(y, q_norm_c[j][None, :] * (C_HEAD_DIM ** -0.5 * LOG2_E),
                                            k_norm_c[j][None, :], cos_c, sin_c,
                                            j, kc_new, vc_new)
            kc = cache_k_c[:, j].reshape(DEC_BATCH * PAST_LEN, C_KV_WIDTH).astype(BF16)
            vc = cache_v_c[:, j].reshape(DEC_BATCH * PAST_LEN, C_KV_WIDTH).astype(BF16)
            sink = jnp.broadcast_to(sink_c[j][:, None, None], (C_HEADS, 1, LANES))
            mix = _attn_c(q, k, v, y, sink, None)
            mix = _attn_c(q, k, v, y, sink, mix, kc=kc, vc=vc)
            x_parts = _out_proj(mix, w_out_c16, j, x_parts, mod, split_out=last)

    cache_c_shape = (BATCH, DEPTH // 2, SEQ, C_KV_HEADS, C_HEAD_DIM)
    return (x_parts[0].reshape(BATCH, SEQ, D_MODEL), x_parts[1].reshape(DEC_BATCH, DEC_SEQ, D_MODEL),
            ckv_new, kpe_new, sf_new, sb_new, kc_new.reshape(cache_c_shape), vc_new.reshape(cache_c_shape))
```

```python
import functools

import numpy as np
import jax
import jax.numpy as jnp
from jax import lax
from jax.experimental import pallas as pl
from jax.experimental.pallas import tpu as pltpu

F32 = jnp.float32
BF16 = jnp.bfloat16

D_MODEL = 2048
BATCH = 16
SEQ = 256
DEPTH = 4
DEC_BATCH = 4
DEC_SEQ = 2048
PAST_LEN = 256
GRID_W = 64
A_HEADS = 8
A_NOPE = 128
A_ROPE = 64
A_VDIM = 128
A_QK = A_NOPE + A_ROPE
A_QK_PAD = 256
Q_LORA = 512
KV_LORA = 256
B_HEADS = 8
B_DK = 128
B_DV = 128
C_HEADS = 16
C_KV_HEADS = 4
C_GROUP = C_HEADS // C_KV_HEADS
C_HEAD_DIM = 128
C_WIDTH = C_HEADS * C_HEAD_DIM
C_KV_WIDTH = C_KV_HEADS * C_HEAD_DIM
WINDOW = 128
ROPE_BASE = 10000.0
EPS = 1e-6
NEG_BIG = -1e30
LOG2_E = 1.4426950408889634

LANES = 128
CTX_ROWS = BATCH * SEQ
LAT_ROWS = DEC_BATCH * DEC_SEQ
ROWS = CTX_ROWS + LAT_ROWS
N_COND = 8

AB_AGATE = 0
AB_BQ = 1024
AB_BFF = 2048
AB_BFB = 3072
AB_BI = 4096
AB_BGATE = 5120
AB_QLAT = 6144
AB_KVLAT = 6656
AB_KPE = 6912
AB_N = 7168
C_Q = 0
C_GATE = 2048
C_K = 4096
C_V = 4608
C_N = 5120

TM_PROJ = 1024
TN_IN = 1024
TM_OUT = 512
TM_MID = 512
TQ_A = 2048
TQ_C = 1024
ATTN_SUB = 512
HG_CHUNK = 128
HG_LEVELS = 7
HG_UNROLL = 8
VMEM_LIMIT = 56 * 1024 * 1024
_MIX_SHAPE = jax.ShapeDtypeStruct((ROWS, D_MODEL), BF16)


def _cparams(sem):
    return pltpu.CompilerParams(dimension_semantics=sem, vmem_limit_bytes=VMEM_LIMIT)


def _sigmoid(x):
    return 0.5 * jnp.tanh(0.5 * x) + 0.5


def _silu(x):
    return x * _sigmoid(x)


def _dot(a, b):
    return jnp.dot(a, b, preferred_element_type=F32)


def _dot_nt(a, b):
    return lax.dot_general(a, b, (((1,), (1,)), ((), ())), preferred_element_type=F32)


def _cond_of_tile(i, tm):
    n_ctx = CTX_ROWS // tm
    per_batch = DEC_SEQ // tm
    return jnp.where(i < n_ctx, 0, 1 + (i - n_ctx) // per_batch)


def _rope_block_of_tile(i, tm):
    n_ctx = CTX_ROWS // tm
    per_batch = DEC_SEQ // tm
    return jnp.where(i < n_ctx, 0, 1 + (i - n_ctx) % per_batch)


def _mod_kernel(c_ref, w_ref, b_ref, o_ref):
    a = _silu(c_ref[...]).astype(BF16)
    o_ref[...] = _dot(a, w_ref[...].astype(BF16)) + b_ref[...]


def _modulation(cond8, w_mod, b_mod):
    n = w_mod.shape[0]
    tn = 1024
    return pl.pallas_call(
        _mod_kernel,
        grid=(n, 3 * D_MODEL // tn),
        in_specs=[pl.BlockSpec((N_COND, D_MODEL), lambda l, j: (0, 0)),
                  pl.BlockSpec((None, D_MODEL, tn), lambda l, j: (l, 0, j)),
                  pl.BlockSpec((None, 1, tn), lambda l, j: (l, 0, j))],
        out_specs=pl.BlockSpec((None, N_COND, tn), lambda l, j: (l, 0, j)),
        out_shape=jax.ShapeDtypeStruct((n, N_COND, 3 * D_MODEL), F32),
        compiler_params=_cparams(("parallel", "parallel")),
        name="adaln_mod",
    )(cond8, w_mod, b_mod.reshape(n, 1, 3 * D_MODEL))


def _prep_w_kernel(w_ref, o_ref):
    n_head = Q_LORA + KV_LORA + A_ROPE
    n_main = AB_N - TN_IN
    half = A_ROPE // 2
    x1_end = Q_LORA + KV_LORA + half
    x2_at = Q_LORA + KV_LORA + LANES // 2
    o_ref[:n_main, :] = w_ref[n_head:, :].astype(BF16)
    o_ref[n_main:, :] = jnp.zeros((TN_IN, o_ref.shape[1]), BF16)
    o_ref[n_main:n_main + x1_end, :] = w_ref[:x1_end, :].astype(BF16)
    o_ref[n_main + x2_at:n_main + x2_at + half, :] = w_ref[x1_end:n_head, :].astype(BF16)


def _prep_w_in_ab(w_t):
    n_layers, n, _ = w_t.shape
    tc = 256
    return pl.pallas_call(
        _prep_w_kernel,
        grid=(n_layers, D_MODEL // tc),
        in_specs=[pl.BlockSpec((None, n, tc), lambda l, i: (l, 0, i))],
        out_specs=pl.BlockSpec((None, AB_N, tc), lambda l, i: (l, 0, i)),
        out_shape=jax.ShapeDtypeStruct((n_layers, AB_N, D_MODEL), BF16),
        compiler_params=_cparams(("parallel", "parallel")),
        name="w_in_ab_layout",
    )(w_t)


def _row_specs(parts, tm, single_buffer=False, tile_of=lambda i, *_: i):
    if len(parts) == 1:
        return [pl.BlockSpec((tm, D_MODEL), lambda *g: (tile_of(*g), 0))]
    n_first = CTX_ROWS // tm
    mode = dict(pipeline_mode=pl.Buffered(1)) if single_buffer else {}
    return [pl.BlockSpec((tm, D_MODEL), lambda *g: (jnp.minimum(tile_of(*g), n_first - 1), 0), **mode),
            pl.BlockSpec((tm, D_MODEL), lambda *g: (jnp.maximum(tile_of(*g) - n_first, 0), 0), **mode)]


IN_SUB = 4
IN_SUB_ROWS = TM_PROJ // IN_SUB


def _norm_mod(x, g, shift, scale):
    r = lax.rsqrt(jnp.mean(x * x, axis=-1, keepdims=True) + EPS)
    return ((x * r * g) * (1.0 + scale) + shift).astype(BF16)


def _norm_tile_kernel(x_ref, g_ref, sh_ref, sc_ref, o_ref):
    o_ref[...] = _norm_mod(x_ref[...], g_ref[...], sh_ref[...], sc_ref[...])


def _norm_first_tile(x0, g, mod):
    const = lambda s: (0, 0)
    return pl.pallas_call(
        _norm_tile_kernel,
        grid=(IN_SUB,),
        in_specs=[pl.BlockSpec((IN_SUB_ROWS, D_MODEL), lambda s: (s, 0)), pl.BlockSpec((1, D_MODEL), const),
                  pl.BlockSpec((None, 1, D_MODEL), lambda s: (0, 0, 0)),
                  pl.BlockSpec((None, 1, D_MODEL), lambda s: (1, 0, 0))],
        out_specs=pl.BlockSpec((IN_SUB_ROWS, D_MODEL), lambda s: (s, 0)),
        out_shape=jax.ShapeDtypeStruct((TM_PROJ, D_MODEL), BF16),
        compiler_params=_cparams(("parallel",)),
        name="norm_mod_first_tile",
    )(x0, g, mod, mod)


def _in_next_tile(i):
    return jnp.minimum(i + 1, ROWS // TM_PROJ - 1)


def _in_next_sub_block(i, j):
    return _in_next_tile(i) * IN_SUB + jnp.clip(j - 1, 0, IN_SUB - 1)


def _in_kernel(*refs, n_x, w_transposed):
    x_refs = refs[:n_x]
    h0_ref, g_ref, sh_ref, sc_ref, w_ref, o_ref, h_ref, hn_ref = refs[n_x:]
    i, j = pl.program_id(0), pl.program_id(1)

    @pl.when((i == 0) & (j == 0))
    def _():
        h_ref[...] = h0_ref[...]

    @pl.when((i > 0) & (j == 0))
    def _():
        h_ref[...] = hn_ref[...]

    if n_x == 1:
        x = x_refs[0][...]
    else:
        x = jnp.where(_in_next_sub_block(i, j) < CTX_ROWS // IN_SUB_ROWS, x_refs[0][...], x_refs[1][...])
    r0 = pl.multiple_of(jnp.clip(j - 1, 0, IN_SUB - 1) * IN_SUB_ROWS, IN_SUB_ROWS)
    hn_ref[pl.ds(r0, IN_SUB_ROWS), :] = _norm_mod(x, g_ref[...], sh_ref[...], sc_ref[...])
    o_ref[...] = (_dot_nt if w_transposed else _dot)(h_ref[...], w_ref[...])


def _in_proj(x_parts, g, mod, w, layer, col_map=None, w_transposed=False):
    tm, tn = TM_PROJ, TN_IN
    n_tiles = (w.shape[1] if w_transposed else w.shape[2]) // tn
    assert n_tiles > IN_SUB
    col_map = col_map or (lambda t: t)
    if w_transposed:
        w_spec = pl.BlockSpec((None, tn, D_MODEL), lambda i, j: (layer, col_map(j), 0))
    else:
        w_spec = pl.BlockSpec((None, D_MODEL, tn), lambda i, j: (layer, 0, col_map(j)))
    mod_row = lambda part: (lambda i, j: (_cond_of_tile(_in_next_tile(i), tm) * 3 + part, 0, 0))
    h0 = _norm_first_tile(x_parts[0], g, mod)
    return pl.pallas_call(
        functools.partial(_in_kernel, n_x=len(x_parts), w_transposed=w_transposed),
        grid=(ROWS // tm, n_tiles),
        in_specs=_row_specs(x_parts, IN_SUB_ROWS, tile_of=_in_next_sub_block) + [
            pl.BlockSpec((tm, D_MODEL), lambda i, j: (0, 0), pipeline_mode=pl.Buffered(1)),
            pl.BlockSpec((1, D_MODEL), lambda i, j: (0, 0)),
            pl.BlockSpec((None, 1, D_MODEL), mod_row(0)),
            pl.BlockSpec((None, 1, D_MODEL), mod_row(1)), w_spec],
        out_specs=pl.BlockSpec((tm, tn), lambda i, j: (i, j)),
        out_shape=jax.ShapeDtypeStruct((ROWS, n_tiles * tn), F32),
        scratch_shapes=[pltpu.VMEM((tm, D_MODEL), BF16), pltpu.VMEM((tm, D_MODEL), BF16)],
        compiler_params=_cparams(("arbitrary", "arbitrary")),
        name="norm_mod_in_proj",
    )(*x_parts, h0, g, mod, mod, w)


def _out_kernel(*refs, n_x, n_o):
    m_ref, w_ref = refs[:2]
    x_refs = refs[2:2 + n_x]
    gt_ref = refs[2 + n_x]
    o_refs = refs[3 + n_x:]
    y = gt_ref[...] * _dot(m_ref[...], w_ref[...])

    def emit(x_ref, o_ref):
        o_ref[...] = x_ref[...] + y

    if n_x == 1 and n_o == 1:
        emit(x_refs[0], o_refs[0])
    else:
        in_first = pl.program_id(0) < CTX_ROWS // TM_OUT
        pl.when(in_first)(lambda: emit(x_refs[0], o_refs[0]))
        pl.when(jnp.logical_not(in_first))(lambda: emit(x_refs[-1], o_refs[-1]))


def _out_proj(mix, w, layer, x_parts, mod, split_out):
    tm = TM_OUT
    if split_out:
        out_parts = [jax.ShapeDtypeStruct((CTX_ROWS, D_MODEL), F32), jax.ShapeDtypeStruct((LAT_ROWS, D_MODEL), F32)]
    else:
        out_parts = [jax.ShapeDtypeStruct((ROWS, D_MODEL), F32)]
    return pl.pallas_call(
        functools.partial(_out_kernel, n_x=len(x_parts), n_o=len(out_parts)),
        grid=(ROWS // tm,),
        in_specs=[pl.BlockSpec((tm, D_MODEL), lambda i: (i, 0)),
                  pl.BlockSpec((None, D_MODEL, D_MODEL), lambda i: (layer, 0, 0))] + _row_specs(x_parts, tm) + [
                  pl.BlockSpec((None, 1, D_MODEL), lambda i: (_cond_of_tile(i, tm) * 3 + 2, 0, 0))],
        out_specs=_row_specs(out_parts, tm),
        out_shape=out_parts,
        compiler_params=_cparams(("arbitrary",)),
        name="out_proj_residual",
    )(mix, w, *x_parts, mod)


def _rope(x, cos_t, sin_t):
    return x * cos_t + pltpu.roll(x, LANES // 2, 1) * sin_t


def _row_sums(x):
    hi = x.astype(BF16)
    lo = (x - hi.astype(F32)).astype(BF16)
    return _dot(jnp.concatenate([hi, lo], axis=1), jnp.ones((2 * x.shape[1], LANES), BF16))


def _amid_kernel(*refs, do_q, norm_kv):
    if do_q:
        (ql_ref, kvl_ref, kpe_ref, wq_ref, wk_ref, wv_ref, qlg_ref, kvlg_ref, qng_ref, kng_ref, cos_ref, sin_ref,
         q_out, k_out, v_out, ckv_out, kpe_out) = refs
    else:
        (kvl_ref, kpe_ref, wk_ref, wv_ref, kvlg_ref, kng_ref, cos_ref, sin_ref, k_out, v_out) = refs
    cos_t = cos_ref[...]
    sin_t = sin_ref[...]
    inv_qk = 1.0 / A_QK

    if do_q:
        ql = ql_ref[...]
        qn = ql * lax.rsqrt(jnp.mean(ql * ql, axis=-1, keepdims=True) + EPS) * qlg_ref[...]
        qu = _dot(qn.astype(BF16), wq_ref[...])
        g_nope = qng_ref[:, :A_NOPE]
        g_rope = qng_ref[:, A_NOPE:]
        for h in range(A_HEADS):
            qh = qu[:, h * A_QK_PAD:(h + 1) * A_QK_PAD]
            r = lax.rsqrt(_row_sums(qh * qh) * inv_qk + EPS)
            q_out[:, h * A_QK_PAD:h * A_QK_PAD + A_NOPE] = (qh[:, :A_NOPE] * r * g_nope).astype(BF16)
            q_out[:, h * A_QK_PAD + A_NOPE:(h + 1) * A_QK_PAD] = _rope(qh[:, A_NOPE:] * r * g_rope,
                                                                       cos_t, sin_t).astype(BF16)

    kvl = kvl_ref[...]
    if norm_kv:
        ckv = kvl * lax.rsqrt(jnp.mean(kvl * kvl, axis=-1, keepdims=True) + EPS) * kvlg_ref[...]
    else:
        ckv = kvl
    ckv_b = ckv.astype(BF16)
    kn = _dot(ckv_b, wk_ref[...])
    v_out[...] = _dot(ckv_b, wv_ref[...]).astype(BF16)
    kpe = kpe_ref[...]
    if do_q:
        @pl.when(pl.program_id(0) < CTX_ROWS // TM_MID)
        def _():
            half = A_ROPE // 2
            ckv_out[...] = ckv.reshape(ckv_out.shape)
            kpe_out[...] = jnp.concatenate([kpe[:, :half], kpe[:, LANES // 2:LANES // 2 + half]],
                                           axis=1).reshape(kpe_out.shape)
    sp = _row_sums(kpe * kpe)
    g_nope = kng_ref[:, :A_NOPE]
    g_rope = kng_ref[:, A_NOPE:]
    for h in range(A_HEADS):
        a = kn[:, h * A_NOPE:(h + 1) * A_NOPE]
        r = lax.rsqrt((_row_sums(a * a) + sp) * inv_qk + EPS)
        k_out[:, h * A_QK_PAD:h * A_QK_PAD + A_NOPE] = (a * r * g_nope).astype(BF16)
        k_out[:, h * A_QK_PAD + A_NOPE:(h + 1) * A_QK_PAD] = _rope(kpe * r * g_rope, cos_t, sin_t).astype(BF16)


def _layer_slab_spec(tm, layer, width):
    n_ctx = CTX_ROWS // tm
    return pl.BlockSpec((tm // SEQ, None, SEQ, width), lambda i: (jnp.minimum(i, n_ctx - 1), layer, 0, 0))


def _call_with_carried(kernel, n_plain_out, carried, **kw):
    def run(*args, in_specs):
        args = list(args)
        in_specs = list(in_specs)
        n_real = len(args)
        aliases = {}
        for idx, arr in enumerate(carried):
            if arr is not None:
                aliases[len(args)] = n_plain_out + idx
                args.append(arr)
                in_specs.append(pl.BlockSpec(memory_space=pl.ANY))
        n_in = len(args)

        def body(*refs):
            kernel(*refs[:n_real], *refs[n_in:])

        return pl.pallas_call(body, in_specs=in_specs, input_output_aliases=aliases, **kw)(*args)
    return run


def _amid_tokens(y, wq, wk, wv, qlg, kvlg, qng, kng, cos_t, sin_t, layer, ckv_new, kpe_new):
    tm = TM_MID
    const = lambda i: (0, 0)
    rope_map = lambda i: (_rope_block_of_tile(i, tm), 0)
    n_ab = (DEPTH + 1) // 2
    run = _call_with_carried(
        functools.partial(_amid_kernel, do_q=True, norm_kv=True), 3, [ckv_new, kpe_new],
        grid=(ROWS // tm,),
        out_specs=[pl.BlockSpec((tm, A_HEADS * A_QK_PAD), lambda i: (i, 0)),
                   pl.BlockSpec((tm, A_HEADS * A_QK_PAD), lambda i: (i, 0)),
                   pl.BlockSpec((tm, A_HEADS * A_VDIM), lambda i: (i, 0)),
                   _layer_slab_spec(tm, layer, KV_LORA), _layer_slab_spec(tm, layer, A_ROPE)],
        out_shape=[jax.ShapeDtypeStruct((ROWS, A_HEADS * A_QK_PAD), BF16),
                   jax.ShapeDtypeStruct((ROWS, A_HEADS * A_QK_PAD), BF16),
                   jax.ShapeDtypeStruct((ROWS, A_HEADS * A_VDIM), BF16),
                   jax.ShapeDtypeStruct((BATCH, n_ab, SEQ, KV_LORA), F32),
                   jax.ShapeDtypeStruct((BATCH, n_ab, SEQ, A_ROPE), F32)],
        compiler_params=_cparams(("arbitrary",)),
        name="mla_qkv_prep")
    return run(y, y, y, wq, wk, wv, qlg, kvlg, qng, kng, cos_t, sin_t,
               in_specs=[pl.BlockSpec((tm, Q_LORA), lambda i: (i, AB_QLAT // Q_LORA)),
                         pl.BlockSpec((tm, KV_LORA), lambda i: (i, AB_KVLAT // KV_LORA)),
                         pl.BlockSpec((tm, LANES), lambda i: (i, AB_KPE // LANES)),
                         pl.BlockSpec(wq.shape, const), pl.BlockSpec(wk.shape, const), pl.BlockSpec(wv.shape, const),
                         pl.BlockSpec(qlg.shape, const), pl.BlockSpec(kvlg.shape, const),
                         pl.BlockSpec(qng.shape, const), pl.BlockSpec(kng.shape, const),
                         pl.BlockSpec((tm, LANES), rope_map), pl.BlockSpec((tm, LANES), rope_map)])


def _amid_cache(ckv, kpe, wk, wv, kvlg, kng, cos_t, sin_t):
    rows = ckv.shape[0]
    tm = TM_MID
    const = lambda i: (0, 0)
    outs = pl.pallas_call(
        functools.partial(_amid_kernel, do_q=False, norm_kv=False),
        grid=(rows // tm,),
        in_specs=[pl.BlockSpec((tm, KV_LORA), lambda i: (i, 0)),
                  pl.BlockSpec((tm, LANES), lambda i: (i, 0)),
                  pl.BlockSpec(wk.shape, const), pl.BlockSpec(wv.shape, const),
                  pl.BlockSpec(kvlg.shape, const), pl.BlockSpec(kng.shape, const),
                  pl.BlockSpec((tm, LANES), const), pl.BlockSpec((tm, LANES), const)],
        out_specs=[pl.BlockSpec((tm, A_HEADS * A_QK_PAD), lambda i: (i, 0)),
                   pl.BlockSpec((tm, A_HEADS * A_VDIM), lambda i: (i, 0))],
        out_shape=[jax.ShapeDtypeStruct((rows, A_HEADS * A_QK_PAD), BF16),
                   jax.ShapeDtypeStruct((rows, A_HEADS * A_VDIM), BF16)],
        compiler_params=_cparams(("parallel",)),
        name="mla_cache_kv_prep",
    )(ckv, kpe, wk, wv, kvlg, kng, cos_t, sin_t)
    return outs[0], outs[1]


def _scores(q, srcs):
    zs = []
    for k, _, bias in srcs:
        z = _dot_nt(q, k)
        if bias is not None:
            z = (z.reshape(z.shape[0] // bias.shape[0], *bias.shape) + bias).reshape(z.shape)
        zs.append(z)
    return zs


def _softmax_pv(zs, srcs, sink_z=None):
    tile_max = None
    for z in zs:
        for j in range(z.shape[1] // LANES):
            blk = z[:, j * LANES:(j + 1) * LANES]
            tile_max = blk if tile_max is None else jnp.maximum(tile_max, blk)
    m = tile_max.max(axis=-1, keepdims=True)
    if sink_z is not None:
        m = jnp.maximum(m, sink_z)
    acc = None
    for z, (_, v, _) in zip(zs, srcs):
        o = _dot(jnp.exp2(z - m).astype(BF16), jnp.concatenate([v, jnp.ones_like(v)], axis=1))
        acc = o if acc is None else acc + o
    dv = acc.shape[1] // 2
    den = acc[:, dv:]
    if sink_z is not None:
        den = den + jnp.exp2(sink_z - m)
    return acc[:, :dv] / den


def _attend_streams(n, q_of, srcs_of, sink_of=None):
    outs = []
    zs = _scores(q_of(0), srcs_of(0))
    for t in range(n):
        nxt = _scores(q_of(t + 1), srcs_of(t + 1)) if t + 1 < n else None
        outs.append(_softmax_pv(zs, srcs_of(t), None if sink_of is None else sink_of(t)))
        zs = nxt
    return outs


def _attn_a_kernel(*refs, n_src, heads):
    q_ref = refs[0]
    k_refs = refs[1:1 + n_src]
    v_refs = refs[1 + n_src:1 + 2 * n_src]
    gate_ref, o_ref = refs[1 + 2 * n_src:]
    sub = min(ATTN_SUB, q_ref.shape[0])
    streams = [(h, r) for h in range(heads) for r in range(q_ref.shape[0] // sub)]

    def rows(t):
        return slice(streams[t][1] * sub, (streams[t][1] + 1) * sub)

    def q_of(t):
        h = streams[t][0]
        return q_ref[rows(t), h * A_QK_PAD:(h + 1) * A_QK_PAD]

    def srcs_of(t):
        h = streams[t][0]
        return [(k_ref[:, h * A_QK_PAD:(h + 1) * A_QK_PAD], v_ref[:, h * A_VDIM:(h + 1) * A_VDIM], None)
                for k_ref, v_ref in zip(k_refs, v_refs)]

    outs = _attend_streams(len(streams), q_of, srcs_of)
    for t, o in enumerate(outs):
        cols = slice(streams[t][0] * A_VDIM, (streams[t][0] + 1) * A_VDIM)
        o_ref[rows(t), cols] = (o * _silu(gate_ref[rows(t), cols])).astype(o_ref.dtype)


def _attn_a(q, k, v, y, mix, kc=None, vc=None):
    latent = kc is not None
    if latent:
        nb, t, tq, hp, row0 = DEC_BATCH, DEC_SEQ, TQ_A, 1, CTX_ROWS
    else:
        nb, t, tq, hp, row0 = BATCH, SEQ, SEQ, A_HEADS, 0
    nq = t // tq
    qrow = lambda b, h, i: row0 // tq + b * nq + i
    in_specs = [pl.BlockSpec((tq, hp * A_QK_PAD), lambda b, h, i: (qrow(b, h, i), h))]
    args = [q]
    if latent:
        in_specs.append(pl.BlockSpec((PAST_LEN, hp * A_QK_PAD), lambda b, h, i: (b, h)))
        args.append(kc)
    in_specs.append(pl.BlockSpec((t, hp * A_QK_PAD), lambda b, h, i: (row0 // t + b, h)))
    args.append(k)
    if latent:
        in_specs.append(pl.BlockSpec((PAST_LEN, hp * A_VDIM), lambda b, h, i: (b, h)))
        args.append(vc)
    in_specs.append(pl.BlockSpec((t, hp * A_VDIM), lambda b, h, i: (row0 // t + b, h)))
    args.append(v)
    in_specs.append(pl.BlockSpec((tq, hp * A_VDIM), lambda b, h, i: (qrow(b, h, i), AB_AGATE // (hp * A_VDIM) + h)))
    args.append(y)
    run = _call_with_carried(
        functools.partial(_attn_a_kernel, n_src=2 if latent else 1, heads=hp), 0, [mix],
        grid=(nb, A_HEADS // hp, nq),
        out_specs=[pl.BlockSpec((tq, hp * A_VDIM), lambda b, h, i: (qrow(b, h, i), h))],
        out_shape=[_MIX_SHAPE],
        compiler_params=_cparams(("parallel", "parallel", "parallel")),
        name="mla_attention_latent" if latent else "mla_attention_context")
    return run(*args, in_specs=in_specs)[0]


def _hgrn_constants():
    c, nl = HG_CHUNK, HG_LEVELS
    t = np.arange(c)[:, None]
    u = np.arange(c)[None, :]
    tri_f = (u <= t).astype(np.float32)
    mask_f = np.zeros((nl, c, c), np.float32)
    coef_f = np.zeros((nl, c, LANES), np.float32)
    for l in range(nl):
        half = c >> (l + 1)
        seg = 2 * half
        mask_f[l] = ((u // seg) == (t // seg)) & ((t % seg) >= half) & ((u % seg) < half)
        later = np.broadcast_to((t % seg) >= half, (c, LANES))
        coef_f[l] = np.where(later, 1.0, -1.0 if half > 1 else 0.0)
    tri_b = tri_f[::-1, ::-1]
    mask_b = mask_f[:, ::-1, ::-1]
    coef_b = coef_f[:, ::-1, :]
    to_tri = lambda a: jnp.asarray(np.concatenate([a, a, a], axis=1), BF16)
    to_f32 = lambda a: jnp.asarray(np.ascontiguousarray(a), F32)
    return (to_tri(tri_f), to_tri(tri_b)), (to_f32(mask_f), to_f32(mask_b)), (to_f32(coef_f), to_f32(coef_b))


def _hgrn_decays(x, lb, tri3):
    kk = (1.0 - lb) * _sigmoid(-x)
    lf = jnp.log2(1.0 - kk)
    hi = lf.astype(BF16)
    r1 = lf - hi.astype(F32)
    mid = r1.astype(BF16)
    lo = (r1 - mid.astype(F32)).astype(BF16)
    return kk, lf, _dot(tri3, jnp.concatenate([hi, mid, lo], axis=0))


def _hgrn_level_arg(l, cs, lf, coef, forward):
    c = HG_CHUNK
    nv = c // 8
    half = c >> (l + 1)
    if half == 1:
        return lf * coef
    cs3 = cs.reshape(nv, 8, LANES)

    def in_vreg_row(r):
        return jnp.broadcast_to(cs3[:, r:r + 1, :], (nv, 8, LANES))

    if half >= 8:
        m = half // 8
        nseg = nv // (2 * m)
        edge = cs3[:, 7:8, :] if forward else cs3[:, 0:1, :]
        e4 = edge.reshape(nseg, 2 * m, 1, LANES)
        a = e4[:, m - 1:m] if forward else e4[:, m:m + 1]
        anchor = jnp.broadcast_to(a, (nseg, 2 * m, 8, LANES)).reshape(c, LANES)
    elif half == 4:
        anchor = in_vreg_row(3 if forward else 4).reshape(c, LANES)
    else:
        r0, r1 = (1, 5) if forward else (2, 6)
        sub = lax.broadcasted_iota(jnp.int32, (nv, 8, LANES), 1)
        anchor = jnp.where(sub < 4, in_vreg_row(r0), in_vreg_row(r1)).reshape(c, LANES)
    return (cs - anchor) * coef


def _hgrn_kernel(*refs, t_len, heads, zero_init, emit_state):
    c = HG_CHUNK
    n_chunks = t_len // c
    it = iter(refs)
    bq_ref, ff_ref, fb_ref, vi_ref, bg_ref, lb_ref, hg_ref = (next(it) for _ in range(7))
    if not zero_init:
        s0f_ref, s0b_ref = next(it), next(it)
    trif_ref, trib_ref, maskf_ref, maskb_ref, coeff_ref, coefb_ref = (next(it) for _ in range(6))
    o_ref = next(it)
    if emit_state:
        sf_ref, sb_ref = next(it), next(it)
    of_ref, ob_ref, stf_ref, stb_ref = (next(it) for _ in range(4))

    for h in range(heads):
        if zero_init:
            stf_ref[h] = jnp.zeros((B_DV, B_DK), F32)
            stb_ref[h] = jnp.zeros((B_DV, B_DK), F32)
        else:
            stf_ref[h] = s0f_ref[h].T
            stb_ref[h] = s0b_ref[h].T
    lb = lb_ref[...]

    nl = HG_LEVELS
    unroll = min(HG_UNROLL // heads, n_chunks)
    dirs = ((ff_ref, 0, trif_ref, maskf_ref, coeff_ref, stf_ref, of_ref, True),
            (fb_ref, 1, trib_ref, maskb_ref, coefb_ref, stb_ref, ob_ref, False))

    def body(i, carry):
        chains = []
        for h in range(heads):
            cols = slice(h * LANES, (h + 1) * LANES)
            for f_ref, lb_row, tri_ref, mask_ref, coef_ref, st_ref, out_ref, forward in dirs:
                for u in range(unroll):
                    k = i * unroll + u
                    r0 = pl.multiple_of((k if forward else n_chunks - 1 - k) * c, c)
                    chains.append(dict(rows=pl.ds(r0, c), cols=cols, head=h, f_ref=f_ref,
                                       lb=lb[lb_row:lb_row + 1, cols], tri_ref=tri_ref, mask_ref=mask_ref,
                                       coef_ref=coef_ref, st_ref=st_ref, out_ref=out_ref, forward=forward))
        for ch in chains:
            ch["kk"], ch["lf"], ch["cs"] = _hgrn_decays(ch["f_ref"][ch["rows"], ch["cols"]], ch["lb"],
                                                        ch["tri_ref"][...])
            ch["q"] = _silu(bq_ref[ch["rows"], ch["cols"]])
            ch["v"] = vi_ref[ch["rows"], ch["cols"]]
            ch["q16"] = ch["q"].astype(BF16)
            ch["kk16"] = ch["kk"].astype(BF16)
            ch["sc"] = jnp.zeros((c, c), F32)
        for l in range(nl):
            for ch in chains:
                el = jnp.exp2(_hgrn_level_arg(l, ch["cs"], ch["lf"], ch["coef_ref"][l], ch["forward"])).astype(BF16)
                ch["sc"] = ch["sc"] + ch["mask_ref"][l] * _dot_nt(ch["q16"] * el, ch["kk16"] * el)
        for ch in chains:
            cs, q, kk, v = ch["cs"], ch["q"], ch["kk"], ch["v"]
            end = cs[c - 1:c, :] if ch["forward"] else cs[0:1, :]
            ch["end"] = end
            ch["q_in"] = (q * jnp.exp2(cs)).astype(BF16)
            ch["upd"] = _dot(v.T.astype(BF16), (kk * jnp.exp2(end - cs)).astype(BF16))
            ch["o"] = _dot(ch["sc"].astype(BF16), v.astype(BF16)) + jnp.sum(q * kk, axis=-1, keepdims=True) * v
        for ch in chains:
            st = ch["st_ref"][ch["head"]]
            ch["out_ref"][ch["rows"], ch["cols"]] = ch["o"] + _dot_nt(ch["q_in"], st.astype(BF16))
            ch["st_ref"][ch["head"]] = jnp.exp2(ch["end"]) * st + ch["upd"]
        return carry

    lax.fori_loop(0, n_chunks // unroll, body, 0)
    if emit_state:
        for h in range(heads):
            sf_ref[h] = stf_ref[h].T
            sb_ref[h] = stb_ref[h].T

    hg = hg_ref[...]
    blk = min(HG_UNROLL, n_chunks) * c

    def finish(i, carry):
        rows = pl.ds(pl.multiple_of(i * blk, blk), blk)
        for h in range(heads):
            cols = slice(h * LANES, (h + 1) * LANES)
            o = of_ref[rows, cols] + ob_ref[rows, cols]
            o = o * lax.rsqrt(jnp.mean(o * o, axis=-1, keepdims=True) + EPS) * hg
            o_ref[rows, cols] = (o * _silu(bg_ref[rows, cols])).astype(o_ref.dtype)
        return carry

    lax.fori_loop(0, t_len // blk, finish, 0)


def _hgrn(y, lb, hg, mix, consts, s0f=None, s0b=None, layer=None, sf_new=None, sb_new=None):
    latent = s0f is not None
    if latent:
        nb, t, row0, hp = DEC_BATCH, DEC_SEQ, CTX_ROWS, 1
    else:
        nb, t, row0, hp = BATCH, SEQ, 0, 4
    w = hp * LANES
    rb = lambda b: row0 // t + b
    col = lambda off: (lambda b, h: (rb(b), off // w + h))
    const2 = lambda b, h: (0, 0)
    const3 = lambda b, h: (0, 0, 0)
    in_specs = [pl.BlockSpec((t, w), col(AB_BQ)), pl.BlockSpec((t, w), col(AB_BFF)),
                pl.BlockSpec((t, w), col(AB_BFB)), pl.BlockSpec((t, w), col(AB_BI)),
                pl.BlockSpec((t, w), col(AB_BGATE)),
                pl.BlockSpec((2, w), lambda b, h: (0, h)),
                pl.BlockSpec((1, LANES), const2)]
    args = [y, y, y, y, y, lb, hg]
    if latent:
        st_spec = pl.BlockSpec((None, hp, B_DK, B_DV), lambda b, h: (b, h, 0, 0))
        in_specs += [st_spec, st_spec]
        args += [s0f, s0b]
    tris, masks, coefs = consts
    in_specs += ([pl.BlockSpec(a.shape, const2) for a in tris] + [pl.BlockSpec(a.shape, const3) for a in masks]
                 + [pl.BlockSpec(a.shape, const3) for a in coefs])
    args += [*tris, *masks, *coefs]
    out_specs = [pl.BlockSpec((t, w), lambda b, h: (rb(b), A_HEADS * A_VDIM // w + h))]
    out_shape = [_MIX_SHAPE]
    carried = [mix]
    if not latent:
        st_out = pl.BlockSpec((None, None, hp, B_DK, B_DV), lambda b, h: (b, layer, h, 0, 0))
        out_specs += [st_out, st_out]
        out_shape += [jax.ShapeDtypeStruct((nb, (DEPTH + 1) // 2, B_HEADS, B_DK, B_DV), F32)] * 2
        carried += [sf_new, sb_new]
    run = _call_with_carried(
        functools.partial(_hgrn_kernel, t_len=t, heads=hp, zero_init=not latent, emit_state=not latent), 0, carried,
        grid=(nb, B_HEADS // hp),
        out_specs=out_specs,
        out_shape=out_shape,
        scratch_shapes=[pltpu.VMEM((t, w), F32), pltpu.VMEM((t, w), F32),
                        pltpu.VMEM((hp, B_DV, B_DK), F32), pltpu.VMEM((hp, B_DV, B_DK), F32)],
        compiler_params=_cparams(("parallel", "parallel")),
        name="hgrn2_latent" if latent else "hgrn2_context")
    return run(*args, in_specs=in_specs)


def _cmid_kernel(q_ref, k_ref, v_ref, qg_ref, kg_ref, cos_ref, sin_ref, q_out, k_out, v_out, kc_out, vc_out):
    cos_t = cos_ref[...]
    sin_t = sin_ref[...]
    qg = qg_ref[...]
    kg = kg_ref[...]
    in_ctx = pl.program_id(0) < CTX_ROWS // TM_MID

    def norm(x, g):
        return x * lax.rsqrt(_row_sums(x * x) * (1.0 / C_HEAD_DIM) + EPS) * g

    def rope(x):
        return _rope(x, cos_t, sin_t)

    for h in range(C_HEADS):
        sl = slice(h * C_HEAD_DIM, (h + 1) * C_HEAD_DIM)
        q_out[:, sl] = rope(norm(q_ref[:, sl], qg)).astype(BF16)
    for h in range(C_KV_HEADS):
        sl = slice(h * C_HEAD_DIM, (h + 1) * C_HEAD_DIM)
        kn = norm(k_ref[:, sl], kg)
        k_out[:, sl] = rope(kn).astype(BF16)

        @pl.when(in_ctx)
        def _():
            kc_out[:, :, sl] = kn.reshape(kc_out.shape[0], SEQ, C_HEAD_DIM)
    v = v_ref[...]
    v_out[...] = v.astype(BF16)

    @pl.when(in_ctx)
    def _():
        vc_out[...] = v.reshape(vc_out.shape)


def _cmid(y, qg, kg, cos_t, sin_t, layer, kc_new, vc_new):
    tm = TM_MID
    const = lambda i: (0, 0)
    rope_map = lambda i: (_rope_block_of_tile(i, tm), 0)
    cache = jax.ShapeDtypeStruct((BATCH, DEPTH // 2, SEQ, C_KV_WIDTH), F32)
    run = _call_with_carried(
        _cmid_kernel, 3, [kc_new, vc_new],
        grid=(ROWS // tm,),
        out_specs=[pl.BlockSpec((tm, C_WIDTH), lambda i: (i, 0)),
                   pl.BlockSpec((tm, C_KV_WIDTH), lambda i: (i, 0)),
                   pl.BlockSpec((tm, C_KV_WIDTH), lambda i: (i, 0)),
                   _layer_slab_spec(tm, layer, C_KV_WIDTH), _layer_slab_spec(tm, layer, C_KV_WIDTH)],
        out_shape=[jax.ShapeDtypeStruct((ROWS, C_WIDTH), BF16),
                   jax.ShapeDtypeStruct((ROWS, C_KV_WIDTH), BF16),
                   jax.ShapeDtypeStruct((ROWS, C_KV_WIDTH), BF16), cache, cache],
        compiler_params=_cparams(("arbitrary",)),
        name="gqa_qkv_prep")
    return run(y, y, y, qg, kg, cos_t, sin_t,
               in_specs=[pl.BlockSpec((tm, C_WIDTH), lambda i: (i, C_Q // C_WIDTH)),
                         pl.BlockSpec((tm, C_KV_WIDTH), lambda i: (i, C_K // C_KV_WIDTH)),
                         pl.BlockSpec((tm, C_KV_WIDTH), lambda i: (i, C_V // C_KV_WIDTH)),
                         pl.BlockSpec((1, C_HEAD_DIM), const), pl.BlockSpec((1, C_HEAD_DIM), const),
                         pl.BlockSpec((tm, LANES), rope_map), pl.BlockSpec((tm, LANES), rope_map)])


def _band_start(i, tq, t_len):
    return pl.multiple_of(jnp.clip(i * tq - WINDOW, 0, t_len - (tq + 2 * WINDOW)), WINDOW)


def _band_bias(tq, t_len):
    width = tq + 2 * WINDOW
    i = np.arange(t_len // tq)[:, None, None]
    qpos = i * tq + np.arange(tq)[None, :, None]
    kpos = np.clip(i * tq - WINDOW, 0, t_len - width) + np.arange(width)[None, None, :]
    return jnp.asarray(np.where(np.abs(kpos - qpos) <= WINDOW, 0.0, NEG_BIG), F32)


def _attn_c_kernel(*refs, band, tq, t_len, groups, stack):
    if band:
        q_ref, kc_ref, vc_ref, kl_ref, vl_ref, bias_ref, sink_ref, gate_ref, o_ref = refs
    else:
        q_ref, kc_ref, vc_ref, sink_ref, gate_ref, o_ref = refs
    hd = C_HEAD_DIM
    sub = WINDOW if band else tq
    width = sub + 2 * WINDOW
    streams = [(g, s, u) for g in range(groups) for u in range(tq // sub) for s in range(C_GROUP // stack)]

    def heads_of(t):
        g, s, _ = streams[t]
        return [g * C_GROUP + s * stack + r for r in range(stack)]

    def rows_of(t):
        u = streams[t][2]
        return slice(u * sub, (u + 1) * sub)

    def q_of(t):
        return jnp.concatenate([q_ref[rows_of(t), h * hd:(h + 1) * hd] for h in heads_of(t)], axis=0)

    def srcs_of(t):
        g, _, u = streams[t]
        cols = slice(g * hd, (g + 1) * hd)
        srcs = [(kc_ref[:, cols], vc_ref[:, cols], None)]
        if band:
            start = _band_start(pl.program_id(2) * (tq // sub) + u, sub, t_len)
            srcs.append((kl_ref[pl.ds(start, width), cols], vl_ref[pl.ds(start, width), cols], bias_ref[u]))
        return srcs

    def sink_of(t):
        return jnp.concatenate([jnp.broadcast_to(sink_ref[h][:, :1] * LOG2_E, (sub, 1)) for h in heads_of(t)], axis=0)

    outs = _attend_streams(len(streams), q_of, srcs_of, sink_of)
    for t, o in enumerate(outs):
        for r, h in enumerate(heads_of(t)):
            cols = slice(h * hd, (h + 1) * hd)
            o_ref[rows_of(t), cols] = (o[r * sub:(r + 1) * sub] * _silu(gate_ref[rows_of(t), cols])).astype(o_ref.dtype)


def _attn_c(q, k, v, y, sink, mix, kc=None, vc=None):
    latent = kc is not None
    if latent:
        nb, t, tq, gp, stack, row0 = DEC_BATCH, DEC_SEQ, TQ_C, 1, 4, CTX_ROWS
    else:
        nb, t, tq, gp, stack, row0 = BATCH, SEQ, SEQ, C_KV_HEADS, C_GROUP, 0
    gw = gp * C_GROUP * C_HEAD_DIM
    kvw = gp * C_HEAD_DIM
    nq = t // tq
    qrow = lambda b, g, i: row0 // tq + b * nq + i
    own_kv = pl.BlockSpec((t, kvw), lambda b, g, i: (row0 // t + b, g))
    in_specs = [pl.BlockSpec((tq, gw), lambda b, g, i: (qrow(b, g, i), g))]
    args = [q]
    if latent:
        ctx_kv = pl.BlockSpec((PAST_LEN, kvw), lambda b, g, i: (b, g))
        in_specs += [ctx_kv, ctx_kv, own_kv, own_kv,
                     pl.BlockSpec((None, tq // WINDOW, WINDOW, 3 * WINDOW), lambda b, g, i: (i, 0, 0, 0))]
        args += [kc, vc, k, v, _band_bias(WINDOW, t).reshape(nq, tq // WINDOW, WINDOW, 3 * WINDOW)]
    else:
        in_specs += [own_kv, own_kv]
        args += [k, v]
    in_specs.append(pl.BlockSpec((gp * C_GROUP, 1, LANES), lambda b, g, i: (g, 0, 0)))
    args.append(sink)
    in_specs.append(pl.BlockSpec((tq, gw), lambda b, g, i: (qrow(b, g, i), C_GATE // gw + g)))
    args.append(y)
    run = _call_with_carried(
        functools.partial(_attn_c_kernel, band=latent, tq=tq, t_len=t, groups=gp, stack=stack), 0, [mix],
        grid=(nb, C_KV_HEADS // gp, nq),
        out_specs=[pl.BlockSpec((tq, gw), lambda b, g, i: (qrow(b, g, i), g))],
        out_shape=[_MIX_SHAPE],
        compiler_params=_cparams(("parallel", "parallel", "parallel")),
        name="gqa_attention_latent" if latent else "gqa_attention_context")
    return run(*args, in_specs=in_specs)[0]


def _axial_angles(n_tokens, rot_dim):
    rows = n_tokens // GRID_W
    row = jnp.repeat(jnp.arange(rows, dtype=F32), GRID_W)
    col = jnp.tile(jnp.arange(GRID_W, dtype=F32), rows)
    n_freq = rot_dim // 4
    inv = ROPE_BASE ** (-jnp.arange(n_freq, dtype=F32) / n_freq)
    return jnp.concatenate([row[:, None] * inv, col[:, None] * inv], axis=-1)


def _spread_halves(a, fill=0.0):
    h = a.shape[-1] // 2
    pad = jnp.full(a.shape[:-1] + (LANES // 2 - h,), fill, a.dtype)
    return jnp.concatenate([a[..., :h], pad, a[..., h:], pad], axis=-1)


def _spread_head(a):
    return jnp.concatenate([a[..., :A_NOPE], _spread_halves(a[..., A_NOPE:])], axis=-1)


def _rope_tables(rot_dim, tm):
    ang = _axial_angles(DEC_SEQ, rot_dim)
    cos, sin = jnp.cos(ang), jnp.sin(ang)
    cos_t = _spread_halves(jnp.concatenate([cos, cos], axis=-1), 1.0)
    sin_t = _spread_halves(jnp.concatenate([-sin, sin], axis=-1))
    cos_t = jnp.concatenate([jnp.ones((tm, LANES), F32), cos_t], axis=0)
    sin_t = jnp.concatenate([jnp.zeros((tm, LANES), F32), sin_t], axis=0)
    return cos_t, sin_t


def _lower_bounds(lb_logits):
    p = jax.nn.softmax(lb_logits.astype(F32), axis=0)
    return jnp.cumsum(p, axis=0) - p[0:1]


def kernel(x_prompt, x_sample, cache_ckv, cache_kpe, state_hgrn_fwd, state_hgrn_bwd, cache_k_c, cache_v_c, c, c_ctx,
           mod_w_ab, mod_b_ab, norm_ab, w_in_ab, q_lora_norm, kv_lora_norm, w_q_up, w_kv_up, q_norm_ab, k_norm_ab,
           hgrn_lb_logits, hgrn_out_norm, w_out_ab, mod_w_c, mod_b_c, norm_c, w_in_c, q_norm_c, k_norm_c, sink_c,
           w_out_c):
    x_parts = [x_prompt.reshape(CTX_ROWS, D_MODEL), x_sample.reshape(LAT_ROWS, D_MODEL)]
    cond8 = jnp.concatenate([c_ctx[None, :], c, jnp.zeros((N_COND - 1 - DEC_BATCH, D_MODEL), F32)], axis=0)
    mods_ab = _modulation(cond8, mod_w_ab, mod_b_ab)
    mods_c = _modulation(cond8, mod_w_c, mod_b_c)
    lower = _lower_bounds(hgrn_lb_logits)
    cos_a, sin_a = _rope_tables(A_ROPE, TM_MID)
    cos_c, sin_c = _rope_tables(C_HEAD_DIM, TM_MID)
    hg_consts = _hgrn_constants()
    w_in_ab16 = _prep_w_in_ab(jnp.swapaxes(w_in_ab, 1, 2))
    w_in_c16, w_out_ab16, w_out_c16 = w_in_c.astype(BF16), w_out_ab.astype(BF16), w_out_c.astype(BF16)

    ckv_new = kpe_new = sf_new = sb_new = kc_new = vc_new = None
    for layer in range(DEPTH):
        j = layer // 2
        last = layer == DEPTH - 1
        if layer % 2 == 0:
            mod = mods_ab[j].reshape(3 * N_COND, 1, D_MODEL)
            wq = _spread_head(w_q_up[j].reshape(Q_LORA, A_HEADS, A_QK)).reshape(Q_LORA, A_HEADS * A_QK_PAD).astype(BF16)
            wkv = w_kv_up[j].reshape(KV_LORA, A_HEADS, A_NOPE + A_VDIM)
            wk = wkv[:, :, :A_NOPE].reshape(KV_LORA, A_HEADS * A_NOPE).astype(BF16)
            wv = wkv[:, :, A_NOPE:].reshape(KV_LORA, A_HEADS * A_VDIM).astype(BF16)
            qlg, kvlg = q_lora_norm[j][None, :], kv_lora_norm[j][None, :]
            qng = _spread_head(q_norm_ab[j] * (A_QK ** -0.5 * LOG2_E))[None, :]
            kng = _spread_head(k_norm_ab[j])[None, :]

            y = _in_proj(x_parts, norm_ab[j][None, :], mod, w_in_ab16, j, w_transposed=True)
            q, k, v, ckv_new, kpe_new = _amid_tokens(y, wq, wk, wv, qlg, kvlg, qng, kng, cos_a, sin_a,
                                                     j, ckv_new, kpe_new)
            kpe_cache = _spread_halves(cache_kpe[:, j].reshape(DEC_BATCH * PAST_LEN, A_ROPE))
            kc, vc = _amid_cache(cache_ckv[:, j].reshape(DEC_BATCH * PAST_LEN, KV_LORA), kpe_cache, wk, wv, kvlg, kng,
                                 cos_a, sin_a)
            mix = _attn_a(q, k, v, y, None)
            mix = _attn_a(q, k, v, y, mix, kc=kc, vc=vc)
            hg = hgrn_out_norm[j][None, :]
            mix, sf_new, sb_new = _hgrn(y, lower[j], hg, mix, hg_consts, layer=j, sf_new=sf_new, sb_new=sb_new)
            (mix,) = _hgrn(y, lower[j], hg, mix, hg_consts, s0f=state_hgrn_fwd[:, j], s0b=state_hgrn_bwd[:, j])
            x_parts = _out_proj(mix, w_out_ab16, j, x_parts, mod, split_out=last)
        else:
            mod = mods_c[j].reshape(3 * N_COND, 1, D_MODEL)
            y = _in_proj(x_parts, norm_c[j][None, :], mod, w_in_c16, j,
                         col_map=lambda t: jnp.where(t < 2, t, jnp.where(t < 4, t + 1, 2)))
            q, k, v, kc_new, vc_new = _cmid(y, q_norm_c[j][None, :] * (C_HEAD_DIM ** -0.5 * LOG2_E),
                                            k_norm_c[j][None, :], cos_c, sin_c,
                                            j, kc_new, vc_new)
            kc = cache_k_c[:, j].reshape(DEC_BATCH * PAST_LEN, C_KV_WIDTH).astype(BF16)
            vc = cache_v_c[:, j].reshape(DEC_BATCH * PAST_LEN, C_KV_WIDTH).astype(BF16)
            sink = jnp.broadcast_to(sink_c[j][:, None, None], (C_HEADS, 1, LANES))
            mix = _attn_c(q, k, v, y, sink, None)
            mix = _attn_c(q, k, v, y, sink, mix, kc=kc, vc=vc)
            x_parts = _out_proj(mix, w_out_c16, j, x_parts, mod, split_out=last)

    cache_c_shape = (BATCH, DEPTH // 2, SEQ, C_KV_HEADS, C_HEAD_DIM)
    return (x_parts[0].reshape(BATCH, SEQ, D_MODEL), x_parts[1].reshape(DEC_BATCH, DEC_SEQ, D_MODEL),
            ckv_new, kpe_new, sf_new, sb_new, kc_new.reshape(cache_c_shape), vc_new.reshape(cache_c_shape))
```

```python
import functools

import numpy as np
import jax
import jax.numpy as jnp
from jax import lax
from jax.experimental import pallas as pl
from jax.experimental.pallas import tpu as pltpu

F32 = jnp.float32
BF16 = jnp.bfloat16

D_MODEL = 2048
BATCH = 16
SEQ = 256
DEPTH = 4
DEC_BATCH = 4
DEC_SEQ = 2048
PAST_LEN = 256
GRID_W = 64
A_HEADS = 8
A_NOPE = 128
A_ROPE = 64
A_VDIM = 128
A_QK = A_NOPE + A_ROPE
A_QK_PAD = 256
Q_LORA = 512
KV_LORA = 256
B_HEADS = 8
B_DK = 128
B_DV = 128
C_HEADS = 16
C_KV_HEADS = 4
C_GROUP = C_HEADS // C_KV_HEADS
C_HEAD_DIM = 128
C_WIDTH = C_HEADS * C_HEAD_DIM
C_KV_WIDTH = C_KV_HEADS * C_HEAD_DIM
WINDOW = 128
ROPE_BASE = 10000.0
EPS = 1e-6
NEG_BIG = -1e30
LOG2_E = 1.4426950408889634

LANES = 128
CTX_ROWS = BATCH * SEQ
LAT_ROWS = DEC_BATCH * DEC_SEQ
ROWS = CTX_ROWS + LAT_ROWS
N_COND = 8

AB_AGATE = 0
AB_BQ = 1024
AB_BFF = 2048
AB_BFB = 3072
AB_BI = 4096
AB_BGATE = 5120
AB_QLAT = 6144
AB_KVLAT = 6656
AB_KPE = 6912
AB_N = 7168
C_Q = 0
C_GATE = 2048
C_K = 4096
C_V = 4608
C_N = 5120

TM_PROJ = 1024
TN_IN = 1024
TM_OUT = 512
TM_MID = 512
TQ_A = 2048
TQ_C = 1024
ATTN_SUB = 512
HG_CHUNK = 128
HG_LEVELS = 7
HG_UNROLL = 16
VMEM_LIMIT = 56 * 1024 * 1024
_MIX_SHAPE = jax.ShapeDtypeStruct((ROWS, D_MODEL), BF16)


def _cparams(sem):
    return pltpu.CompilerParams(dimension_semantics=sem, vmem_limit_bytes=VMEM_LIMIT)


def _sigmoid(x):
    return 0.5 * jnp.tanh(0.5 * x) + 0.5


def _silu(x):
    return x * _sigmoid(x)


def _dot(a, b):
    return jnp.dot(a, b, preferred_element_type=F32)


def _dot_nt(a, b):
    return lax.dot_general(a, b, (((1,), (1,)), ((), ())), preferred_element_type=F32)


def _cond_of_tile(i, tm):
    n_ctx = CTX_ROWS // tm
    per_batch = DEC_SEQ // tm
    return jnp.where(i < n_ctx, 0, 1 + (i - n_ctx) // per_batch)


def _rope_block_of_tile(i, tm):
    n_ctx = CTX_ROWS // tm
    per_batch = DEC_SEQ // tm
    return jnp.where(i < n_ctx, 0, 1 + (i - n_ctx) % per_batch)


def _mod_kernel(c_ref, w_ref, b_ref, o_ref):
    a = _silu(c_ref[...]).astype(BF16)
    o_ref[...] = _dot(a, w_ref[...].astype(BF16)) + b_ref[...]


def _modulation(cond8, w_mod, b_mod):
    n = w_mod.shape[0]
    tn = 1024
    return pl.pallas_call(
        _mod_kernel,
        grid=(n, 3 * D_MODEL // tn),
        in_specs=[pl.BlockSpec((N_COND, D_MODEL), lambda l, j: (0, 0)),
                  pl.BlockSpec((None, D_MODEL, tn), lambda l, j: (l, 0, j)),
                  pl.BlockSpec((None, 1, tn), lambda l, j: (l, 0, j))],
        out_specs=pl.BlockSpec((None, N_COND, tn), lambda l, j: (l, 0, j)),
        out_shape=jax.ShapeDtypeStruct((n, N_COND, 3 * D_MODEL), F32),
        compiler_params=_cparams(("parallel", "parallel")),
        name="adaln_mod",
    )(cond8, w_mod, b_mod.reshape(n, 1, 3 * D_MODEL))


def _prep_w_kernel(w_ref, o_ref):
    n_head = Q_LORA + KV_LORA + A_ROPE
    n_main = AB_N - TN_IN
    half = A_ROPE // 2
    x1_end = Q_LORA + KV_LORA + half
    x2_at = Q_LORA + KV_LORA + LANES // 2
    o_ref[:n_main, :] = w_ref[n_head:, :].astype(BF16)
    o_ref[n_main:, :] = jnp.zeros((TN_IN, o_ref.shape[1]), BF16)
    o_ref[n_main:n_main + x1_end, :] = w_ref[:x1_end, :].astype(BF16)
    o_ref[n_main + x2_at:n_main + x2_at + half, :] = w_ref[x1_end:n_head, :].astype(BF16)


def _prep_w_in_ab(w_t):
    n_layers, n, _ = w_t.shape
    tc = 256
    return pl.pallas_call(
        _prep_w_kernel,
        grid=(n_layers, D_MODEL // tc),
        in_specs=[pl.BlockSpec((None, n, tc), lambda l, i: (l, 0, i))],
        out_specs=pl.BlockSpec((None, AB_N, tc), lambda l, i: (l, 0, i)),
        out_shape=jax.ShapeDtypeStruct((n_layers, AB_N, D_MODEL), BF16),
        compiler_params=_cparams(("parallel", "parallel")),
        name="w_in_ab_layout",
    )(w_t)


def _row_specs(parts, tm, single_buffer=False, tile_of=lambda i, *_: i):
    if len(parts) == 1:
        return [pl.BlockSpec((tm, D_MODEL), lambda *g: (tile_of(*g), 0))]
    n_first = CTX_ROWS // tm
    mode = dict(pipeline_mode=pl.Buffered(1)) if single_buffer else {}
    return [pl.BlockSpec((tm, D_MODEL), lambda *g: (jnp.minimum(tile_of(*g), n_first - 1), 0), **mode),
            pl.BlockSpec((tm, D_MODEL), lambda *g: (jnp.maximum(tile_of(*g) - n_first, 0), 0), **mode)]


IN_SUB = 4
IN_SUB_ROWS = TM_PROJ // IN_SUB


def _norm_mod(x, g, shift, scale):
    r = lax.rsqrt(jnp.mean(x * x, axis=-1, keepdims=True) + EPS)
    return ((x * r * g) * (1.0 + scale) + shift).astype(BF16)


def _norm_tile_kernel(x_ref, g_ref, sh_ref, sc_ref, o_ref):
    o_ref[...] = _norm_mod(x_ref[...], g_ref[...], sh_ref[...], sc_ref[...])


def _norm_first_tile(x0, g, mod):
    const = lambda s: (0, 0)
    return pl.pallas_call(
        _norm_tile_kernel,
        grid=(IN_SUB,),
        in_specs=[pl.BlockSpec((IN_SUB_ROWS, D_MODEL), lambda s: (s, 0)), pl.BlockSpec((1, D_MODEL), const),
                  pl.BlockSpec((None, 1, D_MODEL), lambda s: (0, 0, 0)),
                  pl.BlockSpec((None, 1, D_MODEL), lambda s: (1, 0, 0))],
        out_specs=pl.BlockSpec((IN_SUB_ROWS, D_MODEL), lambda s: (s, 0)),
        out_shape=jax.ShapeDtypeStruct((TM_PROJ, D_MODEL), BF16),
        compiler_params=_cparams(("parallel",)),
        name="norm_mod_first_tile",
    )(x0, g, mod, mod)


def _in_next_tile(i):
    return jnp.minimum(i + 1, ROWS // TM_PROJ - 1)


def _in_next_sub_block(i, j):
    return _in_next_tile(i) * IN_SUB + jnp.clip(j - 1, 0, IN_SUB - 1)


def _in_kernel(*refs, n_x, w_transposed):
    x_refs = refs[:n_x]
    h0_ref, g_ref, sh_ref, sc_ref, w_ref, o_ref, h_ref, hn_ref = refs[n_x:]
    i, j = pl.program_id(0), pl.program_id(1)

    @pl.when((i == 0) & (j == 0))
    def _():
        h_ref[...] = h0_ref[...]

    @pl.when((i > 0) & (j == 0))
    def _():
        h_ref[...] = hn_ref[...]

    if n_x == 1:
        x = x_refs[0][...]
    else:
        x = jnp.where(_in_next_sub_block(i, j) < CTX_ROWS // IN_SUB_ROWS, x_refs[0][...], x_refs[1][...])
    r0 = pl.multiple_of(jnp.clip(j - 1, 0, IN_SUB - 1) * IN_SUB_ROWS, IN_SUB_ROWS)
    hn_ref[pl.ds(r0, IN_SUB_ROWS), :] = _norm_mod(x, g_ref[...], sh_ref[...], sc_ref[...])
    o_ref[...] = (_dot_nt if w_transposed else _dot)(h_ref[...], w_ref[...])


def _in_proj(x_parts, g, mod, w, layer, col_map=None, w_transposed=False):
    tm, tn = TM_PROJ, TN_IN
    n_tiles = (w.shape[1] if w_transposed else w.shape[2]) // tn
    assert n_tiles > IN_SUB
    col_map = col_map or (lambda t: t)
    if w_transposed:
        w_spec = pl.BlockSpec((None, tn, D_MODEL), lambda i, j: (layer, col_map(j), 0))
    else:
        w_spec = pl.BlockSpec((None, D_MODEL, tn), lambda i, j: (layer, 0, col_map(j)))
    mod_row = lambda part: (lambda i, j: (_cond_of_tile(_in_next_tile(i), tm) * 3 + part, 0, 0))
    h0 = _norm_first_tile(x_parts[0], g, mod)
    return pl.pallas_call(
        functools.partial(_in_kernel, n_x=len(x_parts), w_transposed=w_transposed),
        grid=(ROWS // tm, n_tiles),
        in_specs=_row_specs(x_parts, IN_SUB_ROWS, tile_of=_in_next_sub_block) + [
            pl.BlockSpec((tm, D_MODEL), lambda i, j: (0, 0), pipeline_mode=pl.Buffered(1)),
            pl.BlockSpec((1, D_MODEL), lambda i, j: (0, 0)),
            pl.BlockSpec((None, 1, D_MODEL), mod_row(0)),
            pl.BlockSpec((None, 1, D_MODEL), mod_row(1)), w_spec],
        out_specs=pl.BlockSpec((tm, tn), lambda i, j: (i, j)),
        out_shape=jax.ShapeDtypeStruct((ROWS, n_tiles * tn), F32),
        scratch_shapes=[pltpu.VMEM((tm, D_MODEL), BF16), pltpu.VMEM((tm, D_MODEL), BF16)],
        compiler_params=_cparams(("arbitrary", "arbitrary")),
        name="norm_mod_in_proj",
    )(*x_parts, h0, g, mod, mod, w)


def _out_kernel(*refs, n_x, n_o):
    m_ref, w_ref = refs[:2]
    x_refs = refs[2:2 + n_x]
    gt_ref = refs[2 + n_x]
    o_refs = refs[3 + n_x:]
    y = gt_ref[...] * _dot(m_ref[...], w_ref[...])

    def emit(x_ref, o_ref):
        o_ref[...] = x_ref[...] + y

    if n_x == 1 and n_o == 1:
        emit(x_refs[0], o_refs[0])
    else:
        in_first = pl.program_id(0) < CTX_ROWS // TM_OUT
        pl.when(in_first)(lambda: emit(x_refs[0], o_refs[0]))
        pl.when(jnp.logical_not(in_first))(lambda: emit(x_refs[-1], o_refs[-1]))


def _out_proj(mix, w, layer, x_parts, mod, split_out):
    tm = TM_OUT
    if split_out:
        out_parts = [jax.ShapeDtypeStruct((CTX_ROWS, D_MODEL), F32), jax.ShapeDtypeStruct((LAT_ROWS, D_MODEL), F32)]
    else:
        out_parts = [jax.ShapeDtypeStruct((ROWS, D_MODEL), F32)]
    return pl.pallas_call(
        functools.partial(_out_kernel, n_x=len(x_parts), n_o=len(out_parts)),
        grid=(ROWS // tm,),
        in_specs=[pl.BlockSpec((tm, D_MODEL), lambda i: (i, 0)),
                  pl.BlockSpec((None, D_MODEL, D_MODEL), lambda i: (layer, 0, 0))] + _row_specs(x_parts, tm) + [
                  pl.BlockSpec((None, 1, D_MODEL), lambda i: (_cond_of_tile(i, tm) * 3 + 2, 0, 0))],
        out_specs=_row_specs(out_parts, tm),
        out_shape=out_parts,
        compiler_params=_cparams(("arbitrary",)),
        name="out_proj_residual",
    )(mix, w, *x_parts, mod)


def _rope(x, cos_t, sin_t):
    return x * cos_t + pltpu.roll(x, LANES // 2, 1) * sin_t


def _row_sums(x):
    hi = x.astype(BF16)
    lo = (x - hi.astype(F32)).astype(BF16)
    return _dot(jnp.concatenate([hi, lo], axis=1), jnp.ones((2 * x.shape[1], LANES), BF16))


def _amid_kernel(*refs, do_q, norm_kv):
    if do_q:
        (ql_ref, kvl_ref, kpe_ref, wq_ref, wk_ref, wv_ref, qlg_ref, kvlg_ref, qng_ref, kng_ref, cos_ref, sin_ref,
         q_out, k_out, v_out, ckv_out, kpe_out) = refs
    else:
        (kvl_ref, kpe_ref, wk_ref, wv_ref, kvlg_ref, kng_ref, cos_ref, sin_ref, k_out, v_out) = refs
    cos_t = cos_ref[...]
    sin_t = sin_ref[...]
    inv_qk = 1.0 / A_QK

    if do_q:
        ql = ql_ref[...]
        qn = ql * lax.rsqrt(jnp.mean(ql * ql, axis=-1, keepdims=True) + EPS) * qlg_ref[...]
        qu = _dot(qn.astype(BF16), wq_ref[...])
        g_nope = qng_ref[:, :A_NOPE]
        g_rope = qng_ref[:, A_NOPE:]
        for h in range(A_HEADS):
            qh = qu[:, h * A_QK_PAD:(h + 1) * A_QK_PAD]
            r = lax.rsqrt(_row_sums(qh * qh) * inv_qk + EPS)
            q_out[:, h * A_QK_PAD:h * A_QK_PAD + A_NOPE] = (qh[:, :A_NOPE] * r * g_nope).astype(BF16)
            q_out[:, h * A_QK_PAD + A_NOPE:(h + 1) * A_QK_PAD] = _rope(qh[:, A_NOPE:] * r * g_rope,
                                                                       cos_t, sin_t).astype(BF16)

    kvl = kvl_ref[...]
    if norm_kv:
        ckv = kvl * lax.rsqrt(jnp.mean(kvl * kvl, axis=-1, keepdims=True) + EPS) * kvlg_ref[...]
    else:
        ckv = kvl
    ckv_b = ckv.astype(BF16)
    kn = _dot(ckv_b, wk_ref[...])
    v_out[...] = _dot(ckv_b, wv_ref[...]).astype(BF16)
    kpe = kpe_ref[...]
    if do_q:
        @pl.when(pl.program_id(0) < CTX_ROWS // TM_MID)
        def _():
            half = A_ROPE // 2
            ckv_out[...] = ckv.reshape(ckv_out.shape)
            kpe_out[...] = jnp.concatenate([kpe[:, :half], kpe[:, LANES // 2:LANES // 2 + half]],
                                           axis=1).reshape(kpe_out.shape)
    sp = _row_sums(kpe * kpe)
    g_nope = kng_ref[:, :A_NOPE]
    g_rope = kng_ref[:, A_NOPE:]
    for h in range(A_HEADS):
        a = kn[:, h * A_NOPE:(h + 1) * A_NOPE]
        r = lax.rsqrt((_row_sums(a * a) + sp) * inv_qk + EPS)
        k_out[:, h * A_QK_PAD:h * A_QK_PAD + A_NOPE] = (a * r * g_nope).astype(BF16)
        k_out[:, h * A_QK_PAD + A_NOPE:(h + 1) * A_QK_PAD] = _rope(kpe * r * g_rope, cos_t, sin_t).astype(BF16)


def _layer_slab_spec(tm, layer, width):
    n_ctx = CTX_ROWS // tm
    return pl.BlockSpec((tm // SEQ, None, SEQ, width), lambda i: (jnp.minimum(i, n_ctx - 1), layer, 0, 0))


def _call_with_carried(kernel, n_plain_out, carried, **kw):
    def run(*args, in_specs):
        args = list(args)
        in_specs = list(in_specs)
        n_real = len(args)
        aliases = {}
        for idx, arr in enumerate(carried):
            if arr is not None:
                aliases[len(args)] = n_plain_out + idx
                args.append(arr)
                in_specs.append(pl.BlockSpec(memory_space=pl.ANY))
        n_in = len(args)

        def body(*refs):
            kernel(*refs[:n_real], *refs[n_in:])

        return pl.pallas_call(body, in_specs=in_specs, input_output_aliases=aliases, **kw)(*args)
    return run


def _amid_tokens(y, wq, wk, wv, qlg, kvlg, qng, kng, cos_t, sin_t, layer, ckv_new, kpe_new):
    tm = TM_MID
    const = lambda i: (0, 0)
    rope_map = lambda i: (_rope_block_of_tile(i, tm), 0)
    n_ab = (DEPTH + 1) // 2
    run = _call_with_carried(
        functools.partial(_amid_kernel, do_q=True, norm_kv=True), 3, [ckv_new, kpe_new],
        grid=(ROWS // tm,),
        out_specs=[pl.BlockSpec((tm, A_HEADS * A_QK_PAD), lambda i: (i, 0)),
                   pl.BlockSpec((tm, A_HEADS * A_QK_PAD), lambda i: (i, 0)),
                   pl.BlockSpec((tm, A_HEADS * A_VDIM), lambda i: (i, 0)),
                   _layer_slab_spec(tm, layer, KV_LORA), _layer_slab_spec(tm, layer, A_ROPE)],
        out_shape=[jax.ShapeDtypeStruct((ROWS, A_HEADS * A_QK_PAD), BF16),
                   jax.ShapeDtypeStruct((ROWS, A_HEADS * A_QK_PAD), BF16),
                   jax.ShapeDtypeStruct((ROWS, A_HEADS * A_VDIM), BF16),
                   jax.ShapeDtypeStruct((BATCH, n_ab, SEQ, KV_LORA), F32),
                   jax.ShapeDtypeStruct((BATCH, n_ab, SEQ, A_ROPE), F32)],
        compiler_params=_cparams(("arbitrary",)),
        name="mla_qkv_prep")
    return run(y, y, y, wq, wk, wv, qlg, kvlg, qng, kng, cos_t, sin_t,
               in_specs=[pl.BlockSpec((tm, Q_LORA), lambda i: (i, AB_QLAT // Q_LORA)),
                         pl.BlockSpec((tm, KV_LORA), lambda i: (i, AB_KVLAT // KV_LORA)),
                         pl.BlockSpec((tm, LANES), lambda i: (i, AB_KPE // LANES)),
                         pl.BlockSpec(wq.shape, const), pl.BlockSpec(wk.shape, const), pl.BlockSpec(wv.shape, const),
                         pl.BlockSpec(qlg.shape, const), pl.BlockSpec(kvlg.shape, const),
                         pl.BlockSpec(qng.shape, const), pl.BlockSpec(kng.shape, const),
                         pl.BlockSpec((tm, LANES), rope_map), pl.BlockSpec((tm, LANES), rope_map)])


def _amid_cache(ckv, kpe, wk, wv, kvlg, kng, cos_t, sin_t):
    rows = ckv.shape[0]
    tm = TM_MID
    const = lambda i: (0, 0)
    outs = pl.pallas_call(
        functools.partial(_amid_kernel, do_q=False, norm_kv=False),
        grid=(rows // tm,),
        in_specs=[pl.BlockSpec((tm, KV_LORA), lambda i: (i, 0)),
                  pl.BlockSpec((tm, LANES), lambda i: (i, 0)),
                  pl.BlockSpec(wk.shape, const), pl.BlockSpec(wv.shape, const),
                  pl.BlockSpec(kvlg.shape, const), pl.BlockSpec(kng.shape, const),
                  pl.BlockSpec((tm, LANES), const), pl.BlockSpec((tm, LANES), const)],
        out_specs=[pl.BlockSpec((tm, A_HEADS * A_QK_PAD), lambda i: (i, 0)),
                   pl.BlockSpec((tm, A_HEADS * A_VDIM), lambda i: (i, 0))],
        out_shape=[jax.ShapeDtypeStruct((rows, A_HEADS * A_QK_PAD), BF16),
                   jax.ShapeDtypeStruct((rows, A_HEADS * A_VDIM), BF16)],
        compiler_params=_cparams(("parallel",)),
        name="mla_cache_kv_prep",
    )(ckv, kpe, wk, wv, kvlg, kng, cos_t, sin_t)
    return outs[0], outs[1]


def _scores(q, srcs):
    zs = []
    for k, _, bias in srcs:
        z = _dot_nt(q, k)
        if bias is not None:
            z = (z.reshape(z.shape[0] // bias.shape[0], *bias.shape) + bias).reshape(z.shape)
        zs.append(z)
    return zs


def _softmax_pv(zs, srcs, sink_z=None):
    tile_max = None
    for z in zs:
        for j in range(z.shape[1] // LANES):
            blk = z[:, j * LANES:(j + 1) * LANES]
            tile_max = blk if tile_max is None else jnp.maximum(tile_max, blk)
    m = tile_max.max(axis=-1, keepdims=True)
    if sink_z is not None:
        m = jnp.maximum(m, sink_z)
    acc = None
    for z, (_, v, _) in zip(zs, srcs):
        o = _dot(jnp.exp2(z - m).astype(BF16), jnp.concatenate([v, jnp.ones_like(v)], axis=1))
        acc = o if acc is None else acc + o
    dv = acc.shape[1] // 2
    den = acc[:, dv:]
    if sink_z is not None:
        den = den + jnp.exp2(sink_z - m)
    return acc[:, :dv] / den


def _attend_streams(n, q_of, srcs_of, sink_of=None):
    outs = []
    zs = _scores(q_of(0), srcs_of(0))
    for t in range(n):
        nxt = _scores(q_of(t + 1), srcs_of(t + 1)) if t + 1 < n else None
        outs.append(_softmax_pv(zs, srcs_of(t), None if sink_of is None else sink_of(t)))
        zs = nxt
    return outs


def _attn_a_kernel(*refs, n_src, heads):
    q_ref = refs[0]
    k_refs = refs[1:1 + n_src]
    v_refs = refs[1 + n_src:1 + 2 * n_src]
    gate_ref, o_ref = refs[1 + 2 * n_src:]
    sub = min(ATTN_SUB, q_ref.shape[0])
    streams = [(h, r) for h in range(heads) for r in range(q_ref.shape[0] // sub)]

    def rows(t):
        return slice(streams[t][1] * sub, (streams[t][1] + 1) * sub)

    def q_of(t):
        h = streams[t][0]
        return q_ref[rows(t), h * A_QK_PAD:(h + 1) * A_QK_PAD]

    def srcs_of(t):
        h = streams[t][0]
        return [(k_ref[:, h * A_QK_PAD:(h + 1) * A_QK_PAD], v_ref[:, h * A_VDIM:(h + 1) * A_VDIM], None)
                for k_ref, v_ref in zip(k_refs, v_refs)]

    outs = _attend_streams(len(streams), q_of, srcs_of)
    for t, o in enumerate(outs):
        cols = slice(streams[t][0] * A_VDIM, (streams[t][0] + 1) * A_VDIM)
        o_ref[rows(t), cols] = (o * _silu(gate_ref[rows(t), cols])).astype(o_ref.dtype)


def _attn_a(q, k, v, y, mix, kc=None, vc=None):
    latent = kc is not None
    if latent:
        nb, t, tq, hp, row0 = DEC_BATCH, DEC_SEQ, TQ_A, 1, CTX_ROWS
    else:
        nb, t, tq, hp, row0 = BATCH, SEQ, SEQ, A_HEADS, 0
    nq = t // tq
    qrow = lambda b, h, i: row0 // tq + b * nq + i
    in_specs = [pl.BlockSpec((tq, hp * A_QK_PAD), lambda b, h, i: (qrow(b, h, i), h))]
    args = [q]
    if latent:
        in_specs.append(pl.BlockSpec((PAST_LEN, hp * A_QK_PAD), lambda b, h, i: (b, h)))
        args.append(kc)
    in_specs.append(pl.BlockSpec((t, hp * A_QK_PAD), lambda b, h, i: (row0 // t + b, h)))
    args.append(k)
    if latent:
        in_specs.append(pl.BlockSpec((PAST_LEN, hp * A_VDIM), lambda b, h, i: (b, h)))
        args.append(vc)
    in_specs.append(pl.BlockSpec((t, hp * A_VDIM), lambda b, h, i: (row0 // t + b, h)))
    args.append(v)
    in_specs.append(pl.BlockSpec((tq, hp * A_VDIM), lambda b, h, i: (qrow(b, h, i), AB_AGATE // (hp * A_VDIM) + h)))
    args.append(y)
    run = _call_with_carried(
        functools.partial(_attn_a_kernel, n_src=2 if latent else 1, heads=hp), 0, [mix],
        grid=(nb, A_HEADS // hp, nq),
        out_specs=[pl.BlockSpec((tq, hp * A_VDIM), lambda b, h, i: (qrow(b, h, i), h))],
        out_shape=[_MIX_SHAPE],
        compiler_params=_cparams(("parallel", "parallel", "parallel")),
        name="mla_attention_latent" if latent else "mla_attention_context")
    return run(*args, in_specs=in_specs)[0]


def _hgrn_constants():
    c, nl = HG_CHUNK, HG_LEVELS
    t = np.arange(c)[:, None]
    u = np.arange(c)[None, :]
    tri_f = (u <= t).astype(np.float32)
    mask_f = np.zeros((nl, c, c), np.float32)
    coef_f = np.zeros((nl, c, LANES), np.float32)
    for l in range(nl):
        half = c >> (l + 1)
        seg = 2 * half
        mask_f[l] = ((u // seg) == (t // seg)) & ((t % seg) >= half) & ((u % seg) < half)
        later = np.broadcast_to((t % seg) >= half, (c, LANES))
        coef_f[l] = np.where(later, 1.0, -1.0 if half > 1 else 0.0)
    tri_b = tri_f[::-1, ::-1]
    mask_b = mask_f[:, ::-1, ::-1]
    coef_b = coef_f[:, ::-1, :]
    to_tri = lambda a: jnp.asarray(np.concatenate([a, a, a], axis=1), BF16)
    to_f32 = lambda a: jnp.asarray(np.ascontiguousarray(a), F32)
    return (to_tri(tri_f), to_tri(tri_b)), (to_f32(mask_f), to_f32(mask_b)), (to_f32(coef_f), to_f32(coef_b))


def _hgrn_decays(x, lb, tri3):
    kk = (1.0 - lb) * _sigmoid(-x)
    lf = jnp.log2(1.0 - kk)
    hi = lf.astype(BF16)
    r1 = lf - hi.astype(F32)
    mid = r1.astype(BF16)
    lo = (r1 - mid.astype(F32)).astype(BF16)
    return kk, lf, _dot(tri3, jnp.concatenate([hi, mid, lo], axis=0))


def _hgrn_level_arg(l, cs, lf, coef, forward):
    c = HG_CHUNK
    nv = c // 8
    half = c >> (l + 1)
    if half == 1:
        return lf * coef
    cs3 = cs.reshape(nv, 8, LANES)

    def in_vreg_row(r):
        return jnp.broadcast_to(cs3[:, r:r + 1, :], (nv, 8, LANES))

    if half >= 8:
        m = half // 8
        nseg = nv // (2 * m)
        edge = cs3[:, 7:8, :] if forward else cs3[:, 0:1, :]
        e4 = edge.reshape(nseg, 2 * m, 1, LANES)
        a = e4[:, m - 1:m] if forward else e4[:, m:m + 1]
        anchor = jnp.broadcast_to(a, (nseg, 2 * m, 8, LANES)).reshape(c, LANES)
    elif half == 4:
        anchor = in_vreg_row(3 if forward else 4).reshape(c, LANES)
    else:
        r0, r1 = (1, 5) if forward else (2, 6)
        sub = lax.broadcasted_iota(jnp.int32, (nv, 8, LANES), 1)
        anchor = jnp.where(sub < 4, in_vreg_row(r0), in_vreg_row(r1)).reshape(c, LANES)
    return (cs - anchor) * coef


def _hgrn_kernel(*refs, t_len, heads, zero_init, emit_state):
    c = HG_CHUNK
    n_chunks = t_len // c
    it = iter(refs)
    bq_ref, ff_ref, fb_ref, vi_ref, bg_ref, lb_ref, hg_ref = (next(it) for _ in range(7))
    if not zero_init:
        s0f_ref, s0b_ref = next(it), next(it)
    trif_ref, trib_ref, maskf_ref, maskb_ref, coeff_ref, coefb_ref = (next(it) for _ in range(6))
    o_ref = next(it)
    if emit_state:
        sf_ref, sb_ref = next(it), next(it)
    of_ref, ob_ref, stf_ref, stb_ref = (next(it) for _ in range(4))

    for h in range(heads):
        if zero_init:
            stf_ref[h] = jnp.zeros((B_DV, B_DK), F32)
            stb_ref[h] = jnp.zeros((B_DV, B_DK), F32)
        else:
            stf_ref[h] = s0f_ref[h].T
            stb_ref[h] = s0b_ref[h].T
    lb = lb_ref[...]

    nl = HG_LEVELS
    unroll = min(HG_UNROLL // heads, n_chunks)
    dirs = ((ff_ref, 0, trif_ref, maskf_ref, coeff_ref, stf_ref, of_ref, True),
            (fb_ref, 1, trib_ref, maskb_ref, coefb_ref, stb_ref, ob_ref, False))

    def body(i, carry):
        chains = []
        for h in range(heads):
            cols = slice(h * LANES, (h + 1) * LANES)
            for f_ref, lb_row, tri_ref, mask_ref, coef_ref, st_ref, out_ref, forward in dirs:
                for u in range(unroll):
                    k = i * unroll + u
                    r0 = pl.multiple_of((k if forward else n_chunks - 1 - k) * c, c)
                    chains.append(dict(rows=pl.ds(r0, c), cols=cols, head=h, f_ref=f_ref,
                                       lb=lb[lb_row:lb_row + 1, cols], tri_ref=tri_ref, mask_ref=mask_ref,
                                       coef_ref=coef_ref, st_ref=st_ref, out_ref=out_ref, forward=forward))
        for ch in chains:
            ch["kk"], ch["lf"], ch["cs"] = _hgrn_decays(ch["f_ref"][ch["rows"], ch["cols"]], ch["lb"],
                                                        ch["tri_ref"][...])
            ch["q"] = _silu(bq_ref[ch["rows"], ch["cols"]])
            ch["v"] = vi_ref[ch["rows"], ch["cols"]]
            ch["q16"] = ch["q"].astype(BF16)
            ch["kk16"] = ch["kk"].astype(BF16)
            ch["sc"] = jnp.zeros((c, c), F32)
        for l in range(nl):
            for ch in chains:
                el = jnp.exp2(_hgrn_level_arg(l, ch["cs"], ch["lf"], ch["coef_ref"][l], ch["forward"])).astype(BF16)
                ch["sc"] = ch["sc"] + ch["mask_ref"][l] * _dot_nt(ch["q16"] * el, ch["kk16"] * el)
        for ch in chains:
            cs, q, kk, v = ch["cs"], ch["q"], ch["kk"], ch["v"]
            end = cs[c - 1:c, :] if ch["forward"] else cs[0:1, :]
            ch["end"] = end
            ch["q_in"] = (q * jnp.exp2(cs)).astype(BF16)
            ch["upd"] = _dot(v.T.astype(BF16), (kk * jnp.exp2(end - cs)).astype(BF16))
            ch["o"] = _dot(ch["sc"].astype(BF16), v.astype(BF16)) + jnp.sum(q * kk, axis=-1, keepdims=True) * v
        for ch in chains:
            st = ch["st_ref"][ch["head"]]
            ch["out_ref"][ch["rows"], ch["cols"]] = ch["o"] + _dot_nt(ch["q_in"], st.astype(BF16))
            ch["st_ref"][ch["head"]] = jnp.exp2(ch["end"]) * st + ch["upd"]
        return carry

    lax.fori_loop(0, n_chunks // unroll, body, 0)
    if emit_state:
        for h in range(heads):
            sf_ref[h] = stf_ref[h].T
            sb_ref[h] = stb_ref[h].T

    hg = hg_ref[...]
    blk = min(HG_UNROLL, n_chunks) * c

    def finish(i, carry):
        rows = pl.ds(pl.multiple_of(i * blk, blk), blk)
        for h in range(heads):
            cols = slice(h * LANES, (h + 1) * LANES)
            o = of_ref[rows, cols] + ob_ref[rows, cols]
            o = o * lax.rsqrt(jnp.mean(o * o, axis=-1, keepdims=True) + EPS) * hg
            o_ref[rows, cols] = (o * _silu(bg_ref[rows, cols])).astype(o_ref.dtype)
        return carry

    lax.fori_loop(0, t_len // blk, finish, 0)


def _hgrn(y, lb, hg, mix, consts, s0f=None, s0b=None, layer=None, sf_new=None, sb_new=None):
    latent = s0f is not None
    if latent:
        nb, t, row0, hp = DEC_BATCH, DEC_SEQ, CTX_ROWS, 1
    else:
        nb, t, row0, hp = BATCH, SEQ, 0, 8
    w = hp * LANES
    rb = lambda b: row0 // t + b
    col = lambda off: (lambda b, h: (rb(b), off // w + h))
    const2 = lambda b, h: (0, 0)
    const3 = lambda b, h: (0, 0, 0)
    in_specs = [pl.BlockSpec((t, w), col(AB_BQ)), pl.BlockSpec((t, w), col(AB_BFF)),
                pl.BlockSpec((t, w), col(AB_BFB)), pl.BlockSpec((t, w), col(AB_BI)),
                pl.BlockSpec((t, w), col(AB_BGATE)),
                pl.BlockSpec((2, w), lambda b, h: (0, h)),
                pl.BlockSpec((1, LANES), const2)]
    args = [y, y, y, y, y, lb, hg]
    if latent:
        st_spec = pl.BlockSpec((None, hp, B_DK, B_DV), lambda b, h: (b, h, 0, 0))
        in_specs += [st_spec, st_spec]
        args += [s0f, s0b]
    tris, masks, coefs = consts
    in_specs += ([pl.BlockSpec(a.shape, const2) for a in tris] + [pl.BlockSpec(a.shape, const3) for a in masks]
                 + [pl.BlockSpec(a.shape, const3) for a in coefs])
    args += [*tris, *masks, *coefs]
    out_specs = [pl.BlockSpec((t, w), lambda b, h: (rb(b), A_HEADS * A_VDIM // w + h))]
    out_shape = [_MIX_SHAPE]
    carried = [mix]
    if not latent:
        st_out = pl.BlockSpec((None, None, hp, B_DK, B_DV), lambda b, h: (b, layer, h, 0, 0))
        out_specs += [st_out, st_out]
        out_shape += [jax.ShapeDtypeStruct((nb, (DEPTH + 1) // 2, B_HEADS, B_DK, B_DV), F32)] * 2
        carried += [sf_new, sb_new]
    run = _call_with_carried(
        functools.partial(_hgrn_kernel, t_len=t, heads=hp, zero_init=not latent, emit_state=not latent), 0, carried,
        grid=(nb, B_HEADS // hp),
        out_specs=out_specs,
        out_shape=out_shape,
        scratch_shapes=[pltpu.VMEM((t, w), F32), pltpu.VMEM((t, w), F32),
                        pltpu.VMEM((hp, B_DV, B_DK), F32), pltpu.VMEM((hp, B_DV, B_DK), F32)],
        compiler_params=_cparams(("parallel", "parallel")),
        name="hgrn2_latent" if latent else "hgrn2_context")
    return run(*args, in_specs=in_specs)


def _cmid_kernel(q_ref, k_ref, v_ref, qg_ref, kg_ref, cos_ref, sin_ref, q_out, k_out, v_out, kc_out, vc_out):
    cos_t = cos_ref[...]
    sin_t = sin_ref[...]
    qg = qg_ref[...]
    kg = kg_ref[...]
    in_ctx = pl.program_id(0) < CTX_ROWS // TM_MID

    def norm(x, g):
        return x * lax.rsqrt(_row_sums(x * x) * (1.0 / C_HEAD_DIM) + EPS) * g

    def rope(x):
        return _rope(x, cos_t, sin_t)

    for h in range(C_HEADS):
        sl = slice(h * C_HEAD_DIM, (h + 1) * C_HEAD_DIM)
        q_out[:, sl] = rope(norm(q_ref[:, sl], qg)).astype(BF16)
    for h in range(C_KV_HEADS):
        sl = slice(h * C_HEAD_DIM, (h + 1) * C_HEAD_DIM)
        kn = norm(k_ref[:, sl], kg)
        k_out[:, sl] = rope(kn).astype(BF16)

        @pl.when(in_ctx)
        def _():
            kc_out[:, :, sl] = kn.reshape(kc_out.shape[0], SEQ, C_HEAD_DIM)
    v = v_ref[...]
    v_out[...] = v.astype(BF16)

    @pl.when(in_ctx)
    def _():
        vc_out[...] = v.reshape(vc_out.shape)


def _cmid(y, qg, kg, cos_t, sin_t, layer, kc_new, vc_new):
    tm = TM_MID
    const = lambda i: (0, 0)
    rope_map = lambda i: (_rope_block_of_tile(i, tm), 0)
    cache = jax.ShapeDtypeStruct((BATCH, DEPTH // 2, SEQ, C_KV_WIDTH), F32)
    run = _call_with_carried(
        _cmid_kernel, 3, [kc_new, vc_new],
        grid=(ROWS // tm,),
        out_specs=[pl.BlockSpec((tm, C_WIDTH), lambda i: (i, 0)),
                   pl.BlockSpec((tm, C_KV_WIDTH), lambda i: (i, 0)),
                   pl.BlockSpec((tm, C_KV_WIDTH), lambda i: (i, 0)),
                   _layer_slab_spec(tm, layer, C_KV_WIDTH), _layer_slab_spec(tm, layer, C_KV_WIDTH)],
        out_shape=[jax.ShapeDtypeStruct((ROWS, C_WIDTH), BF16),
                   jax.ShapeDtypeStruct((ROWS, C_KV_WIDTH), BF16),
                   jax.ShapeDtypeStruct((ROWS, C_KV_WIDTH), BF16), cache, cache],
        compiler_params=_cparams(("arbitrary",)),
        name="gqa_qkv_prep")
    return run(y, y, y, qg, kg, cos_t, sin_t,
               in_specs=[pl.BlockSpec((tm, C_WIDTH), lambda i: (i, C_Q // C_WIDTH)),
                         pl.BlockSpec((tm, C_KV_WIDTH), lambda i: (i, C_K // C_KV_WIDTH)),
                         pl.BlockSpec((tm, C_KV_WIDTH), lambda i: (i, C_V // C_KV_WIDTH)),
                         pl.BlockSpec((1, C_HEAD_DIM), const), pl.BlockSpec((1, C_HEAD_DIM), const),
                         pl.BlockSpec((tm, LANES), rope_map), pl.BlockSpec((tm, LANES), rope_map)])


def _band_start(i, tq, t_len):
    return pl.multiple_of(jnp.clip(i * tq - WINDOW, 0, t_len - (tq + 2 * WINDOW)), WINDOW)


def _band_bias(tq, t_len):
    width = tq + 2 * WINDOW
    i = np.arange(t_len // tq)[:, None, None]
    qpos = i * tq + np.arange(tq)[None, :, None]
    kpos = np.clip(i * tq - WINDOW, 0, t_len - width) + np.arange(width)[None, None, :]
    return jnp.asarray(np.where(np.abs(kpos - qpos) <= WINDOW, 0.0, NEG_BIG), F32)


def _attn_c_kernel(*refs, band, tq, t_len, groups, stack):
    if band:
        q_ref, kc_ref, vc_ref, kl_ref, vl_ref, bias_ref, sink_ref, gate_ref, o_ref = refs
    else:
        q_ref, kc_ref, vc_ref, sink_ref, gate_ref, o_ref = refs
    hd = C_HEAD_DIM
    sub = WINDOW if band else tq
    width = sub + 2 * WINDOW
    streams = [(g, s, u) for g in range(groups) for u in range(tq // sub) for s in range(C_GROUP // stack)]

    def heads_of(t):
        g, s, _ = streams[t]
        return [g * C_GROUP + s * stack + r for r in range(stack)]

    def rows_of(t):
        u = streams[t][2]
        return slice(u * sub, (u + 1) * sub)

    def q_of(t):
        return jnp.concatenate([q_ref[rows_of(t), h * hd:(h + 1) * hd] for h in heads_of(t)], axis=0)

    def srcs_of(t):
        g, _, u = streams[t]
        cols = slice(g * hd, (g + 1) * hd)
        srcs = [(kc_ref[:, cols], vc_ref[:, cols], None)]
        if band:
            start = _band_start(pl.program_id(2) * (tq // sub) + u, sub, t_len)
            srcs.append((kl_ref[pl.ds(start, width), cols], vl_ref[pl.ds(start, width), cols], bias_ref[u]))
        return srcs

    def sink_of(t):
        return jnp.concatenate([jnp.broadcast_to(sink_ref[h][:, :1] * LOG2_E, (sub, 1)) for h in heads_of(t)], axis=0)

    outs = _attend_streams(len(streams), q_of, srcs_of, sink_of)
    for t, o in enumerate(outs):
        for r, h in enumerate(heads_of(t)):
            cols = slice(h * hd, (h + 1) * hd)
            o_ref[rows_of(t), cols] = (o[r * sub:(r + 1) * sub] * _silu(gate_ref[rows_of(t), cols])).astype(o_ref.dtype)


def _attn_c(q, k, v, y, sink, mix, kc=None, vc=None):
    latent = kc is not None
    if latent:
        nb, t, tq, gp, stack, row0 = DEC_BATCH, DEC_SEQ, TQ_C, 1, 4, CTX_ROWS
    else:
        nb, t, tq, gp, stack, row0 = BATCH, SEQ, SEQ, C_KV_HEADS, 1, 0
    gw = gp * C_GROUP * C_HEAD_DIM
    kvw = gp * C_HEAD_DIM
    nq = t // tq
    qrow = lambda b, g, i: row0 // tq + b * nq + i
    own_kv = pl.BlockSpec((t, kvw), lambda b, g, i: (row0 // t + b, g))
    in_specs = [pl.BlockSpec((tq, gw), lambda b, g, i: (qrow(b, g, i), g))]
    args = [q]
    if latent:
        ctx_kv = pl.BlockSpec((PAST_LEN, kvw), lambda b, g, i: (b, g))
        in_specs += [ctx_kv, ctx_kv, own_kv, own_kv,
                     pl.BlockSpec((None, tq // WINDOW, WINDOW, 3 * WINDOW), lambda b, g, i: (i, 0, 0, 0))]
        args += [kc, vc, k, v, _band_bias(WINDOW, t).reshape(nq, tq // WINDOW, WINDOW, 3 * WINDOW)]
    else:
        in_specs += [own_kv, own_kv]
        args += [k, v]
    in_specs.append(pl.BlockSpec((gp * C_GROUP, 1, LANES), lambda b, g, i: (g, 0, 0)))
    args.append(sink)
    in_specs.append(pl.BlockSpec((tq, gw), lambda b, g, i: (qrow(b, g, i), C_GATE // gw + g)))
    args.append(y)
    run = _call_with_carried(
        functools.partial(_attn_c_kernel, band=latent, tq=tq, t_len=t, groups=gp, stack=stack), 0, [mix],
        grid=(nb, C_KV_HEADS // gp, nq),
        out_specs=[pl.BlockSpec((tq, gw), lambda b, g, i: (qrow(b, g, i), g))],
        out_shape=[_MIX_SHAPE],
        compiler_params=_cparams(("parallel", "parallel", "parallel")),
        name="gqa_attention_latent" if latent else "gqa_attention_context")
    return run(*args, in_specs=in_specs)[0]


def _axial_angles(n_tokens, rot_dim):
    rows = n_tokens // GRID_W
    row = jnp.repeat(jnp.arange(rows, dtype=F32), GRID_W)
    col = jnp.tile(jnp.arange(GRID_W, dtype=F32), rows)
    n_freq = rot_dim // 4
    inv = ROPE_BASE ** (-jnp.arange(n_freq, dtype=F32) / n_freq)
    return jnp.concatenate([row[:, None] * inv, col[:, None] * inv], axis=-1)


def _spread_halves(a, fill=0.0):
    h = a.shape[-1] // 2
    pad = jnp.full(a.shape[:-1] + (LANES // 2 - h,), fill, a.dtype)
    return jnp.concatenate([a[..., :h], pad, a[..., h:], pad], axis=-1)


def _spread_head(a):
    return jnp.concatenate([a[..., :A_NOPE], _spread_halves(a[..., A_NOPE:])], axis=-1)


def _rope_tables(rot_dim, tm):
    ang = _axial_angles(DEC_SEQ, rot_dim)
    cos, sin = jnp.cos(ang), jnp.sin(ang)
    cos_t = _spread_halves(jnp.concatenate([cos, cos], axis=-1), 1.0)
    sin_t = _spread_halves(jnp.concatenate([-sin, sin], axis=-1))
    cos_t = jnp.concatenate([jnp.ones((tm, LANES), F32), cos_t], axis=0)
    sin_t = jnp.concatenate([jnp.zeros((tm, LANES), F32), sin_t], axis=0)
    return cos_t, sin_t


def _lower_bounds(lb_logits):
    p = jax.nn.softmax(lb_logits.astype(F32), axis=0)
    return jnp.cumsum(p, axis=0) - p[0:1]


def kernel(x_prompt, x_sample, cache_ckv, cache_kpe, state_hgrn_fwd, state_hgrn_bwd, cache_k_c, cache_v_c, c, c_ctx,
           mod_w_ab, mod_b_ab, norm_ab, w_in_ab, q_lora_norm, kv_lora_norm, w_q_up, w_kv_up, q_norm_ab, k_norm_ab,
           hgrn_lb_logits, hgrn_out_norm, w_out_ab, mod_w_c, mod_b_c, norm_c, w_in_c, q_norm_c, k_norm_c, sink_c,
           w_out_c):
    x_parts = [x_prompt.reshape(CTX_ROWS, D_MODEL), x_sample.reshape(LAT_ROWS, D_MODEL)]
    cond8 = jnp.concatenate([c_ctx[None, :], c, jnp.zeros((N_COND - 1 - DEC_BATCH, D_MODEL), F32)], axis=0)
    mods_ab = _modulation(cond8, mod_w_ab, mod_b_ab)
    mods_c = _modulation(cond8, mod_w_c, mod_b_c)
    lower = _lower_bounds(hgrn_lb_logits)
    cos_a, sin_a = _rope_tables(A_ROPE, TM_MID)
    cos_c, sin_c = _rope_tables(C_HEAD_DIM, TM_MID)
    hg_consts = _hgrn_constants()
    w_in_ab16 = _prep_w_in_ab(jnp.swapaxes(w_in_ab, 1, 2))
    w_in_c16, w_out_ab16, w_out_c16 = w_in_c.astype(BF16), w_out_ab.astype(BF16), w_out_c.astype(BF16)

    ckv_new = kpe_new = sf_new = sb_new = kc_new = vc_new = None
    for layer in range(DEPTH):
        j = layer // 2
        last = layer == DEPTH - 1
        if layer % 2 == 0:
            mod = mods_ab[j].reshape(3 * N_COND, 1, D_MODEL)
            wq = _spread_head(w_q_up[j].reshape(Q_LORA, A_HEADS, A_QK)).reshape(Q_LORA, A_HEADS * A_QK_PAD).astype(BF16)
            wkv = w_kv_up[j].reshape(KV_LORA, A_HEADS, A_NOPE + A_VDIM)
            wk = wkv[:, :, :A_NOPE].reshape(KV_LORA, A_HEADS * A_NOPE).astype(BF16)
            wv = wkv[:, :, A_NOPE:].reshape(KV_LORA, A_HEADS * A_VDIM).astype(BF16)
            qlg, kvlg = q_lora_norm[j][None, :], kv_lora_norm[j][None, :]
            qng = _spread_head(q_norm_ab[j] * (A_QK ** -0.5 * LOG2_E))[None, :]
            kng = _spread_head(k_norm_ab[j])[None, :]

            y = _in_proj(x_parts, norm_ab[j][None, :], mod, w_in_ab16, j, w_transposed=True)
            q, k, v, ckv_new, kpe_new = _amid_tokens(y, wq, wk, wv, qlg, kvlg, qng, kng, cos_a, sin_a,
                                                     j, ckv_new, kpe_new)
            kpe_cache = _spread_halves(cache_kpe[:, j].reshape(DEC_BATCH * PAST_LEN, A_ROPE))
            kc, vc = _amid_cache(cache_ckv[:, j].reshape(DEC_BATCH * PAST_LEN, KV_LORA), kpe_cache, wk, wv, kvlg, kng,
                                 cos_a, sin_a)
            mix = _attn_a(q, k, v, y, None)
            mix = _attn_a(q, k, v, y, mix, kc=kc, vc=vc)
            hg = hgrn_out_norm[j][None, :]
            mix, sf_new, sb_new = _hgrn(y, lower[j], hg, mix, hg_consts, layer=j, sf_new=sf_new, sb_new=sb_new)
            (mix,) = _hgrn(y, lower[j], hg, mix, hg_consts, s0f=state_hgrn_fwd[:, j], s0b=state_hgrn_bwd[:, j])
            x_parts = _out_proj(mix, w_out_ab16, j, x_parts, mod, split_out=last)
        else:
            mod = mods_c[j].reshape(3 * N_COND, 1, D_MODEL)
            y = _in_proj(x_parts, norm_c[j][None, :], mod, w_in_c16, j,
                         col_map=lambda t: jnp.where(t < 2, t, jnp.where(t < 4, t + 1, 2)))
            q, k, v, kc_new, vc_new = _cmid(y, q_norm_c[j][None, :] * (C_HEAD_DIM ** -0.5 * LOG2_E),
                                            k_norm_c[j][None, :], cos_c, sin_c,
                                            j, kc_new, vc_new)
            kc = cache_k_c[:, j].reshape(DEC_BATCH * PAST_LEN, C_KV_WIDTH).astype(BF16)
            vc = cache_v_c[:, j].reshape(DEC_BATCH * PAST_LEN, C_KV_WIDTH).astype(BF16)
            sink = jnp.broadcast_to(sink_c[j][:, None, None], (C_HEADS, 1, LANES))
            mix = _attn_c(q, k, v, y, sink, None)
            mix = _attn_c(q, k, v, y, sink, mix, kc=kc, vc=vc)
            x_parts = _out_proj(mix, w_out_c16, j, x_parts, mod, split_out=last)

    cache_c_shape = (BATCH, DEPTH // 2, SEQ, C_KV_HEADS, C_HEAD_DIM)
    return (x_parts[0].reshape(BATCH, SEQ, D_MODEL), x_parts[1].reshape(DEC_BATCH, DEC_SEQ, D_MODEL),
            ckv_new, kpe_new, sf_new, sb_new, kc_new.reshape(cache_c_shape), vc_new.reshape(cache_c_shape))
```

```python
import functools

import numpy as np
import jax
import jax.numpy as jnp
from jax import lax
from jax.experimental import pallas as pl
from jax.experimental.pallas import tpu as pltpu

F32 = jnp.float32
BF16 = jnp.bfloat16

D_MODEL = 2048
BATCH = 16
SEQ = 256
DEPTH = 4
DEC_BATCH = 4
DEC_SEQ = 2048
PAST_LEN = 256
GRID_W = 64
A_HEADS = 8
A_NOPE = 128
A_ROPE = 64
A_VDIM = 128
A_QK = A_NOPE + A_ROPE
A_QK_PAD = 256
Q_LORA = 512
KV_LORA = 256
B_HEADS = 8
B_DK = 128
B_DV = 128
C_HEADS = 16
C_KV_HEADS = 4
C_GROUP = C_HEADS // C_KV_HEADS
C_HEAD_DIM = 128
C_WIDTH = C_HEADS * C_HEAD_DIM
C_KV_WIDTH = C_KV_HEADS * C_HEAD_DIM
WINDOW = 128
ROPE_BASE = 10000.0
EPS = 1e-6
NEG_BIG = -1e30
LOG2_E = 1.4426950408889634

LANES = 128
CTX_ROWS = BATCH * SEQ
LAT_ROWS = DEC_BATCH * DEC_SEQ
ROWS = CTX_ROWS + LAT_ROWS
N_COND = 8

AB_AGATE = 0
AB_BQ = 1024
AB_BFF = 2048
AB_BFB = 3072
AB_BI = 4096
AB_BGATE = 5120
AB_QLAT = 6144
AB_KVLAT = 6656
AB_KPE = 6912
AB_N = 7168
C_Q = 0
C_GATE = 2048
C_K = 4096
C_V = 4608
C_N = 5120

TM_PROJ = 1024
TN_IN = 1024
TM_OUT = 512
TM_MID = 1024
TQ_A = 2048
TQ_C = 2048
ATTN_SUB = 512
HG_CHUNK = 128
HG_LEVELS = 7
HG_UNROLL = 16
VMEM_LIMIT = 56 * 1024 * 1024
_MIX_SHAPE = jax.ShapeDtypeStruct((ROWS, D_MODEL), BF16)


def _cparams(sem):
    return pltpu.CompilerParams(dimension_semantics=sem, vmem_limit_bytes=VMEM_LIMIT)


def _sigmoid(x):
    return 0.5 * jnp.tanh(0.5 * x) + 0.5


def _silu(x):
    return x * _sigmoid(x)


def _dot(a, b):
    return jnp.dot(a, b, preferred_element_type=F32)


def _dot_nt(a, b):
    return lax.dot_general(a, b, (((1,), (1,)), ((), ())), preferred_element_type=F32)


def _cond_of_tile(i, tm):
    n_ctx = CTX_ROWS // tm
    per_batch = DEC_SEQ // tm
    return jnp.where(i < n_ctx, 0, 1 + (i - n_ctx) // per_batch)


def _rope_block_of_tile(i, tm):
    n_ctx = CTX_ROWS // tm
    per_batch = DEC_SEQ // tm
    return jnp.where(i < n_ctx, 0, 1 + (i - n_ctx) % per_batch)


def _mod_kernel(c_ref, w_ref, b_ref, o_ref):
    a = _silu(c_ref[...]).astype(BF16)
    o_ref[...] = _dot(a, w_ref[...].astype(BF16)) + b_ref[...]


def _modulation(cond8, w_mod, b_mod):
    n = w_mod.shape[0]
    tn = 1024
    return pl.pallas_call(
        _mod_kernel,
        grid=(n, 3 * D_MODEL // tn),
        in_specs=[pl.BlockSpec((N_COND, D_MODEL), lambda l, j: (0, 0)),
                  pl.BlockSpec((None, D_MODEL, tn), lambda l, j: (l, 0, j)),
                  pl.BlockSpec((None, 1, tn), lambda l, j: (l, 0, j))],
        out_specs=pl.BlockSpec((None, N_COND, tn), lambda l, j: (l, 0, j)),
        out_shape=jax.ShapeDtypeStruct((n, N_COND, 3 * D_MODEL), F32),
        compiler_params=_cparams(("parallel", "parallel")),
        name="adaln_mod",
    )(cond8, w_mod, b_mod.reshape(n, 1, 3 * D_MODEL))


def _prep_w_kernel(w_ref, o_ref):
    n_head = Q_LORA + KV_LORA + A_ROPE
    n_main = AB_N - TN_IN
    half = A_ROPE // 2
    x1_end = Q_LORA + KV_LORA + half
    x2_at = Q_LORA + KV_LORA + LANES // 2
    o_ref[:n_main, :] = w_ref[n_head:, :].astype(BF16)
    o_ref[n_main:, :] = jnp.zeros((TN_IN, o_ref.shape[1]), BF16)
    o_ref[n_main:n_main + x1_end, :] = w_ref[:x1_end, :].astype(BF16)
    o_ref[n_main + x2_at:n_main + x2_at + half, :] = w_ref[x1_end:n_head, :].astype(BF16)


def _prep_w_in_ab(w_t):
    n_layers, n, _ = w_t.shape
    tc = 256
    return pl.pallas_call(
        _prep_w_kernel,
        grid=(n_layers, D_MODEL // tc),
        in_specs=[pl.BlockSpec((None, n, tc), lambda l, i: (l, 0, i))],
        out_specs=pl.BlockSpec((None, AB_N, tc), lambda l, i: (l, 0, i)),
        out_shape=jax.ShapeDtypeStruct((n_layers, AB_N, D_MODEL), BF16),
        compiler_params=_cparams(("parallel", "parallel")),
        name="w_in_ab_layout",
    )(w_t)


def _row_specs(parts, tm, single_buffer=False, tile_of=lambda i, *_: i):
    if len(parts) == 1:
        return [pl.BlockSpec((tm, D_MODEL), lambda *g: (tile_of(*g), 0))]
    n_first = CTX_ROWS // tm
    mode = dict(pipeline_mode=pl.Buffered(1)) if single_buffer else {}
    return [pl.BlockSpec((tm, D_MODEL), lambda *g: (jnp.minimum(tile_of(*g), n_first - 1), 0), **mode),
            pl.BlockSpec((tm, D_MODEL), lambda *g: (jnp.maximum(tile_of(*g) - n_first, 0), 0), **mode)]


IN_SUB = 4
IN_SUB_ROWS = TM_PROJ // IN_SUB


def _norm_mod(x, g, shift, scale):
    r = lax.rsqrt(jnp.mean(x * x, axis=-1, keepdims=True) + EPS)
    return ((x * r * g) * (1.0 + scale) + shift).astype(BF16)


def _norm_tile_kernel(x_ref, g_ref, sh_ref, sc_ref, o_ref):
    o_ref[...] = _norm_mod(x_ref[...], g_ref[...], sh_ref[...], sc_ref[...])


def _norm_first_tile(x0, g, mod):
    const = lambda s: (0, 0)
    return pl.pallas_call(
        _norm_tile_kernel,
        grid=(IN_SUB,),
        in_specs=[pl.BlockSpec((IN_SUB_ROWS, D_MODEL), lambda s: (s, 0)), pl.BlockSpec((1, D_MODEL), const),
                  pl.BlockSpec((None, 1, D_MODEL), lambda s: (0, 0, 0)),
                  pl.BlockSpec((None, 1, D_MODEL), lambda s: (1, 0, 0))],
        out_specs=pl.BlockSpec((IN_SUB_ROWS, D_MODEL), lambda s: (s, 0)),
        out_shape=jax.ShapeDtypeStruct((TM_PROJ, D_MODEL), BF16),
        compiler_params=_cparams(("parallel",)),
        name="norm_mod_first_tile",
    )(x0, g, mod, mod)


def _in_next_tile(i):
    return jnp.minimum(i + 1, ROWS // TM_PROJ - 1)


def _in_next_sub_block(i, j):
    return _in_next_tile(i) * IN_SUB + jnp.clip(j - 1, 0, IN_SUB - 1)


def _in_kernel(*refs, n_x, w_transposed):
    x_refs = refs[:n_x]
    h0_ref, g_ref, sh_ref, sc_ref, w_ref, o_ref, h_ref, hn_ref = refs[n_x:]
    i, j = pl.program_id(0), pl.program_id(1)

    @pl.when((i == 0) & (j == 0))
    def _():
        h_ref[...] = h0_ref[...]

    @pl.when((i > 0) & (j == 0))
    def _():
        h_ref[...] = hn_ref[...]

    if n_x == 1:
        x = x_refs[0][...]
    else:
        x = jnp.where(_in_next_sub_block(i, j) < CTX_ROWS // IN_SUB_ROWS, x_refs[0][...], x_refs[1][...])
    r0 = pl.multiple_of(jnp.clip(j - 1, 0, IN_SUB - 1) * IN_SUB_ROWS, IN_SUB_ROWS)
    hn_ref[pl.ds(r0, IN_SUB_ROWS), :] = _norm_mod(x, g_ref[...], sh_ref[...], sc_ref[...])
    o_ref[...] = (_dot_nt if w_transposed else _dot)(h_ref[...], w_ref[...])


def _in_proj(x_parts, g, mod, w, layer, col_map=None, w_transposed=False):
    tm, tn = TM_PROJ, TN_IN
    n_tiles = (w.shape[1] if w_transposed else w.shape[2]) // tn
    assert n_tiles > IN_SUB
    col_map = col_map or (lambda t: t)
    if w_transposed:
        w_spec = pl.BlockSpec((None, tn, D_MODEL), lambda i, j: (layer, col_map(j), 0))
    else:
        w_spec = pl.BlockSpec((None, D_MODEL, tn), lambda i, j: (layer, 0, col_map(j)))
    mod_row = lambda part: (lambda i, j: (_cond_of_tile(_in_next_tile(i), tm) * 3 + part, 0, 0))
    h0 = _norm_first_tile(x_parts[0], g, mod)
    return pl.pallas_call(
        functools.partial(_in_kernel, n_x=len(x_parts), w_transposed=w_transposed),
        grid=(ROWS // tm, n_tiles),
        in_specs=_row_specs(x_parts, IN_SUB_ROWS, tile_of=_in_next_sub_block) + [
            pl.BlockSpec((tm, D_MODEL), lambda i, j: (0, 0), pipeline_mode=pl.Buffered(1)),
            pl.BlockSpec((1, D_MODEL), lambda i, j: (0, 0)),
            pl.BlockSpec((None, 1, D_MODEL), mod_row(0)),
            pl.BlockSpec((None, 1, D_MODEL), mod_row(1)), w_spec],
        out_specs=pl.BlockSpec((tm, tn), lambda i, j: (i, j)),
        out_shape=jax.ShapeDtypeStruct((ROWS, n_tiles * tn), F32),
        scratch_shapes=[pltpu.VMEM((tm, D_MODEL), BF16), pltpu.VMEM((tm, D_MODEL), BF16)],
        compiler_params=_cparams(("arbitrary", "arbitrary")),
        name="norm_mod_in_proj",
    )(*x_parts, h0, g, mod, mod, w)


def _out_kernel(*refs, n_x, n_o):
    m_ref, w_ref = refs[:2]
    x_refs = refs[2:2 + n_x]
    gt_ref = refs[2 + n_x]
    o_refs = refs[3 + n_x:]
    y = gt_ref[...] * _dot(m_ref[...], w_ref[...])

    def emit(x_ref, o_ref):
        o_ref[...] = x_ref[...] + y

    if n_x == 1 and n_o == 1:
        emit(x_refs[0], o_refs[0])
    else:
        in_first = pl.program_id(0) < CTX_ROWS // TM_OUT
        pl.when(in_first)(lambda: emit(x_refs[0], o_refs[0]))
        pl.when(jnp.logical_not(in_first))(lambda: emit(x_refs[-1], o_refs[-1]))


def _out_proj(mix, w, layer, x_parts, mod, split_out):
    tm = TM_OUT
    if split_out:
        out_parts = [jax.ShapeDtypeStruct((CTX_ROWS, D_MODEL), F32), jax.ShapeDtypeStruct((LAT_ROWS, D_MODEL), F32)]
    else:
        out_parts = [jax.ShapeDtypeStruct((ROWS, D_MODEL), F32)]
    return pl.pallas_call(
        functools.partial(_out_kernel, n_x=len(x_parts), n_o=len(out_parts)),
        grid=(ROWS // tm,),
        in_specs=[pl.BlockSpec((tm, D_MODEL), lambda i: (i, 0)),
                  pl.BlockSpec((None, D_MODEL, D_MODEL), lambda i: (layer, 0, 0))] + _row_specs(x_parts, tm) + [
                  pl.BlockSpec((None, 1, D_MODEL), lambda i: (_cond_of_tile(i, tm) * 3 + 2, 0, 0))],
        out_specs=_row_specs(out_parts, tm),
        out_shape=out_parts,
        compiler_params=_cparams(("arbitrary",)),
        name="out_proj_residual",
    )(mix, w, *x_parts, mod)


def _rope(x, cos_t, sin_t):
    return x * cos_t + pltpu.roll(x, LANES // 2, 1) * sin_t


def _row_sums(x):
    hi = x.astype(BF16)
    lo = (x - hi.astype(F32)).astype(BF16)
    return _dot(jnp.concatenate([hi, lo], axis=1), jnp.ones((2 * x.shape[1], LANES), BF16))


def _amid_kernel(*refs, do_q, norm_kv):
    if do_q:
        (ql_ref, kvl_ref, kpe_ref, wq_ref, wk_ref, wv_ref, qlg_ref, kvlg_ref, qng_ref, kng_ref, cos_ref, sin_ref,
         q_out, k_out, v_out, ckv_out, kpe_out) = refs
    else:
        (kvl_ref, kpe_ref, wk_ref, wv_ref, kvlg_ref, kng_ref, cos_ref, sin_ref, k_out, v_out) = refs
    cos_t = cos_ref[...]
    sin_t = sin_ref[...]
    inv_qk = 1.0 / A_QK

    if do_q:
        ql = ql_ref[...]
        qn = ql * lax.rsqrt(jnp.mean(ql * ql, axis=-1, keepdims=True) + EPS) * qlg_ref[...]
        qu = _dot(qn.astype(BF16), wq_ref[...])
        g_nope = qng_ref[:, :A_NOPE]
        g_rope = qng_ref[:, A_NOPE:]
        for h in range(A_HEADS):
            qh = qu[:, h * A_QK_PAD:(h + 1) * A_QK_PAD]
            r = lax.rsqrt(_row_sums(qh * qh) * inv_qk + EPS)
            q_out[:, h * A_QK_PAD:h * A_QK_PAD + A_NOPE] = (qh[:, :A_NOPE] * r * g_nope).astype(BF16)
            q_out[:, h * A_QK_PAD + A_NOPE:(h + 1) * A_QK_PAD] = _rope(qh[:, A_NOPE:] * r * g_rope,
                                                                       cos_t, sin_t).astype(BF16)

    kvl = kvl_ref[...]
    if norm_kv:
        ckv = kvl * lax.rsqrt(jnp.mean(kvl * kvl, axis=-1, keepdims=True) + EPS) * kvlg_ref[...]
    else:
        ckv = kvl
    ckv_b = ckv.astype(BF16)
    kn = _dot(ckv_b, wk_ref[...])
    v_out[...] = _dot(ckv_b, wv_ref[...]).astype(BF16)
    kpe = kpe_ref[...]
    if do_q:
        @pl.when(pl.program_id(0) < CTX_ROWS // TM_MID)
        def _():
            half = A_ROPE // 2
            ckv_out[...] = ckv.reshape(ckv_out.shape)
            kpe_out[...] = jnp.concatenate([kpe[:, :half], kpe[:, LANES // 2:LANES // 2 + half]],
                                           axis=1).reshape(kpe_out.shape)
    sp = _row_sums(kpe * kpe)
    g_nope = kng_ref[:, :A_NOPE]
    g_rope = kng_ref[:, A_NOPE:]
    for h in range(A_HEADS):
        a = kn[:, h * A_NOPE:(h + 1) * A_NOPE]
        r = lax.rsqrt((_row_sums(a * a) + sp) * inv_qk + EPS)
        k_out[:, h * A_QK_PAD:h * A_QK_PAD + A_NOPE] = (a * r * g_nope).astype(BF16)
        k_out[:, h * A_QK_PAD + A_NOPE:(h + 1) * A_QK_PAD] = _rope(kpe * r * g_rope, cos_t, sin_t).astype(BF16)


def _layer_slab_spec(tm, layer, width):
    n_ctx = CTX_ROWS // tm
    return pl.BlockSpec((tm // SEQ, None, SEQ, width), lambda i: (jnp.minimum(i, n_ctx - 1), layer, 0, 0))


def _call_with_carried(kernel, n_plain_out, carried, **kw):
    def run(*args, in_specs):
        args = list(args)
        in_specs = list(in_specs)
        n_real = len(args)
        aliases = {}
        for idx, arr in enumerate(carried):
            if arr is not None:
                aliases[len(args)] = n_plain_out + idx
                args.append(arr)
                in_specs.append(pl.BlockSpec(memory_space=pl.ANY))
        n_in = len(args)

        def body(*refs):
            kernel(*refs[:n_real], *refs[n_in:])

        return pl.pallas_call(body, in_specs=in_specs, input_output_aliases=aliases, **kw)(*args)
    return run


def _amid_tokens(y, wq, wk, wv, qlg, kvlg, qng, kng, cos_t, sin_t, layer, ckv_new, kpe_new):
    tm = TM_MID
    const = lambda i: (0, 0)
    rope_map = lambda i: (_rope_block_of_tile(i, tm), 0)
    n_ab = (DEPTH + 1) // 2
    run = _call_with_carried(
        functools.partial(_amid_kernel, do_q=True, norm_kv=True), 3, [ckv_new, kpe_new],
        grid=(ROWS // tm,),
        out_specs=[pl.BlockSpec((tm, A_HEADS * A_QK_PAD), lambda i: (i, 0)),
                   pl.BlockSpec((tm, A_HEADS * A_QK_PAD), lambda i: (i, 0)),
                   pl.BlockSpec((tm, A_HEADS * A_VDIM), lambda i: (i, 0)),
                   _layer_slab_spec(tm, layer, KV_LORA), _layer_slab_spec(tm, layer, A_ROPE)],
        out_shape=[jax.ShapeDtypeStruct((ROWS, A_HEADS * A_QK_PAD), BF16),
                   jax.ShapeDtypeStruct((ROWS, A_HEADS * A_QK_PAD), BF16),
                   jax.ShapeDtypeStruct((ROWS, A_HEADS * A_VDIM), BF16),
                   jax.ShapeDtypeStruct((BATCH, n_ab, SEQ, KV_LORA), F32),
                   jax.ShapeDtypeStruct((BATCH, n_ab, SEQ, A_ROPE), F32)],
        compiler_params=_cparams(("arbitrary",)),
        name="mla_qkv_prep")
    return run(y, y, y, wq, wk, wv, qlg, kvlg, qng, kng, cos_t, sin_t,
               in_specs=[pl.BlockSpec((tm, Q_LORA), lambda i: (i, AB_QLAT // Q_LORA)),
                         pl.BlockSpec((tm, KV_LORA), lambda i: (i, AB_KVLAT // KV_LORA)),
                         pl.BlockSpec((tm, LANES), lambda i: (i, AB_KPE // LANES)),
                         pl.BlockSpec(wq.shape, const), pl.BlockSpec(wk.shape, const), pl.BlockSpec(wv.shape, const),
                         pl.BlockSpec(qlg.shape, const), pl.BlockSpec(kvlg.shape, const),
                         pl.BlockSpec(qng.shape, const), pl.BlockSpec(kng.shape, const),
                         pl.BlockSpec((tm, LANES), rope_map), pl.BlockSpec((tm, LANES), rope_map)])


def _amid_cache(ckv, kpe, wk, wv, kvlg, kng, cos_t, sin_t):
    rows = ckv.shape[0]
    tm = TM_MID
    const = lambda i: (0, 0)
    outs = pl.pallas_call(
        functools.partial(_amid_kernel, do_q=False, norm_kv=False),
        grid=(rows // tm,),
        in_specs=[pl.BlockSpec((tm, KV_LORA), lambda i: (i, 0)),
                  pl.BlockSpec((tm, LANES), lambda i: (i, 0)),
                  pl.BlockSpec(wk.shape, const), pl.BlockSpec(wv.shape, const),
                  pl.BlockSpec(kvlg.shape, const), pl.BlockSpec(kng.shape, const),
                  pl.BlockSpec((tm, LANES), const), pl.BlockSpec((tm, LANES), const)],
        out_specs=[pl.BlockSpec((tm, A_HEADS * A_QK_PAD), lambda i: (i, 0)),
                   pl.BlockSpec((tm, A_HEADS * A_VDIM), lambda i: (i, 0))],
        out_shape=[jax.ShapeDtypeStruct((rows, A_HEADS * A_QK_PAD), BF16),
                   jax.ShapeDtypeStruct((rows, A_HEADS * A_VDIM), BF16)],
        compiler_params=_cparams(("parallel",)),
        name="mla_cache_kv_prep",
    )(ckv, kpe, wk, wv, kvlg, kng, cos_t, sin_t)
    return outs[0], outs[1]


def _scores(q, srcs):
    zs = []
    for k, _, bias in srcs:
        z = _dot_nt(q, k)
        if bias is not None:
            z = (z.reshape(z.shape[0] // bias.shape[0], *bias.shape) + bias).reshape(z.shape)
        zs.append(z)
    return zs


def _softmax_pv(zs, srcs, sink_z=None):
    tile_max = None
    for z in zs:
        for j in range(z.shape[1] // LANES):
            blk = z[:, j * LANES:(j + 1) * LANES]
            tile_max = blk if tile_max is None else jnp.maximum(tile_max, blk)
    m = tile_max.max(axis=-1, keepdims=True)
    if sink_z is not None:
        m = jnp.maximum(m, sink_z)
    acc = None
    for z, (_, v, _) in zip(zs, srcs):
        o = _dot(jnp.exp2(z - m).astype(BF16), jnp.concatenate([v, jnp.ones_like(v)], axis=1))
        acc = o if acc is None else acc + o
    dv = acc.shape[1] // 2
    den = acc[:, dv:]
    if sink_z is not None:
        den = den + jnp.exp2(sink_z - m)
    return acc[:, :dv] / den


def _attend_streams(n, q_of, srcs_of, sink_of=None):
    outs = []
    zs = _scores(q_of(0), srcs_of(0))
    for t in range(n):
        nxt = _scores(q_of(t + 1), srcs_of(t + 1)) if t + 1 < n else None
        outs.append(_softmax_pv(zs, srcs_of(t), None if sink_of is None else sink_of(t)))
        zs = nxt
    return outs


def _attn_a_kernel(*refs, n_src, heads):
    q_ref = refs[0]
    k_refs = refs[1:1 + n_src]
    v_refs = refs[1 + n_src:1 + 2 * n_src]
    gate_ref, o_ref = refs[1 + 2 * n_src:]
    sub = min(ATTN_SUB, q_ref.shape[0])
    streams = [(h, r) for h in range(heads) for r in range(q_ref.shape[0] // sub)]

    def rows(t):
        return slice(streams[t][1] * sub, (streams[t][1] + 1) * sub)

    def q_of(t):
        h = streams[t][0]
        return q_ref[rows(t), h * A_QK_PAD:(h + 1) * A_QK_PAD]

    def srcs_of(t):
        h = streams[t][0]
        return [(k_ref[:, h * A_QK_PAD:(h + 1) * A_QK_PAD], v_ref[:, h * A_VDIM:(h + 1) * A_VDIM], None)
                for k_ref, v_ref in zip(k_refs, v_refs)]

    outs = _attend_streams(len(streams), q_of, srcs_of)
    for t, o in enumerate(outs):
        cols = slice(streams[t][0] * A_VDIM, (streams[t][0] + 1) * A_VDIM)
        o_ref[rows(t), cols] = (o * _silu(gate_ref[rows(t), cols])).astype(o_ref.dtype)


def _attn_a(q, k, v, y, mix, kc=None, vc=None):
    latent = kc is not None
    if latent:
        nb, t, tq, hp, row0 = DEC_BATCH, DEC_SEQ, TQ_A, 1, CTX_ROWS
    else:
        nb, t, tq, hp, row0 = BATCH, SEQ, SEQ, A_HEADS, 0
    nq = t // tq
    qrow = lambda b, h, i: row0 // tq + b * nq + i
    in_specs = [pl.BlockSpec((tq, hp * A_QK_PAD), lambda b, h, i: (qrow(b, h, i), h))]
    args = [q]
    if latent:
        in_specs.append(pl.BlockSpec((PAST_LEN, hp * A_QK_PAD), lambda b, h, i: (b, h)))
        args.append(kc)
    in_specs.append(pl.BlockSpec((t, hp * A_QK_PAD), lambda b, h, i: (row0 // t + b, h)))
    args.append(k)
    if latent:
        in_specs.append(pl.BlockSpec((PAST_LEN, hp * A_VDIM), lambda b, h, i: (b, h)))
        args.append(vc)
    in_specs.append(pl.BlockSpec((t, hp * A_VDIM), lambda b, h, i: (row0 // t + b, h)))
    args.append(v)
    in_specs.append(pl.BlockSpec((tq, hp * A_VDIM), lambda b, h, i: (qrow(b, h, i), AB_AGATE // (hp * A_VDIM) + h)))
    args.append(y)
    run = _call_with_carried(
        functools.partial(_attn_a_kernel, n_src=2 if latent else 1, heads=hp), 0, [mix],
        grid=(nb, A_HEADS // hp, nq),
        out_specs=[pl.BlockSpec((tq, hp * A_VDIM), lambda b, h, i: (qrow(b, h, i), h))],
        out_shape=[_MIX_SHAPE],
        compiler_params=_cparams(("parallel", "parallel", "parallel")),
        name="mla_attention_latent" if latent else "mla_attention_context")
    return run(*args, in_specs=in_specs)[0]


def _hgrn_constants():
    c, nl = HG_CHUNK, HG_LEVELS
    t = np.arange(c)[:, None]
    u = np.arange(c)[None, :]
    tri_f = (u <= t).astype(np.float32)
    mask_f = np.zeros((nl, c, c), np.float32)
    coef_f = np.zeros((nl, c, LANES), np.float32)
    for l in range(nl):
        half = c >> (l + 1)
        seg = 2 * half
        mask_f[l] = ((u // seg) == (t // seg)) & ((t % seg) >= half) & ((u % seg) < half)
        later = np.broadcast_to((t % seg) >= half, (c, LANES))
        coef_f[l] = np.where(later, 1.0, -1.0 if half > 1 else 0.0)
    tri_b = tri_f[::-1, ::-1]
    mask_b = mask_f[:, ::-1, ::-1]
    coef_b = coef_f[:, ::-1, :]
    to_tri = lambda a: jnp.asarray(np.concatenate([a, a, a], axis=1), BF16)
    to_f32 = lambda a: jnp.asarray(np.ascontiguousarray(a), F32)
    return (to_tri(tri_f), to_tri(tri_b)), (to_f32(mask_f), to_f32(mask_b)), (to_f32(coef_f), to_f32(coef_b))


def _hgrn_decays(x, lb, tri3):
    kk = (1.0 - lb) * _sigmoid(-x)
    lf = jnp.log2(1.0 - kk)
    hi = lf.astype(BF16)
    r1 = lf - hi.astype(F32)
    mid = r1.astype(BF16)
    lo = (r1 - mid.astype(F32)).astype(BF16)
    return kk, lf, _dot(tri3, jnp.concatenate([hi, mid, lo], axis=0))


def _hgrn_level_arg(l, cs, lf, coef, forward):
    c = HG_CHUNK
    nv = c // 8
    half = c >> (l + 1)
    if half == 1:
        return lf * coef
    cs3 = cs.reshape(nv, 8, LANES)

    def in_vreg_row(r):
        return jnp.broadcast_to(cs3[:, r:r + 1, :], (nv, 8, LANES))

    if half >= 8:
        m = half // 8
        nseg = nv // (2 * m)
        edge = cs3[:, 7:8, :] if forward else cs3[:, 0:1, :]
        e4 = edge.reshape(nseg, 2 * m, 1, LANES)
        a = e4[:, m - 1:m] if forward else e4[:, m:m + 1]
        anchor = jnp.broadcast_to(a, (nseg, 2 * m, 8, LANES)).reshape(c, LANES)
    elif half == 4:
        anchor = in_vreg_row(3 if forward else 4).reshape(c, LANES)
    else:
        r0, r1 = (1, 5) if forward else (2, 6)
        sub = lax.broadcasted_iota(jnp.int32, (nv, 8, LANES), 1)
        anchor = jnp.where(sub < 4, in_vreg_row(r0), in_vreg_row(r1)).reshape(c, LANES)
    return (cs - anchor) * coef


def _hgrn_kernel(*refs, t_len, heads, zero_init, emit_state):
    c = HG_CHUNK
    n_chunks = t_len // c
    it = iter(refs)
    bq_ref, ff_ref, fb_ref, vi_ref, bg_ref, lb_ref, hg_ref = (next(it) for _ in range(7))
    if not zero_init:
        s0f_ref, s0b_ref = next(it), next(it)
    trif_ref, trib_ref, maskf_ref, maskb_ref, coeff_ref, coefb_ref = (next(it) for _ in range(6))
    o_ref = next(it)
    if emit_state:
        sf_ref, sb_ref = next(it), next(it)
    of_ref, ob_ref, stf_ref, stb_ref = (next(it) for _ in range(4))

    for h in range(heads):
        if zero_init:
            stf_ref[h] = jnp.zeros((B_DV, B_DK), F32)
            stb_ref[h] = jnp.zeros((B_DV, B_DK), F32)
        else:
            stf_ref[h] = s0f_ref[h].T
            stb_ref[h] = s0b_ref[h].T
    lb = lb_ref[...]

    nl = HG_LEVELS
    unroll = min(HG_UNROLL // heads, n_chunks)
    dirs = ((ff_ref, 0, trif_ref, maskf_ref, coeff_ref, stf_ref, of_ref, True),
            (fb_ref, 1, trib_ref, maskb_ref, coefb_ref, stb_ref, ob_ref, False))

    def body(i, carry):
        chains = []
        for h in range(heads):
            cols = slice(h * LANES, (h + 1) * LANES)
            for f_ref, lb_row, tri_ref, mask_ref, coef_ref, st_ref, out_ref, forward in dirs:
                for u in range(unroll):
                    k = i * unroll + u
                    r0 = pl.multiple_of((k if forward else n_chunks - 1 - k) * c, c)
                    chains.append(dict(rows=pl.ds(r0, c), cols=cols, head=h, f_ref=f_ref,
                                       lb=lb[lb_row:lb_row + 1, cols], tri_ref=tri_ref, mask_ref=mask_ref,
                                       coef_ref=coef_ref, st_ref=st_ref, out_ref=out_ref, forward=forward))
        for ch in chains:
            ch["kk"], ch["lf"], ch["cs"] = _hgrn_decays(ch["f_ref"][ch["rows"], ch["cols"]], ch["lb"],
                                                        ch["tri_ref"][...])
            ch["q"] = _silu(bq_ref[ch["rows"], ch["cols"]])
            ch["v"] = vi_ref[ch["rows"], ch["cols"]]
            ch["q16"] = ch["q"].astype(BF16)
            ch["kk16"] = ch["kk"].astype(BF16)
            ch["sc"] = jnp.zeros((c, c), F32)
        for l in range(nl):
            for ch in chains:
                el = jnp.exp2(_hgrn_level_arg(l, ch["cs"], ch["lf"], ch["coef_ref"][l], ch["forward"])).astype(BF16)
                ch["sc"] = ch["sc"] + ch["mask_ref"][l] * _dot_nt(ch["q16"] * el, ch["kk16"] * el)
        for ch in chains:
            cs, q, kk, v = ch["cs"], ch["q"], ch["kk"], ch["v"]
            end = cs[c - 1:c, :] if ch["forward"] else cs[0:1, :]
            ch["end"] = end
            ch["q_in"] = (q * jnp.exp2(cs)).astype(BF16)
            ch["upd"] = _dot(v.T.astype(BF16), (kk * jnp.exp2(end - cs)).astype(BF16))
            ch["o"] = _dot(ch["sc"].astype(BF16), v.astype(BF16)) + jnp.sum(q * kk, axis=-1, keepdims=True) * v
        for ch in chains:
            st = ch["st_ref"][ch["head"]]
            ch["out_ref"][ch["rows"], ch["cols"]] = ch["o"] + _dot_nt(ch["q_in"], st.astype(BF16))
            ch["st_ref"][ch["head"]] = jnp.exp2(ch["end"]) * st + ch["upd"]
        return carry

    lax.fori_loop(0, n_chunks // unroll, body, 0)
    if emit_state:
        for h in range(heads):
            sf_ref[h] = stf_ref[h].T
            sb_ref[h] = stb_ref[h].T

    hg = hg_ref[...]
    blk = min(HG_UNROLL, n_chunks) * c

    def finish(i, carry):
        rows = pl.ds(pl.multiple_of(i * blk, blk), blk)
        for h in range(heads):
            cols = slice(h * LANES, (h + 1) * LANES)
            o = of_ref[rows, cols] + ob_ref[rows, cols]
            o = o * lax.rsqrt(jnp.mean(o * o, axis=-1, keepdims=True) + EPS) * hg
            o_ref[rows, cols] = (o * _silu(bg_ref[rows, cols])).astype(o_ref.dtype)
        return carry

    lax.fori_loop(0, t_len // blk, finish, 0)


def _hgrn(y, lb, hg, mix, consts, s0f=None, s0b=None, layer=None, sf_new=None, sb_new=None):
    latent = s0f is not None
    if latent:
        nb, t, row0, hp = DEC_BATCH, DEC_SEQ, CTX_ROWS, 1
    else:
        nb, t, row0, hp = BATCH, SEQ, 0, 8
    w = hp * LANES
    rb = lambda b: row0 // t + b
    col = lambda off: (lambda b, h: (rb(b), off // w + h))
    const2 = lambda b, h: (0, 0)
    const3 = lambda b, h: (0, 0, 0)
    in_specs = [pl.BlockSpec((t, w), col(AB_BQ)), pl.BlockSpec((t, w), col(AB_BFF)),
                pl.BlockSpec((t, w), col(AB_BFB)), pl.BlockSpec((t, w), col(AB_BI)),
                pl.BlockSpec((t, w), col(AB_BGATE)),
                pl.BlockSpec((2, w), lambda b, h: (0, h)),
                pl.BlockSpec((1, LANES), const2)]
    args = [y, y, y, y, y, lb, hg]
    if latent:
        st_spec = pl.BlockSpec((None, hp, B_DK, B_DV), lambda b, h: (b, h, 0, 0))
        in_specs += [st_spec, st_spec]
        args += [s0f, s0b]
    tris, masks, coefs = consts
    in_specs += ([pl.BlockSpec(a.shape, const2) for a in tris] + [pl.BlockSpec(a.shape, const3) for a in masks]
                 + [pl.BlockSpec(a.shape, const3) for a in coefs])
    args += [*tris, *masks, *coefs]
    out_specs = [pl.BlockSpec((t, w), lambda b, h: (rb(b), A_HEADS * A_VDIM // w + h))]
    out_shape = [_MIX_SHAPE]
    carried = [mix]
    if not latent:
        st_out = pl.BlockSpec((None, None, hp, B_DK, B_DV), lambda b, h: (b, layer, h, 0, 0))
        out_specs += [st_out, st_out]
        out_shape += [jax.ShapeDtypeStruct((nb, (DEPTH + 1) // 2, B_HEADS, B_DK, B_DV), F32)] * 2
        carried += [sf_new, sb_new]
    run = _call_with_carried(
        functools.partial(_hgrn_kernel, t_len=t, heads=hp, zero_init=not latent, emit_state=not latent), 0, carried,
        grid=(nb, B_HEADS // hp),
        out_specs=out_specs,
        out_shape=out_shape,
        scratch_shapes=[pltpu.VMEM((t, w), F32), pltpu.VMEM((t, w), F32),
                        pltpu.VMEM((hp, B_DV, B_DK), F32), pltpu.VMEM((hp, B_DV, B_DK), F32)],
        compiler_params=_cparams(("parallel", "parallel")),
        name="hgrn2_latent" if latent else "hgrn2_context")
    return run(*args, in_specs=in_specs)


def _cmid_kernel(q_ref, k_ref, v_ref, qg_ref, kg_ref, cos_ref, sin_ref, q_out, k_out, v_out, kc_out, vc_out):
    cos_t = cos_ref[...]
    sin_t = sin_ref[...]
    qg = qg_ref[...]
    kg = kg_ref[...]
    in_ctx = pl.program_id(0) < CTX_ROWS // TM_MID

    def norm(x, g):
        return x * lax.rsqrt(_row_sums(x * x) * (1.0 / C_HEAD_DIM) + EPS) * g

    def rope(x):
        return _rope(x, cos_t, sin_t)

    for h in range(C_HEADS):
        sl = slice(h * C_HEAD_DIM, (h + 1) * C_HEAD_DIM)
        q_out[:, sl] = rope(norm(q_ref[:, sl], qg)).astype(BF16)
    for h in range(C_KV_HEADS):
        sl = slice(h * C_HEAD_DIM, (h + 1) * C_HEAD_DIM)
        kn = norm(k_ref[:, sl], kg)
        k_out[:, sl] = rope(kn).astype(BF16)

        @pl.when(in_ctx)
        def _():
            kc_out[:, :, sl] = kn.reshape(kc_out.shape[0], SEQ, C_HEAD_DIM)
    v = v_ref[...]
    v_out[...] = v.astype(BF16)

    @pl.when(in_ctx)
    def _():
        vc_out[...] = v.reshape(vc_out.shape)


def _cmid(y, qg, kg, cos_t, sin_t, layer, kc_new, vc_new):
    tm = TM_MID
    const = lambda i: (0, 0)
    rope_map = lambda i: (_rope_block_of_tile(i, tm), 0)
    cache = jax.ShapeDtypeStruct((BATCH, DEPTH // 2, SEQ, C_KV_WIDTH), F32)
    run = _call_with_carried(
        _cmid_kernel, 3, [kc_new, vc_new],
        grid=(ROWS // tm,),
        out_specs=[pl.BlockSpec((tm, C_WIDTH), lambda i: (i, 0)),
                   pl.BlockSpec((tm, C_KV_WIDTH), lambda i: (i, 0)),
                   pl.BlockSpec((tm, C_KV_WIDTH), lambda i: (i, 0)),
                   _layer_slab_spec(tm, layer, C_KV_WIDTH), _layer_slab_spec(tm, layer, C_KV_WIDTH)],
        out_shape=[jax.ShapeDtypeStruct((ROWS, C_WIDTH), BF16),
                   jax.ShapeDtypeStruct((ROWS, C_KV_WIDTH), BF16),
                   jax.ShapeDtypeStruct((ROWS, C_KV_WIDTH), BF16), cache, cache],
        compiler_params=_cparams(("arbitrary",)),
        name="gqa_qkv_prep")
    return run(y, y, y, qg, kg, cos_t, sin_t,
               in_specs=[pl.BlockSpec((tm, C_WIDTH), lambda i: (i, C_Q // C_WIDTH)),
                         pl.BlockSpec((tm, C_KV_WIDTH), lambda i: (i, C_K // C_KV_WIDTH)),
                         pl.BlockSpec((tm, C_KV_WIDTH), lambda i: (i, C_V // C_KV_WIDTH)),
                         pl.BlockSpec((1, C_HEAD_DIM), const), pl.BlockSpec((1, C_HEAD_DIM), const),
                         pl.BlockSpec((tm, LANES), rope_map), pl.BlockSpec((tm, LANES), rope_map)])


def _band_start(i, tq, t_len):
    return pl.multiple_of(jnp.clip(i * tq - WINDOW, 0, t_len - (tq + 2 * WINDOW)), WINDOW)


def _band_bias(tq, t_len):
    width = tq + 2 * WINDOW
    i = np.arange(t_len // tq)[:, None, None]
    qpos = i * tq + np.arange(tq)[None, :, None]
    kpos = np.clip(i * tq - WINDOW, 0, t_len - width) + np.arange(width)[None, None, :]
    return jnp.asarray(np.where(np.abs(kpos - qpos) <= WINDOW, 0.0, NEG_BIG), F32)


def _attn_c_kernel(*refs, band, tq, t_len, groups, stack):
    if band:
        q_ref, kc_ref, vc_ref, kl_ref, vl_ref, bias_ref, sink_ref, gate_ref, o_ref = refs
    else:
        q_ref, kc_ref, vc_ref, sink_ref, gate_ref, o_ref = refs
    hd = C_HEAD_DIM
    sub = WINDOW if band else tq
    width = sub + 2 * WINDOW
    streams = [(g, s, u) for g in range(groups) for u in range(tq // sub) for s in range(C_GROUP // stack)]

    def heads_of(t):
        g, s, _ = streams[t]
        return [g * C_GROUP + s * stack + r for r in range(stack)]

    def rows_of(t):
        u = streams[t][2]
        return slice(u * sub, (u + 1) * sub)

    def q_of(t):
        return jnp.concatenate([q_ref[rows_of(t), h * hd:(h + 1) * hd] for h in heads_of(t)], axis=0)

    def srcs_of(t):
        g, _, u = streams[t]
        cols = slice(g * hd, (g + 1) * hd)
        srcs = [(kc_ref[:, cols], vc_ref[:, cols], None)]
        if band:
            start = _band_start(pl.program_id(2) * (tq // sub) + u, sub, t_len)
            srcs.append((kl_ref[pl.ds(start, width), cols], vl_ref[pl.ds(start, width), cols], bias_ref[u]))
        return srcs

    def sink_of(t):
        return jnp.concatenate([jnp.broadcast_to(sink_ref[h][:, :1] * LOG2_E, (sub, 1)) for h in heads_of(t)], axis=0)

    outs = _attend_streams(len(streams), q_of, srcs_of, sink_of)
    for t, o in enumerate(outs):
        for r, h in enumerate(heads_of(t)):
            cols = slice(h * hd, (h + 1) * hd)
            o_ref[rows_of(t), cols] = (o[r * sub:(r + 1) * sub] * _silu(gate_ref[rows_of(t), cols])).astype(o_ref.dtype)


def _attn_c(q, k, v, y, sink, mix, kc=None, vc=None):
    latent = kc is not None
    if latent:
        nb, t, tq, gp, stack, row0 = DEC_BATCH, DEC_SEQ, TQ_C, 1, 4, CTX_ROWS
    else:
        nb, t, tq, gp, stack, row0 = BATCH, SEQ, SEQ, C_KV_HEADS, 1, 0
    gw = gp * C_GROUP * C_HEAD_DIM
    kvw = gp * C_HEAD_DIM
    nq = t // tq
    qrow = lambda b, g, i: row0 // tq + b * nq + i
    own_kv = pl.BlockSpec((t, kvw), lambda b, g, i: (row0 // t + b, g))
    in_specs = [pl.BlockSpec((tq, gw), lambda b, g, i: (qrow(b, g, i), g))]
    args = [q]
    if latent:
        ctx_kv = pl.BlockSpec((PAST_LEN, kvw), lambda b, g, i: (b, g))
        in_specs += [ctx_kv, ctx_kv, own_kv, own_kv,
                     pl.BlockSpec((None, tq // WINDOW, WINDOW, 3 * WINDOW), lambda b, g, i: (i, 0, 0, 0))]
        args += [kc, vc, k, v, _band_bias(WINDOW, t).reshape(nq, tq // WINDOW, WINDOW, 3 * WINDOW)]
    else:
        in_specs += [own_kv, own_kv]
        args += [k, v]
    in_specs.append(pl.BlockSpec((gp * C_GROUP, 1, LANES), lambda b, g, i: (g, 0, 0)))
    args.append(sink)
    in_specs.append(pl.BlockSpec((tq, gw), lambda b, g, i: (qrow(b, g, i), C_GATE // gw + g)))
    args.append(y)
    run = _call_with_carried(
        functools.partial(_attn_c_kernel, band=latent, tq=tq, t_len=t, groups=gp, stack=stack), 0, [mix],
        grid=(nb, C_KV_HEADS // gp, nq),
        out_specs=[pl.BlockSpec((tq, gw), lambda b, g, i: (qrow(b, g, i), g))],
        out_shape=[_MIX_SHAPE],
        compiler_params=_cparams(("parallel", "parallel", "parallel")),
        name="gqa_attention_latent" if latent else "gqa_attention_context")
    return run(*args, in_specs=in_specs)[0]


def _axial_angles(n_tokens, rot_dim):
    rows = n_tokens // GRID_W
    row = jnp.repeat(jnp.arange(rows, dtype=F32), GRID_W)
    col = jnp.tile(jnp.arange(GRID_W, dtype=F32), rows)
    n_freq = rot_dim // 4
    inv = ROPE_BASE ** (-jnp.arange(n_freq, dtype=F32) / n_freq)
    return jnp.concatenate([row[:, None] * inv, col[:, None] * inv], axis=-1)


def _spread_halves(a, fill=0.0):
    h = a.shape[-1] // 2
    pad = jnp.full(a.shape[:-1] + (LANES // 2 - h,), fill, a.dtype)
    return jnp.concatenate([a[..., :h], pad, a[..., h:], pad], axis=-1)


def _spread_head(a):
    return jnp.concatenate([a[..., :A_NOPE], _spread_halves(a[..., A_NOPE:])], axis=-1)


def _rope_tables(rot_dim, tm):
    ang = _axial_angles(DEC_SEQ, rot_dim)
    cos, sin = jnp.cos(ang), jnp.sin(ang)
    cos_t = _spread_halves(jnp.concatenate([cos, cos], axis=-1), 1.0)
    sin_t = _spread_halves(jnp.concatenate([-sin, sin], axis=-1))
    cos_t = jnp.concatenate([jnp.ones((tm, LANES), F32), cos_t], axis=0)
    sin_t = jnp.concatenate([jnp.zeros((tm, LANES), F32), sin_t], axis=0)
    return cos_t, sin_t


def _lower_bounds(lb_logits):
    p = jax.nn.softmax(lb_logits.astype(F32), axis=0)
    return jnp.cumsum(p, axis=0) - p[0:1]


def kernel(x_prompt, x_sample, cache_ckv, cache_kpe, state_hgrn_fwd, state_hgrn_bwd, cache_k_c, cache_v_c, c, c_ctx,
           mod_w_ab, mod_b_ab, norm_ab, w_in_ab, q_lora_norm, kv_lora_norm, w_q_up, w_kv_up, q_norm_ab, k_norm_ab,
           hgrn_lb_logits, hgrn_out_norm, w_out_ab, mod_w_c, mod_b_c, norm_c, w_in_c, q_norm_c, k_norm_c, sink_c,
           w_out_c):
    x_parts = [x_prompt.reshape(CTX_ROWS, D_MODEL), x_sample.reshape(LAT_ROWS, D_MODEL)]
    cond8 = jnp.concatenate([c_ctx[None, :], c, jnp.zeros((N_COND - 1 - DEC_BATCH, D_MODEL), F32)], axis=0)
    mods_ab = _modulation(cond8, mod_w_ab, mod_b_ab)
    mods_c = _modulation(cond8, mod_w_c, mod_b_c)
    lower = _lower_bounds(hgrn_lb_logits)
    cos_a, sin_a = _rope_tables(A_ROPE, TM_MID)
    cos_c, sin_c = _rope_tables(C_HEAD_DIM, TM_MID)
    hg_consts = _hgrn_constants()
    w_in_ab16 = _prep_w_in_ab(jnp.swapaxes(w_in_ab, 1, 2))
    w_in_c16, w_out_ab16, w_out_c16 = w_in_c.astype(BF16), w_out_ab.astype(BF16), w_out_c.astype(BF16)

    ckv_new = kpe_new = sf_new = sb_new = kc_new = vc_new = None
    for layer in range(DEPTH):
        j = layer // 2
        last = layer == DEPTH - 1
        if layer % 2 == 0:
            mod = mods_ab[j].reshape(3 * N_COND, 1, D_MODEL)
            wq = _spread_head(w_q_up[j].reshape(Q_LORA, A_HEADS, A_QK)).reshape(Q_LORA, A_HEADS * A_QK_PAD).astype(BF16)
            wkv = w_kv_up[j].reshape(KV_LORA, A_HEADS, A_NOPE + A_VDIM)
            wk = wkv[:, :, :A_NOPE].reshape(KV_LORA, A_HEADS * A_NOPE).astype(BF16)
            wv = wkv[:, :, A_NOPE:].reshape(KV_LORA, A_HEADS * A_VDIM).astype(BF16)
            qlg, kvlg = q_lora_norm[j][None, :], kv_lora_norm[j][None, :]
            qng = _spread_head(q_norm_ab[j] * (A_QK ** -0.5 * LOG2_E))[None, :]
            kng = _spread_head(k_norm_ab[j])[None, :]

            y = _in_proj(x_parts, norm_ab[j][None, :], mod, w_in_ab16, j, w_transposed=True)
            q, k, v, ckv_new, kpe_new = _amid_tokens(y, wq, wk, wv, qlg, kvlg, qng, kng, cos_a, sin_a,
                                                     j, ckv_new, kpe_new)
            kpe_cache = _spread_halves(cache_kpe[:, j].reshape(DEC_BATCH * PAST_LEN, A_ROPE))
            kc, vc = _amid_cache(cache_ckv[:, j].reshape(DEC_BATCH * PAST_LEN, KV_LORA), kpe_cache, wk, wv, kvlg, kng,
                                 cos_a, sin_a)
            mix = _attn_a(q, k, v, y, None)
            mix = _attn_a(q, k, v, y, mix, kc=kc, vc=vc)
            hg = hgrn_out_norm[j][None, :]
            mix, sf_new, sb_new = _hgrn(y, lower[j], hg, mix, hg_consts, layer=j, sf_new=sf_new, sb_new=sb_new)
            (mix,) = _hgrn(y, lower[j], hg, mix, hg_consts, s0f=state_hgrn_fwd[:, j], s0b=state_hgrn_bwd[:, j])
            x_parts = _out_proj(mix, w_out_ab16, j, x_parts, mod, split_out=last)
        else:
            mod = mods_c[j].reshape(3 * N_COND, 1, D_MODEL)
            y = _in_proj(x_parts, norm_c[j][None, :], mod, w_in_c16, j,
                         col_map=lambda t: jnp.where(t < 2, t, jnp.where(t < 4, t + 1, 2)))
            q, k, v, kc_new, vc_new = _cmid(y, q_norm_c[j][None, :] * (C_HEAD_DIM ** -0.5 * LOG2_E),
                                            k_norm_c[j][None, :], cos_c, sin_c,
                                            j, kc_new, vc_new)
            kc = cache_k_c[:, j].reshape(DEC_BATCH * PAST_LEN, C_KV_WIDTH).astype(BF16)
            vc = cache_v_c[:, j].reshape(DEC_BATCH * PAST_LEN, C_KV_WIDTH).astype(BF16)
            sink = jnp.broadcast_to(sink_c[j][:, None, None], (C_HEADS, 1, LANES))
            mix = _attn_c(q, k, v, y, sink, None)
            mix = _attn_c(q, k, v, y, sink, mix, kc=kc, vc=vc)
            x_parts = _out_proj(mix, w_out_c16, j, x_parts, mod, split_out=last)

    cache_c_shape = (BATCH, DEPTH // 2, SEQ, C_KV_HEADS, C_HEAD_DIM)
    return (x_parts[0].reshape(BATCH, SEQ, D_MODEL), x_parts[1].reshape(DEC_BATCH, DEC_SEQ, D_MODEL),
            ckv_new, kpe_new, sf_new, sb_new, kc_new.reshape(cache_c_shape), vc_new.reshape(cache_c_shape))
```
